```python
import jax, jax.numpy as jnp
from jax import lax
import numpy as np

D_MODEL = 1024
BATCH = 8
SEQ = 2048
DEPTH = 1
DEC_BATCH = 128
DEC_SEQ = 1
PAST_LEN = 16384
PAGE_SIZE = 128

D_POOL = D_MODEL // 2
D_RNN = D_MODEL - D_POOL
POOL_WINDOWS = (2, 4, 8, 16)
N_POOL_GROUPS = len(POOL_WINDOWS)
POOL_GROUP = D_POOL // N_POOL_GROUPS
POOL_BUF = max(POOL_WINDOWS) - 1
RNN_HEADS = 8
RNN_HEAD_DIM = D_RNN // RNN_HEADS
CONV_WIDTH = 4
LRU_C = 8.0
D_IN = D_POOL + 2 * D_RNN
N_MEM = 256
XA_HEADS = 4
XA_HEAD_DIM = D_MODEL // XA_HEADS
N_EXPERTS = 64
TOP_K = 8
N_EXPERT_GROUPS = 8
TOPK_GROUPS = 4
D_EXPERT = D_MODEL // 4
D_SHARED = D_EXPERT
ROUTED_SCALE = 2.5
MOE_BLOCK = 128
EPS = 1e-6

kernel_name = 'hybrid_pool_rglru_moe_step'


def rms_norm(x, g):
    xf = x.astype(jnp.float32)
    y = xf * lax.rsqrt(jnp.mean(xf * xf, axis=-1, keepdims=True) + EPS)
    return (y * g.astype(jnp.float32)).astype(x.dtype)


def multiscale_pool(u, prev, pos0, w_grp, scale):
    B, S, _ = u.shape
    ext = jnp.concatenate([prev.astype(u.dtype), u], axis=1)
    extf = ext.astype(jnp.float32)
    cs = jnp.concatenate([jnp.zeros((B, 1, D_POOL), jnp.float32), jnp.cumsum(extf, axis=1)], axis=1)
    pos = pos0 + jnp.arange(S)
    hi = cs[:, POOL_BUF + 1:]
    means = []
    for g, w in enumerate(POOL_WINDOWS):
        sl = slice(g * POOL_GROUP, (g + 1) * POOL_GROUP)
        lo = cs[:, POOL_BUF + 1 - w: POOL_BUF + 1 - w + S, sl]
        cnt = jnp.minimum(w, pos + 1).astype(jnp.float32)[None, :, None]
        means.append((hi[..., sl] - lo) / cnt)
    mean = jnp.stack(means, axis=2)
    d = mean - u.astype(jnp.float32).reshape(B, S, N_POOL_GROUPS, POOL_GROUP)
    y = jnp.einsum('bsgc,gcd->bsgd', d, w_grp.astype(jnp.float32)).reshape(B, S, D_POOL)
    y = y * scale.astype(jnp.float32)
    return y.astype(u.dtype), ext[:, -POOL_BUF:]


def causal_conv(u, prev, w, b):
    S = u.shape[1]
    ext = jnp.concatenate([prev.astype(u.dtype), u], axis=1)
    y = b + sum(ext[:, k:k + S] * w[k] for k in range(CONV_WIDTH))
    return y, ext[:, -(CONV_WIDTH - 1):]


def block_diag(x, w, b):
    B, S, _ = x.shape
    y = jnp.einsum('bshi,hij->bshj', x.reshape(B, S, RNN_HEADS, RNN_HEAD_DIM), w.astype(x.dtype))
    return y.reshape(B, S, D_RNN) + b.astype(x.dtype)


def rg_lru(x, h0, pos0, wa, ba, wx, bx, lam):
    B, S, _ = x.shape
    xf = x.astype(jnp.float32)
    r = jax.nn.sigmoid(block_diag(xf, wa, ba))
    i = jax.nn.sigmoid(block_diag(xf, wx, bx))
    log_a = -LRU_C * r * jax.nn.softplus(-lam.astype(jnp.float32))
    a = jnp.exp(log_a)
    mult = jnp.sqrt(-jnp.expm1(2.0 * log_a))
    pos = pos0 + jnp.arange(S)
    mult = jnp.where((pos == 0)[None, :, None], 1.0, mult)
    bt = mult * i * xf

    def step(h, ab):
        a_t, b_t = ab
        h = a_t * h + b_t
        return h, h

    hT, hs = lax.scan(step, h0.astype(jnp.float32), (a.swapaxes(0, 1), bt.swapaxes(0, 1)))
    return hs.swapaxes(0, 1).astype(x.dtype), hT.astype(h0.dtype)


def mixing(x, pool_prev, conv_prev, h0, pos0, g, w_in, pool_w, pool_scale, conv_w, conv_b,
           wa, ba, wx, bx, lam, g_pool, g_rnn, w_out):
    h = rms_norm(x, g)
    z = h @ w_in
    u_pool = z[..., :D_POOL]
    u_rnn = z[..., D_POOL:D_POOL + D_RNN]
    u_gate = z[..., D_POOL + D_RNN:]
    y_pool, pool_state = multiscale_pool(u_pool, pool_prev, pos0, pool_w, pool_scale)
    c, conv_state = causal_conv(u_rnn, conv_prev, conv_w, conv_b)
    hs, hT = rg_lru(c, h0, pos0, wa, ba, wx, bx, lam)
    y_rnn = hs * jax.nn.gelu(u_gate)
    merged = jnp.concatenate([rms_norm(y_pool, g_pool), rms_norm(y_rnn, g_rnn)], axis=-1)
    return merged @ w_out, pool_state, conv_state, hT


def memory_kv(mem, g, wk, wv):
    m = rms_norm(mem, g)
    B, M, _ = m.shape
    k = (m @ wk).reshape(B, M, XA_HEADS, XA_HEAD_DIM)
    v = (m @ wv).reshape(B, M, XA_HEADS, XA_HEAD_DIM)
    return k, v


def cross_attention(x, k, v, g, wq, wo):
    B, S, _ = x.shape
    h = rms_norm(x, g)
    q = (h @ wq).reshape(B, S, XA_HEADS, XA_HEAD_DIM)
    s = jnp.einsum('bshd,bmhd->bhsm', q.astype(jnp.float32), k.astype(jnp.float32)) * (XA_HEAD_DIM ** -0.5)
    p = jax.nn.softmax(s, axis=-1)
    o = jnp.einsum('bhsm,bmhd->bshd', p, v.astype(jnp.float32)).reshape(B, S, D_MODEL)
    return o.astype(x.dtype) @ wo


def route(h, w_r, bias):
    T = h.shape[0]
    scores = jax.nn.sigmoid(h.astype(jnp.float32) @ w_r.astype(jnp.float32))
    biased = scores + bias.astype(jnp.float32)
    grp = biased.reshape(T, N_EXPERT_GROUPS, N_EXPERTS // N_EXPERT_GROUPS)
    grp_score = lax.top_k(grp, 2)[0].sum(-1)
    _, gidx = lax.top_k(grp_score, TOPK_GROUPS)
    gmask = jnp.any(gidx[..., None] == jnp.arange(N_EXPERT_GROUPS), axis=-2)
    emask = jnp.repeat(gmask, N_EXPERTS // N_EXPERT_GROUPS, axis=1)
    masked = jnp.where(emask, biased, -jnp.inf)
    _, eidx = lax.top_k(masked, TOP_K)
    w = jnp.take_along_axis(scores, eidx, axis=-1)
    w = w / jnp.sum(w, axis=-1, keepdims=True) * ROUTED_SCALE
    return eidx, w


def routed_experts(h, eidx, gw, w_gate, w_up, w_down):
    T, D = h.shape
    A = T * TOP_K
    flat_e = eidx.reshape(-1)
    flat_tok = jnp.arange(A, dtype=jnp.int32) // TOP_K
    flat_w = gw.reshape(-1)
    order = jnp.argsort(flat_e)
    se, st, sw = flat_e[order], flat_tok[order], flat_w[order]
    counts = jnp.bincount(flat_e, length=N_EXPERTS)
    padded = (counts + MOE_BLOCK - 1) // MOE_BLOCK * MOE_BLOCK
    pad_end = jnp.cumsum(padded)
    pad_start = pad_end - padded
    start = jnp.cumsum(counts) - counts
    dest = pad_start[se] + jnp.arange(A) - start[se]
    n_blocks = -(-A // MOE_BLOCK) + N_EXPERTS
    cap = n_blocks * MOE_BLOCK
    tok_buf = jnp.full((cap,), T, jnp.int32).at[dest].set(st)
    w_buf = jnp.zeros((cap,), jnp.float32).at[dest].set(sw)
    blk_exp = jnp.minimum(jnp.searchsorted(pad_end, jnp.arange(n_blocks) * MOE_BLOCK, side='right'), N_EXPERTS - 1)
    h_pad = jnp.concatenate([h, jnp.zeros((1, D), h.dtype)], axis=0)

    def body(acc, blk):
        tok, wt, e = blk
        xb = h_pad[tok]
        a = jax.nn.silu(xb @ w_gate[e]) * (xb @ w_up[e])
        yb = (a @ w_down[e]).astype(jnp.float32) * wt[:, None]
        return acc.at[tok].add(yb), None

    acc, _ = lax.scan(body, jnp.zeros((T + 1, D), jnp.float32),
                      (tok_buf.reshape(n_blocks, MOE_BLOCK), w_buf.reshape(n_blocks, MOE_BLOCK), blk_exp))
    return acc[:T].astype(h.dtype)


def moe_ffn(x, g, w_r, bias, w_gate, w_up, w_down, s_gate, s_up, s_down):
    B, S, D = x.shape
    h = rms_norm(x, g).reshape(B * S, D)
    eidx, gw = route(h, w_r, bias)
    y = routed_experts(h, eidx, gw, w_gate, w_up, w_down)
    y = y + (jax.nn.silu(h @ s_gate) * (h @ s_up)) @ s_down
    return y.reshape(B, S, D)


def setup_inputs(seed: int = 0) -> dict:
    key = jax.random.key(seed)
    ks = iter(jax.random.split(key, 48))
    f32 = jnp.float32

    def nrm(shape, scale):
        return jax.random.normal(next(ks), shape, f32) * scale

    def gain(shape):
        return 1.0 + 0.05 * jax.random.normal(next(ks), shape, f32)

    L = DEPTH
    a0 = jax.random.uniform(next(ks), (L, D_RNN), f32, minval=0.9, maxval=0.999)
    return {
        'x_prompt': nrm((BATCH, SEQ, D_MODEL), 1.0),
        'x_sample': nrm((DEC_BATCH, DEC_SEQ, D_MODEL), 1.0),
        'state_pool': nrm((L, DEC_BATCH, POOL_BUF, D_POOL), 1.0),
        'state_conv': nrm((L, DEC_BATCH, CONV_WIDTH - 1, D_RNN), 1.0),
        'state_h': nrm((L, DEC_BATCH, D_RNN), 1.0),
        'cache_mem_k': nrm((L, DEC_BATCH, N_MEM, XA_HEADS, XA_HEAD_DIM), 1.0),
        'cache_mem_v': nrm((L, DEC_BATCH, N_MEM, XA_HEADS, XA_HEAD_DIM), 1.0),
        'mem_prompt': nrm((BATCH, N_MEM, D_MODEL), 1.0),
        'norm_mix': gain((L, D_MODEL)),
        'w_in': nrm((L, D_MODEL, D_IN), D_MODEL ** -0.5),
        'pool_w': nrm((L, N_POOL_GROUPS, POOL_GROUP, POOL_GROUP), POOL_GROUP ** -0.5),
        'pool_scale': gain((L, D_POOL)),
        'conv_w': nrm((L, CONV_WIDTH, D_RNN), 0.5),
        'conv_b': nrm((L, D_RNN), 0.02),
        'gate_a_w': nrm((L, RNN_HEADS, RNN_HEAD_DIM, RNN_HEAD_DIM), RNN_HEAD_DIM ** -0.5),
        'gate_a_b': nrm((L, D_RNN), 0.02),
        'gate_x_w': nrm((L, RNN_HEADS, RNN_HEAD_DIM, RNN_HEAD_DIM), RNN_HEAD_DIM ** -0.5),
        'gate_x_b': nrm((L, D_RNN), 0.02),
        'lru_lambda': jnp.log(a0) - jnp.log1p(-a0),
        'norm_pool_out': gain((L, D_POOL)),
        'norm_rnn_out': gain((L, D_RNN)),
        'w_out': nrm((L, D_MODEL, D_MODEL), D_MODEL ** -0.5),
        'norm_xattn': gain((L, D_MODEL)),
        'norm_mem': gain((L, D_MODEL)),
        'xa_wq': nrm((L, D_MODEL, D_MODEL), D_MODEL ** -0.5),
        'xa_wk': nrm((L, D_MODEL, D_MODEL), D_MODEL ** -0.5),
        'xa_wv': nrm((L, D_MODEL, D_MODEL), D_MODEL ** -0.5),
        'xa_wo': nrm((L, D_MODEL, D_MODEL), D_MODEL ** -0.5),
        'norm_ffn': gain((L, D_MODEL)),
        'router_w': nrm((L, D_MODEL, N_EXPERTS), D_MODEL ** -0.5),
        'router_bias': nrm((L, N_EXPERTS), 0.01),
        'exp_w_gate': nrm((L, N_EXPERTS, D_MODEL, D_EXPERT), D_MODEL ** -0.5),
        'exp_w_up': nrm((L, N_EXPERTS, D_MODEL, D_EXPERT), D_MODEL ** -0.5),
        'exp_w_down': nrm((L, N_EXPERTS, D_EXPERT, D_MODEL), D_EXPERT ** -0.5),
        'sh_w_gate': nrm((L, D_MODEL, D_SHARED), D_MODEL ** -0.5),
        'sh_w_up': nrm((L, D_MODEL, D_SHARED), D_MODEL ** -0.5),
        'sh_w_down': nrm((L, D_SHARED, D_MODEL), D_SHARED ** -0.5),
        'norm_final': gain((D_MODEL,)),
    }


def reference(x_prompt, x_sample, state_pool, state_conv, state_h, cache_mem_k, cache_mem_v, mem_prompt,
              norm_mix, w_in, pool_w, pool_scale, conv_w, conv_b, gate_a_w, gate_a_b, gate_x_w, gate_x_b,
              lru_lambda, norm_pool_out, norm_rnn_out, w_out, norm_xattn, norm_mem, xa_wq, xa_wk, xa_wv, xa_wo,
              norm_ffn, router_w, router_bias, exp_w_gate, exp_w_up, exp_w_down, sh_w_gate, sh_w_up, sh_w_down,
              norm_final):
    xp, xs = x_prompt, x_sample
    Bp = xp.shape[0]
    zero_pool = jnp.zeros((Bp, POOL_BUF, D_POOL), xp.dtype)
    zero_conv = jnp.zeros((Bp, CONV_WIDTH - 1, D_RNN), xp.dtype)
    zero_h = jnp.zeros((Bp, D_RNN), xp.dtype)
    pool_p, conv_p, h_p, mk_p, mv_p = [], [], [], [], []
    pool_s, conv_s, h_s = [], [], []
    for l in range(DEPTH):
        mix_w = (norm_mix[l], w_in[l], pool_w[l], pool_scale[l], conv_w[l], conv_b[l], gate_a_w[l], gate_a_b[l],
                 gate_x_w[l], gate_x_b[l], lru_lambda[l], norm_pool_out[l], norm_rnn_out[l], w_out[l])
        moe_w = (norm_ffn[l], router_w[l], router_bias[l], exp_w_gate[l], exp_w_up[l], exp_w_down[l],
                 sh_w_gate[l], sh_w_up[l], sh_w_down[l])
        d, ps, cs, hT = mixing(xp, zero_pool, zero_conv, zero_h, 0, *mix_w)
        xp = xp + d
        mk, mv = memory_kv(mem_prompt, norm_mem[l], xa_wk[l], xa_wv[l])
        xp = xp + cross_attention(xp, mk, mv, norm_xattn[l], xa_wq[l], xa_wo[l])
        xp = xp + moe_ffn(xp, *moe_w)
        pool_p.append(ps); conv_p.append(cs); h_p.append(hT); mk_p.append(mk); mv_p.append(mv)
        d, ps, cs, hT = mixing(xs, state_pool[l], state_conv[l], state_h[l], PAST_LEN, *mix_w)
        xs = xs + d
        xs = xs + cross_attention(xs, cache_mem_k[l], cache_mem_v[l], norm_xattn[l], xa_wq[l], xa_wo[l])
        xs = xs + moe_ffn(xs, *moe_w)
        pool_s.append(ps); conv_s.append(cs); h_s.append(hT)
    y_prompt = rms_norm(xp, norm_final)
    y_sample = rms_norm(xs, norm_final)
    return (y_prompt, y_sample,
            jnp.stack(pool_p), jnp.stack(conv_p), jnp.stack(h_p), jnp.stack(mk_p), jnp.stack(mv_p),
            jnp.stack(pool_s), jnp.stack(conv_s), jnp.stack(h_s))
```

```python
import functools

import jax
import jax.numpy as jnp
from jax import lax
from jax.experimental import pallas as pl
from jax.experimental.pallas import tpu as pltpu

F32 = jnp.float32
BF16 = jnp.bfloat16
I32 = jnp.int32
U32 = jnp.uint32

D_MODEL = 1024
D_POOL = 512
D_RNN = 512
D_IN = D_POOL + 2 * D_RNN
POOL_WINDOWS = (2, 4, 8, 16)
POOL_GROUP = 128
POOL_BUF = 15
CONV_WIDTH = 4
LRU_C = 8.0
N_MEM = 256
XA_HEADS = 4
XA_HEAD_DIM = 256
N_EXPERTS = 64
TOP_K = 8
N_EXPERT_GROUPS = 8
GROUP_SIZE = N_EXPERTS // N_EXPERT_GROUPS
TOPK_GROUPS = 4
D_EXPERT = 256
ROUTED_SCALE = 2.5
EPS = 1e-6
PAST_LEN = 16384

HALO = 16
CONV_HALO = 8
TS = 256
BM = 256
TM_DISPATCH = 128
TM_COMBINE = 128
VMEM_LIMIT = 56 * 1024 * 1024


def _const_spec(shape):
    nd = len(shape)
    return pl.BlockSpec(shape, lambda *_: (0,) * nd, pipeline_mode=pl.Buffered(1))


def _rms(x, g):
    ms = jnp.mean(x * x, axis=-1, keepdims=True)
    return x * lax.rsqrt(ms + EPS) * g


def _dot(a, b):
    return jnp.dot(a, b, preferred_element_type=F32)


def _dot_nt(a, b, precision=None):
    return lax.dot_general(a, b, (((1,), (1,)), ((), ())), precision=precision,
                           preferred_element_type=F32)


def _softplus(x):
    return jnp.maximum(x, 0.0) + jnp.log1p(jnp.exp(-jnp.abs(x)))


def _pack_bf16_pairs(h):
    c = h.shape[1] // 2
    hb = h.astype(BF16).astype(F32)
    lo = lax.bitcast_convert_type(hb[:, :c], U32)
    hi = lax.bitcast_convert_type(hb[:, c:], U32)
    return (lo >> 16) | (hi & jnp.uint32(0xFFFF0000))


def _unpack_bf16_pairs(p):
    lo = lax.bitcast_convert_type(p << 16, F32)
    hi = lax.bitcast_convert_type(p & jnp.uint32(0xFFFF0000), F32)
    return jnp.concatenate([lo, hi], axis=1).astype(BF16)


def _gates_and_decay(c, pos_is_zero, wa_ref, ba_ref, wx_ref, bx_ref, lam_ref):
    cb = c.astype(BF16)
    half = D_RNN // 2
    ga = jnp.concatenate([_dot(cb[:, :half], wa_ref[0]), _dot(cb[:, half:], wa_ref[1])], axis=1) + ba_ref[...]
    gx = jnp.concatenate([_dot(cb[:, :half], wx_ref[0]), _dot(cb[:, half:], wx_ref[1])], axis=1) + bx_ref[...]
    r = jax.nn.sigmoid(ga)
    i = jax.nn.sigmoid(gx)
    log_a = (-LRU_C) * r * _softplus(-lam_ref[...])
    a = jnp.exp(log_a)
    mult = jnp.sqrt(1.0 - a * a)
    if pos_is_zero is not None:
        mult = jnp.where(pos_is_zero, 1.0, mult)
    return a, mult * i * c


def _pool_project(mean, u_pool, pw_ref, pscale_ref):
    d = (mean - u_pool).astype(BF16)
    half = D_POOL // 2
    y = jnp.concatenate([_dot(d[:, :half], pw_ref[0]), _dot(d[:, half:], pw_ref[1])], axis=1)
    return y * pscale_ref[...]


def _merge_out(y_pool, hs, u_gate, gpool_ref, grnn_ref, wout_ref):
    y_rnn = hs * jax.nn.gelu(u_gate)
    merged = jnp.concatenate([_rms(y_pool, gpool_ref[...]), _rms(y_rnn, grnn_ref[...])], axis=1)
    return _dot(merged.astype(BF16), wout_ref[...])


def _route(h3, wr_ref, rbias_ref, carry):
    r_tok = h3.shape[0]
    logits = _dot_nt(wr_ref[...], h3, precision=lax.Precision.HIGHEST)
    scores = jax.nn.sigmoid(logits)
    biased = scores + rbias_ref[...]
    neg = jnp.float32(-jnp.inf)
    gs = []
    for g in range(N_EXPERT_GROUPS):
        xg = biased[g * GROUP_SIZE:(g + 1) * GROUP_SIZE]
        m1 = jnp.max(xg, axis=0, keepdims=True)
        eq = xg == m1
        cnt = jnp.sum(eq.astype(F32), axis=0, keepdims=True)
        m2 = jnp.max(jnp.where(eq, neg, xg), axis=0, keepdims=True)
        gs.append(m1 + jnp.where(cnt >= 2.0, m1, m2))
    pieces = []
    for g in range(N_EXPERT_GROUPS):
        beaten = jnp.zeros_like(gs[g])
        for o in range(N_EXPERT_GROUPS):
            if o == g:
                continue
            wins = (gs[o] > gs[g]) | (gs[o] == gs[g]) if o < g else (gs[o] > gs[g])
            beaten = beaten + wins.astype(F32)
        keep = beaten < float(TOPK_GROUPS)
        xg = biased[g * GROUP_SIZE:(g + 1) * GROUP_SIZE]
        pieces.append(jnp.where(keep, xg, neg))
    cur = jnp.concatenate(pieces, axis=0)
    eid = lax.broadcasted_iota(I32, (N_EXPERTS, r_tok), 0).astype(F32)
    idx_rows, score_rows = [], []
    sel = jnp.zeros((N_EXPERTS, r_tok), F32)
    for _ in range(TOP_K):
        m = jnp.max(cur, axis=0, keepdims=True)
        idx = jnp.min(jnp.where(cur == m, eid, float(N_EXPERTS)), axis=0, keepdims=True)
        oh = eid == idx
        score_rows.append(jnp.sum(jnp.where(oh, scores, 0.0), axis=0, keepdims=True))
        idx_rows.append(idx)
        sel = sel + oh.astype(F32)
        cur = jnp.where(oh, neg, cur)
    tot = score_rows[0]
    for s in score_rows[1:]:
        tot = tot + s
    w_rows = [s / tot * ROUTED_SCALE for s in score_rows]
    rr = lax.broadcasted_iota(I32, (r_tok, r_tok), 0)
    cc = lax.broadcasted_iota(I32, (r_tok, r_tok), 1)
    tri = (rr < cc).astype(BF16)
    cum = _dot(sel.astype(BF16), tri) + carry
    rank_rows = [jnp.sum(jnp.where(eid == idx, cum, 0.0), axis=0, keepdims=True) for idx in idx_rows]
    counts = jnp.sum(sel, axis=1, keepdims=True)
    return idx_rows, w_rows, rank_rows, counts


def _moe_prologue(x2, gffn_ref, sg_ref, su_ref, sd_ref):
    h3 = _rms(x2, gffn_ref[...])
    h3b = h3.astype(BF16)
    act = jax.nn.silu(_dot(h3b, sg_ref[...])) * _dot(h3b, su_ref[...])
    shared = _dot(act.astype(BF16), sd_ref[...])
    return h3, x2 + shared


def _store_rows(ref, rows, dtype):
    for k, row in enumerate(rows):
        ref[k:k + 1, :] = row.astype(dtype)


def _memkv_kernel(mem_ref, g_ref, wk_ref, wv_ref, k_ref, v_ref, kb_ref, vb_ref):
    m = _rms(mem_ref[0], g_ref[...]).astype(BF16)
    k = _dot(m, wk_ref[...])
    v = _dot(m, wv_ref[...])
    k_ref[0] = k
    v_ref[0] = v
    kb_ref[0] = k.astype(BF16)
    vb_ref[0] = v.astype(BF16)


def _memkv(mem, g, wk, wv):
    b = mem.shape[0]
    blk = pl.BlockSpec((1, N_MEM, D_MODEL), lambda i: (i, 0, 0))
    return pl.pallas_call(
        _memkv_kernel,
        grid=(b,),
        in_specs=[blk, _const_spec((1, D_MODEL)), _const_spec((D_MODEL, D_MODEL)), _const_spec((D_MODEL, D_MODEL))],
        out_specs=[blk, blk, blk, blk],
        out_shape=[jax.ShapeDtypeStruct((b, N_MEM, D_MODEL), F32)] * 2
        + [jax.ShapeDtypeStruct((b, N_MEM, D_MODEL), BF16)] * 2,
        compiler_params=pltpu.CompilerParams(dimension_semantics=("arbitrary",), vmem_limit_bytes=VMEM_LIMIT),
        name="memkv",
    )(mem, g, wk, wv)


def _trunk_p_kernel(x_ref, kb_ref, vb_ref,
                    gmix_ref, win_ref, pw_ref, pscale_ref, cw_ref, cb_ref, wa_ref, ba_ref, wx_ref, bx_ref,
                    lam_ref, gpool_ref, grnn_ref, wout_ref,
                    gxa_ref, wq_ref, wo_ref,
                    gffn_ref, sg_ref, su_ref, sd_ref, wr_ref, rbias_ref,
                    xres_ref, hp_ref, eidx_ref, gw_ref, rank_ref, cnt_ref, pool_ref, conv_ref, hT_ref,
                    pool_prev, conv_prev, h_prev, carry):
    b = pl.program_id(0)
    j = pl.program_id(1)
    n_j = pl.num_programs(1)

    @pl.when(j == 0)
    def _():
        pool_prev[...] = jnp.zeros_like(pool_prev)
        conv_prev[...] = jnp.zeros_like(conv_prev)
        h_prev[...] = jnp.zeros_like(h_prev)

    @pl.when((b == 0) & (j == 0))
    def _():
        carry[...] = jnp.zeros_like(carry)

    x = x_ref[0]
    row = lax.broadcasted_iota(I32, (TS, 1), 0)
    pos = j * TS + row

    h = _rms(x, gmix_ref[...]).astype(BF16)
    z = _dot(h, win_ref[...])
    u_pool = z[:, :D_POOL]
    u_rnn = z[:, D_POOL:D_POOL + D_RNN]
    u_gate = z[:, D_POOL + D_RNN:]

    ext = jnp.concatenate([pool_prev[...], u_pool], axis=0)
    means = []
    for g, w in enumerate(POOL_WINDOWS):
        s = ext[:, g * POOL_GROUP:(g + 1) * POOL_GROUP]
        k = 1
        while k < w:
            s = s + pltpu.roll(s, k, 0)
            k *= 2
        inv = 1.0 / jnp.minimum(pos + 1, w).astype(F32)
        means.append(s[HALO:] * inv)
    mean = jnp.concatenate(means, axis=1)
    y_pool = _pool_project(mean, u_pool, pw_ref, pscale_ref)

    extc = jnp.concatenate([conv_prev[...], u_rnn], axis=0)
    c = u_rnn * cw_ref[CONV_WIDTH - 1:CONV_WIDTH, :]
    for k in range(1, CONV_WIDTH):
        c = c + pltpu.roll(extc, k, 0)[CONV_HALO:] * cw_ref[CONV_WIDTH - 1 - k:CONV_WIDTH - k, :]
    c = c + cb_ref[...]

    a, bt = _gates_and_decay(c, pos == 0, wa_ref, ba_ref, wx_ref, bx_ref, lam_ref)
    k = 1
    while k < TS:
        valid = row >= k
        a_sh = jnp.where(valid, pltpu.roll(a, k, 0), 1.0)
        b_sh = jnp.where(valid, pltpu.roll(bt, k, 0), 0.0)
        bt = bt + a * b_sh
        a = a * a_sh
        k *= 2
    hs = bt + a * h_prev[...]

    pool_prev[...] = u_pool[TS - HALO:]
    conv_prev[...] = u_rnn[TS - CONV_HALO:]
    h_prev[...] = hs[TS - 1:]

    @pl.when(j == n_j - 1)
    def _():
        pool_ref[0] = u_pool[TS - POOL_BUF:]
        conv_ref[0] = u_rnn[TS - (CONV_WIDTH - 1):]
        hT_ref[0] = hs[TS - 1:]

    x1 = x + _merge_out(y_pool, hs, u_gate, gpool_ref, grnn_ref, wout_ref)

    h2 = _rms(x1, gxa_ref[...]).astype(BF16)
    q = (_dot(h2, wq_ref[...]) * (XA_HEAD_DIM ** -0.5)).astype(BF16)
    outs = []
    for hd in range(XA_HEADS):
        sl = slice(hd * XA_HEAD_DIM, (hd + 1) * XA_HEAD_DIM)
        s = _dot_nt(q[:, sl], kb_ref[0, :, sl])
        s = s - jnp.max(s, axis=-1, keepdims=True)
        p = jnp.exp(s)
        p = p / jnp.sum(p, axis=-1, keepdims=True)
        outs.append(_dot(p.astype(BF16), vb_ref[0, :, sl]))
    o = jnp.concatenate(outs, axis=1).astype(BF16)
    x2 = x1 + _dot(o, wo_ref[...])

    h3, xres = _moe_prologue(x2, gffn_ref, sg_ref, su_ref, sd_ref)
    xres_ref[...] = xres
    hp_ref[...] = _pack_bf16_pairs(h3)
    idx_rows, w_rows, rank_rows, counts = _route(h3, wr_ref, rbias_ref, carry[...])
    _store_rows(eidx_ref, idx_rows, I32)
    _store_rows(gw_ref, w_rows, F32)
    _store_rows(rank_ref, rank_rows, I32)
    carry[...] = carry[...] + counts
    cnt_ref[...] = jnp.broadcast_to(carry[...], cnt_ref.shape)


def _trunk_p(x, kb, vb, mixw, xaw, moew):
    bsz, seq, _ = x.shape
    n_j = seq // TS
    t = bsz * seq
    tok = lambda b, j: (b * n_j + j, 0)
    lane_tok = lambda b, j: (0, b * n_j + j)
    per_b = lambda b, j: (b, 0, 0)
    weights = list(mixw) + list(xaw) + list(moew)
    in_specs = [pl.BlockSpec((1, TS, D_MODEL), lambda b, j: (b, j, 0)),
                pl.BlockSpec((1, N_MEM, D_MODEL), per_b),
                pl.BlockSpec((1, N_MEM, D_MODEL), per_b)] + [_const_spec(w.shape) for w in weights]
    out_shape = [jax.ShapeDtypeStruct((t, D_MODEL), F32),
                 jax.ShapeDtypeStruct((t, D_MODEL // 2), U32),
                 jax.ShapeDtypeStruct((TOP_K, t), I32),
                 jax.ShapeDtypeStruct((TOP_K, t), F32),
                 jax.ShapeDtypeStruct((TOP_K, t), I32),
                 jax.ShapeDtypeStruct((N_EXPERTS, 128), F32),
                 jax.ShapeDtypeStruct((bsz, POOL_BUF, D_POOL), F32),
                 jax.ShapeDtypeStruct((bsz, CONV_WIDTH - 1, D_RNN), F32),
                 jax.ShapeDtypeStruct((bsz, 1, D_RNN), F32)]
    out_specs = [pl.BlockSpec((TS, D_MODEL), tok),
                 pl.BlockSpec((TS, D_MODEL // 2), tok),
                 pl.BlockSpec((TOP_K, TS), lane_tok),
                 pl.BlockSpec((TOP_K, TS), lane_tok),
                 pl.BlockSpec((TOP_K, TS), lane_tok),
                 pl.BlockSpec((N_EXPERTS, 128), lambda b, j: (0, 0)),
                 pl.BlockSpec((1, POOL_BUF, D_POOL), per_b),
                 pl.BlockSpec((1, CONV_WIDTH - 1, D_RNN), per_b),
                 pl.BlockSpec((1, 1, D_RNN), per_b)]
    return pl.pallas_call(
        _trunk_p_kernel,
        grid=(bsz, n_j),
        in_specs=in_specs,
        out_specs=out_specs,
        out_shape=out_shape,
        scratch_shapes=[pltpu.VMEM((HALO, D_POOL), F32), pltpu.VMEM((CONV_HALO, D_RNN), F32),
                        pltpu.VMEM((1, D_RNN), F32), pltpu.VMEM((N_EXPERTS, 1), F32)],
        compiler_params=pltpu.CompilerParams(dimension_semantics=("arbitrary", "arbitrary"),
                                             vmem_limit_bytes=VMEM_LIMIT),
        name="trunk_p",
    )(x, kb, vb, *weights)


def _mix_s_kernel(x_ref, pool_ref, conv_ref, h0_ref,
                  gmix_ref, win_ref, pw_ref, pscale_ref, cw_ref, cb_ref, wa_ref, ba_ref, wx_ref, bx_ref,
                  lam_ref, gpool_ref, grnn_ref, wout_ref, gxa_ref, wq_ref,
                  x1_ref, q_ref, npool_ref, nconv_ref, nh_ref):
    x = x_ref[...]
    h = _rms(x, gmix_ref[...]).astype(BF16)
    z = _dot(h, win_ref[...])
    u_pool = z[:, :D_POOL]
    u_rnn = z[:, D_POOL:D_POOL + D_RNN]
    u_gate = z[:, D_POOL + D_RNN:]

    means = []
    for g, w in enumerate(POOL_WINDOWS):
        sl = slice(g * POOL_GROUP, (g + 1) * POOL_GROUP)
        s = u_pool[:, sl]
        for k in range(1, w):
            s = s + pool_ref[POOL_BUF - k, :, sl]
        means.append(s * (1.0 / min(w, PAST_LEN + 1)))
    mean = jnp.concatenate(means, axis=1)
    y_pool = _pool_project(mean, u_pool, pw_ref, pscale_ref)

    c = u_rnn * cw_ref[CONV_WIDTH - 1:CONV_WIDTH, :]
    for k in range(1, CONV_WIDTH):
        c = c + conv_ref[CONV_WIDTH - 1 - k] * cw_ref[CONV_WIDTH - 1 - k:CONV_WIDTH - k, :]
    c = c + cb_ref[...]
    a, bt = _gates_and_decay(c, None, wa_ref, ba_ref, wx_ref, bx_ref, lam_ref)
    hs = a * h0_ref[...] + bt

    x1 = x + _merge_out(y_pool, hs, u_gate, gpool_ref, grnn_ref, wout_ref)
    x1_ref[...] = x1
    h2 = _rms(x1, gxa_ref[...]).astype(BF16)
    q_ref[...] = _dot(h2, wq_ref[...]) * (XA_HEAD_DIM ** -0.5)

    npool_ref[:POOL_BUF - 1] = pool_ref[1:]
    npool_ref[POOL_BUF - 1] = u_pool
    nconv_ref[:CONV_WIDTH - 2] = conv_ref[1:]
    nconv_ref[CONV_WIDTH - 2] = u_rnn
    nh_ref[...] = hs


def _mix_s(x, pool, conv, h0, mixw, gxa, wq):
    bsz = x.shape[0]
    args = [x, pool, conv, h0] + list(mixw) + [gxa, wq]
    return pl.pallas_call(
        _mix_s_kernel,
        grid=(1,),
        in_specs=[_const_spec(a.shape) for a in args],
        out_specs=[_const_spec((bsz, D_MODEL)), _const_spec((bsz, D_MODEL)), _const_spec(pool.shape),
                   _const_spec(conv.shape), _const_spec((bsz, D_RNN))],
        out_shape=[jax.ShapeDtypeStruct((bsz, D_MODEL), F32), jax.ShapeDtypeStruct((bsz, D_MODEL), F32),
                   jax.ShapeDtypeStruct(pool.shape, F32), jax.ShapeDtypeStruct(conv.shape, F32),
                   jax.ShapeDtypeStruct((bsz, D_RNN), F32)],
        compiler_params=pltpu.CompilerParams(dimension_semantics=("arbitrary",), vmem_limit_bytes=VMEM_LIMIT),
        name="mix_s",
    )(*args)


ATTN_S_BB = 4


def _attn_s_kernel(q_ref, k_ref, v_ref, o_ref):
    q = q_ref[...]
    k = k_ref[...]
    v = v_ref[...]
    prod = k * q
    outs = []
    for hd in range(XA_HEADS):
        sl = slice(hd * XA_HEAD_DIM, (hd + 1) * XA_HEAD_DIM)
        s = jnp.sum(prod[:, :, sl], axis=-1, keepdims=True)
        s = s - jnp.max(s, axis=1, keepdims=True)
        p = jnp.exp(s)
        p = p / jnp.sum(p, axis=1, keepdims=True)
        outs.append(jnp.sum(p * v[:, :, sl], axis=1, keepdims=True))
    o_ref[...] = jnp.concatenate(outs, axis=-1)


def _attn_s(q, k, v):
    bsz = q.shape[0]
    q3 = q.reshape(bsz, 1, D_MODEL)
    kv_spec = pl.BlockSpec((ATTN_S_BB, N_MEM, D_MODEL), lambda i: (i, 0, 0))
    q_spec = pl.BlockSpec((ATTN_S_BB, 1, D_MODEL), lambda i: (i, 0, 0))
    o = pl.pallas_call(
        _attn_s_kernel,
        grid=(bsz // ATTN_S_BB,),
        in_specs=[q_spec, kv_spec, kv_spec],
        out_specs=q_spec,
        out_shape=jax.ShapeDtypeStruct((bsz, 1, D_MODEL), F32),
        compiler_params=pltpu.CompilerParams(dimension_semantics=("arbitrary",), vmem_limit_bytes=VMEM_LIMIT),
        name="attn_s",
    )(q3, k, v)
    return o.reshape(bsz, D_MODEL)


def _post_s_kernel(x1_ref, o_ref, wo_ref, gffn_ref, sg_ref, su_ref, sd_ref, wr_ref, rbias_ref,
                   xres_ref, hp_ref, eidx_ref, gw_ref, rank_ref, cnt_ref):
    x2 = x1_ref[...] + _dot(o_ref[...].astype(BF16), wo_ref[...])
    h3, xres = _moe_prologue(x2, gffn_ref, sg_ref, su_ref, sd_ref)
    xres_ref[...] = xres
    hp_ref[...] = _pack_bf16_pairs(h3)
    idx_rows, w_rows, rank_rows, counts = _route(h3, wr_ref, rbias_ref, jnp.zeros((N_EXPERTS, 1), F32))
    _store_rows(eidx_ref, idx_rows, I32)
    _store_rows(gw_ref, w_rows, F32)
    _store_rows(rank_ref, rank_rows, I32)
    cnt_ref[...] = jnp.broadcast_to(counts, cnt_ref.shape)


def _post_s(x1, o, wo, moew):
    bsz = x1.shape[0]
    args = [x1, o, wo] + list(moew)
    out_shape = [jax.ShapeDtypeStruct((bsz, D_MODEL), F32),
                 jax.ShapeDtypeStruct((bsz, D_MODEL // 2), U32),
                 jax.ShapeDtypeStruct((TOP_K, bsz), I32),
                 jax.ShapeDtypeStruct((TOP_K, bsz), F32),
                 jax.ShapeDtypeStruct((TOP_K, bsz), I32),
                 jax.ShapeDtypeStruct((N_EXPERTS, 128), F32)]
    return pl.pallas_call(
        _post_s_kernel,
        grid=(1,),
        in_specs=[_const_spec(a.shape) for a in args],
        out_specs=[_const_spec(s.shape) for s in out_shape],
        out_shape=out_shape,
        compiler_params=pltpu.CompilerParams(dimension_semantics=("arbitrary",), vmem_limit_bytes=VMEM_LIMIT),
        name="post_s",
    )(*args)


def _dest_kernel(eidx_ref, rank_ref, base_ref, dest_ref):
    e = eidx_ref[...]
    acc = rank_ref[...]
    for ex in range(N_EXPERTS):
        acc = acc + jnp.where(e == ex, base_ref[ex], 0)
    dest_ref[...] = acc


def _dest(eidx, rank, base):
    return pl.pallas_call(
        _dest_kernel,
        grid=(1,),
        in_specs=[_const_spec(eidx.shape), _const_spec(rank.shape),
                  pl.BlockSpec(memory_space=pltpu.SMEM)],
        out_specs=_const_spec(eidx.shape),
        out_shape=jax.ShapeDtypeStruct(eidx.shape, I32),
        name="dest",
    )(eidx, rank, base)


def _dispatch_kernel(dest_ref, hp_ref, xs_ref, sem):
    def row_copy(t, k):
        return pltpu.make_async_copy(hp_ref.at[pl.ds(t, 1)], xs_ref.at[pl.ds(dest_ref[k, t], 1)], sem)

    def issue(t, _):
        for k in range(TOP_K):
            row_copy(t, k).start()
        return _

    lax.fori_loop(0, TM_DISPATCH, issue, 0)

    def drain(t, _):
        for k in range(TOP_K):
            row_copy(t, k).wait()
        return _

    lax.fori_loop(0, TM_DISPATCH, drain, 0)


def _dispatch(dest, hp):
    t = hp.shape[0]
    return pl.pallas_call(
        _dispatch_kernel,
        grid=(t // TM_DISPATCH,),
        in_specs=[pl.BlockSpec((TOP_K, TM_DISPATCH), lambda i: (0, i), memory_space=pltpu.SMEM),
                  pl.BlockSpec((TM_DISPATCH, D_MODEL // 2), lambda i: (i, 0))],
        out_specs=pl.BlockSpec(memory_space=pl.ANY),
        out_shape=jax.ShapeDtypeStruct((t * TOP_K, D_MODEL // 2), U32),
        scratch_shapes=[pltpu.SemaphoreType.DMA(())],
        compiler_params=pltpu.CompilerParams(dimension_semantics=("arbitrary",), has_side_effects=True),
        name="dispatch",
    )(dest, hp)


def _gmm_kernel(blk_ref, exp_ref, lo_ref, hi_ref, x_ref, wg_ref, wu_ref, wd_ref, y_ref, wgu, wdn):
    w = pl.program_id(0)
    prev = jnp.maximum(w - 1, 0)
    new_expert = (w == 0) | (exp_ref[w] != exp_ref[prev])
    first_visit = (w == 0) | (blk_ref[w] != blk_ref[prev])
    lo = lo_ref[w]
    hi = hi_ref[w]

    @pl.when(new_expert)
    def _():
        wgu[:, :D_EXPERT] = wg_ref[0].astype(BF16)
        wgu[:, D_EXPERT:] = wu_ref[0].astype(BF16)
        wdn[...] = wd_ref[0].astype(BF16)

    @pl.when(hi > lo)
    def _():
        x = _unpack_bf16_pairs(x_ref[...])
        gu = _dot(x, wgu[...])
        act = jax.nn.silu(gu[:, :D_EXPERT]) * gu[:, D_EXPERT:]
        y = _dot(act.astype(BF16), wdn[...])
        row = lax.broadcasted_iota(I32, (BM, 1), 0)
        mine = (row >= lo) & (row < hi)

        @pl.when(first_visit)
        def _():
            y_ref[...] = jnp.where(mine, y, 0.0)

        @pl.when(jnp.logical_not(first_visit))
        def _():
            y_ref[...] = jnp.where(mine, y, y_ref[...])


def _gmm(meta, xs, wg, wu, wd):
    blk, exp, lo, hi = meta
    n_rows = xs.shape[0]
    n_work = blk.shape[0]
    grid_spec = pltpu.PrefetchScalarGridSpec(
        num_scalar_prefetch=4,
        grid=(n_work,),
        in_specs=[pl.BlockSpec((BM, D_MODEL // 2), lambda w, blk, exp, lo, hi: (blk[w], 0)),
                  pl.BlockSpec((1, D_MODEL, D_EXPERT), lambda w, blk, exp, lo, hi: (exp[w], 0, 0)),
                  pl.BlockSpec((1, D_MODEL, D_EXPERT), lambda w, blk, exp, lo, hi: (exp[w], 0, 0)),
                  pl.BlockSpec((1, D_EXPERT, D_MODEL), lambda w, blk, exp, lo, hi: (exp[w], 0, 0))],
        out_specs=pl.BlockSpec((BM, D_MODEL), lambda w, blk, exp, lo, hi: (blk[w], 0)),
        scratch_shapes=[pltpu.VMEM((D_MODEL, 2 * D_EXPERT), BF16), pltpu.VMEM((D_EXPERT, D_MODEL), BF16)],
    )
    return pl.pallas_call(
        _gmm_kernel,
        grid_spec=grid_spec,
        out_shape=jax.ShapeDtypeStruct((n_rows, D_MODEL), F32),
        compiler_params=pltpu.CompilerParams(dimension_semantics=("arbitrary",), vmem_limit_bytes=VMEM_LIMIT),
        name="gmm",
    )(blk, exp, lo, hi, xs, wg, wu, wd)


def _combine_kernel(dest_ref, dest_next_ref, xres_ref, gw_ref, gfin_ref, ys_ref, out_ref, buf, sem):
    i = pl.program_id(0)
    n = pl.num_programs(0)
    slot = lax.rem(i, 2)

    def row_copy(dref, s, t, k):
        return pltpu.make_async_copy(ys_ref.at[pl.ds(dref[k, t], 1)], buf.at[s, k, pl.ds(t, 1)], sem.at[s])

    def issue(dref, s):
        def body(t, _):
            for k in range(TOP_K):
                row_copy(dref, s, t, k).start()
            return _
        lax.fori_loop(0, TM_COMBINE, body, 0)

    @pl.when(i == 0)
    def _():
        issue(dest_ref, 0)

    @pl.when(i + 1 < n)
    def _():
        issue(dest_next_ref, 1 - slot)

    def drain(t, _):
        for k in range(TOP_K):
            row_copy(dest_ref, slot, t, k).wait()
        return _

    lax.fori_loop(0, TM_COMBINE, drain, 0)

    acc = xres_ref[...]
    gw = gw_ref[...]
    for k in range(TOP_K):
        acc = acc + gw[:, k:k + 1] * buf[slot, k]
    out_ref[...] = _rms(acc, gfin_ref[...])


def _combine(dest, xres, gw_t, gfin, ys):
    t = xres.shape[0]
    n = t // TM_COMBINE
    return pl.pallas_call(
        _combine_kernel,
        grid=(n,),
        in_specs=[pl.BlockSpec((TOP_K, TM_COMBINE), lambda i: (0, i), memory_space=pltpu.SMEM),
                  pl.BlockSpec((TOP_K, TM_COMBINE), lambda i: (0, jnp.minimum(i + 1, n - 1)),
                               memory_space=pltpu.SMEM),
                  pl.BlockSpec((TM_COMBINE, D_MODEL), lambda i: (i, 0)),
                  pl.BlockSpec((TM_COMBINE, TOP_K), lambda i: (i, 0)),
                  _const_spec((1, D_MODEL)),
                  pl.BlockSpec(memory_space=pl.ANY)],
        out_specs=pl.BlockSpec((TM_COMBINE, D_MODEL), lambda i: (i, 0)),
        out_shape=jax.ShapeDtypeStruct((t, D_MODEL), F32),
        scratch_shapes=[pltpu.VMEM((2, TOP_K, TM_COMBINE, D_MODEL), F32), pltpu.SemaphoreType.DMA((2,))],
        compiler_params=pltpu.CompilerParams(dimension_semantics=("arbitrary",), vmem_limit_bytes=VMEM_LIMIT),
        name="combine",
    )(dest, dest, xres, gw_t, gfin, ys)


def _work_items(counts, n_rows):
    n_blocks = n_rows // BM
    n_work = n_blocks + N_EXPERTS - 1
    ends = jnp.cumsum(counts)
    starts = ends - counts
    first = starts // BM
    last = jnp.where(counts > 0, (ends - 1) // BM, first - 1)
    tiles = last - first + 1
    tile_ends = jnp.cumsum(tiles)
    total = tile_ends[-1]
    w = jnp.arange(n_work, dtype=I32)
    e = jnp.minimum(jnp.searchsorted(tile_ends, w, side="right"), N_EXPERTS - 1).astype(I32)
    blk = first[e] + (w - (tile_ends[e] - tiles[e]))
    lo = jnp.maximum(starts[e], blk * BM) - blk * BM
    hi = jnp.minimum(ends[e], (blk + 1) * BM) - blk * BM
    live = w < total
    last_e = e[jnp.maximum(total - 1, 0)]
    blk = jnp.where(live, blk, n_blocks - 1)
    e = jnp.where(live, e, last_e)
    lo = jnp.where(live, lo, 0)
    hi = jnp.where(live, hi, 0)
    return blk.astype(I32), e.astype(I32), lo.astype(I32), hi.astype(I32)


def _block_diag_pairs(w):
    n_h, d, _ = w.shape
    half = n_h // 2
    out = jnp.zeros((2, half * d, half * d), F32)
    for hh in range(n_h):
        p, q = divmod(hh, half)
        out = out.at[p, q * d:(q + 1) * d, q * d:(q + 1) * d].set(w[hh])
    return out.astype(BF16)


def kernel(x_prompt, x_sample, state_pool, state_conv, state_h, cache_mem_k, cache_mem_v, mem_prompt, norm_mix, w_in, pool_w, pool_scale, conv_w, conv_b, gate_a_w, gate_a_b, gate_x_w, gate_x_b, lru_lambda, norm_pool_out, norm_rnn_out, w_out, norm_xattn, norm_mem, xa_wq, xa_wk, xa_wv, xa_wo, norm_ffn, router_w, router_bias, exp_w_gate, exp_w_up, exp_w_down, sh_w_gate, sh_w_up, sh_w_down, norm_final):
    bp, seq, _ = x_prompt.shape
    bs = x_sample.shape[0]
    tp = bp * seq
    row = lambda v: v.reshape(1, -1)
    bf = lambda v: v.astype(BF16)

    mixw = (row(norm_mix[0]), bf(w_in[0]), _block_diag_pairs(pool_w[0]), row(pool_scale[0]), conv_w[0],
            row(conv_b[0]), _block_diag_pairs(gate_a_w[0]), row(gate_a_b[0]), _block_diag_pairs(gate_x_w[0]),
            row(gate_x_b[0]), row(lru_lambda[0]), row(norm_pool_out[0]), row(norm_rnn_out[0]), bf(w_out[0]))
    xaw = (row(norm_xattn[0]), bf(xa_wq[0]), bf(xa_wo[0]))
    moew = (row(norm_ffn[0]), bf(sh_w_gate[0]), bf(sh_w_up[0]), bf(sh_w_down[0]), router_w[0].T,
            router_bias[0].reshape(N_EXPERTS, 1))

    mk, mv, kb, vb = _memkv(mem_prompt, row(norm_mem[0]), bf(xa_wk[0]), bf(xa_wv[0]))
    (xres_p, hp_p, eidx_p, gw_p, rank_p, cnt_p, pool_p, conv_p, h_p) = _trunk_p(x_prompt, kb, vb, mixw, xaw, moew)

    x1_s, q_s, pool_s, conv_s, h_s = _mix_s(x_sample.reshape(bs, D_MODEL), state_pool[0].transpose(1, 0, 2),
                                            state_conv[0].transpose(1, 0, 2), state_h[0], mixw, xaw[0], xaw[1])
    pool_s = pool_s.transpose(1, 0, 2)
    conv_s = conv_s.transpose(1, 0, 2)
    o_s = _attn_s(q_s, cache_mem_k[0].reshape(bs, N_MEM, D_MODEL), cache_mem_v[0].reshape(bs, N_MEM, D_MODEL))
    xres_s, hp_s, eidx_s, gw_s, rank_s, cnt_s = _post_s(x1_s, o_s, xaw[2], moew)

    cp = cnt_p[:, 0].astype(I32)
    cs = cnt_s[:, 0].astype(I32)
    counts = cp + cs
    offs = jnp.cumsum(counts) - counts
    dest_p = _dest(eidx_p, rank_p, offs)
    dest_s = _dest(eidx_s, rank_s, offs + cp)
    dest = jnp.concatenate([dest_p, dest_s], axis=1)
    hp = jnp.concatenate([hp_p, hp_s], axis=0)

    xs = _dispatch(dest, hp)
    ys = _gmm(_work_items(counts, xs.shape[0]), xs, exp_w_gate[0], exp_w_up[0], exp_w_down[0])

    gfin = row(norm_final)
    y_p = _combine(dest_p, xres_p, gw_p.T, gfin, ys)
    y_s = _combine(dest_s, xres_s, gw_s.T, gfin, ys)

    return (y_p.reshape(bp, seq, D_MODEL), y_s.reshape(bs, 1, D_MODEL),
            pool_p[None], conv_p[None], h_p.reshape(1, bp, D_RNN),
            mk.reshape(1, bp, N_MEM, XA_HEADS, XA_HEAD_DIM), mv.reshape(1, bp, N_MEM, XA_HEADS, XA_HEAD_DIM),
            pool_s[None], conv_s[None], h_s[None])
```

```python
import functools

import jax
import jax.numpy as jnp
from jax import lax
from jax.experimental import pallas as pl
from jax.experimental.pallas import tpu as pltpu

F32 = jnp.float32
BF16 = jnp.bfloat16
I32 = jnp.int32
U32 = jnp.uint32

D_MODEL = 1024
D_POOL = 512
D_RNN = 512
D_IN = D_POOL + 2 * D_RNN
POOL_WINDOWS = (2, 4, 8, 16)
POOL_GROUP = 128
POOL_BUF = 15
CONV_WIDTH = 4
LRU_C = 8.0
N_MEM = 256
XA_HEADS = 4
XA_HEAD_DIM = 256
N_EXPERTS = 64
TOP_K = 8
N_EXPERT_GROUPS = 8
GROUP_SIZE = N_EXPERTS // N_EXPERT_GROUPS
TOPK_GROUPS = 4
D_EXPERT = 256
ROUTED_SCALE = 2.5
EPS = 1e-6
PAST_LEN = 16384

HALO = 16
CONV_HALO = 8
TS = 256
BM = 256
TM_DISPATCH = 128
TM_COMBINE = 128
VMEM_LIMIT = 56 * 1024 * 1024


def _const_spec(shape):
    nd = len(shape)
    return pl.BlockSpec(shape, lambda *_: (0,) * nd, pipeline_mode=pl.Buffered(1))


def _rms(x, g):
    ms = jnp.mean(x * x, axis=-1, keepdims=True)
    return x * lax.rsqrt(ms + EPS) * g


def _dot(a, b):
    return jnp.dot(a, b, preferred_element_type=F32)


def _dot_nt(a, b, precision=None):
    return lax.dot_general(a, b, (((1,), (1,)), ((), ())), precision=precision,
                           preferred_element_type=F32)


def _softplus(x):
    return jnp.maximum(x, 0.0) + jnp.log1p(jnp.exp(-jnp.abs(x)))


def _pack_bf16_pairs(h):
    c = h.shape[1] // 2
    hb = h.astype(BF16).astype(F32)
    lo = lax.bitcast_convert_type(hb[:, :c], U32)
    hi = lax.bitcast_convert_type(hb[:, c:], U32)
    return (lo >> 16) | (hi & jnp.uint32(0xFFFF0000))


def _unpack_bf16_pairs(p):
    lo = lax.bitcast_convert_type(p << 16, F32)
    hi = lax.bitcast_convert_type(p & jnp.uint32(0xFFFF0000), F32)
    return jnp.concatenate([lo, hi], axis=1).astype(BF16)


def _gates_and_decay(c, pos_is_zero, wa_ref, ba_ref, wx_ref, bx_ref, lam_ref):
    cb = c.astype(BF16)
    half = D_RNN // 2
    ga = jnp.concatenate([_dot(cb[:, :half], wa_ref[0]), _dot(cb[:, half:], wa_ref[1])], axis=1) + ba_ref[...]
    gx = jnp.concatenate([_dot(cb[:, :half], wx_ref[0]), _dot(cb[:, half:], wx_ref[1])], axis=1) + bx_ref[...]
    r = jax.nn.sigmoid(ga)
    i = jax.nn.sigmoid(gx)
    log_a = (-LRU_C) * r * _softplus(-lam_ref[...])
    a = jnp.exp(log_a)
    mult = jnp.sqrt(1.0 - a * a)
    if pos_is_zero is not None:
        mult = jnp.where(pos_is_zero, 1.0, mult)
    return a, mult * i * c


def _pool_project(mean, u_pool, pw_ref, pscale_ref):
    d = (mean - u_pool).astype(BF16)
    half = D_POOL // 2
    y = jnp.concatenate([_dot(d[:, :half], pw_ref[0]), _dot(d[:, half:], pw_ref[1])], axis=1)
    return y * pscale_ref[...]


def _merge_out(y_pool, hs, u_gate, gpool_ref, grnn_ref, wout_ref):
    y_rnn = hs * jax.nn.gelu(u_gate)
    merged = jnp.concatenate([_rms(y_pool, gpool_ref[...]), _rms(y_rnn, grnn_ref[...])], axis=1)
    return _dot(merged.astype(BF16), wout_ref[...])


def _route(h3, wr_ref, rbias_ref, carry):
    r_tok = h3.shape[0]
    logits = _dot_nt(wr_ref[...], h3, precision=lax.Precision.HIGHEST)
    scores = jax.nn.sigmoid(logits)
    biased = scores + rbias_ref[...]
    neg = jnp.float32(-jnp.inf)
    gs = []
    for g in range(N_EXPERT_GROUPS):
        xg = biased[g * GROUP_SIZE:(g + 1) * GROUP_SIZE]
        m1 = jnp.max(xg, axis=0, keepdims=True)
        eq = xg == m1
        cnt = jnp.sum(eq.astype(F32), axis=0, keepdims=True)
        m2 = jnp.max(jnp.where(eq, neg, xg), axis=0, keepdims=True)
        gs.append(m1 + jnp.where(cnt >= 2.0, m1, m2))
    pieces = []
    for g in range(N_EXPERT_GROUPS):
        beaten = jnp.zeros_like(gs[g])
        for o in range(N_EXPERT_GROUPS):
            if o == g:
                continue
            wins = (gs[o] > gs[g]) | (gs[o] == gs[g]) if o < g else (gs[o] > gs[g])
            beaten = beaten + wins.astype(F32)
        keep = beaten < float(TOPK_GROUPS)
        xg = biased[g * GROUP_SIZE:(g + 1) * GROUP_SIZE]
        pieces.append(jnp.where(keep, xg, neg))
    cur = jnp.concatenate(pieces, axis=0)
    eid = lax.broadcasted_iota(I32, (N_EXPERTS, r_tok), 0).astype(F32)
    idx_rows, score_rows = [], []
    sel = jnp.zeros((N_EXPERTS, r_tok), F32)
    for _ in range(TOP_K):
        m = jnp.max(cur, axis=0, keepdims=True)
        idx = jnp.min(jnp.where(cur == m, eid, float(N_EXPERTS)), axis=0, keepdims=True)
        oh = eid == idx
        score_rows.append(jnp.sum(jnp.where(oh, scores, 0.0), axis=0, keepdims=True))
        idx_rows.append(idx)
        sel = sel + oh.astype(F32)
        cur = jnp.where(oh, neg, cur)
    tot = score_rows[0]
    for s in score_rows[1:]:
        tot = tot + s
    w_rows = [s / tot * ROUTED_SCALE for s in score_rows]
    rr = lax.broadcasted_iota(I32, (r_tok, r_tok), 0)
    cc = lax.broadcasted_iota(I32, (r_tok, r_tok), 1)
    tri = (rr < cc).astype(BF16)
    cum = _dot(sel.astype(BF16), tri) + carry
    rank_rows = [jnp.sum(jnp.where(eid == idx, cum, 0.0), axis=0, keepdims=True) for idx in idx_rows]
    counts = jnp.sum(sel, axis=1, keepdims=True)
    return idx_rows, w_rows, rank_rows, counts


def _moe_prologue(x2, gffn_ref, sg_ref, su_ref, sd_ref):
    h3 = _rms(x2, gffn_ref[...])
    h3b = h3.astype(BF16)
    act = jax.nn.silu(_dot(h3b, sg_ref[...])) * _dot(h3b, su_ref[...])
    shared = _dot(act.astype(BF16), sd_ref[...])
    return h3, x2 + shared


def _store_rows(ref, rows, dtype):
    for k, row in enumerate(rows):
        ref[k:k + 1, :] = row.astype(dtype)


def _memkv_kernel(mem_ref, g_ref, wk_ref, wv_ref, k_ref, v_ref, kb_ref, vb_ref):
    m = _rms(mem_ref[0], g_ref[...]).astype(BF16)
    k = _dot(m, wk_ref[...])
    v = _dot(m, wv_ref[...])
    k_ref[0] = k
    v_ref[0] = v
    kb_ref[0] = k.astype(BF16)
    vb_ref[0] = v.astype(BF16)


def _memkv(mem, g, wk, wv):
    b = mem.shape[0]
    blk = pl.BlockSpec((1, N_MEM, D_MODEL), lambda i: (i, 0, 0))
    return pl.pallas_call(
        _memkv_kernel,
        grid=(b,),
        in_specs=[blk, _const_spec((1, D_MODEL)), _const_spec((D_MODEL, D_MODEL)), _const_spec((D_MODEL, D_MODEL))],
        out_specs=[blk, blk, blk, blk],
        out_shape=[jax.ShapeDtypeStruct((b, N_MEM, D_MODEL), F32)] * 2
        + [jax.ShapeDtypeStruct((b, N_MEM, D_MODEL), BF16)] * 2,
        compiler_params=pltpu.CompilerParams(dimension_semantics=("arbitrary",), vmem_limit_bytes=VMEM_LIMIT),
        name="memkv",
    )(mem, g, wk, wv)


def _trunk_p_kernel(x_ref, kb_ref, vb_ref,
                    gmix_ref, win_ref, pw_ref, pscale_ref, cw_ref, cb_ref, wa_ref, ba_ref, wx_ref, bx_ref,
                    lam_ref, gpool_ref, grnn_ref, wout_ref,
                    gxa_ref, wq_ref, wo_ref,
                    gffn_ref, sg_ref, su_ref, sd_ref, wr_ref, rbias_ref,
                    xres_ref, hp_ref, eidx_ref, gw_ref, rank_ref, cnt_ref, pool_ref, conv_ref, hT_ref,
                    pool_prev, conv_prev, h_prev, carry):
    b = pl.program_id(0)
    j = pl.program_id(1)
    n_j = pl.num_programs(1)

    @pl.when(j == 0)
    def _():
        pool_prev[...] = jnp.zeros_like(pool_prev)
        conv_prev[...] = jnp.zeros_like(conv_prev)
        h_prev[...] = jnp.zeros_like(h_prev)

    @pl.when((b == 0) & (j == 0))
    def _():
        carry[...] = jnp.zeros_like(carry)

    x = x_ref[0]
    row = lax.broadcasted_iota(I32, (TS, 1), 0)
    pos = j * TS + row

    h = _rms(x, gmix_ref[...]).astype(BF16)
    z = _dot(h, win_ref[...])
    u_pool = z[:, :D_POOL]
    u_rnn = z[:, D_POOL:D_POOL + D_RNN]
    u_gate = z[:, D_POOL + D_RNN:]

    ext = jnp.concatenate([pool_prev[...], u_pool], axis=0)
    means = []
    for g, w in enumerate(POOL_WINDOWS):
        s = ext[:, g * POOL_GROUP:(g + 1) * POOL_GROUP]
        k = 1
        while k < w:
            s = s + pltpu.roll(s, k, 0)
            k *= 2
        inv = 1.0 / jnp.minimum(pos + 1, w).astype(F32)
        means.append(s[HALO:] * inv)
    mean = jnp.concatenate(means, axis=1)
    y_pool = _pool_project(mean, u_pool, pw_ref, pscale_ref)

    extc = jnp.concatenate([conv_prev[...], u_rnn], axis=0)
    c = u_rnn * cw_ref[CONV_WIDTH - 1:CONV_WIDTH, :]
    for k in range(1, CONV_WIDTH):
        c = c + pltpu.roll(extc, k, 0)[CONV_HALO:] * cw_ref[CONV_WIDTH - 1 - k:CONV_WIDTH - k, :]
    c = c + cb_ref[...]

    a, bt = _gates_and_decay(c, pos == 0, wa_ref, ba_ref, wx_ref, bx_ref, lam_ref)
    k = 1
    while k < TS:
        valid = row >= k
        a_sh = jnp.where(valid, pltpu.roll(a, k, 0), 1.0)
        b_sh = jnp.where(valid, pltpu.roll(bt, k, 0), 0.0)
        bt = bt + a * b_sh
        a = a * a_sh
        k *= 2
    hs = bt + a * h_prev[...]

    pool_prev[...] = u_pool[TS - HALO:]
    conv_prev[...] = u_rnn[TS - CONV_HALO:]
    h_prev[...] = hs[TS - 1:]

    @pl.when(j == n_j - 1)
    def _():
        pool_ref[0] = u_pool[TS - POOL_BUF:]
        conv_ref[0] = u_rnn[TS - (CONV_WIDTH - 1):]
        hT_ref[0] = hs[TS - 1:]

    x1 = x + _merge_out(y_pool, hs, u_gate, gpool_ref, grnn_ref, wout_ref)

    h2 = _rms(x1, gxa_ref[...]).astype(BF16)
    q = (_dot(h2, wq_ref[...]) * (XA_HEAD_DIM ** -0.5)).astype(BF16)
    outs = []
    for hd in range(XA_HEADS):
        sl = slice(hd * XA_HEAD_DIM, (hd + 1) * XA_HEAD_DIM)
        s = _dot_nt(q[:, sl], kb_ref[0, :, sl])
        s = s - jnp.max(s, axis=-1, keepdims=True)
        p = jnp.exp(s)
        p = p / jnp.sum(p, axis=-1, keepdims=True)
        outs.append(_dot(p.astype(BF16), vb_ref[0, :, sl]))
    o = jnp.concatenate(outs, axis=1).astype(BF16)
    x2 = x1 + _dot(o, wo_ref[...])

    h3, xres = _moe_prologue(x2, gffn_ref, sg_ref, su_ref, sd_ref)
    xres_ref[...] = xres
    hp_ref[...] = _pack_bf16_pairs(h3)
    idx_rows, w_rows, rank_rows, counts = _route(h3, wr_ref, rbias_ref, carry[...])
    _store_rows(eidx_ref, idx_rows, I32)
    _store_rows(gw_ref, w_rows, F32)
    _store_rows(rank_ref, rank_rows, I32)
    carry[...] = carry[...] + counts
    cnt_ref[...] = jnp.broadcast_to(carry[...], cnt_ref.shape)


def _trunk_p(x, kb, vb, mixw, xaw, moew):
    bsz, seq, _ = x.shape
    n_j = seq // TS
    t = bsz * seq
    tok = lambda b, j: (b * n_j + j, 0)
    lane_tok = lambda b, j: (0, b * n_j + j)
    per_b = lambda b, j: (b, 0, 0)
    weights = list(mixw) + list(xaw) + list(moew)
    in_specs = [pl.BlockSpec((1, TS, D_MODEL), lambda b, j: (b, j, 0)),
                pl.BlockSpec((1, N_MEM, D_MODEL), per_b),
                pl.BlockSpec((1, N_MEM, D_MODEL), per_b)] + [_const_spec(w.shape) for w in weights]
    out_shape = [jax.ShapeDtypeStruct((t, D_MODEL), F32),
                 jax.ShapeDtypeStruct((t, D_MODEL // 2), U32),
                 jax.ShapeDtypeStruct((TOP_K, t), I32),
                 jax.ShapeDtypeStruct((TOP_K, t), F32),
                 jax.ShapeDtypeStruct((TOP_K, t), I32),
                 jax.ShapeDtypeStruct((N_EXPERTS, 128), F32),
                 jax.ShapeDtypeStruct((bsz, POOL_BUF, D_POOL), F32),
                 jax.ShapeDtypeStruct((bsz, CONV_WIDTH - 1, D_RNN), F32),
                 jax.ShapeDtypeStruct((bsz, 1, D_RNN), F32)]
    out_specs = [pl.BlockSpec((TS, D_MODEL), tok),
                 pl.BlockSpec((TS, D_MODEL // 2), tok),
                 pl.BlockSpec((TOP_K, TS), lane_tok),
                 pl.BlockSpec((TOP_K, TS), lane_tok),
                 pl.BlockSpec((TOP_K, TS), lane_tok),
                 pl.BlockSpec((N_EXPERTS, 128), lambda b, j: (0, 0)),
                 pl.BlockSpec((1, POOL_BUF, D_POOL), per_b),
                 pl.BlockSpec((1, CONV_WIDTH - 1, D_RNN), per_b),
                 pl.BlockSpec((1, 1, D_RNN), per_b)]
    return pl.pallas_call(
        _trunk_p_kernel,
        grid=(bsz, n_j),
        in_specs=in_specs,
        out_specs=out_specs,
        out_shape=out_shape,
        scratch_shapes=[pltpu.VMEM((HALO, D_POOL), F32), pltpu.VMEM((CONV_HALO, D_RNN), F32),
                        pltpu.VMEM((1, D_RNN), F32), pltpu.VMEM((N_EXPERTS, 1), F32)],
        compiler_params=pltpu.CompilerParams(dimension_semantics=("arbitrary", "arbitrary"),
                                             vmem_limit_bytes=VMEM_LIMIT),
        name="trunk_p",
    )(x, kb, vb, *weights)


def _mix_s_kernel(x_ref, pool_ref, conv_ref, h0_ref,
                  gmix_ref, win_ref, pw_ref, pscale_ref, cw_ref, cb_ref, wa_ref, ba_ref, wx_ref, bx_ref,
                  lam_ref, gpool_ref, grnn_ref, wout_ref, gxa_ref, wq_ref,
                  x1_ref, q_ref, npool_ref, nconv_ref, nh_ref):
    x = x_ref[...]
    h = _rms(x, gmix_ref[...]).astype(BF16)
    z = _dot(h, win_ref[...])
    u_pool = z[:, :D_POOL]
    u_rnn = z[:, D_POOL:D_POOL + D_RNN]
    u_gate = z[:, D_POOL + D_RNN:]

    means = []
    for g, w in enumerate(POOL_WINDOWS):
        sl = slice(g * POOL_GROUP, (g + 1) * POOL_GROUP)
        s = u_pool[:, sl]
        for k in range(1, w):
            s = s + pool_ref[POOL_BUF - k, :, sl]
        means.append(s * (1.0 / min(w, PAST_LEN + 1)))
    mean = jnp.concatenate(means, axis=1)
    y_pool = _pool_project(mean, u_pool, pw_ref, pscale_ref)

    c = u_rnn * cw_ref[CONV_WIDTH - 1:CONV_WIDTH, :]
    for k in range(1, CONV_WIDTH):
        c = c + conv_ref[CONV_WIDTH - 1 - k] * cw_ref[CONV_WIDTH - 1 - k:CONV_WIDTH - k, :]
    c = c + cb_ref[...]
    a, bt = _gates_and_decay(c, None, wa_ref, ba_ref, wx_ref, bx_ref, lam_ref)
    hs = a * h0_ref[...] + bt

    x1 = x + _merge_out(y_pool, hs, u_gate, gpool_ref, grnn_ref, wout_ref)
    x1_ref[...] = x1
    h2 = _rms(x1, gxa_ref[...]).astype(BF16)
    q_ref[...] = _dot(h2, wq_ref[...]) * (XA_HEAD_DIM ** -0.5)

    npool_ref[:POOL_BUF - 1] = pool_ref[1:]
    npool_ref[POOL_BUF - 1] = u_pool
    nconv_ref[:CONV_WIDTH - 2] = conv_ref[1:]
    nconv_ref[CONV_WIDTH - 2] = u_rnn
    nh_ref[...] = hs


def _mix_s(x, pool, conv, h0, mixw, gxa, wq):
    bsz = x.shape[0]
    args = [x, pool, conv, h0] + list(mixw) + [gxa, wq]
    return pl.pallas_call(
        _mix_s_kernel,
        grid=(1,),
        in_specs=[_const_spec(a.shape) for a in args],
        out_specs=[_const_spec((bsz, D_MODEL)), _const_spec((bsz, D_MODEL)), _const_spec(pool.shape),
                   _const_spec(conv.shape), _const_spec((bsz, D_RNN))],
        out_shape=[jax.ShapeDtypeStruct((bsz, D_MODEL), F32), jax.ShapeDtypeStruct((bsz, D_MODEL), F32),
                   jax.ShapeDtypeStruct(pool.shape, F32), jax.ShapeDtypeStruct(conv.shape, F32),
                   jax.ShapeDtypeStruct((bsz, D_RNN), F32)],
        compiler_params=pltpu.CompilerParams(dimension_semantics=("arbitrary",), vmem_limit_bytes=VMEM_LIMIT),
        name="mix_s",
    )(*args)


ATTN_S_BB = 4


def _attn_s_kernel(q_ref, k_ref, v_ref, o_ref):
    q = q_ref[...][:, None]
    s = jnp.sum(k_ref[...] * q, axis=-1, keepdims=True)
    s = s - jnp.max(s, axis=1, keepdims=True)
    p = jnp.exp(s)
    p = p / jnp.sum(p, axis=1, keepdims=True)
    o_ref[...] = jnp.sum(p * v_ref[...], axis=1)


def _attn_s(q, k, v):
    bsz = q.shape[0]
    kv_spec = pl.BlockSpec((ATTN_S_BB, N_MEM, XA_HEADS, XA_HEAD_DIM), lambda i: (i, 0, 0, 0))
    q_spec = pl.BlockSpec((ATTN_S_BB, XA_HEADS, XA_HEAD_DIM), lambda i: (i, 0, 0))
    o = pl.pallas_call(
        _attn_s_kernel,
        grid=(bsz // ATTN_S_BB,),
        in_specs=[q_spec, kv_spec, kv_spec],
        out_specs=q_spec,
        out_shape=jax.ShapeDtypeStruct((bsz, XA_HEADS, XA_HEAD_DIM), F32),
        compiler_params=pltpu.CompilerParams(dimension_semantics=("arbitrary",), vmem_limit_bytes=VMEM_LIMIT),
        name="attn_s",
    )(q.reshape(bsz, XA_HEADS, XA_HEAD_DIM), k, v)
    return o.reshape(bsz, D_MODEL)


def _post_s_kernel(x1_ref, o_ref, wo_ref, gffn_ref, sg_ref, su_ref, sd_ref, wr_ref, rbias_ref,
                   xres_ref, hp_ref, eidx_ref, gw_ref, rank_ref, cnt_ref):
    x2 = x1_ref[...] + _dot(o_ref[...].astype(BF16), wo_ref[...])
    h3, xres = _moe_prologue(x2, gffn_ref, sg_ref, su_ref, sd_ref)
    xres_ref[...] = xres
    hp_ref[...] = _pack_bf16_pairs(h3)
    idx_rows, w_rows, rank_rows, counts = _route(h3, wr_ref, rbias_ref, jnp.zeros((N_EXPERTS, 1), F32))
    _store_rows(eidx_ref, idx_rows, I32)
    _store_rows(gw_ref, w_rows, F32)
    _store_rows(rank_ref, rank_rows, I32)
    cnt_ref[...] = jnp.broadcast_to(counts, cnt_ref.shape)


def _post_s(x1, o, wo, moew):
    bsz = x1.shape[0]
    args = [x1, o, wo] + list(moew)
    out_shape = [jax.ShapeDtypeStruct((bsz, D_MODEL), F32),
                 jax.ShapeDtypeStruct((bsz, D_MODEL // 2), U32),
                 jax.ShapeDtypeStruct((TOP_K, bsz), I32),
                 jax.ShapeDtypeStruct((TOP_K, bsz), F32),
                 jax.ShapeDtypeStruct((TOP_K, bsz), I32),
                 jax.ShapeDtypeStruct((N_EXPERTS, 128), F32)]
    return pl.pallas_call(
        _post_s_kernel,
        grid=(1,),
        in_specs=[_const_spec(a.shape) for a in args],
        out_specs=[_const_spec(s.shape) for s in out_shape],
        out_shape=out_shape,
        compiler_params=pltpu.CompilerParams(dimension_semantics=("arbitrary",), vmem_limit_bytes=VMEM_LIMIT),
        name="post_s",
    )(*args)


def _plan_kernel(cp_ref, cs_ref, blk_ref, exp_ref, lo_ref, hi_ref, base_p_ref, base_s_ref):
    n_work = blk_ref.shape[0]
    shift = BM.bit_length() - 1

    def per_expert(e, carry):
        start, w, last_e = carry
        cnt = cp_ref[e] + cs_ref[e]
        base_p_ref[e] = start
        base_s_ref[e] = start + cp_ref[e]
        end = start + cnt
        first = lax.shift_right_logical(start, shift)
        n_tiles = jnp.where(cnt > 0, lax.shift_right_logical(end - 1, shift) - first + 1, 0)

        def per_tile(i, w):
            b = first + i
            blk_ref[w] = b
            exp_ref[w] = e
            lo_ref[w] = jnp.maximum(start, b * BM) - b * BM
            hi_ref[w] = jnp.minimum(end, (b + 1) * BM) - b * BM
            return w + 1

        w = lax.fori_loop(0, n_tiles, per_tile, w)
        return end, w, jnp.where(cnt > 0, e, last_e)

    total, w, last_e = lax.fori_loop(0, N_EXPERTS, per_expert, (jnp.int32(0), jnp.int32(0), jnp.int32(0)))

    def fill(i, _):
        blk_ref[i] = lax.shift_right_logical(total - 1, shift)
        exp_ref[i] = last_e
        lo_ref[i] = 0
        hi_ref[i] = 0
        return _

    lax.fori_loop(w, n_work, fill, 0)


def _plan(cp, cs, n_rows):
    assert BM & (BM - 1) == 0 and n_rows % BM == 0
    n_work = n_rows // BM + N_EXPERTS - 1
    smem = pl.BlockSpec(memory_space=pltpu.SMEM)
    return pl.pallas_call(
        _plan_kernel,
        in_specs=[smem, smem],
        out_specs=[smem] * 6,
        out_shape=[jax.ShapeDtypeStruct((n_work,), I32)] * 4 + [jax.ShapeDtypeStruct((N_EXPERTS,), I32)] * 2,
        name="plan",
    )(cp, cs)


def _dest_kernel(eidx_ref, rank_ref, base_ref, dest_ref):
    e = eidx_ref[...]
    acc = rank_ref[...]
    for ex in range(N_EXPERTS):
        acc = acc + jnp.where(e == ex, base_ref[ex], 0)
    dest_ref[...] = acc


def _dest(eidx, rank, base):
    return pl.pallas_call(
        _dest_kernel,
        grid=(1,),
        in_specs=[_const_spec(eidx.shape), _const_spec(rank.shape),
                  pl.BlockSpec(memory_space=pltpu.SMEM)],
        out_specs=_const_spec(eidx.shape),
        out_shape=jax.ShapeDtypeStruct(eidx.shape, I32),
        name="dest",
    )(eidx, rank, base)


def _dispatch_kernel(dest_ref, hp_ref, xs_ref, sem):
    def row_copy(t, k):
        return pltpu.make_async_copy(hp_ref.at[pl.ds(t, 1)], xs_ref.at[pl.ds(dest_ref[k, t], 1)], sem)

    def issue(t, _):
        for k in range(TOP_K):
            row_copy(t, k).start()
        return _

    lax.fori_loop(0, TM_DISPATCH, issue, 0)

    def drain(t, _):
        for k in range(TOP_K):
            row_copy(t, k).wait()
        return _

    lax.fori_loop(0, TM_DISPATCH, drain, 0)


def _dispatch(dest, hp):
    t = hp.shape[0]
    return pl.pallas_call(
        _dispatch_kernel,
        grid=(t // TM_DISPATCH,),
        in_specs=[pl.BlockSpec((TOP_K, TM_DISPATCH), lambda i: (0, i), memory_space=pltpu.SMEM),
                  pl.BlockSpec((TM_DISPATCH, D_MODEL // 2), lambda i: (i, 0))],
        out_specs=pl.BlockSpec(memory_space=pl.ANY),
        out_shape=jax.ShapeDtypeStruct((t * TOP_K, D_MODEL // 2), U32),
        scratch_shapes=[pltpu.SemaphoreType.DMA(())],
        compiler_params=pltpu.CompilerParams(dimension_semantics=("arbitrary",), has_side_effects=True),
        name="dispatch",
    )(dest, hp)


def _gmm_kernel(blk_ref, exp_ref, lo_ref, hi_ref, x_ref, wg_ref, wu_ref, wd_ref, y_ref, wgu, wdn):
    w = pl.program_id(0)
    prev = jnp.maximum(w - 1, 0)
    new_expert = (w == 0) | (exp_ref[w] != exp_ref[prev])
    first_visit = (w == 0) | (blk_ref[w] != blk_ref[prev])
    lo = lo_ref[w]
    hi = hi_ref[w]

    @pl.when(new_expert)
    def _():
        wgu[:, :D_EXPERT] = wg_ref[0].astype(BF16)
        wgu[:, D_EXPERT:] = wu_ref[0].astype(BF16)
        wdn[...] = wd_ref[0].astype(BF16)

    @pl.when(hi > lo)
    def _():
        x = _unpack_bf16_pairs(x_ref[...])
        gu = _dot(x, wgu[...])
        act = jax.nn.silu(gu[:, :D_EXPERT]) * gu[:, D_EXPERT:]
        y = _dot(act.astype(BF16), wdn[...])
        row = lax.broadcasted_iota(I32, (BM, 1), 0)
        mine = (row >= lo) & (row < hi)

        @pl.when(first_visit)
        def _():
            y_ref[...] = jnp.where(mine, y, 0.0)

        @pl.when(jnp.logical_not(first_visit))
        def _():
            y_ref[...] = jnp.where(mine, y, y_ref[...])


def _gmm(meta, xs, wg, wu, wd):
    blk, exp, lo, hi = meta
    n_rows = xs.shape[0]
    n_work = blk.shape[0]
    grid_spec = pltpu.PrefetchScalarGridSpec(
        num_scalar_prefetch=4,
        grid=(n_work,),
        in_specs=[pl.BlockSpec((BM, D_MODEL // 2), lambda w, blk, exp, lo, hi: (blk[w], 0)),
                  pl.BlockSpec((1, D_MODEL, D_EXPERT), lambda w, blk, exp, lo, hi: (exp[w], 0, 0)),
                  pl.BlockSpec((1, D_MODEL, D_EXPERT), lambda w, blk, exp, lo, hi: (exp[w], 0, 0)),
                  pl.BlockSpec((1, D_EXPERT, D_MODEL), lambda w, blk, exp, lo, hi: (exp[w], 0, 0))],
        out_specs=pl.BlockSpec((BM, D_MODEL), lambda w, blk, exp, lo, hi: (blk[w], 0)),
        scratch_shapes=[pltpu.VMEM((D_MODEL, 2 * D_EXPERT), BF16), pltpu.VMEM((D_EXPERT, D_MODEL), BF16)],
    )
    return pl.pallas_call(
        _gmm_kernel,
        grid_spec=grid_spec,
        out_shape=jax.ShapeDtypeStruct((n_rows, D_MODEL), F32),
        compiler_params=pltpu.CompilerParams(dimension_semantics=("arbitrary",), vmem_limit_bytes=VMEM_LIMIT),
        name="gmm",
    )(blk, exp, lo, hi, xs, wg, wu, wd)


def _combine_kernel(dest_ref, dest_next_ref, xres_ref, gw_ref, gfin_ref, ys_ref, out_ref, buf, sem):
    i = pl.program_id(0)
    n = pl.num_programs(0)
    slot = lax.rem(i, 2)

    def row_copy(dref, s, t, k):
        return pltpu.make_async_copy(ys_ref.at[pl.ds(dref[k, t], 1)], buf.at[s, k, pl.ds(t, 1)], sem.at[s])

    def issue(dref, s):
        def body(t, _):
            for k in range(TOP_K):
                row_copy(dref, s, t, k).start()
            return _
        lax.fori_loop(0, TM_COMBINE, body, 0)

    @pl.when(i == 0)
    def _():
        issue(dest_ref, 0)

    @pl.when(i + 1 < n)
    def _():
        issue(dest_next_ref, 1 - slot)

    def drain(t, _):
        for k in range(TOP_K):
            row_copy(dest_ref, slot, t, k).wait()
        return _

    lax.fori_loop(0, TM_COMBINE, drain, 0)

    acc = xres_ref[...]
    gw = gw_ref[...]
    for k in range(TOP_K):
        acc = acc + gw[:, k:k + 1] * buf[slot, k]
    out_ref[...] = _rms(acc, gfin_ref[...])


def _combine(dest, xres, gw_t, gfin, ys):
    t = xres.shape[0]
    n = t // TM_COMBINE
    return pl.pallas_call(
        _combine_kernel,
        grid=(n,),
        in_specs=[pl.BlockSpec((TOP_K, TM_COMBINE), lambda i: (0, i), memory_space=pltpu.SMEM),
                  pl.BlockSpec((TOP_K, TM_COMBINE), lambda i: (0, jnp.minimum(i + 1, n - 1)),
                               memory_space=pltpu.SMEM),
                  pl.BlockSpec((TM_COMBINE, D_MODEL), lambda i: (i, 0)),
                  pl.BlockSpec((TM_COMBINE, TOP_K), lambda i: (i, 0)),
                  _const_spec((1, D_MODEL)),
                  pl.BlockSpec(memory_space=pl.ANY)],
        out_specs=pl.BlockSpec((TM_COMBINE, D_MODEL), lambda i: (i, 0)),
        out_shape=jax.ShapeDtypeStruct((t, D_MODEL), F32),
        scratch_shapes=[pltpu.VMEM((2, TOP_K, TM_COMBINE, D_MODEL), F32), pltpu.SemaphoreType.DMA((2,))],
        compiler_params=pltpu.CompilerParams(dimension_semantics=("arbitrary",), vmem_limit_bytes=VMEM_LIMIT),
        name="combine",
    )(dest, dest, xres, gw_t, gfin, ys)


def _block_diag_pairs(w):
    n_h, d, _ = w.shape
    half = n_h // 2
    out = jnp.zeros((2, half * d, half * d), F32)
    for hh in range(n_h):
        p, q = divmod(hh, half)
        out = out.at[p, q * d:(q + 1) * d, q * d:(q + 1) * d].set(w[hh])
    return out.astype(BF16)


def kernel(x_prompt, x_sample, state_pool, state_conv, state_h, cache_mem_k, cache_mem_v, mem_prompt, norm_mix, w_in, pool_w, pool_scale, conv_w, conv_b, gate_a_w, gate_a_b, gate_x_w, gate_x_b, lru_lambda, norm_pool_out, norm_rnn_out, w_out, norm_xattn, norm_mem, xa_wq, xa_wk, xa_wv, xa_wo, norm_ffn, router_w, router_bias, exp_w_gate, exp_w_up, exp_w_down, sh_w_gate, sh_w_up, sh_w_down, norm_final):
    bp, seq, _ = x_prompt.shape
    bs = x_sample.shape[0]
    tp = bp * seq
    row = lambda v: v.reshape(1, -1)
    bf = lambda v: v.astype(BF16)

    mixw = (row(norm_mix[0]), bf(w_in[0]), _block_diag_pairs(pool_w[0]), row(pool_scale[0]), conv_w[0],
            row(conv_b[0]), _block_diag_pairs(gate_a_w[0]), row(gate_a_b[0]), _block_diag_pairs(gate_x_w[0]),
            row(gate_x_b[0]), row(lru_lambda[0]), row(norm_pool_out[0]), row(norm_rnn_out[0]), bf(w_out[0]))
    xaw = (row(norm_xattn[0]), bf(xa_wq[0]), bf(xa_wo[0]))
    moew = (row(norm_ffn[0]), bf(sh_w_gate[0]), bf(sh_w_up[0]), bf(sh_w_down[0]), router_w[0].T,
            router_bias[0].reshape(N_EXPERTS, 1))

    mk, mv, kb, vb = _memkv(mem_prompt, row(norm_mem[0]), bf(xa_wk[0]), bf(xa_wv[0]))
    (xres_p, hp_p, eidx_p, gw_p, rank_p, cnt_p, pool_p, conv_p, h_p) = _trunk_p(x_prompt, kb, vb, mixw, xaw, moew)

    x1_s, q_s, pool_s, conv_s, h_s = _mix_s(x_sample.reshape(bs, D_MODEL), state_pool[0].transpose(1, 0, 2),
                                            state_conv[0].transpose(1, 0, 2), state_h[0], mixw, xaw[0], xaw[1])
    pool_s = pool_s.transpose(1, 0, 2)
    conv_s = conv_s.transpose(1, 0, 2)
    o_s = _attn_s(q_s, cache_mem_k[0], cache_mem_v[0])
    xres_s, hp_s, eidx_s, gw_s, rank_s, cnt_s = _post_s(x1_s, o_s, xaw[2], moew)

    n_rows = (tp + bs) * TOP_K
    blk, exp, lo, hi, base_p, base_s = _plan(cnt_p[:, 0].astype(I32), cnt_s[:, 0].astype(I32), n_rows)
    dest_p = _dest(eidx_p, rank_p, base_p)
    dest_s = _dest(eidx_s, rank_s, base_s)
    dest = jnp.concatenate([dest_p, dest_s], axis=1)
    hp = jnp.concatenate([hp_p, hp_s], axis=0)

    xs = _dispatch(dest, hp)
    ys = _gmm((blk, exp, lo, hi), xs, exp_w_gate[0], exp_w_up[0], exp_w_down[0])

    gfin = row(norm_final)
    y_p = _combine(dest_p, xres_p, gw_p.T, gfin, ys)
    y_s = _combine(dest_s, xres_s, gw_s.T, gfin, ys)

    return (y_p.reshape(bp, seq, D_MODEL), y_s.reshape(bs, 1, D_MODEL),
            pool_p[None], conv_p[None], h_p.reshape(1, bp, D_RNN),
            mk.reshape(1, bp, N_MEM, XA_HEADS, XA_HEAD_DIM), mv.reshape(1, bp, N_MEM, XA_HEADS, XA_HEAD_DIM),
            pool_s[None], conv_s[None], h_s[None])
```

```python
import functools

import jax
import jax.numpy as jnp
from jax import lax
from jax.experimental import pallas as pl
from jax.experimental.pallas import tpu as pltpu

F32 = jnp.float32
BF16 = jnp.bfloat16
I32 = jnp.int32
U32 = jnp.uint32

D_MODEL = 1024
D_POOL = 512
D_RNN = 512
D_IN = D_POOL + 2 * D_RNN
POOL_WINDOWS = (2, 4, 8, 16)
POOL_GROUP = 128
POOL_BUF = 15
CONV_WIDTH = 4
LRU_C = 8.0
N_MEM = 256
XA_HEADS = 4
XA_HEAD_DIM = 256
N_EXPERTS = 64
TOP_K = 8
N_EXPERT_GROUPS = 8
GROUP_SIZE = N_EXPERTS // N_EXPERT_GROUPS
TOPK_GROUPS = 4
D_EXPERT = 256
ROUTED_SCALE = 2.5
EPS = 1e-6
PAST_LEN = 16384

HALO = 16
CONV_HALO = 8
TS = 256
BM = 256
RUN_ALIGN = 16
RUN_CHUNKS = (64, 32, 16)
SORT_CHUNK = 512
VMEM_LIMIT = 56 * 1024 * 1024


def _round_up(x, m):
    return (x + m - 1) // m * m


def _sorted_rows(tokens):
    return _round_up(tokens * TOP_K + N_EXPERTS * (RUN_ALIGN - 1), SORT_CHUNK)


def _const_spec(shape):
    nd = len(shape)
    return pl.BlockSpec(shape, lambda *_: (0,) * nd, pipeline_mode=pl.Buffered(1))


def _rms(x, g):
    ms = jnp.mean(x * x, axis=-1, keepdims=True)
    return x * lax.rsqrt(ms + EPS) * g


def _dot(a, b):
    return jnp.dot(a, b, preferred_element_type=F32)


def _dot_nt(a, b, precision=None):
    return lax.dot_general(a, b, (((1,), (1,)), ((), ())), precision=precision,
                           preferred_element_type=F32)


def _softplus(x):
    return jnp.maximum(x, 0.0) + jnp.log1p(jnp.exp(-jnp.abs(x)))


def _gates_and_decay(c, pos_is_zero, wa_ref, ba_ref, wx_ref, bx_ref, lam_ref):
    cb = c.astype(BF16)
    half = D_RNN // 2
    ga = jnp.concatenate([_dot(cb[:, :half], wa_ref[0]), _dot(cb[:, half:], wa_ref[1])], axis=1) + ba_ref[...]
    gx = jnp.concatenate([_dot(cb[:, :half], wx_ref[0]), _dot(cb[:, half:], wx_ref[1])], axis=1) + bx_ref[...]
    r = jax.nn.sigmoid(ga)
    i = jax.nn.sigmoid(gx)
    log_a = (-LRU_C) * r * _softplus(-lam_ref[...])
    a = jnp.exp(log_a)
    mult = jnp.sqrt(1.0 - a * a)
    if pos_is_zero is not None:
        mult = jnp.where(pos_is_zero, 1.0, mult)
    return a, mult * i * c


def _pool_project(mean, u_pool, pw_ref, pscale_ref):
    d = (mean - u_pool).astype(BF16)
    half = D_POOL // 2
    y = jnp.concatenate([_dot(d[:, :half], pw_ref[0]), _dot(d[:, half:], pw_ref[1])], axis=1)
    return y * pscale_ref[...]


def _merge_out(y_pool, hs, u_gate, gpool_ref, grnn_ref, wout_ref):
    y_rnn = hs * jax.nn.gelu(u_gate)
    merged = jnp.concatenate([_rms(y_pool, gpool_ref[...]), _rms(y_rnn, grnn_ref[...])], axis=1)
    return _dot(merged.astype(BF16), wout_ref[...])


def _route(h3, wr_ref, rbias_ref):
    r_tok = h3.shape[0]
    logits = _dot_nt(wr_ref[...], h3, precision=lax.Precision.HIGHEST)
    scores = jax.nn.sigmoid(logits)
    biased = scores + rbias_ref[...]
    neg = jnp.float32(-jnp.inf)
    gs = []
    for g in range(N_EXPERT_GROUPS):
        xg = biased[g * GROUP_SIZE:(g + 1) * GROUP_SIZE]
        m1 = jnp.max(xg, axis=0, keepdims=True)
        eq = xg == m1
        cnt = jnp.sum(eq.astype(F32), axis=0, keepdims=True)
        m2 = jnp.max(jnp.where(eq, neg, xg), axis=0, keepdims=True)
        gs.append(m1 + jnp.where(cnt >= 2.0, m1, m2))
    pieces = []
    for g in range(N_EXPERT_GROUPS):
        beaten = jnp.zeros_like(gs[g])
        for o in range(N_EXPERT_GROUPS):
            if o == g:
                continue
            wins = (gs[o] > gs[g]) | (gs[o] == gs[g]) if o < g else (gs[o] > gs[g])
            beaten = beaten + wins.astype(F32)
        keep = beaten < float(TOPK_GROUPS)
        xg = biased[g * GROUP_SIZE:(g + 1) * GROUP_SIZE]
        pieces.append(jnp.where(keep, xg, neg))
    cur = jnp.concatenate(pieces, axis=0)
    eid = lax.broadcasted_iota(I32, (N_EXPERTS, r_tok), 0).astype(F32)
    idx_rows, score_rows = [], []
    sel = jnp.zeros((N_EXPERTS, r_tok), F32)
    for _ in range(TOP_K):
        m = jnp.max(cur, axis=0, keepdims=True)
        idx = jnp.min(jnp.where(cur == m, eid, float(N_EXPERTS)), axis=0, keepdims=True)
        oh = eid == idx
        score_rows.append(jnp.sum(jnp.where(oh, scores, 0.0), axis=0, keepdims=True))
        idx_rows.append(idx)
        sel = sel + oh.astype(F32)
        cur = jnp.where(oh, neg, cur)
    tot = score_rows[0]
    for s in score_rows[1:]:
        tot = tot + s
    w_rows = [s / tot * ROUTED_SCALE for s in score_rows]
    rr = lax.broadcasted_iota(I32, (r_tok, r_tok), 0)
    cc = lax.broadcasted_iota(I32, (r_tok, r_tok), 1)
    earlier = _dot(sel.astype(BF16), (rr < cc).astype(BF16))
    counts = jnp.sum(sel, axis=1, keepdims=True)
    run_len = jnp.floor((counts + (RUN_ALIGN - 1.0)) * (1.0 / RUN_ALIGN)) * RUN_ALIGN
    er = lax.broadcasted_iota(I32, (N_EXPERTS, N_EXPERTS), 0)
    ec = lax.broadcasted_iota(I32, (N_EXPERTS, N_EXPERTS), 1)
    run_start = _dot((ec < er).astype(BF16), jnp.broadcast_to(run_len, (N_EXPERTS, 128)).astype(BF16))[:, :1]
    slot = earlier + run_start
    slot_rows = [jnp.sum(jnp.where(eid == idx, slot, 0.0), axis=0, keepdims=True) for idx in idx_rows]
    return slot_rows, w_rows, run_len


def _moe_prologue(x2, gffn_ref, sg_ref, su_ref, sd_ref):
    h3 = _rms(x2, gffn_ref[...])
    h3b = h3.astype(BF16)
    act = jax.nn.silu(_dot(h3b, sg_ref[...])) * _dot(h3b, su_ref[...])
    shared = _dot(act.astype(BF16), sd_ref[...])
    return h3, x2 + shared


def _store_rows(ref, rows, dtype):
    for k, row in enumerate(rows):
        ref[k:k + 1, :] = row.astype(dtype)


def _memkv_kernel(mem_ref, g_ref, wk_ref, wv_ref, k_ref, v_ref, kb_ref, vb_ref):
    m = _rms(mem_ref[0], g_ref[...]).astype(BF16)
    k = _dot(m, wk_ref[...])
    v = _dot(m, wv_ref[...])
    k_ref[0] = k
    v_ref[0] = v
    kb_ref[0] = k.astype(BF16)
    vb_ref[0] = v.astype(BF16)


def _memkv(mem, g, wk, wv):
    b = mem.shape[0]
    blk = pl.BlockSpec((1, N_MEM, D_MODEL), lambda i: (i, 0, 0))
    return pl.pallas_call(
        _memkv_kernel,
        grid=(b,),
        in_specs=[blk, _const_spec((1, D_MODEL)), _const_spec((D_MODEL, D_MODEL)), _const_spec((D_MODEL, D_MODEL))],
        out_specs=[blk, blk, blk, blk],
        out_shape=[jax.ShapeDtypeStruct((b, N_MEM, D_MODEL), F32)] * 2
        + [jax.ShapeDtypeStruct((b, N_MEM, D_MODEL), BF16)] * 2,
        compiler_params=pltpu.CompilerParams(dimension_semantics=("arbitrary",), vmem_limit_bytes=VMEM_LIMIT),
        name="memkv",
    )(mem, g, wk, wv)


def _trunk_p_kernel(x_ref, kb_ref, vb_ref,
                    gmix_ref, win_ref, pw_ref, pscale_ref, cw_ref, cb_ref, wa_ref, ba_ref, wx_ref, bx_ref,
                    lam_ref, gpool_ref, grnn_ref, wout_ref,
                    gxa_ref, wq_ref, wo_ref,
                    gffn_ref, sg_ref, su_ref, sd_ref, wr_ref, rbias_ref,
                    xres_ref, hb_ref, slot_ref, gw_ref, len_ref, before_ref, cnt_ref, pool_ref, conv_ref, hT_ref,
                    pool_prev, conv_prev, h_prev, carry):
    b = pl.program_id(0)
    j = pl.program_id(1)
    n_j = pl.num_programs(1)

    @pl.when(j == 0)
    def _():
        pool_prev[...] = jnp.zeros_like(pool_prev)
        conv_prev[...] = jnp.zeros_like(conv_prev)
        h_prev[...] = jnp.zeros_like(h_prev)

    @pl.when((b == 0) & (j == 0))
    def _():
        carry[...] = jnp.zeros_like(carry)

    x = x_ref[0]
    row = lax.broadcasted_iota(I32, (TS, 1), 0)
    pos = j * TS + row

    h = _rms(x, gmix_ref[...]).astype(BF16)
    z = _dot(h, win_ref[...])
    u_pool = z[:, :D_POOL]
    u_rnn = z[:, D_POOL:D_POOL + D_RNN]
    u_gate = z[:, D_POOL + D_RNN:]

    ext = jnp.concatenate([pool_prev[...], u_pool], axis=0)
    means = []
    for g, w in enumerate(POOL_WINDOWS):
        s = ext[:, g * POOL_GROUP:(g + 1) * POOL_GROUP]
        k = 1
        while k < w:
            s = s + pltpu.roll(s, k, 0)
            k *= 2
        inv = 1.0 / jnp.minimum(pos + 1, w).astype(F32)
        means.append(s[HALO:] * inv)
    mean = jnp.concatenate(means, axis=1)
    y_pool = _pool_project(mean, u_pool, pw_ref, pscale_ref)

    extc = jnp.concatenate([conv_prev[...], u_rnn], axis=0)
    c = u_rnn * cw_ref[CONV_WIDTH - 1:CONV_WIDTH, :]
    for k in range(1, CONV_WIDTH):
        c = c + pltpu.roll(extc, k, 0)[CONV_HALO:] * cw_ref[CONV_WIDTH - 1 - k:CONV_WIDTH - k, :]
    c = c + cb_ref[...]

    a, bt = _gates_and_decay(c, pos == 0, wa_ref, ba_ref, wx_ref, bx_ref, lam_ref)
    k = 1
    while k < TS:
        valid = row >= k
        a_sh = jnp.where(valid, pltpu.roll(a, k, 0), 1.0)
        b_sh = jnp.where(valid, pltpu.roll(bt, k, 0), 0.0)
        bt = bt + a * b_sh
        a = a * a_sh
        k *= 2
    hs = bt + a * h_prev[...]

    pool_prev[...] = u_pool[TS - HALO:]
    conv_prev[...] = u_rnn[TS - CONV_HALO:]
    h_prev[...] = hs[TS - 1:]

    @pl.when(j == n_j - 1)
    def _():
        pool_ref[0] = u_pool[TS - POOL_BUF:]
        conv_ref[0] = u_rnn[TS - (CONV_WIDTH - 1):]
        hT_ref[0] = hs[TS - 1:]

    x1 = x + _merge_out(y_pool, hs, u_gate, gpool_ref, grnn_ref, wout_ref)

    h2 = _rms(x1, gxa_ref[...]).astype(BF16)
    q = (_dot(h2, wq_ref[...]) * (XA_HEAD_DIM ** -0.5)).astype(BF16)
    outs = []
    for hd in range(XA_HEADS):
        sl = slice(hd * XA_HEAD_DIM, (hd + 1) * XA_HEAD_DIM)
        s = _dot_nt(q[:, sl], kb_ref[0, :, sl])
        s = s - jnp.max(s, axis=-1, keepdims=True)
        p = jnp.exp(s)
        p = p / jnp.sum(p, axis=-1, keepdims=True)
        outs.append(_dot(p.astype(BF16), vb_ref[0, :, sl]))
    o = jnp.concatenate(outs, axis=1).astype(BF16)
    x2 = x1 + _dot(o, wo_ref[...])

    h3, xres = _moe_prologue(x2, gffn_ref, sg_ref, su_ref, sd_ref)
    xres_ref[...] = xres
    hb_ref[...] = h3.astype(BF16)
    slot_rows, w_rows, run_len = _route(h3, wr_ref, rbias_ref)
    _store_rows(slot_ref, slot_rows, I32)
    _store_rows(gw_ref, w_rows, F32)
    len_ref[0] = jnp.broadcast_to(run_len, len_ref.shape[1:])
    before_ref[0] = jnp.broadcast_to(carry[...], before_ref.shape[1:])
    carry[...] = carry[...] + run_len
    cnt_ref[...] = jnp.broadcast_to(carry[...], cnt_ref.shape)


def _trunk_p(x, kb, vb, mixw, xaw, moew):
    bsz, seq, _ = x.shape
    n_j = seq // TS
    t = bsz * seq
    tok = lambda b, j: (b * n_j + j, 0)
    lane_tok = lambda b, j: (0, b * n_j + j)
    per_b = lambda b, j: (b, 0, 0)
    weights = list(mixw) + list(xaw) + list(moew)
    in_specs = [pl.BlockSpec((1, TS, D_MODEL), lambda b, j: (b, j, 0)),
                pl.BlockSpec((1, N_MEM, D_MODEL), per_b),
                pl.BlockSpec((1, N_MEM, D_MODEL), per_b)] + [_const_spec(w.shape) for w in weights]
    per_tile = lambda b, j: (b * n_j + j, 0, 0)
    out_shape = [jax.ShapeDtypeStruct((t, D_MODEL), F32),
                 jax.ShapeDtypeStruct((t, D_MODEL), BF16),
                 jax.ShapeDtypeStruct((TOP_K, t), I32),
                 jax.ShapeDtypeStruct((TOP_K, t), F32),
                 jax.ShapeDtypeStruct((bsz * n_j, N_EXPERTS, 128), F32),
                 jax.ShapeDtypeStruct((bsz * n_j, N_EXPERTS, 128), F32),
                 jax.ShapeDtypeStruct((N_EXPERTS, 128), F32),
                 jax.ShapeDtypeStruct((bsz, POOL_BUF, D_POOL), F32),
                 jax.ShapeDtypeStruct((bsz, CONV_WIDTH - 1, D_RNN), F32),
                 jax.ShapeDtypeStruct((bsz, 1, D_RNN), F32)]
    out_specs = [pl.BlockSpec((TS, D_MODEL), tok),
                 pl.BlockSpec((TS, D_MODEL), tok),
                 pl.BlockSpec((TOP_K, TS), lane_tok),
                 pl.BlockSpec((TOP_K, TS), lane_tok),
                 pl.BlockSpec((1, N_EXPERTS, 128), per_tile),
                 pl.BlockSpec((1, N_EXPERTS, 128), per_tile),
                 pl.BlockSpec((N_EXPERTS, 128), lambda b, j: (0, 0)),
                 pl.BlockSpec((1, POOL_BUF, D_POOL), per_b),
                 pl.BlockSpec((1, CONV_WIDTH - 1, D_RNN), per_b),
                 pl.BlockSpec((1, 1, D_RNN), per_b)]
    return pl.pallas_call(
        _trunk_p_kernel,
        grid=(bsz, n_j),
        in_specs=in_specs,
        out_specs=out_specs,
        out_shape=out_shape,
        scratch_shapes=[pltpu.VMEM((HALO, D_POOL), F32), pltpu.VMEM((CONV_HALO, D_RNN), F32),
                        pltpu.VMEM((1, D_RNN), F32), pltpu.VMEM((N_EXPERTS, 1), F32)],
        compiler_params=pltpu.CompilerParams(dimension_semantics=("arbitrary", "arbitrary"),
                                             vmem_limit_bytes=VMEM_LIMIT),
        name="trunk_p",
    )(x, kb, vb, *weights)


def _mix_s_kernel(x_ref, pool_ref, conv_ref, h0_ref,
                  gmix_ref, win_ref, pw_ref, pscale_ref, cw_ref, cb_ref, wa_ref, ba_ref, wx_ref, bx_ref,
                  lam_ref, gpool_ref, grnn_ref, wout_ref, gxa_ref, wq_ref,
                  x1_ref, q_ref, npool_ref, nconv_ref, nh_ref):
    x = x_ref[...]
    h = _rms(x, gmix_ref[...]).astype(BF16)
    z = _dot(h, win_ref[...])
    u_pool = z[:, :D_POOL]
    u_rnn = z[:, D_POOL:D_POOL + D_RNN]
    u_gate = z[:, D_POOL + D_RNN:]

    means = []
    for g, w in enumerate(POOL_WINDOWS):
        sl = slice(g * POOL_GROUP, (g + 1) * POOL_GROUP)
        s = u_pool[:, sl]
        for k in range(1, w):
            s = s + pool_ref[POOL_BUF - k, :, sl]
        means.append(s * (1.0 / min(w, PAST_LEN + 1)))
    mean = jnp.concatenate(means, axis=1)
    y_pool = _pool_project(mean, u_pool, pw_ref, pscale_ref)

    c = u_rnn * cw_ref[CONV_WIDTH - 1:CONV_WIDTH, :]
    for k in range(1, CONV_WIDTH):
        c = c + conv_ref[CONV_WIDTH - 1 - k] * cw_ref[CONV_WIDTH - 1 - k:CONV_WIDTH - k, :]
    c = c + cb_ref[...]
    a, bt = _gates_and_decay(c, None, wa_ref, ba_ref, wx_ref, bx_ref, lam_ref)
    hs = a * h0_ref[...] + bt

    x1 = x + _merge_out(y_pool, hs, u_gate, gpool_ref, grnn_ref, wout_ref)
    x1_ref[...] = x1
    h2 = _rms(x1, gxa_ref[...]).astype(BF16)
    q_ref[...] = _dot(h2, wq_ref[...]) * (XA_HEAD_DIM ** -0.5)

    npool_ref[:POOL_BUF - 1] = pool_ref[1:]
    npool_ref[POOL_BUF - 1] = u_pool
    nconv_ref[:CONV_WIDTH - 2] = conv_ref[1:]
    nconv_ref[CONV_WIDTH - 2] = u_rnn
    nh_ref[...] = hs


def _mix_s(x, pool, conv, h0, mixw, gxa, wq):
    bsz = x.shape[0]
    args = [x, pool, conv, h0] + list(mixw) + [gxa, wq]
    return pl.pallas_call(
        _mix_s_kernel,
        grid=(1,),
        in_specs=[_const_spec(a.shape) for a in args],
        out_specs=[_const_spec((bsz, D_MODEL)), _const_spec((bsz, D_MODEL)), _const_spec(pool.shape),
                   _const_spec(conv.shape), _const_spec((bsz, D_RNN))],
        out_shape=[jax.ShapeDtypeStruct((bsz, D_MODEL), F32), jax.ShapeDtypeStruct((bsz, D_MODEL), F32),
                   jax.ShapeDtypeStruct(pool.shape, F32), jax.ShapeDtypeStruct(conv.shape, F32),
                   jax.ShapeDtypeStruct((bsz, D_RNN), F32)],
        compiler_params=pltpu.CompilerParams(dimension_semantics=("arbitrary",), vmem_limit_bytes=VMEM_LIMIT),
        name="mix_s",
    )(*args)


ATTN_S_BB = 4


def _attn_s_kernel(q_ref, k_ref, v_ref, o_ref):
    q = q_ref[...][:, None]
    s = jnp.sum(k_ref[...] * q, axis=-1, keepdims=True)
    s = s - jnp.max(s, axis=1, keepdims=True)
    p = jnp.exp(s)
    p = p / jnp.sum(p, axis=1, keepdims=True)
    o_ref[...] = jnp.sum(p * v_ref[...], axis=1)


def _attn_s(q, k, v):
    bsz = q.shape[0]
    kv_spec = pl.BlockSpec((ATTN_S_BB, N_MEM, XA_HEADS, XA_HEAD_DIM), lambda i: (i, 0, 0, 0))
    q_spec = pl.BlockSpec((ATTN_S_BB, XA_HEADS, XA_HEAD_DIM), lambda i: (i, 0, 0))
    o = pl.pallas_call(
        _attn_s_kernel,
        grid=(bsz // ATTN_S_BB,),
        in_specs=[q_spec, kv_spec, kv_spec],
        out_specs=q_spec,
        out_shape=jax.ShapeDtypeStruct((bsz, XA_HEADS, XA_HEAD_DIM), F32),
        compiler_params=pltpu.CompilerParams(dimension_semantics=("arbitrary",), vmem_limit_bytes=VMEM_LIMIT),
        name="attn_s",
    )(q.reshape(bsz, XA_HEADS, XA_HEAD_DIM), k, v)
    return o.reshape(bsz, D_MODEL)


def _post_s_kernel(x1_ref, o_ref, wo_ref, gffn_ref, sg_ref, su_ref, sd_ref, wr_ref, rbias_ref,
                   xres_ref, hb_ref, slot_ref, gw_ref, len_ref):
    x2 = x1_ref[...] + _dot(o_ref[...].astype(BF16), wo_ref[...])
    h3, xres = _moe_prologue(x2, gffn_ref, sg_ref, su_ref, sd_ref)
    xres_ref[...] = xres
    hb_ref[...] = h3.astype(BF16)
    slot_rows, w_rows, run_len = _route(h3, wr_ref, rbias_ref)
    _store_rows(slot_ref, slot_rows, I32)
    _store_rows(gw_ref, w_rows, F32)
    len_ref[...] = jnp.broadcast_to(run_len, len_ref.shape)


def _post_s(x1, o, wo, moew):
    bsz = x1.shape[0]
    args = [x1, o, wo] + list(moew)
    out_shape = [jax.ShapeDtypeStruct((bsz, D_MODEL), F32),
                 jax.ShapeDtypeStruct((bsz, D_MODEL), BF16),
                 jax.ShapeDtypeStruct((TOP_K, bsz), I32),
                 jax.ShapeDtypeStruct((TOP_K, bsz), F32),
                 jax.ShapeDtypeStruct((N_EXPERTS, 128), F32)]
    return pl.pallas_call(
        _post_s_kernel,
        grid=(1,),
        in_specs=[_const_spec(a.shape) for a in args],
        out_specs=[_const_spec(s.shape) for s in out_shape],
        out_shape=out_shape,
        compiler_params=pltpu.CompilerParams(dimension_semantics=("arbitrary",), vmem_limit_bytes=VMEM_LIMIT),
        name="post_s",
    )(*args)


def _plan_kernel(cp_ref, cs_ref, blk_ref, exp_ref, lo_ref, hi_ref, base_p_ref, base_s_ref, tot_ref):
    n_work = blk_ref.shape[0]
    shift = BM.bit_length() - 1

    def per_expert(e, carry):
        start, w, last_e = carry
        cnt = cp_ref[e] + cs_ref[e]
        base_p_ref[e] = start
        base_s_ref[e] = start + cp_ref[e]
        end = start + cnt
        first = lax.shift_right_logical(start, shift)
        n_tiles = jnp.where(cnt > 0, lax.shift_right_logical(end - 1, shift) - first + 1, 0)

        def per_tile(i, w):
            b = first + i
            blk_ref[w] = b
            exp_ref[w] = e
            lo_ref[w] = jnp.maximum(start, b * BM) - b * BM
            hi_ref[w] = jnp.minimum(end, (b + 1) * BM) - b * BM
            return w + 1

        w = lax.fori_loop(0, n_tiles, per_tile, w)
        return end, w, jnp.where(cnt > 0, e, last_e)

    total, w, last_e = lax.fori_loop(0, N_EXPERTS, per_expert, (jnp.int32(0), jnp.int32(0), jnp.int32(0)))
    tot_ref[0] = total

    n_blocks = n_work - (N_EXPERTS - 1)
    next_blk = lax.shift_right_logical(total + (BM - 1), shift)

    def fill(i, _):
        blk_ref[i] = jnp.minimum(next_blk + (i - w), n_blocks - 1)
        exp_ref[i] = last_e
        lo_ref[i] = 0
        hi_ref[i] = 0
        return _

    lax.fori_loop(w, n_work, fill, 0)


def _plan(cp, cs, n_rows):
    assert BM & (BM - 1) == 0 and n_rows % BM == 0
    n_work = n_rows // BM + N_EXPERTS - 1
    smem = pl.BlockSpec(memory_space=pltpu.SMEM)
    return pl.pallas_call(
        _plan_kernel,
        in_specs=[smem, smem],
        out_specs=[smem] * 7,
        out_shape=[jax.ShapeDtypeStruct((n_work,), I32)] * 4 + [jax.ShapeDtypeStruct((N_EXPERTS,), I32)] * 2
        + [jax.ShapeDtypeStruct((1,), I32)],
        name="plan",
    )(cp, cs)


def _groups(n_rows):
    return lax.shift_right_logical(n_rows, RUN_ALIGN.bit_length() - 1)


def _tile_rows(i, len_ref):
    return lax.fori_loop(0, N_EXPERTS, lambda e, acc: acc + len_ref[i, e], jnp.int32(0))


def _for_each_run_chunk(i, len_ref, before_ref, base_ref, fn):
    def per_expert(e, local):
        n = len_ref[i, e]
        first = base_ref[e] + before_ref[i, e]
        done = jnp.int32(0)
        for size in RUN_CHUNKS:
            shift = size.bit_length() - 1
            pieces = lax.shift_right_logical(n - done, shift)

            def per_piece(p, _, done=done, size=size):
                off = done + p * size
                fn(pl.multiple_of(local + off, RUN_ALIGN), pl.multiple_of(first + off, RUN_ALIGN), size)
                return _

            lax.fori_loop(0, pieces, per_piece, 0)
            done = done + lax.shift_left(pieces, shift)
        return local + n

    lax.fori_loop(0, N_EXPERTS, per_expert, jnp.int32(0))


def _dispatch_kernel(len_p, before_p, base_p, len_s, before_s, base_s, tot_ref,
                     slot_p_ref, h_p_ref, slot_s_ref, h_s_ref, xs_ref, sbuf, zbuf, sem):
    i = pl.program_id(0)
    last = pl.num_programs(0) - 1
    cur = lax.rem(i, 2)

    def drain(s, n_rows):
        def body(g, _):
            pltpu.make_async_copy(sbuf.at[s, pl.ds(0, RUN_ALIGN)], xs_ref.at[pl.ds(0, RUN_ALIGN)], sem.at[s]).wait()
            return _
        lax.fori_loop(0, _groups(n_rows), body, 0)

    def tile(tile_idx, len_ref, before_ref, base_ref, slot_ref, h_ref):
        h = h_ref[...]
        for c in range(_sorted_rows(h.shape[0]) // SORT_CHUNK):
            rid = c * SORT_CHUNK + lax.broadcasted_iota(I32, (SORT_CHUNK, 1), 0)
            hit = rid == slot_ref[0:1, :]
            for k in range(1, TOP_K):
                hit = hit | (rid == slot_ref[k:k + 1, :])
            sbuf[cur, c * SORT_CHUNK:(c + 1) * SORT_CHUNK] = _dot(hit.astype(BF16), h).astype(BF16)

        def start(local, glob, size):
            pltpu.make_async_copy(sbuf.at[cur, pl.ds(local, size)], xs_ref.at[pl.ds(glob, size)], sem.at[cur]).start()

        _for_each_run_chunk(tile_idx, len_ref, before_ref, base_ref, start)

    @pl.when(i >= 2)
    def _():
        drain(cur, _tile_rows(i - 2, len_p))

    @pl.when(i < last)
    def _():
        tile(i, len_p, before_p, base_p, slot_p_ref, h_p_ref)

    @pl.when(i == last)
    def _():
        tile(0, len_s, before_s, base_s, slot_s_ref, h_s_ref)
        drain(1 - cur, _tile_rows(last - 1, len_p))
        drain(cur, _tile_rows(0, len_s))
        zbuf[...] = jnp.zeros_like(zbuf)
        total = tot_ref[0]
        shift = BM.bit_length() - 1
        first_blk = lax.shift_right_logical(total + (BM - 1), shift)
        n_groups = _groups(lax.shift_left(first_blk, shift) - total)
        n_blks = xs_ref.shape[0] // BM - first_blk

        def group_fill(g):
            return pltpu.make_async_copy(
                zbuf.at[pl.ds(0, RUN_ALIGN)],
                xs_ref.at[pl.ds(pl.multiple_of(total + g * RUN_ALIGN, RUN_ALIGN), RUN_ALIGN)], sem.at[2])

        def blk_fill(b):
            return pltpu.make_async_copy(zbuf, xs_ref.at[pl.ds(pl.multiple_of((first_blk + b) * BM, BM), BM)],
                                         sem.at[2])

        def fill_all(n, fill):
            def start(j, _):
                fill(j).start()
                return _

            def wait(j, _):
                fill(j).wait()
                return _

            lax.fori_loop(0, n, start, 0)
            lax.fori_loop(0, n, wait, 0)

        fill_all(n_groups, group_fill)
        fill_all(n_blks, blk_fill)


def _dispatch(tab_p, tab_s, total, slot_p, hb_p, slot_s, hb_s, n_rows):
    n_p = hb_p.shape[0] // TS
    smem = pl.BlockSpec(memory_space=pltpu.SMEM)
    clamp = lambda i: jnp.minimum(i, n_p - 1)
    return pl.pallas_call(
        _dispatch_kernel,
        grid=(n_p + 1,),
        in_specs=[smem] * 7 + [pl.BlockSpec((TOP_K, TS), lambda i: (0, clamp(i))),
                               pl.BlockSpec((TS, D_MODEL), lambda i: (clamp(i), 0)),
                               _const_spec(slot_s.shape), _const_spec(hb_s.shape)],
        out_specs=pl.BlockSpec(memory_space=pl.ANY),
        out_shape=jax.ShapeDtypeStruct((n_rows, D_MODEL), BF16),
        scratch_shapes=[pltpu.VMEM((2, _sorted_rows(TS), D_MODEL), BF16),
                        pltpu.VMEM((BM, D_MODEL), BF16), pltpu.SemaphoreType.DMA((3,))],
        compiler_params=pltpu.CompilerParams(dimension_semantics=("arbitrary",), has_side_effects=True,
                                             vmem_limit_bytes=VMEM_LIMIT),
        name="dispatch",
    )(*tab_p, *tab_s, total, slot_p, hb_p, slot_s, hb_s)


def _gmm_kernel(blk_ref, exp_ref, lo_ref, hi_ref, x_ref, wg_ref, wu_ref, wd_ref, y_ref, wgu, wdn):
    w = pl.program_id(0)
    prev = jnp.maximum(w - 1, 0)
    new_expert = (w == 0) | (exp_ref[w] != exp_ref[prev])
    first_visit = (w == 0) | (blk_ref[w] != blk_ref[prev])
    lo = lo_ref[w]
    hi = hi_ref[w]

    @pl.when(new_expert)
    def _():
        wgu[:, :D_EXPERT] = wg_ref[0].astype(BF16)
        wgu[:, D_EXPERT:] = wu_ref[0].astype(BF16)
        wdn[...] = wd_ref[0].astype(BF16)

    @pl.when(hi > lo)
    def _():
        gu = _dot(x_ref[...], wgu[...])
        act = jax.nn.silu(gu[:, :D_EXPERT]) * gu[:, D_EXPERT:]
        y = _dot(act.astype(BF16), wdn[...]).astype(BF16)
        row = lax.broadcasted_iota(I32, (BM, 1), 0)
        mine = (row >= lo) & (row < hi)

        @pl.when(first_visit)
        def _():
            y_ref[...] = jnp.where(mine, y, jnp.zeros_like(y))

        @pl.when(jnp.logical_not(first_visit))
        def _():
            y_ref[...] = jnp.where(mine, y, y_ref[...])

    @pl.when((hi <= lo) & first_visit)
    def _():
        y_ref[...] = jnp.zeros_like(y_ref)


def _gmm(meta, xs, wg, wu, wd):
    blk, exp, lo, hi = meta
    n_rows = xs.shape[0]
    n_work = blk.shape[0]
    grid_spec = pltpu.PrefetchScalarGridSpec(
        num_scalar_prefetch=4,
        grid=(n_work,),
        in_specs=[pl.BlockSpec((BM, D_MODEL), lambda w, blk, exp, lo, hi: (blk[w], 0)),
                  pl.BlockSpec((1, D_MODEL, D_EXPERT), lambda w, blk, exp, lo, hi: (exp[w], 0, 0)),
                  pl.BlockSpec((1, D_MODEL, D_EXPERT), lambda w, blk, exp, lo, hi: (exp[w], 0, 0)),
                  pl.BlockSpec((1, D_EXPERT, D_MODEL), lambda w, blk, exp, lo, hi: (exp[w], 0, 0))],
        out_specs=pl.BlockSpec((BM, D_MODEL), lambda w, blk, exp, lo, hi: (blk[w], 0)),
        scratch_shapes=[pltpu.VMEM((D_MODEL, 2 * D_EXPERT), BF16), pltpu.VMEM((D_EXPERT, D_MODEL), BF16)],
    )
    return pl.pallas_call(
        _gmm_kernel,
        grid_spec=grid_spec,
        out_shape=jax.ShapeDtypeStruct((n_rows, D_MODEL), BF16),
        compiler_params=pltpu.CompilerParams(dimension_semantics=("arbitrary",), vmem_limit_bytes=VMEM_LIMIT),
        name="gmm",
    )(blk, exp, lo, hi, xs, wg, wu, wd)


def _combine_kernel(len_ref, before_ref, base_ref, slot_ref, gw_ref, xres_ref, gfin_ref, ys_ref, out_ref,
                    ybuf, sem):
    i = pl.program_id(0)
    n = pl.num_programs(0)
    cur = lax.rem(i, 2)
    rows = ybuf.shape[1]

    def piece_copy(s, local, glob, size):
        return pltpu.make_async_copy(ys_ref.at[pl.ds(glob, size)], ybuf.at[s, pl.ds(local, size)], sem.at[s])

    def gather(tile, s):
        _for_each_run_chunk(tile, len_ref, before_ref, base_ref, lambda l, g, size: piece_copy(s, l, g, size).start())

    @pl.when(i == 0)
    def _():
        ybuf[...] = jnp.zeros_like(ybuf)
        gather(0, 0)

    @pl.when(i + 1 < n)
    def _():
        gather(i + 1, 1 - cur)

    def drain(g, _):
        piece_copy(cur, 0, 0, RUN_ALIGN).wait()
        return _

    lax.fori_loop(0, _groups(_tile_rows(i, len_ref)), drain, 0)

    acc = xres_ref[...]
    slot = slot_ref[...]
    gw = gw_ref[...]
    for c in range(rows // SORT_CHUNK):
        col = c * SORT_CHUNK + lax.broadcasted_iota(I32, (1, SORT_CHUNK), 1)
        w = jnp.zeros((slot.shape[0], SORT_CHUNK), F32)
        for k in range(TOP_K):
            w = w + jnp.where(slot[:, k:k + 1] == col, gw[:, k:k + 1], 0.0)
        w_hi = w.astype(BF16)
        w_lo = (w - w_hi.astype(F32)).astype(BF16)
        y = ybuf[cur, c * SORT_CHUNK:(c + 1) * SORT_CHUNK]
        acc = acc + _dot(w_hi, y) + _dot(w_lo, y)
    out_ref[...] = _rms(acc, gfin_ref[...])


def _combine(tables, slot_t, gw_t, xres, gfin, ys):
    t = xres.shape[0]
    tm = min(t, TS)
    smem = pl.BlockSpec(memory_space=pltpu.SMEM)
    return pl.pallas_call(
        _combine_kernel,
        grid=(t // tm,),
        in_specs=[smem, smem, smem,
                  pl.BlockSpec((tm, TOP_K), lambda i: (i, 0)),
                  pl.BlockSpec((tm, TOP_K), lambda i: (i, 0)),
                  pl.BlockSpec((tm, D_MODEL), lambda i: (i, 0)),
                  _const_spec((1, D_MODEL)),
                  pl.BlockSpec(memory_space=pl.ANY)],
        out_specs=pl.BlockSpec((tm, D_MODEL), lambda i: (i, 0)),
        out_shape=jax.ShapeDtypeStruct((t, D_MODEL), F32),
        scratch_shapes=[pltpu.VMEM((2, _sorted_rows(tm), D_MODEL), BF16), pltpu.SemaphoreType.DMA((2,))],
        compiler_params=pltpu.CompilerParams(dimension_semantics=("arbitrary",), vmem_limit_bytes=VMEM_LIMIT),
        name="combine",
    )(*tables, slot_t, gw_t, xres, gfin, ys)


def _block_diag_pairs(w):
    n_h, d, _ = w.shape
    half = n_h // 2
    out = jnp.zeros((2, half * d, half * d), F32)
    for hh in range(n_h):
        p, q = divmod(hh, half)
        out = out.at[p, q * d:(q + 1) * d, q * d:(q + 1) * d].set(w[hh])
    return out.astype(BF16)


def kernel(x_prompt, x_sample, state_pool, state_conv, state_h, cache_mem_k, cache_mem_v, mem_prompt, norm_mix, w_in, pool_w, pool_scale, conv_w, conv_b, gate_a_w, gate_a_b, gate_x_w, gate_x_b, lru_lambda, norm_pool_out, norm_rnn_out, w_out, norm_xattn, norm_mem, xa_wq, xa_wk, xa_wv, xa_wo, norm_ffn, router_w, router_bias, exp_w_gate, exp_w_up, exp_w_down, sh_w_gate, sh_w_up, sh_w_down, norm_final):
    bp, seq, _ = x_prompt.shape
    bs = x_sample.shape[0]
    tp = bp * seq
    row = lambda v: v.reshape(1, -1)
    bf = lambda v: v.astype(BF16)

    mixw = (row(norm_mix[0]), bf(w_in[0]), _block_diag_pairs(pool_w[0]), row(pool_scale[0]), conv_w[0],
            row(conv_b[0]), _block_diag_pairs(gate_a_w[0]), row(gate_a_b[0]), _block_diag_pairs(gate_x_w[0]),
            row(gate_x_b[0]), row(lru_lambda[0]), row(norm_pool_out[0]), row(norm_rnn_out[0]), bf(w_out[0]))
    xaw = (row(norm_xattn[0]), bf(xa_wq[0]), bf(xa_wo[0]))
    moew = (row(norm_ffn[0]), bf(sh_w_gate[0]), bf(sh_w_up[0]), bf(sh_w_down[0]), router_w[0].T,
            router_bias[0].reshape(N_EXPERTS, 1))

    mk, mv, kb, vb = _memkv(mem_prompt, row(norm_mem[0]), bf(xa_wk[0]), bf(xa_wv[0]))
    (xres_p, hb_p, slot_p, gw_p, len_p, before_p, cnt_p, pool_p, conv_p, h_p) = _trunk_p(
        x_prompt, kb, vb, mixw, xaw, moew)

    x1_s, q_s, pool_s, conv_s, h_s = _mix_s(x_sample.reshape(bs, D_MODEL), state_pool[0].transpose(1, 0, 2),
                                            state_conv[0].transpose(1, 0, 2), state_h[0], mixw, xaw[0], xaw[1])
    pool_s = pool_s.transpose(1, 0, 2)
    conv_s = conv_s.transpose(1, 0, 2)
    o_s = _attn_s(q_s, cache_mem_k[0], cache_mem_v[0])
    xres_s, hb_s, slot_s, gw_s, len_s = _post_s(x1_s, o_s, xaw[2], moew)

    n_tiles = tp // TS + 1
    n_rows = _round_up((tp + bs) * TOP_K + n_tiles * N_EXPERTS * (RUN_ALIGN - 1), BM)
    ints = lambda v: v[..., 0].astype(I32)
    len_p, before_p, len_s = ints(len_p), ints(before_p), ints(len_s)[None]
    blk, exp, lo, hi, base_p, base_s, total = _plan(ints(cnt_p), len_s[0], n_rows)
    tab_p = (len_p, before_p, base_p)
    tab_s = (len_s, jnp.zeros_like(len_s), base_s)

    xs = _dispatch(tab_p, tab_s, total, slot_p, hb_p, slot_s, hb_s, n_rows)
    ys = _gmm((blk, exp, lo, hi), xs, exp_w_gate[0], exp_w_up[0], exp_w_down[0])

    gfin = row(norm_final)
    y_p = _combine(tab_p, slot_p.T, gw_p.T, xres_p, gfin, ys)
    y_s = _combine(tab_s, slot_s.T, gw_s.T, xres_s, gfin, ys)

    return (y_p.reshape(bp, seq, D_MODEL), y_s.reshape(bs, 1, D_MODEL),
            pool_p[None], conv_p[None], h_p.reshape(1, bp, D_RNN),
            mk.reshape(1, bp, N_MEM, XA_HEADS, XA_HEAD_DIM), mv.reshape(1, bp, N_MEM, XA_HEADS, XA_HEAD_DIM),
            pool_s[None], conv_s[None], h_s[None])
```

```python
import functools

import jax
import jax.numpy as jnp
from jax import lax
from jax.experimental import pallas as pl
from jax.experimental.pallas import tpu as pltpu

F32 = jnp.float32
BF16 = jnp.bfloat16
I32 = jnp.int32
U32 = jnp.uint32

D_MODEL = 1024
D_POOL = 512
D_RNN = 512
D_IN = D_POOL + 2 * D_RNN
POOL_WINDOWS = (2, 4, 8, 16)
POOL_GROUP = 128
POOL_BUF = 15
CONV_WIDTH = 4
LRU_C = 8.0
N_MEM = 256
XA_HEADS = 4
XA_HEAD_DIM = 256
N_EXPERTS = 64
TOP_K = 8
N_EXPERT_GROUPS = 8
GROUP_SIZE = N_EXPERTS // N_EXPERT_GROUPS
TOPK_GROUPS = 4
D_EXPERT = 256
ROUTED_SCALE = 2.5
EPS = 1e-6
PAST_LEN = 16384

HALO = 16
CONV_HALO = 8
TS = 256
BM = 256
RUN_ALIGN = 16
RUN_CHUNKS = (64, 32, 16)
SORT_CHUNK = 512
VMEM_LIMIT = 56 * 1024 * 1024


def _round_up(x, m):
    return (x + m - 1) // m * m


def _sorted_rows(tokens):
    return _round_up(tokens * TOP_K + N_EXPERTS * (RUN_ALIGN - 1), SORT_CHUNK)


def _group_lanes(tokens):
    return _round_up(_sorted_rows(tokens) // RUN_ALIGN, 128)


def _const_spec(shape):
    nd = len(shape)
    return pl.BlockSpec(shape, lambda *_: (0,) * nd, pipeline_mode=pl.Buffered(1))


def _rms(x, g):
    ms = jnp.mean(x * x, axis=-1, keepdims=True)
    return x * lax.rsqrt(ms + EPS) * g


def _dot(a, b):
    return jnp.dot(a, b, preferred_element_type=F32)


def _dot_nt(a, b, precision=None):
    return lax.dot_general(a, b, (((1,), (1,)), ((), ())), precision=precision,
                           preferred_element_type=F32)


def _softplus(x):
    return jnp.maximum(x, 0.0) + jnp.log1p(jnp.exp(-jnp.abs(x)))


def _gates_and_decay(c, pos_is_zero, wa_ref, ba_ref, wx_ref, bx_ref, lam_ref):
    cb = c.astype(BF16)
    half = D_RNN // 2
    ga = jnp.concatenate([_dot(cb[:, :half], wa_ref[0]), _dot(cb[:, half:], wa_ref[1])], axis=1) + ba_ref[...]
    gx = jnp.concatenate([_dot(cb[:, :half], wx_ref[0]), _dot(cb[:, half:], wx_ref[1])], axis=1) + bx_ref[...]
    r = jax.nn.sigmoid(ga)
    i = jax.nn.sigmoid(gx)
    log_a = (-LRU_C) * r * _softplus(-lam_ref[...])
    a = jnp.exp(log_a)
    mult = jnp.sqrt(1.0 - a * a)
    if pos_is_zero is not None:
        mult = jnp.where(pos_is_zero, 1.0, mult)
    return a, mult * i * c


def _pool_project(mean, u_pool, pw_ref, pscale_ref):
    d = (mean - u_pool).astype(BF16)
    half = D_POOL // 2
    y = jnp.concatenate([_dot(d[:, :half], pw_ref[0]), _dot(d[:, half:], pw_ref[1])], axis=1)
    return y * pscale_ref[...]


def _merge_out(y_pool, hs, u_gate, gpool_ref, grnn_ref, wout_ref):
    y_rnn = hs * jax.nn.gelu(u_gate)
    merged = jnp.concatenate([_rms(y_pool, gpool_ref[...]), _rms(y_rnn, grnn_ref[...])], axis=1)
    return _dot(merged.astype(BF16), wout_ref[...])


def _route(h3, wr_ref, rbias_ref, before):
    r_tok = h3.shape[0]
    logits = _dot_nt(wr_ref[...], h3, precision=lax.Precision.HIGHEST)
    scores = jax.nn.sigmoid(logits)
    biased = scores + rbias_ref[...]
    neg = jnp.float32(-jnp.inf)
    gs = []
    for g in range(N_EXPERT_GROUPS):
        xg = biased[g * GROUP_SIZE:(g + 1) * GROUP_SIZE]
        m1 = jnp.max(xg, axis=0, keepdims=True)
        eq = xg == m1
        cnt = jnp.sum(eq.astype(F32), axis=0, keepdims=True)
        m2 = jnp.max(jnp.where(eq, neg, xg), axis=0, keepdims=True)
        gs.append(m1 + jnp.where(cnt >= 2.0, m1, m2))
    pieces = []
    for g in range(N_EXPERT_GROUPS):
        beaten = jnp.zeros_like(gs[g])
        for o in range(N_EXPERT_GROUPS):
            if o == g:
                continue
            wins = (gs[o] > gs[g]) | (gs[o] == gs[g]) if o < g else (gs[o] > gs[g])
            beaten = beaten + wins.astype(F32)
        keep = beaten < float(TOPK_GROUPS)
        xg = biased[g * GROUP_SIZE:(g + 1) * GROUP_SIZE]
        pieces.append(jnp.where(keep, xg, neg))
    cur = jnp.concatenate(pieces, axis=0)
    eid = lax.broadcasted_iota(I32, (N_EXPERTS, r_tok), 0).astype(F32)
    idx_rows, score_rows = [], []
    sel = jnp.zeros((N_EXPERTS, r_tok), F32)
    for _ in range(TOP_K):
        m = jnp.max(cur, axis=0, keepdims=True)
        idx = jnp.min(jnp.where(cur == m, eid, float(N_EXPERTS)), axis=0, keepdims=True)
        oh = eid == idx
        score_rows.append(jnp.sum(jnp.where(oh, scores, 0.0), axis=0, keepdims=True))
        idx_rows.append(idx)
        sel = sel + oh.astype(F32)
        cur = jnp.where(oh, neg, cur)
    tot = score_rows[0]
    for s in score_rows[1:]:
        tot = tot + s
    w_rows = [s / tot * ROUTED_SCALE for s in score_rows]
    rr = lax.broadcasted_iota(I32, (r_tok, r_tok), 0)
    cc = lax.broadcasted_iota(I32, (r_tok, r_tok), 1)
    earlier = _dot(sel.astype(BF16), (rr < cc).astype(BF16))
    counts = jnp.sum(sel, axis=1, keepdims=True)
    run_len = jnp.floor((counts + (RUN_ALIGN - 1.0)) * (1.0 / RUN_ALIGN)) * RUN_ALIGN
    er = lax.broadcasted_iota(I32, (N_EXPERTS, N_EXPERTS), 0)
    ec = lax.broadcasted_iota(I32, (N_EXPERTS, N_EXPERTS), 1)
    run_start = _dot((ec < er).astype(BF16), jnp.broadcast_to(run_len, (N_EXPERTS, 128)).astype(BF16))[:, :1]
    slot = earlier + run_start
    slot_rows = [jnp.sum(jnp.where(eid == idx, slot, 0.0), axis=0, keepdims=True) for idx in idx_rows]
    n_lanes = _group_lanes(r_tok)
    g_row = lax.broadcasted_iota(I32, (N_EXPERTS, n_lanes), 1).astype(F32) * float(RUN_ALIGN)
    owns = (run_start <= g_row) & (g_row < run_start + run_len)
    e_col = lax.broadcasted_iota(I32, (N_EXPERTS, n_lanes), 0).astype(F32)
    g_exp = jnp.sum(jnp.where(owns, e_col, 0.0), axis=0, keepdims=True)
    g_rel = jnp.sum(jnp.where(owns, before + g_row - run_start, 0.0), axis=0, keepdims=True)
    return slot_rows, w_rows, run_len, g_exp, g_rel


def _moe_prologue(x2, gffn_ref, sg_ref, su_ref, sd_ref):
    h3 = _rms(x2, gffn_ref[...])
    h3b = h3.astype(BF16)
    act = jax.nn.silu(_dot(h3b, sg_ref[...])) * _dot(h3b, su_ref[...])
    shared = _dot(act.astype(BF16), sd_ref[...])
    return h3, x2 + shared


def _store_rows(ref, rows, dtype):
    for k, row in enumerate(rows):
        ref[k:k + 1, :] = row.astype(dtype)


def _memkv_kernel(mem_ref, g_ref, wk_ref, wv_ref, k_ref, v_ref, kb_ref, vb_ref):
    m = _rms(mem_ref[0], g_ref[...]).astype(BF16)
    k = _dot(m, wk_ref[...])
    v = _dot(m, wv_ref[...])
    k_ref[0] = k
    v_ref[0] = v
    kb_ref[0] = k.astype(BF16)
    vb_ref[0] = v.astype(BF16)


def _memkv(mem, g, wk, wv):
    b = mem.shape[0]
    blk = pl.BlockSpec((1, N_MEM, D_MODEL), lambda i: (i, 0, 0))
    return pl.pallas_call(
        _memkv_kernel,
        grid=(b,),
        in_specs=[blk, _const_spec((1, D_MODEL)), _const_spec((D_MODEL, D_MODEL)), _const_spec((D_MODEL, D_MODEL))],
        out_specs=[blk, blk, blk, blk],
        out_shape=[jax.ShapeDtypeStruct((b, N_MEM, D_MODEL), F32)] * 2
        + [jax.ShapeDtypeStruct((b, N_MEM, D_MODEL), BF16)] * 2,
        compiler_params=pltpu.CompilerParams(dimension_semantics=("arbitrary",), vmem_limit_bytes=VMEM_LIMIT),
        name="memkv",
    )(mem, g, wk, wv)


def _trunk_p_kernel(x_ref, kb_ref, vb_ref,
                    gmix_ref, win_ref, pw_ref, pscale_ref, cw_ref, cb_ref, wa_ref, ba_ref, wx_ref, bx_ref,
                    lam_ref, gpool_ref, grnn_ref, wout_ref,
                    gxa_ref, wq_ref, wo_ref,
                    gffn_ref, sg_ref, su_ref, sd_ref, wr_ref, rbias_ref,
                    xres_ref, hb_ref, slot_ref, gw_ref, gexp_ref, grel_ref, cover_ref, cnt_ref,
                    pool_ref, conv_ref, hT_ref,
                    pool_prev, conv_prev, h_prev, carry):
    b = pl.program_id(0)
    j = pl.program_id(1)
    n_j = pl.num_programs(1)

    @pl.when(j == 0)
    def _():
        pool_prev[...] = jnp.zeros_like(pool_prev)
        conv_prev[...] = jnp.zeros_like(conv_prev)
        h_prev[...] = jnp.zeros_like(h_prev)

    @pl.when((b == 0) & (j == 0))
    def _():
        carry[...] = jnp.zeros_like(carry)

    x = x_ref[0]
    row = lax.broadcasted_iota(I32, (TS, 1), 0)
    pos = j * TS + row

    h = _rms(x, gmix_ref[...]).astype(BF16)
    z = _dot(h, win_ref[...])
    u_pool = z[:, :D_POOL]
    u_rnn = z[:, D_POOL:D_POOL + D_RNN]
    u_gate = z[:, D_POOL + D_RNN:]

    ext = jnp.concatenate([pool_prev[...], u_pool], axis=0)
    means = []
    for g, w in enumerate(POOL_WINDOWS):
        s = ext[:, g * POOL_GROUP:(g + 1) * POOL_GROUP]
        k = 1
        while k < w:
            s = s + pltpu.roll(s, k, 0)
            k *= 2
        inv = 1.0 / jnp.minimum(pos + 1, w).astype(F32)
        means.append(s[HALO:] * inv)
    mean = jnp.concatenate(means, axis=1)
    y_pool = _pool_project(mean, u_pool, pw_ref, pscale_ref)

    extc = jnp.concatenate([conv_prev[...], u_rnn], axis=0)
    c = u_rnn * cw_ref[CONV_WIDTH - 1:CONV_WIDTH, :]
    for k in range(1, CONV_WIDTH):
        c = c + pltpu.roll(extc, k, 0)[CONV_HALO:] * cw_ref[CONV_WIDTH - 1 - k:CONV_WIDTH - k, :]
    c = c + cb_ref[...]

    a, bt = _gates_and_decay(c, pos == 0, wa_ref, ba_ref, wx_ref, bx_ref, lam_ref)
    k = 1
    while k < TS:
        valid = row >= k
        a_sh = jnp.where(valid, pltpu.roll(a, k, 0), 1.0)
        b_sh = jnp.where(valid, pltpu.roll(bt, k, 0), 0.0)
        bt = bt + a * b_sh
        a = a * a_sh
        k *= 2
    hs = bt + a * h_prev[...]

    pool_prev[...] = u_pool[TS - HALO:]
    conv_prev[...] = u_rnn[TS - CONV_HALO:]
    h_prev[...] = hs[TS - 1:]

    @pl.when(j == n_j - 1)
    def _():
        pool_ref[0] = u_pool[TS - POOL_BUF:]
        conv_ref[0] = u_rnn[TS - (CONV_WIDTH - 1):]
        hT_ref[0] = hs[TS - 1:]

    x1 = x + _merge_out(y_pool, hs, u_gate, gpool_ref, grnn_ref, wout_ref)

    h2 = _rms(x1, gxa_ref[...]).astype(BF16)
    q = (_dot(h2, wq_ref[...]) * (XA_HEAD_DIM ** -0.5)).astype(BF16)
    outs = []
    for hd in range(XA_HEADS):
        sl = slice(hd * XA_HEAD_DIM, (hd + 1) * XA_HEAD_DIM)
        s = _dot_nt(q[:, sl], kb_ref[0, :, sl])
        s = s - jnp.max(s, axis=-1, keepdims=True)
        p = jnp.exp(s)
        p = p / jnp.sum(p, axis=-1, keepdims=True)
        outs.append(_dot(p.astype(BF16), vb_ref[0, :, sl]))
    o = jnp.concatenate(outs, axis=1).astype(BF16)
    x2 = x1 + _dot(o, wo_ref[...])

    h3, xres = _moe_prologue(x2, gffn_ref, sg_ref, su_ref, sd_ref)
    xres_ref[...] = xres
    hb_ref[...] = h3.astype(BF16)
    slot_rows, w_rows, run_len, g_exp, g_rel = _route(h3, wr_ref, rbias_ref, carry[...])
    _store_rows(slot_ref, slot_rows, I32)
    _store_rows(gw_ref, w_rows, F32)
    gexp_ref[0] = g_exp.astype(I32)
    grel_ref[0] = g_rel.astype(I32)
    cover_ref[0] = jnp.broadcast_to(jnp.sum(run_len, axis=0, keepdims=True), cover_ref.shape[1:])
    carry[...] = carry[...] + run_len
    cnt_ref[...] = jnp.broadcast_to(carry[...], cnt_ref.shape)


def _trunk_p(x, kb, vb, mixw, xaw, moew):
    bsz, seq, _ = x.shape
    n_j = seq // TS
    t = bsz * seq
    tok = lambda b, j: (b * n_j + j, 0)
    lane_tok = lambda b, j: (0, b * n_j + j)
    per_b = lambda b, j: (b, 0, 0)
    weights = list(mixw) + list(xaw) + list(moew)
    in_specs = [pl.BlockSpec((1, TS, D_MODEL), lambda b, j: (b, j, 0)),
                pl.BlockSpec((1, N_MEM, D_MODEL), per_b),
                pl.BlockSpec((1, N_MEM, D_MODEL), per_b)] + [_const_spec(w.shape) for w in weights]
    per_tile = lambda b, j: (b * n_j + j, 0, 0)
    out_shape = [jax.ShapeDtypeStruct((t, D_MODEL), F32),
                 jax.ShapeDtypeStruct((t, D_MODEL), BF16),
                 jax.ShapeDtypeStruct((TOP_K, t), I32),
                 jax.ShapeDtypeStruct((TOP_K, t), F32),
                 jax.ShapeDtypeStruct((bsz * n_j, 1, _group_lanes(TS)), I32),
                 jax.ShapeDtypeStruct((bsz * n_j, 1, _group_lanes(TS)), I32),
                 jax.ShapeDtypeStruct((bsz * n_j, 1, 128), F32),
                 jax.ShapeDtypeStruct((N_EXPERTS, 128), F32),
                 jax.ShapeDtypeStruct((bsz, POOL_BUF, D_POOL), F32),
                 jax.ShapeDtypeStruct((bsz, CONV_WIDTH - 1, D_RNN), F32),
                 jax.ShapeDtypeStruct((bsz, 1, D_RNN), F32)]
    out_specs = [pl.BlockSpec((TS, D_MODEL), tok),
                 pl.BlockSpec((TS, D_MODEL), tok),
                 pl.BlockSpec((TOP_K, TS), lane_tok),
                 pl.BlockSpec((TOP_K, TS), lane_tok),
                 pl.BlockSpec((1, 1, _group_lanes(TS)), per_tile),
                 pl.BlockSpec((1, 1, _group_lanes(TS)), per_tile),
                 pl.BlockSpec((1, 1, 128), per_tile),
                 pl.BlockSpec((N_EXPERTS, 128), lambda b, j: (0, 0)),
                 pl.BlockSpec((1, POOL_BUF, D_POOL), per_b),
                 pl.BlockSpec((1, CONV_WIDTH - 1, D_RNN), per_b),
                 pl.BlockSpec((1, 1, D_RNN), per_b)]
    return pl.pallas_call(
        _trunk_p_kernel,
        grid=(bsz, n_j),
        in_specs=in_specs,
        out_specs=out_specs,
        out_shape=out_shape,
        scratch_shapes=[pltpu.VMEM((HALO, D_POOL), F32), pltpu.VMEM((CONV_HALO, D_RNN), F32),
                        pltpu.VMEM((1, D_RNN), F32), pltpu.VMEM((N_EXPERTS, 1), F32)],
        compiler_params=pltpu.CompilerParams(dimension_semantics=("arbitrary", "arbitrary"),
                                             vmem_limit_bytes=VMEM_LIMIT),
        name="trunk_p",
    )(x, kb, vb, *weights)


def _mix_s_kernel(x_ref, pool_ref, conv_ref, h0_ref,
                  gmix_ref, win_ref, pw_ref, pscale_ref, cw_ref, cb_ref, wa_ref, ba_ref, wx_ref, bx_ref,
                  lam_ref, gpool_ref, grnn_ref, wout_ref, gxa_ref, wq_ref,
                  x1_ref, q_ref, npool_ref, nconv_ref, nh_ref):
    x = x_ref[...]
    h = _rms(x, gmix_ref[...]).astype(BF16)
    z = _dot(h, win_ref[...])
    u_pool = z[:, :D_POOL]
    u_rnn = z[:, D_POOL:D_POOL + D_RNN]
    u_gate = z[:, D_POOL + D_RNN:]

    means = []
    for g, w in enumerate(POOL_WINDOWS):
        sl = slice(g * POOL_GROUP, (g + 1) * POOL_GROUP)
        s = u_pool[:, sl]
        for k in range(1, w):
            s = s + pool_ref[POOL_BUF - k, :, sl]
        means.append(s * (1.0 / min(w, PAST_LEN + 1)))
    mean = jnp.concatenate(means, axis=1)
    y_pool = _pool_project(mean, u_pool, pw_ref, pscale_ref)

    c = u_rnn * cw_ref[CONV_WIDTH - 1:CONV_WIDTH, :]
    for k in range(1, CONV_WIDTH):
        c = c + conv_ref[CONV_WIDTH - 1 - k] * cw_ref[CONV_WIDTH - 1 - k:CONV_WIDTH - k, :]
    c = c + cb_ref[...]
    a, bt = _gates_and_decay(c, None, wa_ref, ba_ref, wx_ref, bx_ref, lam_ref)
    hs = a * h0_ref[...] + bt

    x1 = x + _merge_out(y_pool, hs, u_gate, gpool_ref, grnn_ref, wout_ref)
    x1_ref[...] = x1
    h2 = _rms(x1, gxa_ref[...]).astype(BF16)
    q_ref[...] = _dot(h2, wq_ref[...]) * (XA_HEAD_DIM ** -0.5)

    npool_ref[:POOL_BUF - 1] = pool_ref[1:]
    npool_ref[POOL_BUF - 1] = u_pool
    nconv_ref[:CONV_WIDTH - 2] = conv_ref[1:]
    nconv_ref[CONV_WIDTH - 2] = u_rnn
    nh_ref[...] = hs


def _mix_s(x, pool, conv, h0, mixw, gxa, wq):
    bsz = x.shape[0]
    args = [x, pool, conv, h0] + list(mixw) + [gxa, wq]
    return pl.pallas_call(
        _mix_s_kernel,
        grid=(1,),
        in_specs=[_const_spec(a.shape) for a in args],
        out_specs=[_const_spec((bsz, D_MODEL)), _const_spec((bsz, D_MODEL)), _const_spec(pool.shape),
                   _const_spec(conv.shape), _const_spec((bsz, D_RNN))],
        out_shape=[jax.ShapeDtypeStruct((bsz, D_MODEL), F32), jax.ShapeDtypeStruct((bsz, D_MODEL), F32),
                   jax.ShapeDtypeStruct(pool.shape, F32), jax.ShapeDtypeStruct(conv.shape, F32),
                   jax.ShapeDtypeStruct((bsz, D_RNN), F32)],
        compiler_params=pltpu.CompilerParams(dimension_semantics=("arbitrary",), vmem_limit_bytes=VMEM_LIMIT),
        name="mix_s",
    )(*args)


ATTN_S_BB = 4


def _attn_s_kernel(q_ref, k_ref, v_ref, o_ref):
    q = q_ref[...][:, None]
    s = jnp.sum(k_ref[...] * q, axis=-1, keepdims=True)
    s = s - jnp.max(s, axis=1, keepdims=True)
    p = jnp.exp(s)
    p = p / jnp.sum(p, axis=1, keepdims=True)
    o_ref[...] = jnp.sum(p * v_ref[...], axis=1)


def _attn_s(q, k, v):
    bsz = q.shape[0]
    kv_spec = pl.BlockSpec((ATTN_S_BB, N_MEM, XA_HEADS, XA_HEAD_DIM), lambda i: (i, 0, 0, 0))
    q_spec = pl.BlockSpec((ATTN_S_BB, XA_HEADS, XA_HEAD_DIM), lambda i: (i, 0, 0))
    o = pl.pallas_call(
        _attn_s_kernel,
        grid=(bsz // ATTN_S_BB,),
        in_specs=[q_spec, kv_spec, kv_spec],
        out_specs=q_spec,
        out_shape=jax.ShapeDtypeStruct((bsz, XA_HEADS, XA_HEAD_DIM), F32),
        compiler_params=pltpu.CompilerParams(dimension_semantics=("arbitrary",), vmem_limit_bytes=VMEM_LIMIT),
        name="attn_s",
    )(q.reshape(bsz, XA_HEADS, XA_HEAD_DIM), k, v)
    return o.reshape(bsz, D_MODEL)


def _post_s_kernel(x1_ref, o_ref, wo_ref, gffn_ref, sg_ref, su_ref, sd_ref, wr_ref, rbias_ref,
                   xres_ref, hb_ref, slot_ref, gw_ref, gexp_ref, grel_ref, cover_ref, cnt_ref):
    x2 = x1_ref[...] + _dot(o_ref[...].astype(BF16), wo_ref[...])
    h3, xres = _moe_prologue(x2, gffn_ref, sg_ref, su_ref, sd_ref)
    xres_ref[...] = xres
    hb_ref[...] = h3.astype(BF16)
    slot_rows, w_rows, run_len, g_exp, g_rel = _route(h3, wr_ref, rbias_ref, jnp.zeros((N_EXPERTS, 1), F32))
    _store_rows(slot_ref, slot_rows, I32)
    _store_rows(gw_ref, w_rows, F32)
    gexp_ref[0] = g_exp.astype(I32)
    grel_ref[0] = g_rel.astype(I32)
    cover_ref[0] = jnp.broadcast_to(jnp.sum(run_len, axis=0, keepdims=True), cover_ref.shape[1:])
    cnt_ref[...] = jnp.broadcast_to(run_len, cnt_ref.shape)


def _post_s(x1, o, wo, moew):
    bsz = x1.shape[0]
    args = [x1, o, wo] + list(moew)
    out_shape = [jax.ShapeDtypeStruct((bsz, D_MODEL), F32),
                 jax.ShapeDtypeStruct((bsz, D_MODEL), BF16),
                 jax.ShapeDtypeStruct((TOP_K, bsz), I32),
                 jax.ShapeDtypeStruct((TOP_K, bsz), F32),
                 jax.ShapeDtypeStruct((1, 1, _group_lanes(bsz)), I32),
                 jax.ShapeDtypeStruct((1, 1, _group_lanes(bsz)), I32),
                 jax.ShapeDtypeStruct((1, 1, 128), F32),
                 jax.ShapeDtypeStruct((N_EXPERTS, 128), F32)]
    return pl.pallas_call(
        _post_s_kernel,
        grid=(1,),
        in_specs=[_const_spec(a.shape) for a in args],
        out_specs=[_const_spec(s.shape) for s in out_shape],
        out_shape=out_shape,
        compiler_params=pltpu.CompilerParams(dimension_semantics=("arbitrary",), vmem_limit_bytes=VMEM_LIMIT),
        name="post_s",
    )(*args)


def _plan_kernel(rp_ref, rs_ref, base_p_ref, base_s_ref, fill_at_ref, fill_n_ref, exp_ref, nreal_ref):
    n_blocks = exp_ref.shape[0]
    shift = BM.bit_length() - 1

    def per_expert(e, carry):
        blk0, last_e = carry
        rows = rp_ref[e] + rs_ref[e]
        start = lax.shift_left(blk0, shift)
        n_blk = lax.shift_right_logical(rows + (BM - 1), shift)
        base_p_ref[e] = start
        base_s_ref[e] = start + rp_ref[e]
        fill_at_ref[e] = start + rows
        fill_n_ref[e] = _groups(lax.shift_left(n_blk, shift) - rows)

        def per_block(j, _):
            exp_ref[blk0 + j] = e
            return _

        lax.fori_loop(0, n_blk, per_block, 0)
        return blk0 + n_blk, jnp.where(rows > 0, e, last_e)

    n_real, last_e = lax.fori_loop(0, N_EXPERTS, per_expert, (jnp.int32(0), jnp.int32(0)))
    nreal_ref[0] = n_real

    def rest(b, _):
        exp_ref[b] = last_e
        return _

    lax.fori_loop(n_real, n_blocks, rest, 0)


def _plan(rows_p, rows_s, n_rows):
    assert BM & (BM - 1) == 0 and n_rows % BM == 0
    smem = pl.BlockSpec(memory_space=pltpu.SMEM)
    return pl.pallas_call(
        _plan_kernel,
        in_specs=[smem, smem],
        out_specs=[smem] * 6,
        out_shape=[jax.ShapeDtypeStruct((N_EXPERTS,), I32)] * 4
        + [jax.ShapeDtypeStruct((n_rows // BM,), I32), jax.ShapeDtypeStruct((1,), I32)],
        name="plan",
    )(rows_p, rows_s)


def _groups(n_rows):
    return lax.shift_right_logical(n_rows, RUN_ALIGN.bit_length() - 1)


def _for_each_group(n_groups, gexp_ref, grel_ref, base_ref, fn):
    def body(g, _):
        glob = base_ref[gexp_ref[0, 0, g]] + grel_ref[0, 0, g]
        fn(pl.multiple_of(g * RUN_ALIGN, RUN_ALIGN), pl.multiple_of(glob, RUN_ALIGN))
        return _

    lax.fori_loop(0, n_groups, body, 0)


def _dispatch_kernel(cover_p, base_p, cover_s, base_s, fill_at, fill_n, nreal_ref,
                     gexp_p, grel_p, gexp_s, grel_s,
                     slot_p_ref, h_p_ref, slot_s_ref, h_s_ref, xs_ref, sbuf, zbuf, sem):
    i = pl.program_id(0)
    last = pl.num_programs(0) - 1
    cur = lax.rem(i, 2)

    def drain(s, n_rows):
        def body(g, _):
            pltpu.make_async_copy(sbuf.at[s, pl.ds(0, RUN_ALIGN)], xs_ref.at[pl.ds(0, RUN_ALIGN)], sem.at[s]).wait()
            return _
        lax.fori_loop(0, _groups(n_rows), body, 0)

    def tile(n_rows, gexp_ref, grel_ref, base_ref, slot_ref, h_ref):
        h = h_ref[...]
        for c in range(_sorted_rows(h.shape[0]) // SORT_CHUNK):
            rid = c * SORT_CHUNK + lax.broadcasted_iota(I32, (SORT_CHUNK, 1), 0)
            hit = rid == slot_ref[0:1, :]
            for k in range(1, TOP_K):
                hit = hit | (rid == slot_ref[k:k + 1, :])
            sbuf[cur, c * SORT_CHUNK:(c + 1) * SORT_CHUNK] = _dot(hit.astype(BF16), h).astype(BF16)

        def send(local, glob):
            pltpu.make_async_copy(sbuf.at[cur, pl.ds(local, RUN_ALIGN)], xs_ref.at[pl.ds(glob, RUN_ALIGN)],
                                  sem.at[cur]).start()

        _for_each_group(_groups(n_rows), gexp_ref, grel_ref, base_ref, send)

    @pl.when(i >= 2)
    def _():
        drain(cur, cover_p[i - 2])

    @pl.when(i < last)
    def _():
        tile(cover_p[i], gexp_p, grel_p, base_p, slot_p_ref, h_p_ref)

    @pl.when(i == last)
    def _():
        tile(cover_s[0], gexp_s, grel_s, base_s, slot_s_ref, h_s_ref)
        drain(1 - cur, cover_p[last - 1])
        drain(cur, cover_s[0])
        zbuf[...] = jnp.zeros_like(zbuf)

        def group_fill(e, g):
            return pltpu.make_async_copy(
                zbuf.at[pl.ds(0, RUN_ALIGN)],
                xs_ref.at[pl.ds(pl.multiple_of(fill_at[e] + g * RUN_ALIGN, RUN_ALIGN), RUN_ALIGN)], sem.at[2])

        def per_expert(e, n):
            def start(g, _):
                group_fill(e, g).start()
                return _
            lax.fori_loop(0, fill_n[e], start, 0)
            return n + fill_n[e]

        n_fill = lax.fori_loop(0, N_EXPERTS, per_expert, jnp.int32(0))

        def wait_group(g, _):
            group_fill(0, 0).wait()
            return _

        lax.fori_loop(0, n_fill, wait_group, 0)

        n_real = nreal_ref[0]
        n_tail = xs_ref.shape[0] // BM - n_real

        def blk_fill(b):
            return pltpu.make_async_copy(zbuf, xs_ref.at[pl.ds(pl.multiple_of((n_real + b) * BM, BM), BM)], sem.at[2])

        def start_blk(b, _):
            blk_fill(b).start()
            return _

        def wait_blk(b, _):
            blk_fill(b).wait()
            return _

        lax.fori_loop(0, n_tail, start_blk, 0)
        lax.fori_loop(0, n_tail, wait_blk, 0)


def _dispatch(tab_p, tab_s, fills, gtab_p, gtab_s, slot_p, hb_p, slot_s, hb_s, n_rows):
    n_p = hb_p.shape[0] // TS
    smem = pl.BlockSpec(memory_space=pltpu.SMEM)
    clamp = lambda i: jnp.minimum(i, n_p - 1)
    per_tile = pl.BlockSpec((1, 1, gtab_p[0].shape[-1]), lambda i: (clamp(i), 0, 0), memory_space=pltpu.SMEM)
    return pl.pallas_call(
        _dispatch_kernel,
        grid=(n_p + 1,),
        in_specs=[smem] * 7 + [per_tile, per_tile, smem, smem,
                               pl.BlockSpec((TOP_K, TS), lambda i: (0, clamp(i))),
                               pl.BlockSpec((TS, D_MODEL), lambda i: (clamp(i), 0)),
                               _const_spec(slot_s.shape), _const_spec(hb_s.shape)],
        out_specs=pl.BlockSpec(memory_space=pl.ANY),
        out_shape=jax.ShapeDtypeStruct((n_rows, D_MODEL), BF16),
        scratch_shapes=[pltpu.VMEM((2, _sorted_rows(TS), D_MODEL), BF16),
                        pltpu.VMEM((BM, D_MODEL), BF16), pltpu.SemaphoreType.DMA((3,))],
        compiler_params=pltpu.CompilerParams(dimension_semantics=("arbitrary",), has_side_effects=True,
                                             vmem_limit_bytes=VMEM_LIMIT),
        name="dispatch",
    )(*tab_p, *tab_s, *fills, *gtab_p, *gtab_s, slot_p, hb_p, slot_s, hb_s)


def _gmm_kernel(exp_ref, nreal_ref, x_ref, wg_ref, wu_ref, wd_ref, y_ref, wgu, wdn):
    b = pl.program_id(0)
    real = b < nreal_ref[0]
    new_expert = (b == 0) | (exp_ref[b] != exp_ref[jnp.maximum(b - 1, 0)])

    @pl.when(real & new_expert)
    def _():
        wgu[:, :D_EXPERT] = wg_ref[0].astype(BF16)
        wgu[:, D_EXPERT:] = wu_ref[0].astype(BF16)
        wdn[...] = wd_ref[0].astype(BF16)

    @pl.when(real)
    def _():
        gu = _dot(x_ref[...], wgu[...])
        act = jax.nn.silu(gu[:, :D_EXPERT]) * gu[:, D_EXPERT:]
        y_ref[...] = _dot(act.astype(BF16), wdn[...]).astype(BF16)

    @pl.when(jnp.logical_not(real))
    def _():
        y_ref[...] = jnp.zeros_like(y_ref)


def _gmm(blk_exp, n_real, xs, wg, wu, wd):
    n_rows = xs.shape[0]
    weight = lambda b, exp, nreal: (exp[b], 0, 0)
    grid_spec = pltpu.PrefetchScalarGridSpec(
        num_scalar_prefetch=2,
        grid=(n_rows // BM,),
        in_specs=[pl.BlockSpec((BM, D_MODEL), lambda b, exp, nreal: (jnp.minimum(b, nreal[0] - 1), 0)),
                  pl.BlockSpec((1, D_MODEL, D_EXPERT), weight),
                  pl.BlockSpec((1, D_MODEL, D_EXPERT), weight),
                  pl.BlockSpec((1, D_EXPERT, D_MODEL), weight)],
        out_specs=pl.BlockSpec((BM, D_MODEL), lambda b, exp, nreal: (b, 0)),
        scratch_shapes=[pltpu.VMEM((D_MODEL, 2 * D_EXPERT), BF16), pltpu.VMEM((D_EXPERT, D_MODEL), BF16)],
    )
    return pl.pallas_call(
        _gmm_kernel,
        grid_spec=grid_spec,
        out_shape=jax.ShapeDtypeStruct((n_rows, D_MODEL), BF16),
        compiler_params=pltpu.CompilerParams(dimension_semantics=("arbitrary",), vmem_limit_bytes=VMEM_LIMIT),
        name="gmm",
    )(blk_exp, n_real, xs, wg, wu, wd)


def _combine_kernel(cover_ref, base_ref, gexp_ref, grel_ref, gexp_next_ref, grel_next_ref,
                    slot_ref, gw_ref, xres_ref, gfin_ref, ys_ref, out_ref, ybuf, sem):
    i = pl.program_id(0)
    n = pl.num_programs(0)
    cur = lax.rem(i, 2)
    rows = ybuf.shape[1]

    def group_copy(s, local, glob):
        return pltpu.make_async_copy(ys_ref.at[pl.ds(glob, RUN_ALIGN)], ybuf.at[s, pl.ds(local, RUN_ALIGN)],
                                     sem.at[s])

    def gather(n_rows, ge_ref, gr_ref, s):
        _for_each_group(_groups(n_rows), ge_ref, gr_ref, base_ref, lambda l, g: group_copy(s, l, g).start())

    @pl.when(i == 0)
    def _():
        ybuf[...] = jnp.zeros_like(ybuf)
        gather(cover_ref[0], gexp_ref, grel_ref, 0)

    @pl.when(i + 1 < n)
    def _():
        gather(cover_ref[jnp.minimum(i + 1, n - 1)], gexp_next_ref, grel_next_ref, 1 - cur)

    def drain(g, _):
        group_copy(cur, 0, 0).wait()
        return _

    lax.fori_loop(0, _groups(cover_ref[i]), drain, 0)

    acc = xres_ref[...]
    slot = slot_ref[...]
    gw = gw_ref[...]
    for c in range(rows // SORT_CHUNK):
        col = c * SORT_CHUNK + lax.broadcasted_iota(I32, (1, SORT_CHUNK), 1)
        w = jnp.zeros((slot.shape[0], SORT_CHUNK), F32)
        for k in range(TOP_K):
            w = w + jnp.where(slot[:, k:k + 1] == col, gw[:, k:k + 1], 0.0)
        w_hi = w.astype(BF16)
        w_lo = (w - w_hi.astype(F32)).astype(BF16)
        y = ybuf[cur, c * SORT_CHUNK:(c + 1) * SORT_CHUNK]
        acc = acc + _dot(w_hi, y) + _dot(w_lo, y)
    out_ref[...] = _rms(acc, gfin_ref[...])


def _combine(tab, gtab, slot_t, gw_t, xres, gfin, ys):
    t = xres.shape[0]
    tm = min(t, TS)
    n = t // tm
    smem = pl.BlockSpec(memory_space=pltpu.SMEM)
    lanes = gtab[0].shape[-1]
    this_tile = pl.BlockSpec((1, 1, lanes), lambda i: (i, 0, 0), memory_space=pltpu.SMEM)
    next_tile = pl.BlockSpec((1, 1, lanes), lambda i: (jnp.minimum(i + 1, n - 1), 0, 0), memory_space=pltpu.SMEM)
    return pl.pallas_call(
        _combine_kernel,
        grid=(n,),
        in_specs=[smem, smem, this_tile, this_tile, next_tile, next_tile,
                  pl.BlockSpec((tm, TOP_K), lambda i: (i, 0)),
                  pl.BlockSpec((tm, TOP_K), lambda i: (i, 0)),
                  pl.BlockSpec((tm, D_MODEL), lambda i: (i, 0)),
                  _const_spec((1, D_MODEL)),
                  pl.BlockSpec(memory_space=pl.ANY)],
        out_specs=pl.BlockSpec((tm, D_MODEL), lambda i: (i, 0)),
        out_shape=jax.ShapeDtypeStruct((t, D_MODEL), F32),
        scratch_shapes=[pltpu.VMEM((2, _sorted_rows(tm), D_MODEL), BF16), pltpu.SemaphoreType.DMA((2,))],
        compiler_params=pltpu.CompilerParams(dimension_semantics=("arbitrary",), vmem_limit_bytes=VMEM_LIMIT),
        name="combine",
    )(*tab, *gtab, *gtab, slot_t, gw_t, xres, gfin, ys)


def _block_diag_pairs(w):
    n_h, d, _ = w.shape
    half = n_h // 2
    out = jnp.zeros((2, half * d, half * d), F32)
    for hh in range(n_h):
        p, q = divmod(hh, half)
        out = out.at[p, q * d:(q + 1) * d, q * d:(q + 1) * d].set(w[hh])
    return out.astype(BF16)


def kernel(x_prompt, x_sample, state_pool, state_conv, state_h, cache_mem_k, cache_mem_v, mem_prompt, norm_mix, w_in, pool_w, pool_scale, conv_w, conv_b, gate_a_w, gate_a_b, gate_x_w, gate_x_b, lru_lambda, norm_pool_out, norm_rnn_out, w_out, norm_xattn, norm_mem, xa_wq, xa_wk, xa_wv, xa_wo, norm_ffn, router_w, router_bias, exp_w_gate, exp_w_up, exp_w_down, sh_w_gate, sh_w_up, sh_w_down, norm_final):
    bp, seq, _ = x_prompt.shape
    bs = x_sample.shape[0]
    tp = bp * seq
    row = lambda v: v.reshape(1, -1)
    bf = lambda v: v.astype(BF16)

    mixw = (row(norm_mix[0]), bf(w_in[0]), _block_diag_pairs(pool_w[0]), row(pool_scale[0]), conv_w[0],
            row(conv_b[0]), _block_diag_pairs(gate_a_w[0]), row(gate_a_b[0]), _block_diag_pairs(gate_x_w[0]),
            row(gate_x_b[0]), row(lru_lambda[0]), row(norm_pool_out[0]), row(norm_rnn_out[0]), bf(w_out[0]))
    xaw = (row(norm_xattn[0]), bf(xa_wq[0]), bf(xa_wo[0]))
    moew = (row(norm_ffn[0]), bf(sh_w_gate[0]), bf(sh_w_up[0]), bf(sh_w_down[0]), router_w[0].T,
            router_bias[0].reshape(N_EXPERTS, 1))

    mk, mv, kb, vb = _memkv(mem_prompt, row(norm_mem[0]), bf(xa_wk[0]), bf(xa_wv[0]))
    (xres_p, hb_p, slot_p, gw_p, gexp_p, grel_p, cover_p, cnt_p, pool_p, conv_p, h_p) = _trunk_p(
        x_prompt, kb, vb, mixw, xaw, moew)

    x1_s, q_s, pool_s, conv_s, h_s = _mix_s(x_sample.reshape(bs, D_MODEL), state_pool[0].transpose(1, 0, 2),
                                            state_conv[0].transpose(1, 0, 2), state_h[0], mixw, xaw[0], xaw[1])
    pool_s = pool_s.transpose(1, 0, 2)
    conv_s = conv_s.transpose(1, 0, 2)
    o_s = _attn_s(q_s, cache_mem_k[0], cache_mem_v[0])
    xres_s, hb_s, slot_s, gw_s, gexp_s, grel_s, cover_s, cnt_s = _post_s(x1_s, o_s, xaw[2], moew)

    n_tiles = tp // TS + 1
    n_rows = _round_up((tp + bs) * TOP_K + n_tiles * N_EXPERTS * (RUN_ALIGN - 1) + N_EXPERTS * (BM - 1), BM)
    ints = lambda v: v[..., 0].astype(I32)
    base_p, base_s, fill_at, fill_n, blk_exp, n_real = _plan(ints(cnt_p), ints(cnt_s), n_rows)
    tab_p = (ints(cover_p).reshape(-1), base_p)
    tab_s = (ints(cover_s).reshape(-1), base_s)
    gtab_p = (gexp_p, grel_p)
    gtab_s = (gexp_s, grel_s)

    xs = _dispatch(tab_p, tab_s, (fill_at, fill_n, n_real), gtab_p, gtab_s, slot_p, hb_p, slot_s, hb_s, n_rows)
    ys = _gmm(blk_exp, n_real, xs, exp_w_gate[0], exp_w_up[0], exp_w_down[0])

    gfin = row(norm_final)
    y_p = _combine(tab_p, gtab_p, slot_p.T, gw_p.T, xres_p, gfin, ys)
    y_s = _combine(tab_s, gtab_s, slot_s.T, gw_s.T, xres_s, gfin, ys)

    return (y_p.reshape(bp, seq, D_MODEL), y_s.reshape(bs, 1, D_MODEL),
            pool_p[None], conv_p[None], h_p.reshape(1, bp, D_RNN),
            mk.reshape(1, bp, N_MEM, XA_HEADS, XA_HEAD_DIM), mv.reshape(1, bp, N_MEM, XA_HEADS, XA_HEAD_DIM),
            pool_s[None], conv_s[None], h_s[None])
```

```python
import functools

import jax
import jax.numpy as jnp
from jax import lax
from jax.experimental import pallas as pl
from jax.experimental.pallas import tpu as pltpu

F32 = jnp.float32
BF16 = jnp.bfloat16
I32 = jnp.int32
U32 = jnp.uint32

D_MODEL = 1024
D_POOL = 512
D_RNN = 512
D_IN = D_POOL + 2 * D_RNN
POOL_WINDOWS = (2, 4, 8, 16)
POOL_GROUP = 128
POOL_BUF = 15
CONV_WIDTH = 4
LRU_C = 8.0
N_MEM = 256
XA_HEADS = 4
XA_HEAD_DIM = 256
N_EXPERTS = 64
TOP_K = 8
N_EXPERT_GROUPS = 8
GROUP_SIZE = N_EXPERTS // N_EXPERT_GROUPS
TOPK_GROUPS = 4
D_EXPERT = 256
ROUTED_SCALE = 2.5
EPS = 1e-6
PAST_LEN = 16384

HALO = 16
CONV_HALO = 8
TS = 256
BM = 256
RUN_ALIGN = 16
RUN_CHUNKS = (64, 32, 16)
SORT_CHUNK = 512
VMEM_LIMIT = 56 * 1024 * 1024


def _round_up(x, m):
    return (x + m - 1) // m * m


def _sorted_rows(tokens):
    return _round_up(tokens * TOP_K + N_EXPERTS * (RUN_ALIGN - 1), SORT_CHUNK)


def _group_lanes(tokens):
    return _round_up(_sorted_rows(tokens) // RUN_ALIGN, 128)


def _const_spec(shape):
    nd = len(shape)
    return pl.BlockSpec(shape, lambda *_: (0,) * nd, pipeline_mode=pl.Buffered(1))


def _rms(x, g):
    ms = jnp.mean(x * x, axis=-1, keepdims=True)
    return x * lax.rsqrt(ms + EPS) * g


def _dot(a, b):
    return jnp.dot(a, b, preferred_element_type=F32)


def _dot_nt(a, b, precision=None):
    return lax.dot_general(a, b, (((1,), (1,)), ((), ())), precision=precision,
                           preferred_element_type=F32)


def _softplus(x):
    return jnp.maximum(x, 0.0) + jnp.log1p(jnp.exp(-jnp.abs(x)))


def _gates_and_decay(c, pos_is_zero, wa_ref, ba_ref, wx_ref, bx_ref, lam_ref):
    cb = c.astype(BF16)
    half = D_RNN // 2
    ga = jnp.concatenate([_dot(cb[:, :half], wa_ref[0]), _dot(cb[:, half:], wa_ref[1])], axis=1) + ba_ref[...]
    gx = jnp.concatenate([_dot(cb[:, :half], wx_ref[0]), _dot(cb[:, half:], wx_ref[1])], axis=1) + bx_ref[...]
    r = jax.nn.sigmoid(ga)
    i = jax.nn.sigmoid(gx)
    log_a = (-LRU_C) * r * _softplus(-lam_ref[...])
    a = jnp.exp(log_a)
    mult = jnp.sqrt(1.0 - a * a)
    if pos_is_zero is not None:
        mult = jnp.where(pos_is_zero, 1.0, mult)
    return a, mult * i * c


def _pool_project(mean, u_pool, pw_ref, pscale_ref):
    d = (mean - u_pool).astype(BF16)
    half = D_POOL // 2
    y = jnp.concatenate([_dot(d[:, :half], pw_ref[0]), _dot(d[:, half:], pw_ref[1])], axis=1)
    return y * pscale_ref[...]


def _merge_out(y_pool, hs, u_gate, gpool_ref, grnn_ref, wout_ref):
    y_rnn = hs * jax.nn.gelu(u_gate)
    merged = jnp.concatenate([_rms(y_pool, gpool_ref[...]), _rms(y_rnn, grnn_ref[...])], axis=1)
    return _dot(merged.astype(BF16), wout_ref[...])


def _route(h3, wr_ref, rbias_ref, before):
    r_tok = h3.shape[0]
    logits = _dot_nt(wr_ref[...], h3, precision=lax.Precision.HIGHEST)
    scores = jax.nn.sigmoid(logits)
    biased = scores + rbias_ref[...]
    neg = jnp.float32(-jnp.inf)
    gs = []
    for g in range(N_EXPERT_GROUPS):
        xg = biased[g * GROUP_SIZE:(g + 1) * GROUP_SIZE]
        m1 = jnp.max(xg, axis=0, keepdims=True)
        eq = xg == m1
        cnt = jnp.sum(eq.astype(F32), axis=0, keepdims=True)
        m2 = jnp.max(jnp.where(eq, neg, xg), axis=0, keepdims=True)
        gs.append(m1 + jnp.where(cnt >= 2.0, m1, m2))
    pieces = []
    for g in range(N_EXPERT_GROUPS):
        beaten = jnp.zeros_like(gs[g])
        for o in range(N_EXPERT_GROUPS):
            if o == g:
                continue
            wins = (gs[o] > gs[g]) | (gs[o] == gs[g]) if o < g else (gs[o] > gs[g])
            beaten = beaten + wins.astype(F32)
        keep = beaten < float(TOPK_GROUPS)
        xg = biased[g * GROUP_SIZE:(g + 1) * GROUP_SIZE]
        pieces.append(jnp.where(keep, xg, neg))
    cur = jnp.concatenate(pieces, axis=0)
    eid = lax.broadcasted_iota(I32, (N_EXPERTS, r_tok), 0).astype(F32)
    idx_rows, score_rows = [], []
    sel = jnp.zeros((N_EXPERTS, r_tok), F32)
    for _ in range(TOP_K):
        m = jnp.max(cur, axis=0, keepdims=True)
        idx = jnp.min(jnp.where(cur == m, eid, float(N_EXPERTS)), axis=0, keepdims=True)
        oh = eid == idx
        score_rows.append(jnp.sum(jnp.where(oh, scores, 0.0), axis=0, keepdims=True))
        idx_rows.append(idx)
        sel = sel + oh.astype(F32)
        cur = jnp.where(oh, neg, cur)
    tot = score_rows[0]
    for s in score_rows[1:]:
        tot = tot + s
    w_rows = [s / tot * ROUTED_SCALE for s in score_rows]
    rr = lax.broadcasted_iota(I32, (r_tok, r_tok), 0)
    cc = lax.broadcasted_iota(I32, (r_tok, r_tok), 1)
    earlier = _dot(sel.astype(BF16), (rr < cc).astype(BF16))
    counts = jnp.sum(sel, axis=1, keepdims=True)
    run_len = jnp.floor((counts + (RUN_ALIGN - 1.0)) * (1.0 / RUN_ALIGN)) * RUN_ALIGN
    er = lax.broadcasted_iota(I32, (N_EXPERTS, N_EXPERTS), 0)
    ec = lax.broadcasted_iota(I32, (N_EXPERTS, N_EXPERTS), 1)
    run_start = _dot((ec < er).astype(BF16), jnp.broadcast_to(run_len, (N_EXPERTS, 128)).astype(BF16))[:, :1]
    slot = earlier + run_start
    slot_rows = [jnp.sum(jnp.where(eid == idx, slot, 0.0), axis=0, keepdims=True) for idx in idx_rows]
    n_lanes = _group_lanes(r_tok)
    g_row = lax.broadcasted_iota(I32, (N_EXPERTS, n_lanes), 1).astype(F32) * float(RUN_ALIGN)
    owns = (run_start <= g_row) & (g_row < run_start + run_len)
    e_col = lax.broadcasted_iota(I32, (N_EXPERTS, n_lanes), 0).astype(F32)
    g_exp = jnp.sum(jnp.where(owns, e_col, 0.0), axis=0, keepdims=True)
    g_rel = jnp.sum(jnp.where(owns, before + g_row - run_start, 0.0), axis=0, keepdims=True)
    return slot_rows, w_rows, run_len, g_exp, g_rel


def _moe_prologue(x2, gffn_ref, sg_ref, su_ref, sd_ref):
    h3 = _rms(x2, gffn_ref[...])
    h3b = h3.astype(BF16)
    act = jax.nn.silu(_dot(h3b, sg_ref[...])) * _dot(h3b, su_ref[...])
    shared = _dot(act.astype(BF16), sd_ref[...])
    return h3, x2 + shared


def _store_rows(ref, rows, dtype):
    for k, row in enumerate(rows):
        ref[k:k + 1, :] = row.astype(dtype)


def _memkv_kernel(mem_ref, g_ref, wk_ref, wv_ref, k_ref, v_ref, kb_ref, vb_ref):
    m = _rms(mem_ref[0], g_ref[...]).astype(BF16)
    k = _dot(m, wk_ref[...])
    v = _dot(m, wv_ref[...])
    k_ref[0] = k
    v_ref[0] = v
    kb_ref[0] = k.astype(BF16)
    vb_ref[0] = v.astype(BF16)


def _memkv(mem, g, wk, wv):
    b = mem.shape[0]
    blk = pl.BlockSpec((1, N_MEM, D_MODEL), lambda i: (i, 0, 0))
    return pl.pallas_call(
        _memkv_kernel,
        grid=(b,),
        in_specs=[blk, _const_spec((1, D_MODEL)), _const_spec((D_MODEL, D_MODEL)), _const_spec((D_MODEL, D_MODEL))],
        out_specs=[blk, blk, blk, blk],
        out_shape=[jax.ShapeDtypeStruct((b, N_MEM, D_MODEL), F32)] * 2
        + [jax.ShapeDtypeStruct((b, N_MEM, D_MODEL), BF16)] * 2,
        compiler_params=pltpu.CompilerParams(dimension_semantics=("arbitrary",), vmem_limit_bytes=VMEM_LIMIT),
        name="memkv",
    )(mem, g, wk, wv)


def _trunk_p_kernel(x_ref, kb_ref, vb_ref,
                    gmix_ref, win_ref, pw_ref, pscale_ref, cw_ref, cb_ref, wa_ref, ba_ref, wx_ref, bx_ref,
                    lam_ref, gpool_ref, grnn_ref, wout_ref,
                    gxa_ref, wq_ref, wo_ref,
                    gffn_ref, sg_ref, su_ref, sd_ref, wr_ref, rbias_ref,
                    xres_ref, hb_ref, slot_ref, gw_ref, gexp_ref, grel_ref, cover_ref, cnt_ref,
                    pool_ref, conv_ref, hT_ref,
                    pool_prev, conv_prev, h_prev, carry):
    b = pl.program_id(0)
    j = pl.program_id(1)
    n_j = pl.num_programs(1)

    @pl.when(j == 0)
    def _():
        pool_prev[...] = jnp.zeros_like(pool_prev)
        conv_prev[...] = jnp.zeros_like(conv_prev)
        h_prev[...] = jnp.zeros_like(h_prev)

    @pl.when((b == 0) & (j == 0))
    def _():
        carry[...] = jnp.zeros_like(carry)

    x = x_ref[0]
    row = lax.broadcasted_iota(I32, (TS, 1), 0)
    pos = j * TS + row

    h = _rms(x, gmix_ref[...]).astype(BF16)
    z = _dot(h, win_ref[...])
    u_pool = z[:, :D_POOL]
    u_rnn = z[:, D_POOL:D_POOL + D_RNN]
    u_gate = z[:, D_POOL + D_RNN:]

    ext = jnp.concatenate([pool_prev[...], u_pool], axis=0)
    means = []
    for g, w in enumerate(POOL_WINDOWS):
        s = ext[:, g * POOL_GROUP:(g + 1) * POOL_GROUP]
        k = 1
        while k < w:
            s = s + pltpu.roll(s, k, 0)
            k *= 2
        inv = 1.0 / jnp.minimum(pos + 1, w).astype(F32)
        means.append(s[HALO:] * inv)
    mean = jnp.concatenate(means, axis=1)
    y_pool = _pool_project(mean, u_pool, pw_ref, pscale_ref)

    extc = jnp.concatenate([conv_prev[...], u_rnn], axis=0)
    c = u_rnn * cw_ref[CONV_WIDTH - 1:CONV_WIDTH, :]
    for k in range(1, CONV_WIDTH):
        c = c + pltpu.roll(extc, k, 0)[CONV_HALO:] * cw_ref[CONV_WIDTH - 1 - k:CONV_WIDTH - k, :]
    c = c + cb_ref[...]

    a, bt = _gates_and_decay(c, pos == 0, wa_ref, ba_ref, wx_ref, bx_ref, lam_ref)
    k = 1
    while k < TS:
        valid = row >= k
        a_sh = jnp.where(valid, pltpu.roll(a, k, 0), 1.0)
        b_sh = jnp.where(valid, pltpu.roll(bt, k, 0), 0.0)
        bt = bt + a * b_sh
        a = a * a_sh
        k *= 2
    hs = bt + a * h_prev[...]

    pool_prev[...] = u_pool[TS - HALO:]
    conv_prev[...] = u_rnn[TS - CONV_HALO:]
    h_prev[...] = hs[TS - 1:]

    @pl.when(j == n_j - 1)
    def _():
        pool_ref[0] = u_pool[TS - POOL_BUF:]
        conv_ref[0] = u_rnn[TS - (CONV_WIDTH - 1):]
        hT_ref[0] = hs[TS - 1:]

    x1 = x + _merge_out(y_pool, hs, u_gate, gpool_ref, grnn_ref, wout_ref)

    h2 = _rms(x1, gxa_ref[...]).astype(BF16)
    q = (_dot(h2, wq_ref[...]) * (XA_HEAD_DIM ** -0.5)).astype(BF16)
    outs = []
    for hd in range(XA_HEADS):
        sl = slice(hd * XA_HEAD_DIM, (hd + 1) * XA_HEAD_DIM)
        s = _dot_nt(q[:, sl], kb_ref[0, :, sl])
        s = s - jnp.max(s, axis=-1, keepdims=True)
        p = jnp.exp(s)
        p = p / jnp.sum(p, axis=-1, keepdims=True)
        outs.append(_dot(p.astype(BF16), vb_ref[0, :, sl]))
    o = jnp.concatenate(outs, axis=1).astype(BF16)
    x2 = x1 + _dot(o, wo_ref[...])

    h3, xres = _moe_prologue(x2, gffn_ref, sg_ref, su_ref, sd_ref)
    xres_ref[...] = xres
    hb_ref[...] = h3.astype(BF16)
    slot_rows, w_rows, run_len, g_exp, g_rel = _route(h3, wr_ref, rbias_ref, carry[...])
    _store_rows(slot_ref, slot_rows, I32)
    _store_rows(gw_ref, w_rows, F32)
    gexp_ref[0] = g_exp.astype(I32)
    grel_ref[0] = g_rel.astype(I32)
    cover_ref[0] = jnp.broadcast_to(jnp.sum(run_len, axis=0, keepdims=True), cover_ref.shape[1:])
    carry[...] = carry[...] + run_len
    cnt_ref[...] = jnp.broadcast_to(carry[...], cnt_ref.shape)


def _trunk_p(x, kb, vb, mixw, xaw, moew):
    bsz, seq, _ = x.shape
    n_j = seq // TS
    t = bsz * seq
    tok = lambda b, j: (b * n_j + j, 0)
    lane_tok = lambda b, j: (0, b * n_j + j)
    per_b = lambda b, j: (b, 0, 0)
    weights = list(mixw) + list(xaw) + list(moew)
    in_specs = [pl.BlockSpec((1, TS, D_MODEL), lambda b, j: (b, j, 0)),
                pl.BlockSpec((1, N_MEM, D_MODEL), per_b),
                pl.BlockSpec((1, N_MEM, D_MODEL), per_b)] + [_const_spec(w.shape) for w in weights]
    per_tile = lambda b, j: (b * n_j + j, 0, 0)
    out_shape = [jax.ShapeDtypeStruct((t, D_MODEL), F32),
                 jax.ShapeDtypeStruct((t, D_MODEL), BF16),
                 jax.ShapeDtypeStruct((TOP_K, t), I32),
                 jax.ShapeDtypeStruct((TOP_K, t), F32),
                 jax.ShapeDtypeStruct((bsz * n_j, 1, _group_lanes(TS)), I32),
                 jax.ShapeDtypeStruct((bsz * n_j, 1, _group_lanes(TS)), I32),
                 jax.ShapeDtypeStruct((bsz * n_j, 1, 128), F32),
                 jax.ShapeDtypeStruct((N_EXPERTS, 128), F32),
                 jax.ShapeDtypeStruct((bsz, POOL_BUF, D_POOL), F32),
                 jax.ShapeDtypeStruct((bsz, CONV_WIDTH - 1, D_RNN), F32),
                 jax.ShapeDtypeStruct((bsz, 1, D_RNN), F32)]
    out_specs = [pl.BlockSpec((TS, D_MODEL), tok),
                 pl.BlockSpec((TS, D_MODEL), tok),
                 pl.BlockSpec((TOP_K, TS), lane_tok),
                 pl.BlockSpec((TOP_K, TS), lane_tok),
                 pl.BlockSpec((1, 1, _group_lanes(TS)), per_tile),
                 pl.BlockSpec((1, 1, _group_lanes(TS)), per_tile),
                 pl.BlockSpec((1, 1, 128), per_tile),
                 pl.BlockSpec((N_EXPERTS, 128), lambda b, j: (0, 0)),
                 pl.BlockSpec((1, POOL_BUF, D_POOL), per_b),
                 pl.BlockSpec((1, CONV_WIDTH - 1, D_RNN), per_b),
                 pl.BlockSpec((1, 1, D_RNN), per_b)]
    return pl.pallas_call(
        _trunk_p_kernel,
        grid=(bsz, n_j),
        in_specs=in_specs,
        out_specs=out_specs,
        out_shape=out_shape,
        scratch_shapes=[pltpu.VMEM((HALO, D_POOL), F32), pltpu.VMEM((CONV_HALO, D_RNN), F32),
                        pltpu.VMEM((1, D_RNN), F32), pltpu.VMEM((N_EXPERTS, 1), F32)],
        compiler_params=pltpu.CompilerParams(dimension_semantics=("arbitrary", "arbitrary"),
                                             vmem_limit_bytes=VMEM_LIMIT),
        name="trunk_p",
    )(x, kb, vb, *weights)


def _mix_s_kernel(x_ref, pool_ref, conv_ref, h0_ref,
                  gmix_ref, win_ref, pw_ref, pscale_ref, cw_ref, cb_ref, wa_ref, ba_ref, wx_ref, bx_ref,
                  lam_ref, gpool_ref, grnn_ref, wout_ref, gxa_ref, wq_ref,
                  x1_ref, q_ref, npool_ref, nconv_ref, nh_ref):
    x = x_ref[...]
    h = _rms(x, gmix_ref[...]).astype(BF16)
    z = _dot(h, win_ref[...])
    u_pool = z[:, :D_POOL]
    u_rnn = z[:, D_POOL:D_POOL + D_RNN]
    u_gate = z[:, D_POOL + D_RNN:]

    means = []
    for g, w in enumerate(POOL_WINDOWS):
        sl = slice(g * POOL_GROUP, (g + 1) * POOL_GROUP)
        s = u_pool[:, sl]
        for k in range(1, w):
            s = s + pool_ref[POOL_BUF - k, :, sl]
        means.append(s * (1.0 / min(w, PAST_LEN + 1)))
    mean = jnp.concatenate(means, axis=1)
    y_pool = _pool_project(mean, u_pool, pw_ref, pscale_ref)

    c = u_rnn * cw_ref[CONV_WIDTH - 1:CONV_WIDTH, :]
    for k in range(1, CONV_WIDTH):
        c = c + conv_ref[CONV_WIDTH - 1 - k] * cw_ref[CONV_WIDTH - 1 - k:CONV_WIDTH - k, :]
    c = c + cb_ref[...]
    a, bt = _gates_and_decay(c, None, wa_ref, ba_ref, wx_ref, bx_ref, lam_ref)
    hs = a * h0_ref[...] + bt

    x1 = x + _merge_out(y_pool, hs, u_gate, gpool_ref, grnn_ref, wout_ref)
    x1_ref[...] = x1
    h2 = _rms(x1, gxa_ref[...]).astype(BF16)
    q_ref[...] = _dot(h2, wq_ref[...]) * (XA_HEAD_DIM ** -0.5)

    npool_ref[:POOL_BUF - 1] = pool_ref[1:]
    npool_ref[POOL_BUF - 1] = u_pool
    nconv_ref[:CONV_WIDTH - 2] = conv_ref[1:]
    nconv_ref[CONV_WIDTH - 2] = u_rnn
    nh_ref[...] = hs


def _mix_s(x, pool, conv, h0, mixw, gxa, wq):
    bsz = x.shape[0]
    args = [x, pool, conv, h0] + list(mixw) + [gxa, wq]
    return pl.pallas_call(
        _mix_s_kernel,
        grid=(1,),
        in_specs=[_const_spec(a.shape) for a in args],
        out_specs=[_const_spec((bsz, D_MODEL)), _const_spec((bsz, D_MODEL)), _const_spec(pool.shape),
                   _const_spec(conv.shape), _const_spec((bsz, D_RNN))],
        out_shape=[jax.ShapeDtypeStruct((bsz, D_MODEL), F32), jax.ShapeDtypeStruct((bsz, D_MODEL), F32),
                   jax.ShapeDtypeStruct(pool.shape, F32), jax.ShapeDtypeStruct(conv.shape, F32),
                   jax.ShapeDtypeStruct((bsz, D_RNN), F32)],
        compiler_params=pltpu.CompilerParams(dimension_semantics=("arbitrary",), vmem_limit_bytes=VMEM_LIMIT),
        name="mix_s",
    )(*args)


ATTN_S_BB = 4


def _attn_s_kernel(q_ref, k_ref, v_ref, o_ref):
    q = q_ref[...][:, None]
    s = jnp.sum(k_ref[...] * q, axis=-1, keepdims=True)
    s = s - jnp.max(s, axis=1, keepdims=True)
    p = jnp.exp(s)
    p = p / jnp.sum(p, axis=1, keepdims=True)
    o_ref[...] = jnp.sum(p * v_ref[...], axis=1)


def _attn_s(q, k, v):
    bsz = q.shape[0]
    kv_spec = pl.BlockSpec((ATTN_S_BB, N_MEM, XA_HEADS, XA_HEAD_DIM), lambda i: (i, 0, 0, 0))
    q_spec = pl.BlockSpec((ATTN_S_BB, XA_HEADS, XA_HEAD_DIM), lambda i: (i, 0, 0))
    o = pl.pallas_call(
        _attn_s_kernel,
        grid=(bsz // ATTN_S_BB,),
        in_specs=[q_spec, kv_spec, kv_spec],
        out_specs=q_spec,
        out_shape=jax.ShapeDtypeStruct((bsz, XA_HEADS, XA_HEAD_DIM), F32),
        compiler_params=pltpu.CompilerParams(dimension_semantics=("arbitrary",), vmem_limit_bytes=VMEM_LIMIT),
        name="attn_s",
    )(q.reshape(bsz, XA_HEADS, XA_HEAD_DIM), k, v)
    return o.reshape(bsz, D_MODEL)


def _post_s_kernel(x1_ref, o_ref, wo_ref, gffn_ref, sg_ref, su_ref, sd_ref, wr_ref, rbias_ref,
                   xres_ref, hb_ref, slot_ref, gw_ref, gexp_ref, grel_ref, cover_ref, cnt_ref):
    x2 = x1_ref[...] + _dot(o_ref[...].astype(BF16), wo_ref[...])
    h3, xres = _moe_prologue(x2, gffn_ref, sg_ref, su_ref, sd_ref)
    xres_ref[...] = xres
    hb_ref[...] = h3.astype(BF16)
    slot_rows, w_rows, run_len, g_exp, g_rel = _route(h3, wr_ref, rbias_ref, jnp.zeros((N_EXPERTS, 1), F32))
    _store_rows(slot_ref, slot_rows, I32)
    _store_rows(gw_ref, w_rows, F32)
    gexp_ref[0] = g_exp.astype(I32)
    grel_ref[0] = g_rel.astype(I32)
    cover_ref[0] = jnp.broadcast_to(jnp.sum(run_len, axis=0, keepdims=True), cover_ref.shape[1:])
    cnt_ref[...] = jnp.broadcast_to(run_len, cnt_ref.shape)


def _post_s(x1, o, wo, moew):
    bsz = x1.shape[0]
    args = [x1, o, wo] + list(moew)
    out_shape = [jax.ShapeDtypeStruct((bsz, D_MODEL), F32),
                 jax.ShapeDtypeStruct((bsz, D_MODEL), BF16),
                 jax.ShapeDtypeStruct((TOP_K, bsz), I32),
                 jax.ShapeDtypeStruct((TOP_K, bsz), F32),
                 jax.ShapeDtypeStruct((1, 1, _group_lanes(bsz)), I32),
                 jax.ShapeDtypeStruct((1, 1, _group_lanes(bsz)), I32),
                 jax.ShapeDtypeStruct((1, 1, 128), F32),
                 jax.ShapeDtypeStruct((N_EXPERTS, 128), F32)]
    return pl.pallas_call(
        _post_s_kernel,
        grid=(1,),
        in_specs=[_const_spec(a.shape) for a in args],
        out_specs=[_const_spec(s.shape) for s in out_shape],
        out_shape=out_shape,
        compiler_params=pltpu.CompilerParams(dimension_semantics=("arbitrary",), vmem_limit_bytes=VMEM_LIMIT),
        name="post_s",
    )(*args)


def _plan_kernel(rp_ref, rs_ref, base_p_ref, base_s_ref, fill_at_ref, fill_n_ref, exp_ref, nreal_ref):
    n_blocks = exp_ref.shape[0]
    shift = BM.bit_length() - 1

    def per_expert(e, carry):
        blk0, last_e = carry
        rows = rp_ref[e] + rs_ref[e]
        start = lax.shift_left(blk0, shift)
        n_blk = lax.shift_right_logical(rows + (BM - 1), shift)
        base_p_ref[e] = start
        base_s_ref[e] = start + rp_ref[e]
        fill_at_ref[e] = start + rows
        fill_n_ref[e] = _groups(lax.shift_left(n_blk, shift) - rows)

        def per_block(j, _):
            exp_ref[blk0 + j] = e
            return _

        lax.fori_loop(0, n_blk, per_block, 0)
        return blk0 + n_blk, jnp.where(rows > 0, e, last_e)

    n_real, last_e = lax.fori_loop(0, N_EXPERTS, per_expert, (jnp.int32(0), jnp.int32(0)))
    nreal_ref[0] = n_real

    def rest(b, _):
        exp_ref[b] = last_e
        return _

    lax.fori_loop(n_real, n_blocks, rest, 0)


def _plan(rows_p, rows_s, n_rows):
    assert BM & (BM - 1) == 0 and n_rows % BM == 0
    smem = pl.BlockSpec(memory_space=pltpu.SMEM)
    return pl.pallas_call(
        _plan_kernel,
        in_specs=[smem, smem],
        out_specs=[smem] * 6,
        out_shape=[jax.ShapeDtypeStruct((N_EXPERTS,), I32)] * 4
        + [jax.ShapeDtypeStruct((n_rows // BM,), I32), jax.ShapeDtypeStruct((1,), I32)],
        name="plan",
    )(rows_p, rows_s)


def _groups(n_rows):
    return lax.shift_right_logical(n_rows, RUN_ALIGN.bit_length() - 1)


def _for_each_group(n_groups, gexp_ref, grel_ref, base_ref, fn):
    def body(g, _):
        glob = base_ref[gexp_ref[0, 0, g]] + grel_ref[0, 0, g]
        fn(pl.multiple_of(g * RUN_ALIGN, RUN_ALIGN), pl.multiple_of(glob, RUN_ALIGN))
        return _

    lax.fori_loop(0, n_groups, body, 0)


def _dispatch_kernel(cover_p, base_p, cover_s, base_s, fill_at, fill_n, nreal_ref,
                     gexp_p, grel_p, gexp_s, grel_s,
                     slot_p_ref, h_p_ref, slot_s_ref, h_s_ref, xs_ref, sbuf, zbuf, sem):
    i = pl.program_id(0)
    last = pl.num_programs(0) - 1
    cur = lax.rem(i, 2)

    def drain(s, n_rows):
        def body(g, _):
            pltpu.make_async_copy(sbuf.at[s, pl.ds(0, RUN_ALIGN)], xs_ref.at[pl.ds(0, RUN_ALIGN)], sem.at[s]).wait()
            return _
        lax.fori_loop(0, _groups(n_rows), body, 0)

    def tile(n_rows, gexp_ref, grel_ref, base_ref, slot_ref, h_ref):
        for c in range(_sorted_rows(h_ref.shape[0]) // SORT_CHUNK):
            @pl.when(c * SORT_CHUNK < n_rows)
            def _():
                rid = c * SORT_CHUNK + lax.broadcasted_iota(I32, (SORT_CHUNK, 1), 0)
                hit = rid == slot_ref[0:1, :]
                for k in range(1, TOP_K):
                    hit = hit | (rid == slot_ref[k:k + 1, :])
                sbuf[cur, c * SORT_CHUNK:(c + 1) * SORT_CHUNK] = _dot(hit.astype(BF16), h_ref[...]).astype(BF16)

        def send(local, glob):
            pltpu.make_async_copy(sbuf.at[cur, pl.ds(local, RUN_ALIGN)], xs_ref.at[pl.ds(glob, RUN_ALIGN)],
                                  sem.at[cur]).start()

        _for_each_group(_groups(n_rows), gexp_ref, grel_ref, base_ref, send)

    @pl.when(i >= 2)
    def _():
        drain(cur, cover_p[i - 2])

    @pl.when(i < last)
    def _():
        tile(cover_p[i], gexp_p, grel_p, base_p, slot_p_ref, h_p_ref)

    @pl.when(i == last)
    def _():
        tile(cover_s[0], gexp_s, grel_s, base_s, slot_s_ref, h_s_ref)
        drain(1 - cur, cover_p[last - 1])
        drain(cur, cover_s[0])
        zbuf[...] = jnp.zeros_like(zbuf)

        def group_fill(e, g):
            return pltpu.make_async_copy(
                zbuf.at[pl.ds(0, RUN_ALIGN)],
                xs_ref.at[pl.ds(pl.multiple_of(fill_at[e] + g * RUN_ALIGN, RUN_ALIGN), RUN_ALIGN)], sem.at[2])

        def per_expert(e, n):
            def start(g, _):
                group_fill(e, g).start()
                return _
            lax.fori_loop(0, fill_n[e], start, 0)
            return n + fill_n[e]

        n_fill = lax.fori_loop(0, N_EXPERTS, per_expert, jnp.int32(0))

        def wait_group(g, _):
            group_fill(0, 0).wait()
            return _

        lax.fori_loop(0, n_fill, wait_group, 0)

        n_real = nreal_ref[0]
        n_tail = xs_ref.shape[0] // BM - n_real

        def blk_fill(b):
            return pltpu.make_async_copy(zbuf, xs_ref.at[pl.ds(pl.multiple_of((n_real + b) * BM, BM), BM)], sem.at[2])

        def start_blk(b, _):
            blk_fill(b).start()
            return _

        def wait_blk(b, _):
            blk_fill(b).wait()
            return _

        lax.fori_loop(0, n_tail, start_blk, 0)
        lax.fori_loop(0, n_tail, wait_blk, 0)


def _dispatch(tab_p, tab_s, fills, gtab_p, gtab_s, slot_p, hb_p, slot_s, hb_s, n_rows):
    n_p = hb_p.shape[0] // TS
    smem = pl.BlockSpec(memory_space=pltpu.SMEM)
    clamp = lambda i: jnp.minimum(i, n_p - 1)
    per_tile = pl.BlockSpec((1, 1, gtab_p[0].shape[-1]), lambda i: (clamp(i), 0, 0), memory_space=pltpu.SMEM)
    return pl.pallas_call(
        _dispatch_kernel,
        grid=(n_p + 1,),
        in_specs=[smem] * 7 + [per_tile, per_tile, smem, smem,
                               pl.BlockSpec((TOP_K, TS), lambda i: (0, clamp(i))),
                               pl.BlockSpec((TS, D_MODEL), lambda i: (clamp(i), 0)),
                               _const_spec(slot_s.shape), _const_spec(hb_s.shape)],
        out_specs=pl.BlockSpec(memory_space=pl.ANY),
        out_shape=jax.ShapeDtypeStruct((n_rows, D_MODEL), BF16),
        scratch_shapes=[pltpu.VMEM((2, _sorted_rows(TS), D_MODEL), BF16),
                        pltpu.VMEM((BM, D_MODEL), BF16), pltpu.SemaphoreType.DMA((3,))],
        compiler_params=pltpu.CompilerParams(dimension_semantics=("arbitrary",), has_side_effects=True,
                                             vmem_limit_bytes=VMEM_LIMIT),
        name="dispatch",
    )(*tab_p, *tab_s, *fills, *gtab_p, *gtab_s, slot_p, hb_p, slot_s, hb_s)


GMM_LANES = 2


def _gmm_kernel(exp_ref, nreal_ref, *refs):
    x_ref = refs[0]
    w_refs = refs[1:1 + 3 * GMM_LANES]
    y_ref = refs[1 + 3 * GMM_LANES]
    scratch = refs[2 + 3 * GMM_LANES:]
    s = pl.program_id(0)
    n_real = nreal_ref[0]

    for lane in range(GMM_LANES):
        wg_ref, wu_ref, wd_ref = w_refs[3 * lane:3 * lane + 3]
        wgu, wdn = scratch[2 * lane:2 * lane + 2]
        b = s * GMM_LANES + lane
        new_expert = (s == 0) | (exp_ref[b] != exp_ref[jnp.maximum(b - GMM_LANES, 0)])

        @pl.when((b < n_real) & new_expert)
        def _():
            wgu[:, :D_EXPERT] = wg_ref[0].astype(BF16)
            wgu[:, D_EXPERT:] = wu_ref[0].astype(BF16)
            wdn[...] = wd_ref[0].astype(BF16)

    @pl.when(s * GMM_LANES < n_real)
    def _():
        for lane in range(GMM_LANES):
            wgu, wdn = scratch[2 * lane:2 * lane + 2]
            rows = slice(lane * BM, (lane + 1) * BM)
            gu = _dot(x_ref[rows, :], wgu[...])
            act = jax.nn.silu(gu[:, :D_EXPERT]) * gu[:, D_EXPERT:]
            y = _dot(act.astype(BF16), wdn[...]).astype(BF16)
            y_ref[rows, :] = jnp.where(s * GMM_LANES + lane < n_real, y, jnp.zeros_like(y))

    @pl.when(s * GMM_LANES >= n_real)
    def _():
        y_ref[...] = jnp.zeros_like(y_ref)


def _gmm(blk_exp, n_real, xs, wg, wu, wd):
    n_rows = xs.shape[0]
    step_rows = BM * GMM_LANES
    assert n_rows % step_rows == 0
    last_step = lambda nreal: (nreal[0] - 1) // GMM_LANES
    w_specs, scratch = [], []
    for lane in range(GMM_LANES):
        weight = lambda s, exp, nreal, lane=lane: (exp[s * GMM_LANES + lane], 0, 0)
        w_specs += [pl.BlockSpec((1, D_MODEL, D_EXPERT), weight), pl.BlockSpec((1, D_MODEL, D_EXPERT), weight),
                    pl.BlockSpec((1, D_EXPERT, D_MODEL), weight)]
        scratch += [pltpu.VMEM((D_MODEL, 2 * D_EXPERT), BF16), pltpu.VMEM((D_EXPERT, D_MODEL), BF16)]
    grid_spec = pltpu.PrefetchScalarGridSpec(
        num_scalar_prefetch=2,
        grid=(n_rows // step_rows,),
        in_specs=[pl.BlockSpec((step_rows, D_MODEL), lambda s, exp, nreal: (jnp.minimum(s, last_step(nreal)), 0))]
        + w_specs,
        out_specs=pl.BlockSpec((step_rows, D_MODEL), lambda s, exp, nreal: (s, 0)),
        scratch_shapes=scratch,
    )
    return pl.pallas_call(
        _gmm_kernel,
        grid_spec=grid_spec,
        out_shape=jax.ShapeDtypeStruct((n_rows, D_MODEL), BF16),
        compiler_params=pltpu.CompilerParams(dimension_semantics=("arbitrary",), vmem_limit_bytes=VMEM_LIMIT),
        name="gmm",
    )(blk_exp, n_real, xs, *([wg, wu, wd] * GMM_LANES))


def _combine_kernel(cover_ref, base_ref, gexp_ref, grel_ref, gexp_next_ref, grel_next_ref,
                    slot_ref, gw_ref, xres_ref, gfin_ref, ys_ref, out_ref, ybuf, sem):
    i = pl.program_id(0)
    n = pl.num_programs(0)
    cur = lax.rem(i, 2)
    rows = ybuf.shape[1]

    def group_copy(s, local, glob):
        return pltpu.make_async_copy(ys_ref.at[pl.ds(glob, RUN_ALIGN)], ybuf.at[s, pl.ds(local, RUN_ALIGN)],
                                     sem.at[s])

    def gather(n_rows, ge_ref, gr_ref, s):
        _for_each_group(_groups(n_rows), ge_ref, gr_ref, base_ref, lambda l, g: group_copy(s, l, g).start())

    @pl.when(i == 0)
    def _():
        ybuf[...] = jnp.zeros_like(ybuf)
        gather(cover_ref[0], gexp_ref, grel_ref, 0)

    @pl.when(i + 1 < n)
    def _():
        gather(cover_ref[jnp.minimum(i + 1, n - 1)], gexp_next_ref, grel_next_ref, 1 - cur)

    def drain(g, _):
        group_copy(cur, 0, 0).wait()
        return _

    lax.fori_loop(0, _groups(cover_ref[i]), drain, 0)

    out_ref[...] = xres_ref[...]
    slot = slot_ref[...]
    gw = gw_ref[...]
    for c in range(rows // SORT_CHUNK):
        @pl.when(c * SORT_CHUNK < cover_ref[i])
        def _():
            col = c * SORT_CHUNK + lax.broadcasted_iota(I32, (1, SORT_CHUNK), 1)
            w = jnp.zeros((slot.shape[0], SORT_CHUNK), F32)
            for k in range(TOP_K):
                w = jnp.where(slot[:, k:k + 1] == col, gw[:, k:k + 1], w)
            out_ref[...] += _dot(w.astype(BF16), ybuf[cur, c * SORT_CHUNK:(c + 1) * SORT_CHUNK])
    out_ref[...] = _rms(out_ref[...], gfin_ref[...])


def _combine(tab, gtab, slot_t, gw_t, xres, gfin, ys):
    t = xres.shape[0]
    tm = min(t, TS)
    n = t // tm
    smem = pl.BlockSpec(memory_space=pltpu.SMEM)
    lanes = gtab[0].shape[-1]
    this_tile = pl.BlockSpec((1, 1, lanes), lambda i: (i, 0, 0), memory_space=pltpu.SMEM)
    next_tile = pl.BlockSpec((1, 1, lanes), lambda i: (jnp.minimum(i + 1, n - 1), 0, 0), memory_space=pltpu.SMEM)
    return pl.pallas_call(
        _combine_kernel,
        grid=(n,),
        in_specs=[smem, smem, this_tile, this_tile, next_tile, next_tile,
                  pl.BlockSpec((tm, TOP_K), lambda i: (i, 0)),
                  pl.BlockSpec((tm, TOP_K), lambda i: (i, 0)),
                  pl.BlockSpec((tm, D_MODEL), lambda i: (i, 0)),
                  _const_spec((1, D_MODEL)),
                  pl.BlockSpec(memory_space=pl.ANY)],
        out_specs=pl.BlockSpec((tm, D_MODEL), lambda i: (i, 0)),
        out_shape=jax.ShapeDtypeStruct((t, D_MODEL), F32),
        scratch_shapes=[pltpu.VMEM((2, _sorted_rows(tm), D_MODEL), BF16), pltpu.SemaphoreType.DMA((2,))],
        compiler_params=pltpu.CompilerParams(dimension_semantics=("arbitrary",), vmem_limit_bytes=VMEM_LIMIT),
        name="combine",
    )(*tab, *gtab, *gtab, slot_t, gw_t, xres, gfin, ys)


def _block_diag_pairs(w):
    n_h, d, _ = w.shape
    half = n_h // 2
    out = jnp.zeros((2, half * d, half * d), F32)
    for hh in range(n_h):
        p, q = divmod(hh, half)
        out = out.at[p, q * d:(q + 1) * d, q * d:(q + 1) * d].set(w[hh])
    return out.astype(BF16)


def kernel(x_prompt, x_sample, state_pool, state_conv, state_h, cache_mem_k, cache_mem_v, mem_prompt, norm_mix, w_in, pool_w, pool_scale, conv_w, conv_b, gate_a_w, gate_a_b, gate_x_w, gate_x_b, lru_lambda, norm_pool_out, norm_rnn_out, w_out, norm_xattn, norm_mem, xa_wq, xa_wk, xa_wv, xa_wo, norm_ffn, router_w, router_bias, exp_w_gate, exp_w_up, exp_w_down, sh_w_gate, sh_w_up, sh_w_down, norm_final):
    bp, seq, _ = x_prompt.shape
    bs = x_sample.shape[0]
    tp = bp * seq
    row = lambda v: v.reshape(1, -1)
    bf = lambda v: v.astype(BF16)

    mixw = (row(norm_mix[0]), bf(w_in[0]), _block_diag_pairs(pool_w[0]), row(pool_scale[0]), conv_w[0],
            row(conv_b[0]), _block_diag_pairs(gate_a_w[0]), row(gate_a_b[0]), _block_diag_pairs(gate_x_w[0]),
            row(gate_x_b[0]), row(lru_lambda[0]), row(norm_pool_out[0]), row(norm_rnn_out[0]), bf(w_out[0]))
    xaw = (row(norm_xattn[0]), bf(xa_wq[0]), bf(xa_wo[0]))
    moew = (row(norm_ffn[0]), bf(sh_w_gate[0]), bf(sh_w_up[0]), bf(sh_w_down[0]), router_w[0].T,
            router_bias[0].reshape(N_EXPERTS, 1))

    mk, mv, kb, vb = _memkv(mem_prompt, row(norm_mem[0]), bf(xa_wk[0]), bf(xa_wv[0]))
    (xres_p, hb_p, slot_p, gw_p, gexp_p, grel_p, cover_p, cnt_p, pool_p, conv_p, h_p) = _trunk_p(
        x_prompt, kb, vb, mixw, xaw, moew)

    x1_s, q_s, pool_s, conv_s, h_s = _mix_s(x_sample.reshape(bs, D_MODEL), state_pool[0].transpose(1, 0, 2),
                                            state_conv[0].transpose(1, 0, 2), state_h[0], mixw, xaw[0], xaw[1])
    pool_s = pool_s.transpose(1, 0, 2)
    conv_s = conv_s.transpose(1, 0, 2)
    o_s = _attn_s(q_s, cache_mem_k[0], cache_mem_v[0])
    xres_s, hb_s, slot_s, gw_s, gexp_s, grel_s, cover_s, cnt_s = _post_s(x1_s, o_s, xaw[2], moew)

    n_tiles = tp // TS + 1
    n_rows = _round_up((tp + bs) * TOP_K + n_tiles * N_EXPERTS * (RUN_ALIGN - 1) + N_EXPERTS * (BM - 1),
                       BM * GMM_LANES)
    ints = lambda v: v[..., 0].astype(I32)
    base_p, base_s, fill_at, fill_n, blk_exp, n_real = _plan(ints(cnt_p), ints(cnt_s), n_rows)
    tab_p = (ints(cover_p).reshape(-1), base_p)
    tab_s = (ints(cover_s).reshape(-1), base_s)
    gtab_p = (gexp_p, grel_p)
    gtab_s = (gexp_s, grel_s)

    xs = _dispatch(tab_p, tab_s, (fill_at, fill_n, n_real), gtab_p, gtab_s, slot_p, hb_p, slot_s, hb_s, n_rows)
    ys = _gmm(blk_exp, n_real, xs, exp_w_gate[0], exp_w_up[0], exp_w_down[0])

    gfin = row(norm_final)
    y_p = _combine(tab_p, gtab_p, slot_p.T, gw_p.T, xres_p, gfin, ys)
    y_s = _combine(tab_s, gtab_s, slot_s.T, gw_s.T, xres_s, gfin, ys)

    return (y_p.reshape(bp, seq, D_MODEL), y_s.reshape(bs, 1, D_MODEL),
            pool_p[None], conv_p[None], h_p.reshape(1, bp, D_RNN),
            mk.reshape(1, bp, N_MEM, XA_HEADS, XA_HEAD_DIM), mv.reshape(1, bp, N_MEM, XA_HEADS, XA_HEAD_DIM),
            pool_s[None], conv_s[None], h_s[None])
```

```python
import functools

import jax
import jax.numpy as jnp
from jax import lax
from jax.experimental import pallas as pl
from jax.experimental.pallas import tpu as pltpu

F32 = jnp.float32
BF16 = jnp.bfloat16
I32 = jnp.int32
U32 = jnp.uint32

D_MODEL = 1024
D_POOL = 512
D_RNN = 512
D_IN = D_POOL + 2 * D_RNN
POOL_WINDOWS = (2, 4, 8, 16)
POOL_GROUP = 128
POOL_BUF = 15
CONV_WIDTH = 4
LRU_C = 8.0
N_MEM = 256
XA_HEADS = 4
XA_HEAD_DIM = 256
N_EXPERTS = 64
TOP_K = 8
N_EXPERT_GROUPS = 8
GROUP_SIZE = N_EXPERTS // N_EXPERT_GROUPS
TOPK_GROUPS = 4
D_EXPERT = 256
ROUTED_SCALE = 2.5
EPS = 1e-6
PAST_LEN = 16384

HALO = 16
CONV_HALO = 8
TS = 256
BM = 256
RUN_ALIGN = 16
RUN_CHUNKS = (64, 32, 16)
SORT_CHUNK = 512
VMEM_LIMIT = 56 * 1024 * 1024


def _round_up(x, m):
    return (x + m - 1) // m * m


def _sorted_rows(tokens):
    return _round_up(tokens * TOP_K + N_EXPERTS * (RUN_ALIGN - 1), SORT_CHUNK)


def _group_lanes(tokens):
    return _round_up(_sorted_rows(tokens) // RUN_ALIGN, 128)


def _const_spec(shape):
    nd = len(shape)
    return pl.BlockSpec(shape, lambda *_: (0,) * nd, pipeline_mode=pl.Buffered(1))


def _rms(x, g):
    ms = jnp.mean(x * x, axis=-1, keepdims=True)
    return x * lax.rsqrt(ms + EPS) * g


def _dot(a, b):
    return jnp.dot(a, b, preferred_element_type=F32)


def _dot_nt(a, b, precision=None):
    return lax.dot_general(a, b, (((1,), (1,)), ((), ())), precision=precision,
                           preferred_element_type=F32)


def _softplus(x):
    return jnp.maximum(x, 0.0) + jnp.log1p(jnp.exp(-jnp.abs(x)))


def _gates_and_decay(c, pos_is_zero, wa_ref, ba_ref, wx_ref, bx_ref, lam_ref):
    cb = c.astype(BF16)
    half = D_RNN // 2
    ga = jnp.concatenate([_dot(cb[:, :half], wa_ref[0]), _dot(cb[:, half:], wa_ref[1])], axis=1) + ba_ref[...]
    gx = jnp.concatenate([_dot(cb[:, :half], wx_ref[0]), _dot(cb[:, half:], wx_ref[1])], axis=1) + bx_ref[...]
    r = jax.nn.sigmoid(ga)
    i = jax.nn.sigmoid(gx)
    log_a = (-LRU_C) * r * _softplus(-lam_ref[...])
    a = jnp.exp(log_a)
    mult = jnp.sqrt(1.0 - a * a)
    if pos_is_zero is not None:
        mult = jnp.where(pos_is_zero, 1.0, mult)
    return a, mult * i * c


def _pool_project(mean, u_pool, pw_ref, pscale_ref):
    d = (mean - u_pool).astype(BF16)
    half = D_POOL // 2
    y = jnp.concatenate([_dot(d[:, :half], pw_ref[0]), _dot(d[:, half:], pw_ref[1])], axis=1)
    return y * pscale_ref[...]


def _merge_out(y_pool, hs, u_gate, gpool_ref, grnn_ref, wout_ref):
    y_rnn = hs * jax.nn.gelu(u_gate)
    merged = jnp.concatenate([_rms(y_pool, gpool_ref[...]), _rms(y_rnn, grnn_ref[...])], axis=1)
    return _dot(merged.astype(BF16), wout_ref[...])


def _route(h3, wr_ref, rbias_ref, before):
    r_tok = h3.shape[0]
    logits = _dot_nt(wr_ref[...], h3, precision=lax.Precision.HIGHEST)
    scores = jax.nn.sigmoid(logits)
    biased = scores + rbias_ref[...]
    neg = jnp.float32(-jnp.inf)
    gs = []
    for g in range(N_EXPERT_GROUPS):
        xg = biased[g * GROUP_SIZE:(g + 1) * GROUP_SIZE]
        m1 = jnp.max(xg, axis=0, keepdims=True)
        eq = xg == m1
        cnt = jnp.sum(eq.astype(F32), axis=0, keepdims=True)
        m2 = jnp.max(jnp.where(eq, neg, xg), axis=0, keepdims=True)
        gs.append(m1 + jnp.where(cnt >= 2.0, m1, m2))
    pieces = []
    for g in range(N_EXPERT_GROUPS):
        beaten = jnp.zeros_like(gs[g])
        for o in range(N_EXPERT_GROUPS):
            if o == g:
                continue
            wins = (gs[o] > gs[g]) | (gs[o] == gs[g]) if o < g else (gs[o] > gs[g])
            beaten = beaten + wins.astype(F32)
        keep = beaten < float(TOPK_GROUPS)
        xg = biased[g * GROUP_SIZE:(g + 1) * GROUP_SIZE]
        pieces.append(jnp.where(keep, xg, neg))
    cur = jnp.concatenate(pieces, axis=0)
    eid = lax.broadcasted_iota(I32, (N_EXPERTS, r_tok), 0).astype(F32)
    idx_rows, score_rows = [], []
    sel = jnp.zeros((N_EXPERTS, r_tok), F32)
    for _ in range(TOP_K):
        m = jnp.max(cur, axis=0, keepdims=True)
        idx = jnp.min(jnp.where(cur == m, eid, float(N_EXPERTS)), axis=0, keepdims=True)
        oh = eid == idx
        score_rows.append(jnp.sum(jnp.where(oh, scores, 0.0), axis=0, keepdims=True))
        idx_rows.append(idx)
        sel = sel + oh.astype(F32)
        cur = jnp.where(oh, neg, cur)
    tot = score_rows[0]
    for s in score_rows[1:]:
        tot = tot + s
    w_rows = [s / tot * ROUTED_SCALE for s in score_rows]
    rr = lax.broadcasted_iota(I32, (r_tok, r_tok), 0)
    cc = lax.broadcasted_iota(I32, (r_tok, r_tok), 1)
    earlier = _dot(sel.astype(BF16), (rr < cc).astype(BF16))
    counts = jnp.sum(sel, axis=1, keepdims=True)
    run_len = jnp.floor((counts + (RUN_ALIGN - 1.0)) * (1.0 / RUN_ALIGN)) * RUN_ALIGN
    er = lax.broadcasted_iota(I32, (N_EXPERTS, N_EXPERTS), 0)
    ec = lax.broadcasted_iota(I32, (N_EXPERTS, N_EXPERTS), 1)
    run_start = _dot((ec < er).astype(BF16), jnp.broadcast_to(run_len, (N_EXPERTS, 128)).astype(BF16))[:, :1]
    slot = earlier + run_start
    slot_rows = [jnp.sum(jnp.where(eid == idx, slot, 0.0), axis=0, keepdims=True) for idx in idx_rows]
    n_lanes = _group_lanes(r_tok)
    g_row = lax.broadcasted_iota(I32, (N_EXPERTS, n_lanes), 1).astype(F32) * float(RUN_ALIGN)
    owns = (run_start <= g_row) & (g_row < run_start + run_len)
    e_col = lax.broadcasted_iota(I32, (N_EXPERTS, n_lanes), 0).astype(F32)
    g_exp = jnp.sum(jnp.where(owns, e_col, 0.0), axis=0, keepdims=True)
    g_rel = jnp.sum(jnp.where(owns, before + g_row - run_start, 0.0), axis=0, keepdims=True)
    return slot_rows, w_rows, run_len, g_exp, g_rel


def _moe_prologue(x2, gffn_ref, sg_ref, su_ref, sd_ref):
    h3 = _rms(x2, gffn_ref[...])
    h3b = h3.astype(BF16)
    act = jax.nn.silu(_dot(h3b, sg_ref[...])) * _dot(h3b, su_ref[...])
    shared = _dot(act.astype(BF16), sd_ref[...])
    return h3, x2 + shared


def _store_rows(ref, rows, dtype):
    for k, row in enumerate(rows):
        ref[k:k + 1, :] = row.astype(dtype)


def _memkv_kernel(mem_ref, g_ref, wk_ref, wv_ref, k_ref, v_ref, kb_ref, vb_ref):
    m = _rms(mem_ref[0], g_ref[...]).astype(BF16)
    k = _dot(m, wk_ref[...])
    v = _dot(m, wv_ref[...])
    k_ref[0] = k
    v_ref[0] = v
    kb_ref[0] = k.astype(BF16)
    vb_ref[0] = v.astype(BF16)


def _memkv(mem, g, wk, wv):
    b = mem.shape[0]
    blk = pl.BlockSpec((1, N_MEM, D_MODEL), lambda i: (i, 0, 0))
    return pl.pallas_call(
        _memkv_kernel,
        grid=(b,),
        in_specs=[blk, _const_spec((1, D_MODEL)), _const_spec((D_MODEL, D_MODEL)), _const_spec((D_MODEL, D_MODEL))],
        out_specs=[blk, blk, blk, blk],
        out_shape=[jax.ShapeDtypeStruct((b, N_MEM, D_MODEL), F32)] * 2
        + [jax.ShapeDtypeStruct((b, N_MEM, D_MODEL), BF16)] * 2,
        compiler_params=pltpu.CompilerParams(dimension_semantics=("arbitrary",), vmem_limit_bytes=VMEM_LIMIT),
        name="memkv",
    )(mem, g, wk, wv)


def _trunk_p_kernel(x_ref, kb_ref, vb_ref,
                    gmix_ref, win_ref, pw_ref, pscale_ref, cw_ref, cb_ref, wa_ref, ba_ref, wx_ref, bx_ref,
                    lam_ref, gpool_ref, grnn_ref, wout_ref,
                    gxa_ref, wq_ref, wo_ref,
                    gffn_ref, sg_ref, su_ref, sd_ref, wr_ref, rbias_ref,
                    xres_ref, hb_ref, slot_ref, gw_ref, gexp_ref, grel_ref, cover_ref, cnt_ref,
                    pool_ref, conv_ref, hT_ref,
                    pool_prev, conv_prev, h_prev, carry):
    b = pl.program_id(0)
    j = pl.program_id(1)
    n_j = pl.num_programs(1)

    @pl.when(j == 0)
    def _():
        pool_prev[...] = jnp.zeros_like(pool_prev)
        conv_prev[...] = jnp.zeros_like(conv_prev)
        h_prev[...] = jnp.zeros_like(h_prev)

    @pl.when((b == 0) & (j == 0))
    def _():
        carry[...] = jnp.zeros_like(carry)

    x = x_ref[0]
    row = lax.broadcasted_iota(I32, (TS, 1), 0)
    pos = j * TS + row

    h = _rms(x, gmix_ref[...]).astype(BF16)
    z = _dot(h, win_ref[...])
    u_pool = z[:, :D_POOL]
    u_rnn = z[:, D_POOL:D_POOL + D_RNN]
    u_gate = z[:, D_POOL + D_RNN:]

    ext = jnp.concatenate([pool_prev[...], u_pool], axis=0)
    means = []
    for g, w in enumerate(POOL_WINDOWS):
        s = ext[:, g * POOL_GROUP:(g + 1) * POOL_GROUP]
        k = 1
        while k < w:
            s = s + pltpu.roll(s, k, 0)
            k *= 2
        inv = 1.0 / jnp.minimum(pos + 1, w).astype(F32)
        means.append(s[HALO:] * inv)
    mean = jnp.concatenate(means, axis=1)
    y_pool = _pool_project(mean, u_pool, pw_ref, pscale_ref)

    extc = jnp.concatenate([conv_prev[...], u_rnn], axis=0)
    c = u_rnn * cw_ref[CONV_WIDTH - 1:CONV_WIDTH, :]
    for k in range(1, CONV_WIDTH):
        c = c + pltpu.roll(extc, k, 0)[CONV_HALO:] * cw_ref[CONV_WIDTH - 1 - k:CONV_WIDTH - k, :]
    c = c + cb_ref[...]

    a, bt = _gates_and_decay(c, pos == 0, wa_ref, ba_ref, wx_ref, bx_ref, lam_ref)
    k = 1
    while k < TS:
        valid = row >= k
        a_sh = jnp.where(valid, pltpu.roll(a, k, 0), 1.0)
        b_sh = jnp.where(valid, pltpu.roll(bt, k, 0), 0.0)
        bt = bt + a * b_sh
        a = a * a_sh
        k *= 2
    hs = bt + a * h_prev[...]

    pool_prev[...] = u_pool[TS - HALO:]
    conv_prev[...] = u_rnn[TS - CONV_HALO:]
    h_prev[...] = hs[TS - 1:]

    @pl.when(j == n_j - 1)
    def _():
        pool_ref[0] = u_pool[TS - POOL_BUF:]
        conv_ref[0] = u_rnn[TS - (CONV_WIDTH - 1):]
        hT_ref[0] = hs[TS - 1:]

    x1 = x + _merge_out(y_pool, hs, u_gate, gpool_ref, grnn_ref, wout_ref)

    h2 = _rms(x1, gxa_ref[...]).astype(BF16)
    q = (_dot(h2, wq_ref[...]) * (XA_HEAD_DIM ** -0.5)).astype(BF16)
    outs = []
    for hd in range(XA_HEADS):
        sl = slice(hd * XA_HEAD_DIM, (hd + 1) * XA_HEAD_DIM)
        s = _dot_nt(q[:, sl], kb_ref[0, :, sl])
        s = s - jnp.max(s, axis=-1, keepdims=True)
        p = jnp.exp(s)
        p = p / jnp.sum(p, axis=-1, keepdims=True)
        outs.append(_dot(p.astype(BF16), vb_ref[0, :, sl]))
    o = jnp.concatenate(outs, axis=1).astype(BF16)
    x2 = x1 + _dot(o, wo_ref[...])

    h3, xres = _moe_prologue(x2, gffn_ref, sg_ref, su_ref, sd_ref)
    xres_ref[...] = xres
    hb_ref[...] = h3.astype(BF16)
    slot_rows, w_rows, run_len, g_exp, g_rel = _route(h3, wr_ref, rbias_ref, carry[...])
    _store_rows(slot_ref, slot_rows, I32)
    _store_rows(gw_ref, w_rows, F32)
    gexp_ref[0] = g_exp.astype(I32)
    grel_ref[0] = g_rel.astype(I32)
    cover_ref[0] = jnp.broadcast_to(jnp.sum(run_len, axis=0, keepdims=True), cover_ref.shape[1:])
    carry[...] = carry[...] + run_len
    cnt_ref[...] = jnp.broadcast_to(carry[...], cnt_ref.shape)


def _trunk_p(x, kb, vb, mixw, xaw, moew):
    bsz, seq, _ = x.shape
    n_j = seq // TS
    t = bsz * seq
    tok = lambda b, j: (b * n_j + j, 0)
    lane_tok = lambda b, j: (0, b * n_j + j)
    per_b = lambda b, j: (b, 0, 0)
    weights = list(mixw) + list(xaw) + list(moew)
    in_specs = [pl.BlockSpec((1, TS, D_MODEL), lambda b, j: (b, j, 0)),
                pl.BlockSpec((1, N_MEM, D_MODEL), per_b),
                pl.BlockSpec((1, N_MEM, D_MODEL), per_b)] + [_const_spec(w.shape) for w in weights]
    per_tile = lambda b, j: (b * n_j + j, 0, 0)
    out_shape = [jax.ShapeDtypeStruct((t, D_MODEL), F32),
                 jax.ShapeDtypeStruct((t, D_MODEL), BF16),
                 jax.ShapeDtypeStruct((TOP_K, t), I32),
                 jax.ShapeDtypeStruct((TOP_K, t), F32),
                 jax.ShapeDtypeStruct((bsz * n_j, 1, _group_lanes(TS)), I32),
                 jax.ShapeDtypeStruct((bsz * n_j, 1, _group_lanes(TS)), I32),
                 jax.ShapeDtypeStruct((bsz * n_j, 1, 128), F32),
                 jax.ShapeDtypeStruct((N_EXPERTS, 128), F32),
                 jax.ShapeDtypeStruct((bsz, POOL_BUF, D_POOL), F32),
                 jax.ShapeDtypeStruct((bsz, CONV_WIDTH - 1, D_RNN), F32),
                 jax.ShapeDtypeStruct((bsz, 1, D_RNN), F32)]
    out_specs = [pl.BlockSpec((TS, D_MODEL), tok),
                 pl.BlockSpec((TS, D_MODEL), tok),
                 pl.BlockSpec((TOP_K, TS), lane_tok),
                 pl.BlockSpec((TOP_K, TS), lane_tok),
                 pl.BlockSpec((1, 1, _group_lanes(TS)), per_tile),
                 pl.BlockSpec((1, 1, _group_lanes(TS)), per_tile),
                 pl.BlockSpec((1, 1, 128), per_tile),
                 pl.BlockSpec((N_EXPERTS, 128), lambda b, j: (0, 0)),
                 pl.BlockSpec((1, POOL_BUF, D_POOL), per_b),
                 pl.BlockSpec((1, CONV_WIDTH - 1, D_RNN), per_b),
                 pl.BlockSpec((1, 1, D_RNN), per_b)]
    return pl.pallas_call(
        _trunk_p_kernel,
        grid=(bsz, n_j),
        in_specs=in_specs,
        out_specs=out_specs,
        out_shape=out_shape,
        scratch_shapes=[pltpu.VMEM((HALO, D_POOL), F32), pltpu.VMEM((CONV_HALO, D_RNN), F32),
                        pltpu.VMEM((1, D_RNN), F32), pltpu.VMEM((N_EXPERTS, 1), F32)],
        compiler_params=pltpu.CompilerParams(dimension_semantics=("arbitrary", "arbitrary"),
                                             vmem_limit_bytes=VMEM_LIMIT),
        name="trunk_p",
    )(x, kb, vb, *weights)


def _mix_s_kernel(x_ref, pool_ref, conv_ref, h0_ref,
                  gmix_ref, win_ref, pw_ref, pscale_ref, cw_ref, cb_ref, wa_ref, ba_ref, wx_ref, bx_ref,
                  lam_ref, gpool_ref, grnn_ref, wout_ref, gxa_ref, wq_ref,
                  x1_ref, q_ref, npool_ref, nconv_ref, nh_ref):
    x = x_ref[...]
    h = _rms(x, gmix_ref[...]).astype(BF16)
    z = _dot(h, win_ref[...])
    u_pool = z[:, :D_POOL]
    u_rnn = z[:, D_POOL:D_POOL + D_RNN]
    u_gate = z[:, D_POOL + D_RNN:]

    means = []
    for g, w in enumerate(POOL_WINDOWS):
        sl = slice(g * POOL_GROUP, (g + 1) * POOL_GROUP)
        s = u_pool[:, sl]
        for k in range(1, w):
            s = s + pool_ref[POOL_BUF - k, :, sl]
        means.append(s * (1.0 / min(w, PAST_LEN + 1)))
    mean = jnp.concatenate(means, axis=1)
    y_pool = _pool_project(mean, u_pool, pw_ref, pscale_ref)

    c = u_rnn * cw_ref[CONV_WIDTH - 1:CONV_WIDTH, :]
    for k in range(1, CONV_WIDTH):
        c = c + conv_ref[CONV_WIDTH - 1 - k] * cw_ref[CONV_WIDTH - 1 - k:CONV_WIDTH - k, :]
    c = c + cb_ref[...]
    a, bt = _gates_and_decay(c, None, wa_ref, ba_ref, wx_ref, bx_ref, lam_ref)
    hs = a * h0_ref[...] + bt

    x1 = x + _merge_out(y_pool, hs, u_gate, gpool_ref, grnn_ref, wout_ref)
    x1_ref[...] = x1
    h2 = _rms(x1, gxa_ref[...]).astype(BF16)
    q_ref[...] = _dot(h2, wq_ref[...]) * (XA_HEAD_DIM ** -0.5)

    npool_ref[:POOL_BUF - 1] = pool_ref[1:]
    npool_ref[POOL_BUF - 1] = u_pool
    nconv_ref[:CONV_WIDTH - 2] = conv_ref[1:]
    nconv_ref[CONV_WIDTH - 2] = u_rnn
    nh_ref[...] = hs


def _mix_s(x, pool, conv, h0, mixw, gxa, wq):
    bsz = x.shape[0]
    args = [x, pool, conv, h0] + list(mixw) + [gxa, wq]
    return pl.pallas_call(
        _mix_s_kernel,
        grid=(1,),
        in_specs=[_const_spec(a.shape) for a in args],
        out_specs=[_const_spec((bsz, D_MODEL)), _const_spec((bsz, D_MODEL)), _const_spec(pool.shape),
                   _const_spec(conv.shape), _const_spec((bsz, D_RNN))],
        out_shape=[jax.ShapeDtypeStruct((bsz, D_MODEL), F32), jax.ShapeDtypeStruct((bsz, D_MODEL), F32),
                   jax.ShapeDtypeStruct(pool.shape, F32), jax.ShapeDtypeStruct(conv.shape, F32),
                   jax.ShapeDtypeStruct((bsz, D_RNN), F32)],
        compiler_params=pltpu.CompilerParams(dimension_semantics=("arbitrary",), vmem_limit_bytes=VMEM_LIMIT),
        name="mix_s",
    )(*args)


ATTN_S_BB = 4


def _attn_s_kernel(q_ref, k_ref, v_ref, o_ref):
    q = q_ref[...][:, None]
    s = jnp.sum(k_ref[...] * q, axis=-1, keepdims=True)
    s = s - jnp.max(s, axis=1, keepdims=True)
    p = jnp.exp(s)
    p = p / jnp.sum(p, axis=1, keepdims=True)
    o_ref[...] = jnp.sum(p * v_ref[...], axis=1)


def _attn_s(q, k, v):
    bsz = q.shape[0]
    kv_spec = pl.BlockSpec((ATTN_S_BB, N_MEM, XA_HEADS, XA_HEAD_DIM), lambda i: (i, 0, 0, 0))
    q_spec = pl.BlockSpec((ATTN_S_BB, XA_HEADS, XA_HEAD_DIM), lambda i: (i, 0, 0))
    o = pl.pallas_call(
        _attn_s_kernel,
        grid=(bsz // ATTN_S_BB,),
        in_specs=[q_spec, kv_spec, kv_spec],
        out_specs=q_spec,
        out_shape=jax.ShapeDtypeStruct((bsz, XA_HEADS, XA_HEAD_DIM), F32),
        compiler_params=pltpu.CompilerParams(dimension_semantics=("arbitrary",), vmem_limit_bytes=VMEM_LIMIT),
        name="attn_s",
    )(q.reshape(bsz, XA_HEADS, XA_HEAD_DIM), k, v)
    return o.reshape(bsz, D_MODEL)


def _post_s_kernel(x1_ref, o_ref, wo_ref, gffn_ref, sg_ref, su_ref, sd_ref, wr_ref, rbias_ref,
                   xres_ref, hb_ref, slot_ref, gw_ref, gexp_ref, grel_ref, cover_ref, cnt_ref):
    x2 = x1_ref[...] + _dot(o_ref[...].astype(BF16), wo_ref[...])
    h3, xres = _moe_prologue(x2, gffn_ref, sg_ref, su_ref, sd_ref)
    xres_ref[...] = xres
    hb_ref[...] = h3.astype(BF16)
    slot_rows, w_rows, run_len, g_exp, g_rel = _route(h3, wr_ref, rbias_ref, jnp.zeros((N_EXPERTS, 1), F32))
    _store_rows(slot_ref, slot_rows, I32)
    _store_rows(gw_ref, w_rows, F32)
    gexp_ref[0] = g_exp.astype(I32)
    grel_ref[0] = g_rel.astype(I32)
    cover_ref[0] = jnp.broadcast_to(jnp.sum(run_len, axis=0, keepdims=True), cover_ref.shape[1:])
    cnt_ref[...] = jnp.broadcast_to(run_len, cnt_ref.shape)


def _post_s(x1, o, wo, moew):
    bsz = x1.shape[0]
    args = [x1, o, wo] + list(moew)
    out_shape = [jax.ShapeDtypeStruct((bsz, D_MODEL), F32),
                 jax.ShapeDtypeStruct((bsz, D_MODEL), BF16),
                 jax.ShapeDtypeStruct((TOP_K, bsz), I32),
                 jax.ShapeDtypeStruct((TOP_K, bsz), F32),
                 jax.ShapeDtypeStruct((1, 1, _group_lanes(bsz)), I32),
                 jax.ShapeDtypeStruct((1, 1, _group_lanes(bsz)), I32),
                 jax.ShapeDtypeStruct((1, 1, 128), F32),
                 jax.ShapeDtypeStruct((N_EXPERTS, 128), F32)]
    return pl.pallas_call(
        _post_s_kernel,
        grid=(1,),
        in_specs=[_const_spec(a.shape) for a in args],
        out_specs=[_const_spec(s.shape) for s in out_shape],
        out_shape=out_shape,
        compiler_params=pltpu.CompilerParams(dimension_semantics=("arbitrary",), vmem_limit_bytes=VMEM_LIMIT),
        name="post_s",
    )(*args)


def _plan_kernel(rp_ref, rs_ref, base_p_ref, base_s_ref, fill_at_ref, fill_n_ref, exp_ref, nreal_ref):
    n_blocks = exp_ref.shape[0]
    shift = BM.bit_length() - 1

    def per_expert(e, carry):
        blk0, last_e = carry
        rows = rp_ref[e] + rs_ref[e]
        start = lax.shift_left(blk0, shift)
        n_blk = lax.shift_right_logical(rows + (BM - 1), shift)
        base_p_ref[e] = start
        base_s_ref[e] = start + rp_ref[e]
        fill_at_ref[e] = start + rows
        fill_n_ref[e] = _groups(lax.shift_left(n_blk, shift) - rows)

        def per_block(j, _):
            exp_ref[blk0 + j] = e
            return _

        lax.fori_loop(0, n_blk, per_block, 0)
        return blk0 + n_blk, jnp.where(rows > 0, e, last_e)

    n_real, last_e = lax.fori_loop(0, N_EXPERTS, per_expert, (jnp.int32(0), jnp.int32(0)))
    nreal_ref[0] = n_real

    def rest(b, _):
        exp_ref[b] = last_e
        return _

    lax.fori_loop(n_real, n_blocks, rest, 0)


def _plan(rows_p, rows_s, n_rows):
    assert BM & (BM - 1) == 0 and n_rows % BM == 0
    smem = pl.BlockSpec(memory_space=pltpu.SMEM)
    return pl.pallas_call(
        _plan_kernel,
        in_specs=[smem, smem],
        out_specs=[smem] * 6,
        out_shape=[jax.ShapeDtypeStruct((N_EXPERTS,), I32)] * 4
        + [jax.ShapeDtypeStruct((n_rows // BM,), I32), jax.ShapeDtypeStruct((1,), I32)],
        name="plan",
    )(rows_p, rows_s)


def _groups(n_rows):
    return lax.shift_right_logical(n_rows, RUN_ALIGN.bit_length() - 1)


GROUP_UNROLL = 8


def _for_each_group(n_groups, grow_ref, fn):
    def one(g):
        fn(pl.multiple_of(g * RUN_ALIGN, RUN_ALIGN), pl.multiple_of(grow_ref[0, 0, g], RUN_ALIGN))

    def several(j, _):
        for u in range(GROUP_UNROLL):
            one(j * GROUP_UNROLL + u)
        return _

    def single(g, _):
        one(g)
        return _

    n_full = lax.shift_right_logical(n_groups, GROUP_UNROLL.bit_length() - 1)
    lax.fori_loop(0, n_full, several, 0)
    lax.fori_loop(n_full * GROUP_UNROLL, n_groups, single, 0)


def _wait_groups(n_groups, group_wait, bulk_wait):
    def bulk(j, _):
        bulk_wait()
        return _

    def single(g, _):
        group_wait()
        return _

    n_full = lax.shift_right_logical(n_groups, GROUP_UNROLL.bit_length() - 1)
    lax.fori_loop(0, n_full, bulk, 0)
    lax.fori_loop(n_full * GROUP_UNROLL, n_groups, single, 0)


def _group_rows_kernel(gexp_ref, grel_ref, base_ref, out_ref):
    rows = grel_ref[...]
    e = gexp_ref[...]
    for ex in range(N_EXPERTS):
        rows = rows + jnp.where(e == ex, base_ref[ex], 0)
    out_ref[...] = rows


def _group_rows(gexp, grel, base):
    shape = gexp.shape
    flat = (shape[0], shape[-1])
    out = pl.pallas_call(
        _group_rows_kernel,
        in_specs=[pl.BlockSpec(memory_space=pltpu.VMEM), pl.BlockSpec(memory_space=pltpu.VMEM),
                  pl.BlockSpec(memory_space=pltpu.SMEM)],
        out_specs=pl.BlockSpec(memory_space=pltpu.VMEM),
        out_shape=jax.ShapeDtypeStruct(flat, I32),
        name="group_rows",
    )(gexp.reshape(flat), grel.reshape(flat), base)
    return out.reshape(shape)


def _dispatch_kernel(cover_p, cover_s, fill_at, fill_n, nreal_ref, grow_p, grow_s,
                     slot_p_ref, h_p_ref, slot_s_ref, h_s_ref, xs_ref, sbuf, zbuf, sem):
    i = pl.program_id(0)
    last = pl.num_programs(0) - 1
    cur = lax.rem(i, 2)

    def drain(s, n_rows):
        def wait_rows(n):
            pltpu.make_async_copy(sbuf.at[s, pl.ds(0, n)], xs_ref.at[pl.ds(0, n)], sem.at[s]).wait()
        _wait_groups(_groups(n_rows), lambda: wait_rows(RUN_ALIGN), lambda: wait_rows(RUN_ALIGN * GROUP_UNROLL))

    def tile(n_rows, grow_ref, slot_ref, h_ref):
        for c in range(_sorted_rows(h_ref.shape[0]) // SORT_CHUNK):
            @pl.when(c * SORT_CHUNK < n_rows)
            def _():
                rid = c * SORT_CHUNK + lax.broadcasted_iota(I32, (SORT_CHUNK, 1), 0)
                hit = rid == slot_ref[0:1, :]
                for k in range(1, TOP_K):
                    hit = hit | (rid == slot_ref[k:k + 1, :])
                sbuf[cur, c * SORT_CHUNK:(c + 1) * SORT_CHUNK] = _dot(hit.astype(BF16), h_ref[...]).astype(BF16)

        def send(local, glob):
            pltpu.make_async_copy(sbuf.at[cur, pl.ds(local, RUN_ALIGN)], xs_ref.at[pl.ds(glob, RUN_ALIGN)],
                                  sem.at[cur]).start()

        _for_each_group(_groups(n_rows), grow_ref, send)

    @pl.when(i >= 2)
    def _():
        drain(cur, cover_p[i - 2])

    @pl.when(i < last)
    def _():
        tile(cover_p[i], grow_p, slot_p_ref, h_p_ref)

    @pl.when(i == last)
    def _():
        tile(cover_s[0], grow_s, slot_s_ref, h_s_ref)
        drain(1 - cur, cover_p[last - 1])
        drain(cur, cover_s[0])
        zbuf[...] = jnp.zeros_like(zbuf)

        def group_fill(e, g):
            return pltpu.make_async_copy(
                zbuf.at[pl.ds(0, RUN_ALIGN)],
                xs_ref.at[pl.ds(pl.multiple_of(fill_at[e] + g * RUN_ALIGN, RUN_ALIGN), RUN_ALIGN)], sem.at[2])

        def per_expert(e, n):
            def start(g, _):
                group_fill(e, g).start()
                return _
            lax.fori_loop(0, fill_n[e], start, 0)
            return n + fill_n[e]

        n_fill = lax.fori_loop(0, N_EXPERTS, per_expert, jnp.int32(0))

        def wait_group(g, _):
            group_fill(0, 0).wait()
            return _

        lax.fori_loop(0, n_fill, wait_group, 0)

        n_real = nreal_ref[0]
        n_tail = xs_ref.shape[0] // BM - n_real

        def blk_fill(b):
            return pltpu.make_async_copy(zbuf, xs_ref.at[pl.ds(pl.multiple_of((n_real + b) * BM, BM), BM)], sem.at[2])

        def start_blk(b, _):
            blk_fill(b).start()
            return _

        def wait_blk(b, _):
            blk_fill(b).wait()
            return _

        lax.fori_loop(0, n_tail, start_blk, 0)
        lax.fori_loop(0, n_tail, wait_blk, 0)


def _dispatch(cover_p, cover_s, fills, grow_p, grow_s, slot_p, hb_p, slot_s, hb_s, n_rows):
    n_p = hb_p.shape[0] // TS
    smem = pl.BlockSpec(memory_space=pltpu.SMEM)
    clamp = lambda i: jnp.minimum(i, n_p - 1)
    per_tile = pl.BlockSpec((1, 1, grow_p.shape[-1]), lambda i: (clamp(i), 0, 0), memory_space=pltpu.SMEM)
    return pl.pallas_call(
        _dispatch_kernel,
        grid=(n_p + 1,),
        in_specs=[smem] * 5 + [per_tile, smem,
                               pl.BlockSpec((TOP_K, TS), lambda i: (0, clamp(i))),
                               pl.BlockSpec((TS, D_MODEL), lambda i: (clamp(i), 0)),
                               _const_spec(slot_s.shape), _const_spec(hb_s.shape)],
        out_specs=pl.BlockSpec(memory_space=pl.ANY),
        out_shape=jax.ShapeDtypeStruct((n_rows, D_MODEL), BF16),
        scratch_shapes=[pltpu.VMEM((2, _sorted_rows(TS), D_MODEL), BF16),
                        pltpu.VMEM((BM, D_MODEL), BF16), pltpu.SemaphoreType.DMA((3,))],
        compiler_params=pltpu.CompilerParams(dimension_semantics=("arbitrary",), has_side_effects=True,
                                             vmem_limit_bytes=VMEM_LIMIT),
        name="dispatch",
    )(cover_p, cover_s, *fills, grow_p, grow_s, slot_p, hb_p, slot_s, hb_s)


GMM_LANES = 2


def _gmm_kernel(exp_ref, nreal_ref, *refs):
    x_ref = refs[0]
    w_refs = refs[1:1 + 3 * GMM_LANES]
    y_ref = refs[1 + 3 * GMM_LANES]
    scratch = refs[2 + 3 * GMM_LANES:]
    s = pl.program_id(0)
    n_real = nreal_ref[0]

    for lane in range(GMM_LANES):
        wg_ref, wu_ref, wd_ref = w_refs[3 * lane:3 * lane + 3]
        wgu, wdn = scratch[2 * lane:2 * lane + 2]
        b = s * GMM_LANES + lane
        new_expert = (s == 0) | (exp_ref[b] != exp_ref[jnp.maximum(b - GMM_LANES, 0)])

        @pl.when((b < n_real) & new_expert)
        def _():
            wgu[:, :D_EXPERT] = wg_ref[0].astype(BF16)
            wgu[:, D_EXPERT:] = wu_ref[0].astype(BF16)
            wdn[...] = wd_ref[0].astype(BF16)

    @pl.when(s * GMM_LANES < n_real)
    def _():
        for lane in range(GMM_LANES):
            wgu, wdn = scratch[2 * lane:2 * lane + 2]
            rows = slice(lane * BM, (lane + 1) * BM)
            gu = _dot(x_ref[rows, :], wgu[...])
            act = jax.nn.silu(gu[:, :D_EXPERT]) * gu[:, D_EXPERT:]
            y = _dot(act.astype(BF16), wdn[...]).astype(BF16)
            y_ref[rows, :] = jnp.where(s * GMM_LANES + lane < n_real, y, jnp.zeros_like(y))

    @pl.when(s * GMM_LANES >= n_real)
    def _():
        y_ref[...] = jnp.zeros_like(y_ref)


def _gmm(blk_exp, n_real, xs, wg, wu, wd):
    n_rows = xs.shape[0]
    step_rows = BM * GMM_LANES
    assert n_rows % step_rows == 0
    last_step = lambda nreal: (nreal[0] - 1) // GMM_LANES
    w_specs, scratch = [], []
    for lane in range(GMM_LANES):
        weight = lambda s, exp, nreal, lane=lane: (exp[s * GMM_LANES + lane], 0, 0)
        w_specs += [pl.BlockSpec((1, D_MODEL, D_EXPERT), weight), pl.BlockSpec((1, D_MODEL, D_EXPERT), weight),
                    pl.BlockSpec((1, D_EXPERT, D_MODEL), weight)]
        scratch += [pltpu.VMEM((D_MODEL, 2 * D_EXPERT), BF16), pltpu.VMEM((D_EXPERT, D_MODEL), BF16)]
    grid_spec = pltpu.PrefetchScalarGridSpec(
        num_scalar_prefetch=2,
        grid=(n_rows // step_rows,),
        in_specs=[pl.BlockSpec((step_rows, D_MODEL), lambda s, exp, nreal: (jnp.minimum(s, last_step(nreal)), 0))]
        + w_specs,
        out_specs=pl.BlockSpec((step_rows, D_MODEL), lambda s, exp, nreal: (s, 0)),
        scratch_shapes=scratch,
    )
    return pl.pallas_call(
        _gmm_kernel,
        grid_spec=grid_spec,
        out_shape=jax.ShapeDtypeStruct((n_rows, D_MODEL), BF16),
        compiler_params=pltpu.CompilerParams(dimension_semantics=("arbitrary",), vmem_limit_bytes=VMEM_LIMIT),
        name="gmm",
    )(blk_exp, n_real, xs, *([wg, wu, wd] * GMM_LANES))


def _combine_kernel(cover_ref, grow_ref, grow_next_ref,
                    slot_ref, gw_ref, xres_ref, gfin_ref, ys_ref, out_ref, ybuf, sem):
    i = pl.program_id(0)
    n = pl.num_programs(0)
    cur = lax.rem(i, 2)
    rows = ybuf.shape[1]

    def rows_copy(s, local, glob, n_rows):
        return pltpu.make_async_copy(ys_ref.at[pl.ds(glob, n_rows)], ybuf.at[s, pl.ds(local, n_rows)], sem.at[s])

    def gather(n_rows, gr_ref, s):
        _for_each_group(_groups(n_rows), gr_ref, lambda l, g: rows_copy(s, l, g, RUN_ALIGN).start())

    @pl.when(i == 0)
    def _():
        ybuf[...] = jnp.zeros_like(ybuf)
        gather(cover_ref[0], grow_ref, 0)

    @pl.when(i + 1 < n)
    def _():
        gather(cover_ref[jnp.minimum(i + 1, n - 1)], grow_next_ref, 1 - cur)

    _wait_groups(_groups(cover_ref[i]), lambda: rows_copy(cur, 0, 0, RUN_ALIGN).wait(),
                 lambda: rows_copy(cur, 0, 0, RUN_ALIGN * GROUP_UNROLL).wait())

    out_ref[...] = xres_ref[...]
    tm = slot_ref.shape[0]
    slot_b = [jnp.broadcast_to(slot_ref[:, k:k + 1], (tm, 128)) for k in range(TOP_K)]
    gw_b = [jnp.broadcast_to(gw_ref[:, k:k + 1], (tm, 128)) for k in range(TOP_K)]
    lane = lax.broadcasted_iota(I32, (1, 128), 1)
    for c in range(rows // SORT_CHUNK):
        @pl.when(c * SORT_CHUNK < cover_ref[i])
        def _():
            pieces = []
            for p in range(SORT_CHUNK // 128):
                col = lane + (c * SORT_CHUNK + p * 128)
                w = jnp.zeros((tm, 128), F32)
                for k in range(TOP_K):
                    w = jnp.where(slot_b[k] == col, gw_b[k], w)
                pieces.append(w.astype(BF16))
            w = jnp.concatenate(pieces, axis=1)
            out_ref[...] += _dot(w, ybuf[cur, c * SORT_CHUNK:(c + 1) * SORT_CHUNK])
    out_ref[...] = _rms(out_ref[...], gfin_ref[...])


def _combine(cover, grow, slot_t, gw_t, xres, gfin, ys):
    t = xres.shape[0]
    tm = min(t, TS)
    n = t // tm
    smem = pl.BlockSpec(memory_space=pltpu.SMEM)
    lanes = grow.shape[-1]
    this_tile = pl.BlockSpec((1, 1, lanes), lambda i: (i, 0, 0), memory_space=pltpu.SMEM)
    next_tile = pl.BlockSpec((1, 1, lanes), lambda i: (jnp.minimum(i + 1, n - 1), 0, 0), memory_space=pltpu.SMEM)
    return pl.pallas_call(
        _combine_kernel,
        grid=(n,),
        in_specs=[smem, this_tile, next_tile,
                  pl.BlockSpec((tm, TOP_K), lambda i: (i, 0)),
                  pl.BlockSpec((tm, TOP_K), lambda i: (i, 0)),
                  pl.BlockSpec((tm, D_MODEL), lambda i: (i, 0)),
                  _const_spec((1, D_MODEL)),
                  pl.BlockSpec(memory_space=pl.ANY)],
        out_specs=pl.BlockSpec((tm, D_MODEL), lambda i: (i, 0)),
        out_shape=jax.ShapeDtypeStruct((t, D_MODEL), F32),
        scratch_shapes=[pltpu.VMEM((2, _sorted_rows(tm), D_MODEL), BF16), pltpu.SemaphoreType.DMA((2,))],
        compiler_params=pltpu.CompilerParams(dimension_semantics=("arbitrary",), vmem_limit_bytes=VMEM_LIMIT),
        name="combine",
    )(cover, grow, grow, slot_t, gw_t, xres, gfin, ys)


def _block_diag_pairs(w):
    n_h, d, _ = w.shape
    half = n_h // 2
    out = jnp.zeros((2, half * d, half * d), F32)
    for hh in range(n_h):
        p, q = divmod(hh, half)
        out = out.at[p, q * d:(q + 1) * d, q * d:(q + 1) * d].set(w[hh])
    return out.astype(BF16)


def kernel(x_prompt, x_sample, state_pool, state_conv, state_h, cache_mem_k, cache_mem_v, mem_prompt, norm_mix, w_in, pool_w, pool_scale, conv_w, conv_b, gate_a_w, gate_a_b, gate_x_w, gate_x_b, lru_lambda, norm_pool_out, norm_rnn_out, w_out, norm_xattn, norm_mem, xa_wq, xa_wk, xa_wv, xa_wo, norm_ffn, router_w, router_bias, exp_w_gate, exp_w_up, exp_w_down, sh_w_gate, sh_w_up, sh_w_down, norm_final):
    bp, seq, _ = x_prompt.shape
    bs = x_sample.shape[0]
    tp = bp * seq
    row = lambda v: v.reshape(1, -1)
    bf = lambda v: v.astype(BF16)

    mixw = (row(norm_mix[0]), bf(w_in[0]), _block_diag_pairs(pool_w[0]), row(pool_scale[0]), conv_w[0],
            row(conv_b[0]), _block_diag_pairs(gate_a_w[0]), row(gate_a_b[0]), _block_diag_pairs(gate_x_w[0]),
            row(gate_x_b[0]), row(lru_lambda[0]), row(norm_pool_out[0]), row(norm_rnn_out[0]), bf(w_out[0]))
    xaw = (row(norm_xattn[0]), bf(xa_wq[0]), bf(xa_wo[0]))
    moew = (row(norm_ffn[0]), bf(sh_w_gate[0]), bf(sh_w_up[0]), bf(sh_w_down[0]), router_w[0].T,
            router_bias[0].reshape(N_EXPERTS, 1))

    mk, mv, kb, vb = _memkv(mem_prompt, row(norm_mem[0]), bf(xa_wk[0]), bf(xa_wv[0]))
    (xres_p, hb_p, slot_p, gw_p, gexp_p, grel_p, cover_p, cnt_p, pool_p, conv_p, h_p) = _trunk_p(
        x_prompt, kb, vb, mixw, xaw, moew)

    x1_s, q_s, pool_s, conv_s, h_s = _mix_s(x_sample.reshape(bs, D_MODEL), state_pool[0].transpose(1, 0, 2),
                                            state_conv[0].transpose(1, 0, 2), state_h[0], mixw, xaw[0], xaw[1])
    pool_s = pool_s.transpose(1, 0, 2)
    conv_s = conv_s.transpose(1, 0, 2)
    o_s = _attn_s(q_s, cache_mem_k[0], cache_mem_v[0])
    xres_s, hb_s, slot_s, gw_s, gexp_s, grel_s, cover_s, cnt_s = _post_s(x1_s, o_s, xaw[2], moew)

    n_tiles = tp // TS + 1
    n_rows = _round_up((tp + bs) * TOP_K + n_tiles * N_EXPERTS * (RUN_ALIGN - 1) + N_EXPERTS * (BM - 1),
                       BM * GMM_LANES)
    ints = lambda v: v[..., 0].astype(I32)
    base_p, base_s, fill_at, fill_n, blk_exp, n_real = _plan(ints(cnt_p), ints(cnt_s), n_rows)
    cover_p, cover_s = ints(cover_p).reshape(-1), ints(cover_s).reshape(-1)
    grow_p = _group_rows(gexp_p, grel_p, base_p)
    grow_s = _group_rows(gexp_s, grel_s, base_s)

    xs = _dispatch(cover_p, cover_s, (fill_at, fill_n, n_real), grow_p, grow_s, slot_p, hb_p, slot_s, hb_s, n_rows)
    ys = _gmm(blk_exp, n_real, xs, exp_w_gate[0], exp_w_up[0], exp_w_down[0])

    gfin = row(norm_final)
    y_p = _combine(cover_p, grow_p, slot_p.T, gw_p.T, xres_p, gfin, ys)
    y_s = _combine(cover_s, grow_s, slot_s.T, gw_s.T, xres_s, gfin, ys)

    return (y_p.reshape(bp, seq, D_MODEL), y_s.reshape(bs, 1, D_MODEL),
            pool_p[None], conv_p[None], h_p.reshape(1, bp, D_RNN),
            mk.reshape(1, bp, N_MEM, XA_HEADS, XA_HEAD_DIM), mv.reshape(1, bp, N_MEM, XA_HEADS, XA_HEAD_DIM),
            pool_s[None], conv_s[None], h_s[None])
```

```python
import functools

import jax
import jax.numpy as jnp
from jax import lax
from jax.experimental import pallas as pl
from jax.experimental.pallas import tpu as pltpu

F32 = jnp.float32
BF16 = jnp.bfloat16
I32 = jnp.int32
U32 = jnp.uint32

D_MODEL = 1024
D_POOL = 512
D_RNN = 512
D_IN = D_POOL + 2 * D_RNN
POOL_WINDOWS = (2, 4, 8, 16)
POOL_GROUP = 128
POOL_BUF = 15
CONV_WIDTH = 4
LRU_C = 8.0
N_MEM = 256
XA_HEADS = 4
XA_HEAD_DIM = 256
N_EXPERTS = 64
TOP_K = 8
N_EXPERT_GROUPS = 8
GROUP_SIZE = N_EXPERTS // N_EXPERT_GROUPS
TOPK_GROUPS = 4
D_EXPERT = 256
ROUTED_SCALE = 2.5
EPS = 1e-6
PAST_LEN = 16384

HALO = 16
CONV_HALO = 8
TS = 256
BM = 512
RUN_ALIGN = 16
RUN_CHUNKS = (64, 32, 16)
SORT_CHUNK = 512
VMEM_LIMIT = 56 * 1024 * 1024


def _round_up(x, m):
    return (x + m - 1) // m * m


def _sorted_rows(tokens):
    return _round_up(tokens * TOP_K + N_EXPERTS * (RUN_ALIGN - 1), SORT_CHUNK)


def _group_lanes(tokens):
    return _round_up(_sorted_rows(tokens) // RUN_ALIGN, 128)


def _const_spec(shape):
    nd = len(shape)
    return pl.BlockSpec(shape, lambda *_: (0,) * nd, pipeline_mode=pl.Buffered(1))


def _rms(x, g):
    ms = jnp.mean(x * x, axis=-1, keepdims=True)
    return x * lax.rsqrt(ms + EPS) * g


def _dot(a, b):
    return jnp.dot(a, b, preferred_element_type=F32)


def _dot_nt(a, b, precision=None):
    return lax.dot_general(a, b, (((1,), (1,)), ((), ())), precision=precision,
                           preferred_element_type=F32)


def _softplus(x):
    return jnp.maximum(x, 0.0) + jnp.log1p(jnp.exp(-jnp.abs(x)))


def _gates_and_decay(c, pos_is_zero, wa_ref, ba_ref, wx_ref, bx_ref, lam_ref):
    cb = c.astype(BF16)
    half = D_RNN // 2
    ga = jnp.concatenate([_dot(cb[:, :half], wa_ref[0]), _dot(cb[:, half:], wa_ref[1])], axis=1) + ba_ref[...]
    gx = jnp.concatenate([_dot(cb[:, :half], wx_ref[0]), _dot(cb[:, half:], wx_ref[1])], axis=1) + bx_ref[...]
    r = jax.nn.sigmoid(ga)
    i = jax.nn.sigmoid(gx)
    log_a = (-LRU_C) * r * _softplus(-lam_ref[...])
    a = jnp.exp(log_a)
    mult = jnp.sqrt(1.0 - a * a)
    if pos_is_zero is not None:
        mult = jnp.where(pos_is_zero, 1.0, mult)
    return a, mult * i * c


def _pool_project(mean, u_pool, pw_ref, pscale_ref):
    d = (mean - u_pool).astype(BF16)
    half = D_POOL // 2
    y = jnp.concatenate([_dot(d[:, :half], pw_ref[0]), _dot(d[:, half:], pw_ref[1])], axis=1)
    return y * pscale_ref[...]


def _merge_out(y_pool, hs, u_gate, gpool_ref, grnn_ref, wout_ref):
    y_rnn = hs * jax.nn.gelu(u_gate)
    merged = jnp.concatenate([_rms(y_pool, gpool_ref[...]), _rms(y_rnn, grnn_ref[...])], axis=1)
    return _dot(merged.astype(BF16), wout_ref[...])


def _route(h3, wr_ref, rbias_ref, before):
    r_tok = h3.shape[0]
    logits = _dot_nt(wr_ref[...], h3, precision=lax.Precision.HIGHEST)
    scores = jax.nn.sigmoid(logits)
    biased = scores + rbias_ref[...]
    neg = jnp.float32(-jnp.inf)
    gs = []
    for g in range(N_EXPERT_GROUPS):
        xg = biased[g * GROUP_SIZE:(g + 1) * GROUP_SIZE]
        m1 = jnp.max(xg, axis=0, keepdims=True)
        eq = xg == m1
        cnt = jnp.sum(eq.astype(F32), axis=0, keepdims=True)
        m2 = jnp.max(jnp.where(eq, neg, xg), axis=0, keepdims=True)
        gs.append(m1 + jnp.where(cnt >= 2.0, m1, m2))
    pieces = []
    for g in range(N_EXPERT_GROUPS):
        beaten = jnp.zeros_like(gs[g])
        for o in range(N_EXPERT_GROUPS):
            if o == g:
                continue
            wins = (gs[o] > gs[g]) | (gs[o] == gs[g]) if o < g else (gs[o] > gs[g])
            beaten = beaten + wins.astype(F32)
        keep = beaten < float(TOPK_GROUPS)
        xg = biased[g * GROUP_SIZE:(g + 1) * GROUP_SIZE]
        pieces.append(jnp.where(keep, xg, neg))
    cur = jnp.concatenate(pieces, axis=0)
    eid = lax.broadcasted_iota(I32, (N_EXPERTS, r_tok), 0).astype(F32)
    idx_rows, score_rows = [], []
    sel = jnp.zeros((N_EXPERTS, r_tok), F32)
    for _ in range(TOP_K):
        m = jnp.max(cur, axis=0, keepdims=True)
        idx = jnp.min(jnp.where(cur == m, eid, float(N_EXPERTS)), axis=0, keepdims=True)
        oh = eid == idx
        score_rows.append(jnp.sum(jnp.where(oh, scores, 0.0), axis=0, keepdims=True))
        idx_rows.append(idx)
        sel = sel + oh.astype(F32)
        cur = jnp.where(oh, neg, cur)
    tot = score_rows[0]
    for s in score_rows[1:]:
        tot = tot + s
    w_rows = [s / tot * ROUTED_SCALE for s in score_rows]
    rr = lax.broadcasted_iota(I32, (r_tok, r_tok), 0)
    cc = lax.broadcasted_iota(I32, (r_tok, r_tok), 1)
    earlier = _dot(sel.astype(BF16), (rr < cc).astype(BF16))
    counts = jnp.sum(sel, axis=1, keepdims=True)
    run_len = jnp.floor((counts + (RUN_ALIGN - 1.0)) * (1.0 / RUN_ALIGN)) * RUN_ALIGN
    er = lax.broadcasted_iota(I32, (N_EXPERTS, N_EXPERTS), 0)
    ec = lax.broadcasted_iota(I32, (N_EXPERTS, N_EXPERTS), 1)
    run_start = _dot((ec < er).astype(BF16), jnp.broadcast_to(run_len, (N_EXPERTS, 128)).astype(BF16))[:, :1]
    slot = earlier + run_start
    slot_rows = [jnp.sum(jnp.where(eid == idx, slot, 0.0), axis=0, keepdims=True) for idx in idx_rows]
    n_lanes = _group_lanes(r_tok)
    g_row = lax.broadcasted_iota(I32, (N_EXPERTS, n_lanes), 1).astype(F32) * float(RUN_ALIGN)
    owns = (run_start <= g_row) & (g_row < run_start + run_len)
    e_col = lax.broadcasted_iota(I32, (N_EXPERTS, n_lanes), 0).astype(F32)
    g_exp = jnp.sum(jnp.where(owns, e_col, 0.0), axis=0, keepdims=True)
    g_rel = jnp.sum(jnp.where(owns, before + g_row - run_start, 0.0), axis=0, keepdims=True)
    return slot_rows, w_rows, run_len, g_exp, g_rel


def _moe_prologue(x2, gffn_ref, sg_ref, su_ref, sd_ref):
    h3 = _rms(x2, gffn_ref[...])
    h3b = h3.astype(BF16)
    act = jax.nn.silu(_dot(h3b, sg_ref[...])) * _dot(h3b, su_ref[...])
    shared = _dot(act.astype(BF16), sd_ref[...])
    return h3, x2 + shared


def _store_rows(ref, rows, dtype):
    for k, row in enumerate(rows):
        ref[k:k + 1, :] = row.astype(dtype)


def _memkv_kernel(mem_ref, g_ref, wk_ref, wv_ref, k_ref, v_ref, kb_ref, vb_ref):
    m = _rms(mem_ref[0], g_ref[...]).astype(BF16)
    k = _dot(m, wk_ref[...])
    v = _dot(m, wv_ref[...])
    k_ref[0] = k
    v_ref[0] = v
    kb_ref[0] = k.astype(BF16)
    vb_ref[0] = v.astype(BF16)


def _memkv(mem, g, wk, wv):
    b = mem.shape[0]
    blk = pl.BlockSpec((1, N_MEM, D_MODEL), lambda i: (i, 0, 0))
    return pl.pallas_call(
        _memkv_kernel,
        grid=(b,),
        in_specs=[blk, _const_spec((1, D_MODEL)), _const_spec((D_MODEL, D_MODEL)), _const_spec((D_MODEL, D_MODEL))],
        out_specs=[blk, blk, blk, blk],
        out_shape=[jax.ShapeDtypeStruct((b, N_MEM, D_MODEL), F32)] * 2
        + [jax.ShapeDtypeStruct((b, N_MEM, D_MODEL), BF16)] * 2,
        compiler_params=pltpu.CompilerParams(dimension_semantics=("arbitrary",), vmem_limit_bytes=VMEM_LIMIT),
        name="memkv",
    )(mem, g, wk, wv)


def _trunk_p_kernel(x_ref, kb_ref, vb_ref,
                    gmix_ref, win_ref, pw_ref, pscale_ref, cw_ref, cb_ref, wa_ref, ba_ref, wx_ref, bx_ref,
                    lam_ref, gpool_ref, grnn_ref, wout_ref,
                    gxa_ref, wq_ref, wo_ref,
                    gffn_ref, sg_ref, su_ref, sd_ref, wr_ref, rbias_ref,
                    xres_ref, hb_ref, slot_ref, gw_ref, gexp_ref, grel_ref, cover_ref, cnt_ref,
                    pool_ref, conv_ref, hT_ref,
                    pool_prev, conv_prev, h_prev, carry):
    b = pl.program_id(0)
    j = pl.program_id(1)
    n_j = pl.num_programs(1)

    @pl.when(j == 0)
    def _():
        pool_prev[...] = jnp.zeros_like(pool_prev)
        conv_prev[...] = jnp.zeros_like(conv_prev)
        h_prev[...] = jnp.zeros_like(h_prev)

    @pl.when((b == 0) & (j == 0))
    def _():
        carry[...] = jnp.zeros_like(carry)

    x = x_ref[0]
    row = lax.broadcasted_iota(I32, (TS, 1), 0)
    pos = j * TS + row

    h = _rms(x, gmix_ref[...]).astype(BF16)
    z = _dot(h, win_ref[...])
    u_pool = z[:, :D_POOL]
    u_rnn = z[:, D_POOL:D_POOL + D_RNN]
    u_gate = z[:, D_POOL + D_RNN:]

    ext = jnp.concatenate([pool_prev[...], u_pool], axis=0)
    means = []
    for g, w in enumerate(POOL_WINDOWS):
        s = ext[:, g * POOL_GROUP:(g + 1) * POOL_GROUP]
        k = 1
        while k < w:
            s = s + pltpu.roll(s, k, 0)
            k *= 2
        inv = 1.0 / jnp.minimum(pos + 1, w).astype(F32)
        means.append(s[HALO:] * inv)
    mean = jnp.concatenate(means, axis=1)
    y_pool = _pool_project(mean, u_pool, pw_ref, pscale_ref)

    extc = jnp.concatenate([conv_prev[...], u_rnn], axis=0)
    c = u_rnn * cw_ref[CONV_WIDTH - 1:CONV_WIDTH, :]
    for k in range(1, CONV_WIDTH):
        c = c + pltpu.roll(extc, k, 0)[CONV_HALO:] * cw_ref[CONV_WIDTH - 1 - k:CONV_WIDTH - k, :]
    c = c + cb_ref[...]

    a, bt = _gates_and_decay(c, pos == 0, wa_ref, ba_ref, wx_ref, bx_ref, lam_ref)
    k = 1
    while k < TS:
        valid = row >= k
        a_sh = jnp.where(valid, pltpu.roll(a, k, 0), 1.0)
        b_sh = jnp.where(valid, pltpu.roll(bt, k, 0), 0.0)
        bt = bt + a * b_sh
        a = a * a_sh
        k *= 2
    hs = bt + a * h_prev[...]

    pool_prev[...] = u_pool[TS - HALO:]
    conv_prev[...] = u_rnn[TS - CONV_HALO:]
    h_prev[...] = hs[TS - 1:]

    @pl.when(j == n_j - 1)
    def _():
        pool_ref[0] = u_pool[TS - POOL_BUF:]
        conv_ref[0] = u_rnn[TS - (CONV_WIDTH - 1):]
        hT_ref[0] = hs[TS - 1:]

    x1 = x + _merge_out(y_pool, hs, u_gate, gpool_ref, grnn_ref, wout_ref)

    h2 = _rms(x1, gxa_ref[...]).astype(BF16)
    q = (_dot(h2, wq_ref[...]) * (XA_HEAD_DIM ** -0.5)).astype(BF16)
    outs = []
    for hd in range(XA_HEADS):
        sl = slice(hd * XA_HEAD_DIM, (hd + 1) * XA_HEAD_DIM)
        s = _dot_nt(q[:, sl], kb_ref[0, :, sl])
        s = s - jnp.max(s, axis=-1, keepdims=True)
        p = jnp.exp(s)
        p = p / jnp.sum(p, axis=-1, keepdims=True)
        outs.append(_dot(p.astype(BF16), vb_ref[0, :, sl]))
    o = jnp.concatenate(outs, axis=1).astype(BF16)
    x2 = x1 + _dot(o, wo_ref[...])

    h3, xres = _moe_prologue(x2, gffn_ref, sg_ref, su_ref, sd_ref)
    xres_ref[...] = xres
    hb_ref[...] = h3.astype(BF16)
    slot_rows, w_rows, run_len, g_exp, g_rel = _route(h3, wr_ref, rbias_ref, carry[...])
    _store_rows(slot_ref, slot_rows, I32)
    _store_rows(gw_ref, w_rows, F32)
    gexp_ref[0] = g_exp.astype(I32)
    grel_ref[0] = g_rel.astype(I32)
    cover_ref[0] = jnp.broadcast_to(jnp.sum(run_len, axis=0, keepdims=True), cover_ref.shape[1:])
    carry[...] = carry[...] + run_len
    cnt_ref[...] = jnp.broadcast_to(carry[...], cnt_ref.shape)


def _trunk_p(x, kb, vb, mixw, xaw, moew):
    bsz, seq, _ = x.shape
    n_j = seq // TS
    t = bsz * seq
    tok = lambda b, j: (b * n_j + j, 0)
    lane_tok = lambda b, j: (0, b * n_j + j)
    per_b = lambda b, j: (b, 0, 0)
    weights = list(mixw) + list(xaw) + list(moew)
    in_specs = [pl.BlockSpec((1, TS, D_MODEL), lambda b, j: (b, j, 0)),
                pl.BlockSpec((1, N_MEM, D_MODEL), per_b),
                pl.BlockSpec((1, N_MEM, D_MODEL), per_b)] + [_const_spec(w.shape) for w in weights]
    per_tile = lambda b, j: (b * n_j + j, 0, 0)
    out_shape = [jax.ShapeDtypeStruct((t, D_MODEL), F32),
                 jax.ShapeDtypeStruct((t, D_MODEL), BF16),
                 jax.ShapeDtypeStruct((TOP_K, t), I32),
                 jax.ShapeDtypeStruct((TOP_K, t), F32),
                 jax.ShapeDtypeStruct((bsz * n_j, 1, _group_lanes(TS)), I32),
                 jax.ShapeDtypeStruct((bsz * n_j, 1, _group_lanes(TS)), I32),
                 jax.ShapeDtypeStruct((bsz * n_j, 1, 128), F32),
                 jax.ShapeDtypeStruct((N_EXPERTS, 128), F32),
                 jax.ShapeDtypeStruct((bsz, POOL_BUF, D_POOL), F32),
                 jax.ShapeDtypeStruct((bsz, CONV_WIDTH - 1, D_RNN), F32),
                 jax.ShapeDtypeStruct((bsz, 1, D_RNN), F32)]
    out_specs = [pl.BlockSpec((TS, D_MODEL), tok),
                 pl.BlockSpec((TS, D_MODEL), tok),
                 pl.BlockSpec((TOP_K, TS), lane_tok),
                 pl.BlockSpec((TOP_K, TS), lane_tok),
                 pl.BlockSpec((1, 1, _group_lanes(TS)), per_tile),
                 pl.BlockSpec((1, 1, _group_lanes(TS)), per_tile),
                 pl.BlockSpec((1, 1, 128), per_tile),
                 pl.BlockSpec((N_EXPERTS, 128), lambda b, j: (0, 0)),
                 pl.BlockSpec((1, POOL_BUF, D_POOL), per_b),
                 pl.BlockSpec((1, CONV_WIDTH - 1, D_RNN), per_b),
                 pl.BlockSpec((1, 1, D_RNN), per_b)]
    return pl.pallas_call(
        _trunk_p_kernel,
        grid=(bsz, n_j),
        in_specs=in_specs,
        out_specs=out_specs,
        out_shape=out_shape,
        scratch_shapes=[pltpu.VMEM((HALO, D_POOL), F32), pltpu.VMEM((CONV_HALO, D_RNN), F32),
                        pltpu.VMEM((1, D_RNN), F32), pltpu.VMEM((N_EXPERTS, 1), F32)],
        compiler_params=pltpu.CompilerParams(dimension_semantics=("arbitrary", "arbitrary"),
                                             vmem_limit_bytes=VMEM_LIMIT),
        name="trunk_p",
    )(x, kb, vb, *weights)


def _mix_s_kernel(x_ref, pool_ref, conv_ref, h0_ref,
                  gmix_ref, win_ref, pw_ref, pscale_ref, cw_ref, cb_ref, wa_ref, ba_ref, wx_ref, bx_ref,
                  lam_ref, gpool_ref, grnn_ref, wout_ref, gxa_ref, wq_ref,
                  x1_ref, q_ref, npool_ref, nconv_ref, nh_ref):
    x = x_ref[...]
    h = _rms(x, gmix_ref[...]).astype(BF16)
    z = _dot(h, win_ref[...])
    u_pool = z[:, :D_POOL]
    u_rnn = z[:, D_POOL:D_POOL + D_RNN]
    u_gate = z[:, D_POOL + D_RNN:]

    means = []
    for g, w in enumerate(POOL_WINDOWS):
        sl = slice(g * POOL_GROUP, (g + 1) * POOL_GROUP)
        s = u_pool[:, sl]
        for k in range(1, w):
            s = s + pool_ref[POOL_BUF - k, :, sl]
        means.append(s * (1.0 / min(w, PAST_LEN + 1)))
    mean = jnp.concatenate(means, axis=1)
    y_pool = _pool_project(mean, u_pool, pw_ref, pscale_ref)

    c = u_rnn * cw_ref[CONV_WIDTH - 1:CONV_WIDTH, :]
    for k in range(1, CONV_WIDTH):
        c = c + conv_ref[CONV_WIDTH - 1 - k] * cw_ref[CONV_WIDTH - 1 - k:CONV_WIDTH - k, :]
    c = c + cb_ref[...]
    a, bt = _gates_and_decay(c, None, wa_ref, ba_ref, wx_ref, bx_ref, lam_ref)
    hs = a * h0_ref[...] + bt

    x1 = x + _merge_out(y_pool, hs, u_gate, gpool_ref, grnn_ref, wout_ref)
    x1_ref[...] = x1
    h2 = _rms(x1, gxa_ref[...]).astype(BF16)
    q_ref[...] = _dot(h2, wq_ref[...]) * (XA_HEAD_DIM ** -0.5)

    npool_ref[:POOL_BUF - 1] = pool_ref[1:]
    npool_ref[POOL_BUF - 1] = u_pool
    nconv_ref[:CONV_WIDTH - 2] = conv_ref[1:]
    nconv_ref[CONV_WIDTH - 2] = u_rnn
    nh_ref[...] = hs


def _mix_s(x, pool, conv, h0, mixw, gxa, wq):
    bsz = x.shape[0]
    args = [x, pool, conv, h0] + list(mixw) + [gxa, wq]
    return pl.pallas_call(
        _mix_s_kernel,
        grid=(1,),
        in_specs=[_const_spec(a.shape) for a in args],
        out_specs=[_const_spec((bsz, D_MODEL)), _const_spec((bsz, D_MODEL)), _const_spec(pool.shape),
                   _const_spec(conv.shape), _const_spec((bsz, D_RNN))],
        out_shape=[jax.ShapeDtypeStruct((bsz, D_MODEL), F32), jax.ShapeDtypeStruct((bsz, D_MODEL), F32),
                   jax.ShapeDtypeStruct(pool.shape, F32), jax.ShapeDtypeStruct(conv.shape, F32),
                   jax.ShapeDtypeStruct((bsz, D_RNN), F32)],
        compiler_params=pltpu.CompilerParams(dimension_semantics=("arbitrary",), vmem_limit_bytes=VMEM_LIMIT),
        name="mix_s",
    )(*args)


ATTN_S_BB = 4


def _attn_s_kernel(q_ref, k_ref, v_ref, o_ref):
    q = q_ref[...][:, None]
    s = jnp.sum(k_ref[...] * q, axis=-1, keepdims=True)
    s = s - jnp.max(s, axis=1, keepdims=True)
    p = jnp.exp(s)
    p = p / jnp.sum(p, axis=1, keepdims=True)
    o_ref[...] = jnp.sum(p * v_ref[...], axis=1)


def _attn_s(q, k, v):
    bsz = q.shape[0]
    kv_spec = pl.BlockSpec((ATTN_S_BB, N_MEM, XA_HEADS, XA_HEAD_DIM), lambda i: (i, 0, 0, 0))
    q_spec = pl.BlockSpec((ATTN_S_BB, XA_HEADS, XA_HEAD_DIM), lambda i: (i, 0, 0))
    o = pl.pallas_call(
        _attn_s_kernel,
        grid=(bsz // ATTN_S_BB,),
        in_specs=[q_spec, kv_spec, kv_spec],
        out_specs=q_spec,
        out_shape=jax.ShapeDtypeStruct((bsz, XA_HEADS, XA_HEAD_DIM), F32),
        compiler_params=pltpu.CompilerParams(dimension_semantics=("arbitrary",), vmem_limit_bytes=VMEM_LIMIT),
        name="attn_s",
    )(q.reshape(bsz, XA_HEADS, XA_HEAD_DIM), k, v)
    return o.reshape(bsz, D_MODEL)


def _post_s_kernel(x1_ref, o_ref, wo_ref, gffn_ref, sg_ref, su_ref, sd_ref, wr_ref, rbias_ref,
                   xres_ref, hb_ref, slot_ref, gw_ref, gexp_ref, grel_ref, cover_ref, cnt_ref):
    x2 = x1_ref[...] + _dot(o_ref[...].astype(BF16), wo_ref[...])
    h3, xres = _moe_prologue(x2, gffn_ref, sg_ref, su_ref, sd_ref)
    xres_ref[...] = xres
    hb_ref[...] = h3.astype(BF16)
    slot_rows, w_rows, run_len, g_exp, g_rel = _route(h3, wr_ref, rbias_ref, jnp.zeros((N_EXPERTS, 1), F32))
    _store_rows(slot_ref, slot_rows, I32)
    _store_rows(gw_ref, w_rows, F32)
    gexp_ref[0] = g_exp.astype(I32)
    grel_ref[0] = g_rel.astype(I32)
    cover_ref[0] = jnp.broadcast_to(jnp.sum(run_len, axis=0, keepdims=True), cover_ref.shape[1:])
    cnt_ref[...] = jnp.broadcast_to(run_len, cnt_ref.shape)


def _post_s(x1, o, wo, moew):
    bsz = x1.shape[0]
    args = [x1, o, wo] + list(moew)
    out_shape = [jax.ShapeDtypeStruct((bsz, D_MODEL), F32),
                 jax.ShapeDtypeStruct((bsz, D_MODEL), BF16),
                 jax.ShapeDtypeStruct((TOP_K, bsz), I32),
                 jax.ShapeDtypeStruct((TOP_K, bsz), F32),
                 jax.ShapeDtypeStruct((1, 1, _group_lanes(bsz)), I32),
                 jax.ShapeDtypeStruct((1, 1, _group_lanes(bsz)), I32),
                 jax.ShapeDtypeStruct((1, 1, 128), F32),
                 jax.ShapeDtypeStruct((N_EXPERTS, 128), F32)]
    return pl.pallas_call(
        _post_s_kernel,
        grid=(1,),
        in_specs=[_const_spec(a.shape) for a in args],
        out_specs=[_const_spec(s.shape) for s in out_shape],
        out_shape=out_shape,
        compiler_params=pltpu.CompilerParams(dimension_semantics=("arbitrary",), vmem_limit_bytes=VMEM_LIMIT),
        name="post_s",
    )(*args)


def _plan_kernel(rp_ref, rs_ref, base_p_ref, base_s_ref, fill_at_ref, fill_n_ref, exp_ref, nreal_ref):
    n_blocks = exp_ref.shape[0]
    shift = BM.bit_length() - 1

    def per_expert(e, carry):
        blk0, last_e = carry
        rows = rp_ref[e] + rs_ref[e]
        start = lax.shift_left(blk0, shift)
        n_blk = lax.shift_right_logical(rows + (BM - 1), shift)
        base_p_ref[e] = start
        base_s_ref[e] = start + rp_ref[e]
        fill_at_ref[e] = start + rows
        fill_n_ref[e] = _groups(lax.shift_left(n_blk, shift) - rows)

        def per_block(j, _):
            exp_ref[blk0 + j] = e
            return _

        lax.fori_loop(0, n_blk, per_block, 0)
        return blk0 + n_blk, jnp.where(rows > 0, e, last_e)

    n_real, last_e = lax.fori_loop(0, N_EXPERTS, per_expert, (jnp.int32(0), jnp.int32(0)))
    nreal_ref[0] = n_real

    def rest(b, _):
        exp_ref[b] = last_e
        return _

    lax.fori_loop(n_real, n_blocks, rest, 0)


def _plan(rows_p, rows_s, n_rows):
    assert BM & (BM - 1) == 0 and n_rows % BM == 0
    smem = pl.BlockSpec(memory_space=pltpu.SMEM)
    return pl.pallas_call(
        _plan_kernel,
        in_specs=[smem, smem],
        out_specs=[smem] * 6,
        out_shape=[jax.ShapeDtypeStruct((N_EXPERTS,), I32)] * 4
        + [jax.ShapeDtypeStruct((n_rows // BM,), I32), jax.ShapeDtypeStruct((1,), I32)],
        name="plan",
    )(rows_p, rows_s)


def _groups(n_rows):
    return lax.shift_right_logical(n_rows, RUN_ALIGN.bit_length() - 1)


GROUP_UNROLL = 8


def _for_each_group(n_groups, grow_ref, fn):
    def one(g):
        fn(pl.multiple_of(g * RUN_ALIGN, RUN_ALIGN), pl.multiple_of(grow_ref[0, 0, g], RUN_ALIGN))

    def several(j, _):
        for u in range(GROUP_UNROLL):
            one(j * GROUP_UNROLL + u)
        return _

    def single(g, _):
        one(g)
        return _

    n_full = lax.shift_right_logical(n_groups, GROUP_UNROLL.bit_length() - 1)
    lax.fori_loop(0, n_full, several, 0)
    lax.fori_loop(n_full * GROUP_UNROLL, n_groups, single, 0)


def _wait_groups(n_groups, group_wait, bulk_wait):
    def bulk(j, _):
        bulk_wait()
        return _

    def single(g, _):
        group_wait()
        return _

    n_full = lax.shift_right_logical(n_groups, GROUP_UNROLL.bit_length() - 1)
    lax.fori_loop(0, n_full, bulk, 0)
    lax.fori_loop(n_full * GROUP_UNROLL, n_groups, single, 0)


def _group_rows_kernel(gexp_ref, grel_ref, base_ref, out_ref):
    rows = grel_ref[...]
    e = gexp_ref[...]
    for ex in range(N_EXPERTS):
        rows = rows + jnp.where(e == ex, base_ref[ex], 0)
    out_ref[...] = rows


def _group_rows(gexp, grel, base):
    shape = gexp.shape
    flat = (shape[0], shape[-1])
    out = pl.pallas_call(
        _group_rows_kernel,
        in_specs=[pl.BlockSpec(memory_space=pltpu.VMEM), pl.BlockSpec(memory_space=pltpu.VMEM),
                  pl.BlockSpec(memory_space=pltpu.SMEM)],
        out_specs=pl.BlockSpec(memory_space=pltpu.VMEM),
        out_shape=jax.ShapeDtypeStruct(flat, I32),
        name="group_rows",
    )(gexp.reshape(flat), grel.reshape(flat), base)
    return out.reshape(shape)


def _dispatch_kernel(cover_p, cover_s, fill_at, fill_n, nreal_ref, grow_p, grow_s,
                     slot_p_ref, h_p_ref, slot_s_ref, h_s_ref, xs_ref, sbuf, zbuf, sem):
    i = pl.program_id(0)
    last = pl.num_programs(0) - 1
    cur = lax.rem(i, 2)

    def drain(s, n_rows):
        def wait_rows(n):
            pltpu.make_async_copy(sbuf.at[s, pl.ds(0, n)], xs_ref.at[pl.ds(0, n)], sem.at[s]).wait()
        _wait_groups(_groups(n_rows), lambda: wait_rows(RUN_ALIGN), lambda: wait_rows(RUN_ALIGN * GROUP_UNROLL))

    def tile(n_rows, grow_ref, slot_ref, h_ref):
        def sort_chunk(c):
            rid = c * SORT_CHUNK + lax.broadcasted_iota(I32, (SORT_CHUNK, 1), 0)
            hit = rid == slot_ref[0:1, :]
            for k in range(1, TOP_K):
                hit = hit | (rid == slot_ref[k:k + 1, :])
            sbuf[cur, c * SORT_CHUNK:(c + 1) * SORT_CHUNK] = _dot(hit.astype(BF16), h_ref[...]).astype(BF16)

        always = h_ref.shape[0] * TOP_K // SORT_CHUNK
        for c in range(_sorted_rows(h_ref.shape[0]) // SORT_CHUNK):
            if c < always:
                sort_chunk(c)
            else:
                pl.when(c * SORT_CHUNK < n_rows)(functools.partial(sort_chunk, c))

        def send(local, glob):
            pltpu.make_async_copy(sbuf.at[cur, pl.ds(local, RUN_ALIGN)], xs_ref.at[pl.ds(glob, RUN_ALIGN)],
                                  sem.at[cur]).start()

        _for_each_group(_groups(n_rows), grow_ref, send)

    @pl.when(i >= 2)
    def _():
        drain(cur, cover_p[i - 2])

    @pl.when(i < last)
    def _():
        tile(cover_p[i], grow_p, slot_p_ref, h_p_ref)

    @pl.when(i == last)
    def _():
        tile(cover_s[0], grow_s, slot_s_ref, h_s_ref)
        drain(1 - cur, cover_p[last - 1])
        drain(cur, cover_s[0])
        zbuf[...] = jnp.zeros_like(zbuf)

        def group_fill(e, g):
            return pltpu.make_async_copy(
                zbuf.at[pl.ds(0, RUN_ALIGN)],
                xs_ref.at[pl.ds(pl.multiple_of(fill_at[e] + g * RUN_ALIGN, RUN_ALIGN), RUN_ALIGN)], sem.at[2])

        def per_expert(e, n):
            def start(g, _):
                group_fill(e, g).start()
                return _
            lax.fori_loop(0, fill_n[e], start, 0)
            return n + fill_n[e]

        n_fill = lax.fori_loop(0, N_EXPERTS, per_expert, jnp.int32(0))

        def wait_group(g, _):
            group_fill(0, 0).wait()
            return _

        lax.fori_loop(0, n_fill, wait_group, 0)

        n_real = nreal_ref[0]
        n_tail = xs_ref.shape[0] // BM - n_real

        def blk_fill(b):
            return pltpu.make_async_copy(zbuf, xs_ref.at[pl.ds(pl.multiple_of((n_real + b) * BM, BM), BM)], sem.at[2])

        def start_blk(b, _):
            blk_fill(b).start()
            return _

        def wait_blk(b, _):
            blk_fill(b).wait()
            return _

        lax.fori_loop(0, n_tail, start_blk, 0)
        lax.fori_loop(0, n_tail, wait_blk, 0)


def _dispatch(cover_p, cover_s, fills, grow_p, grow_s, slot_p, hb_p, slot_s, hb_s, n_rows):
    n_p = hb_p.shape[0] // TS
    smem = pl.BlockSpec(memory_space=pltpu.SMEM)
    clamp = lambda i: jnp.minimum(i, n_p - 1)
    per_tile = pl.BlockSpec((1, 1, grow_p.shape[-1]), lambda i: (clamp(i), 0, 0), memory_space=pltpu.SMEM)
    return pl.pallas_call(
        _dispatch_kernel,
        grid=(n_p + 1,),
        in_specs=[smem] * 5 + [per_tile, smem,
                               pl.BlockSpec((TOP_K, TS), lambda i: (0, clamp(i))),
                               pl.BlockSpec((TS, D_MODEL), lambda i: (clamp(i), 0)),
                               _const_spec(slot_s.shape), _const_spec(hb_s.shape)],
        out_specs=pl.BlockSpec(memory_space=pl.ANY),
        out_shape=jax.ShapeDtypeStruct((n_rows, D_MODEL), BF16),
        scratch_shapes=[pltpu.VMEM((2, _sorted_rows(TS), D_MODEL), BF16),
                        pltpu.VMEM((BM, D_MODEL), BF16), pltpu.SemaphoreType.DMA((3,))],
        compiler_params=pltpu.CompilerParams(dimension_semantics=("arbitrary",), has_side_effects=True,
                                             vmem_limit_bytes=VMEM_LIMIT),
        name="dispatch",
    )(cover_p, cover_s, *fills, grow_p, grow_s, slot_p, hb_p, slot_s, hb_s)


GMM_LANES = 2


def _gmm_kernel(exp_ref, nreal_ref, *refs):
    x_ref = refs[0]
    w_refs = refs[1:1 + 3 * GMM_LANES]
    y_ref = refs[1 + 3 * GMM_LANES]
    scratch = refs[2 + 3 * GMM_LANES:]
    s = pl.program_id(0)
    n_real = nreal_ref[0]

    for lane in range(GMM_LANES):
        wg_ref, wu_ref, wd_ref = w_refs[3 * lane:3 * lane + 3]
        wgu, wdn = scratch[2 * lane:2 * lane + 2]
        b = s * GMM_LANES + lane
        new_expert = (s == 0) | (exp_ref[b] != exp_ref[jnp.maximum(b - GMM_LANES, 0)])

        @pl.when((b < n_real) & new_expert)
        def _():
            wgu[:, :D_EXPERT] = wg_ref[0].astype(BF16)
            wgu[:, D_EXPERT:] = wu_ref[0].astype(BF16)
            wdn[...] = wd_ref[0].astype(BF16)

    @pl.when(s * GMM_LANES < n_real)
    def _():
        for lane in range(GMM_LANES):
            wgu, wdn = scratch[2 * lane:2 * lane + 2]
            rows = slice(lane * BM, (lane + 1) * BM)
            gu = _dot(x_ref[rows, :], wgu[...])
            act = jax.nn.silu(gu[:, :D_EXPERT]) * gu[:, D_EXPERT:]
            y = _dot(act.astype(BF16), wdn[...]).astype(BF16)
            y_ref[rows, :] = jnp.where(s * GMM_LANES + lane < n_real, y, jnp.zeros_like(y))

    @pl.when(s * GMM_LANES >= n_real)
    def _():
        y_ref[...] = jnp.zeros_like(y_ref)


def _gmm(blk_exp, n_real, xs, wg, wu, wd):
    n_rows = xs.shape[0]
    step_rows = BM * GMM_LANES
    assert n_rows % step_rows == 0
    last_step = lambda nreal: (nreal[0] - 1) // GMM_LANES
    w_specs, scratch = [], []
    for lane in range(GMM_LANES):
        weight = lambda s, exp, nreal, lane=lane: (exp[s * GMM_LANES + lane], 0, 0)
        w_specs += [pl.BlockSpec((1, D_MODEL, D_EXPERT), weight), pl.BlockSpec((1, D_MODEL, D_EXPERT), weight),
                    pl.BlockSpec((1, D_EXPERT, D_MODEL), weight)]
        scratch += [pltpu.VMEM((D_MODEL, 2 * D_EXPERT), BF16), pltpu.VMEM((D_EXPERT, D_MODEL), BF16)]
    grid_spec = pltpu.PrefetchScalarGridSpec(
        num_scalar_prefetch=2,
        grid=(n_rows // step_rows,),
        in_specs=[pl.BlockSpec((step_rows, D_MODEL), lambda s, exp, nreal: (jnp.minimum(s, last_step(nreal)), 0))]
        + w_specs,
        out_specs=pl.BlockSpec((step_rows, D_MODEL), lambda s, exp, nreal: (s, 0)),
        scratch_shapes=scratch,
    )
    return pl.pallas_call(
        _gmm_kernel,
        grid_spec=grid_spec,
        out_shape=jax.ShapeDtypeStruct((n_rows, D_MODEL), BF16),
        compiler_params=pltpu.CompilerParams(dimension_semantics=("arbitrary",), vmem_limit_bytes=VMEM_LIMIT),
        name="gmm",
    )(blk_exp, n_real, xs, *([wg, wu, wd] * GMM_LANES))


def _combine_kernel(cover_ref, grow_ref, grow_next_ref,
                    slot_ref, gw_ref, xres_ref, gfin_ref, ys_ref, out_ref, ybuf, sem):
    i = pl.program_id(0)
    n = pl.num_programs(0)
    cur = lax.rem(i, 2)
    rows = ybuf.shape[1]

    def rows_copy(s, local, glob, n_rows):
        return pltpu.make_async_copy(ys_ref.at[pl.ds(glob, n_rows)], ybuf.at[s, pl.ds(local, n_rows)], sem.at[s])

    def gather(n_rows, gr_ref, s):
        _for_each_group(_groups(n_rows), gr_ref, lambda l, g: rows_copy(s, l, g, RUN_ALIGN).start())

    @pl.when(i == 0)
    def _():
        ybuf[...] = jnp.zeros_like(ybuf)
        gather(cover_ref[0], grow_ref, 0)

    @pl.when(i + 1 < n)
    def _():
        gather(cover_ref[jnp.minimum(i + 1, n - 1)], grow_next_ref, 1 - cur)

    _wait_groups(_groups(cover_ref[i]), lambda: rows_copy(cur, 0, 0, RUN_ALIGN).wait(),
                 lambda: rows_copy(cur, 0, 0, RUN_ALIGN * GROUP_UNROLL).wait())

    tm = slot_ref.shape[0]
    slot_b = [jnp.broadcast_to(slot_ref[:, k:k + 1], (tm, 128)) for k in range(TOP_K)]
    gw_b = [jnp.broadcast_to(gw_ref[:, k:k + 1], (tm, 128)) for k in range(TOP_K)]
    lane = lax.broadcasted_iota(I32, (1, 128), 1)

    def chunk_sum(c):
        pieces = []
        for p in range(SORT_CHUNK // 128):
            col = lane + (c * SORT_CHUNK + p * 128)
            w = jnp.zeros((tm, 128), F32)
            for k in range(TOP_K):
                w = jnp.where(slot_b[k] == col, gw_b[k], w)
            pieces.append(w.astype(BF16))
        return _dot(jnp.concatenate(pieces, axis=1), ybuf[cur, c * SORT_CHUNK:(c + 1) * SORT_CHUNK])

    always = tm * TOP_K // SORT_CHUNK
    acc = xres_ref[...]
    for c in range(always):
        acc = acc + chunk_sum(c)
    out_ref[...] = acc
    for c in range(always, rows // SORT_CHUNK):
        @pl.when(c * SORT_CHUNK < cover_ref[i])
        def _():
            out_ref[...] += chunk_sum(c)
    out_ref[...] = _rms(out_ref[...], gfin_ref[...])


def _combine(cover, grow, slot_t, gw_t, xres, gfin, ys):
    t = xres.shape[0]
    tm = min(t, TS)
    n = t // tm
    smem = pl.BlockSpec(memory_space=pltpu.SMEM)
    lanes = grow.shape[-1]
    this_tile = pl.BlockSpec((1, 1, lanes), lambda i: (i, 0, 0), memory_space=pltpu.SMEM)
    next_tile = pl.BlockSpec((1, 1, lanes), lambda i: (jnp.minimum(i + 1, n - 1), 0, 0), memory_space=pltpu.SMEM)
    return pl.pallas_call(
        _combine_kernel,
        grid=(n,),
        in_specs=[smem, this_tile, next_tile,
                  pl.BlockSpec((tm, TOP_K), lambda i: (i, 0)),
                  pl.BlockSpec((tm, TOP_K), lambda i: (i, 0)),
                  pl.BlockSpec((tm, D_MODEL), lambda i: (i, 0)),
                  _const_spec((1, D_MODEL)),
                  pl.BlockSpec(memory_space=pl.ANY)],
        out_specs=pl.BlockSpec((tm, D_MODEL), lambda i: (i, 0)),
        out_shape=jax.ShapeDtypeStruct((t, D_MODEL), F32),
        scratch_shapes=[pltpu.VMEM((2, _sorted_rows(tm), D_MODEL), BF16), pltpu.SemaphoreType.DMA((2,))],
        compiler_params=pltpu.CompilerParams(dimension_semantics=("arbitrary",), vmem_limit_bytes=VMEM_LIMIT),
        name="combine",
    )(cover, grow, grow, slot_t, gw_t, xres, gfin, ys)


def _block_diag_pairs(w):
    n_h, d, _ = w.shape
    half = n_h // 2
    out = jnp.zeros((2, half * d, half * d), F32)
    for hh in range(n_h):
        p, q = divmod(hh, half)
        out = out.at[p, q * d:(q + 1) * d, q * d:(q + 1) * d].set(w[hh])
    return out.astype(BF16)


def kernel(x_prompt, x_sample, state_pool, state_conv, state_h, cache_mem_k, cache_mem_v, mem_prompt, norm_mix, w_in, pool_w, pool_scale, conv_w, conv_b, gate_a_w, gate_a_b, gate_x_w, gate_x_b, lru_lambda, norm_pool_out, norm_rnn_out, w_out, norm_xattn, norm_mem, xa_wq, xa_wk, xa_wv, xa_wo, norm_ffn, router_w, router_bias, exp_w_gate, exp_w_up, exp_w_down, sh_w_gate, sh_w_up, sh_w_down, norm_final):
    bp, seq, _ = x_prompt.shape
    bs = x_sample.shape[0]
    tp = bp * seq
    row = lambda v: v.reshape(1, -1)
    bf = lambda v: v.astype(BF16)

    mixw = (row(norm_mix[0]), bf(w_in[0]), _block_diag_pairs(pool_w[0]), row(pool_scale[0]), conv_w[0],
            row(conv_b[0]), _block_diag_pairs(gate_a_w[0]), row(gate_a_b[0]), _block_diag_pairs(gate_x_w[0]),
            row(gate_x_b[0]), row(lru_lambda[0]), row(norm_pool_out[0]), row(norm_rnn_out[0]), bf(w_out[0]))
    xaw = (row(norm_xattn[0]), bf(xa_wq[0]), bf(xa_wo[0]))
    moew = (row(norm_ffn[0]), bf(sh_w_gate[0]), bf(sh_w_up[0]), bf(sh_w_down[0]), router_w[0].T,
            router_bias[0].reshape(N_EXPERTS, 1))

    mk, mv, kb, vb = _memkv(mem_prompt, row(norm_mem[0]), bf(xa_wk[0]), bf(xa_wv[0]))
    (xres_p, hb_p, slot_p, gw_p, gexp_p, grel_p, cover_p, cnt_p, pool_p, conv_p, h_p) = _trunk_p(
        x_prompt, kb, vb, mixw, xaw, moew)

    x1_s, q_s, pool_s, conv_s, h_s = _mix_s(x_sample.reshape(bs, D_MODEL), state_pool[0].transpose(1, 0, 2),
                                            state_conv[0].transpose(1, 0, 2), state_h[0], mixw, xaw[0], xaw[1])
    pool_s = pool_s.transpose(1, 0, 2)
    conv_s = conv_s.transpose(1, 0, 2)
    o_s = _attn_s(q_s, cache_mem_k[0], cache_mem_v[0])
    xres_s, hb_s, slot_s, gw_s, gexp_s, grel_s, cover_s, cnt_s = _post_s(x1_s, o_s, xaw[2], moew)

    n_tiles = tp // TS + 1
    n_rows = _round_up((tp + bs) * TOP_K + n_tiles * N_EXPERTS * (RUN_ALIGN - 1) + N_EXPERTS * (BM - 1),
                       BM * GMM_LANES)
    ints = lambda v: v[..., 0].astype(I32)
    base_p, base_s, fill_at, fill_n, blk_exp, n_real = _plan(ints(cnt_p), ints(cnt_s), n_rows)
    cover_p, cover_s = ints(cover_p).reshape(-1), ints(cover_s).reshape(-1)
    grow_p = _group_rows(gexp_p, grel_p, base_p)
    grow_s = _group_rows(gexp_s, grel_s, base_s)

    xs = _dispatch(cover_p, cover_s, (fill_at, fill_n, n_real), grow_p, grow_s, slot_p, hb_p, slot_s, hb_s, n_rows)
    ys = _gmm(blk_exp, n_real, xs, exp_w_gate[0], exp_w_up[0], exp_w_down[0])

    gfin = row(norm_final)
    y_p = _combine(cover_p, grow_p, slot_p.T, gw_p.T, xres_p, gfin, ys)
    y_s = _combine(cover_s, grow_s, slot_s.T, gw_s.T, xres_s, gfin, ys)

    return (y_p.reshape(bp, seq, D_MODEL), y_s.reshape(bs, 1, D_MODEL),
            pool_p[None], conv_p[None], h_p.reshape(1, bp, D_RNN),
            mk.reshape(1, bp, N_MEM, XA_HEADS, XA_HEAD_DIM), mv.reshape(1, bp, N_MEM, XA_HEADS, XA_HEAD_DIM),
            pool_s[None], conv_s[None], h_s[None])
```

```python
import functools

import jax
import jax.numpy as jnp
from jax import lax
from jax.experimental import pallas as pl
from jax.experimental.pallas import tpu as pltpu

F32 = jnp.float32
BF16 = jnp.bfloat16
I32 = jnp.int32
U32 = jnp.uint32

D_MODEL = 1024
D_POOL = 512
D_RNN = 512
D_IN = D_POOL + 2 * D_RNN
POOL_WINDOWS = (2, 4, 8, 16)
POOL_GROUP = 128
POOL_BUF = 15
CONV_WIDTH = 4
LRU_C = 8.0
N_MEM = 256
XA_HEADS = 4
XA_HEAD_DIM = 256
N_EXPERTS = 64
TOP_K = 8
N_EXPERT_GROUPS = 8
GROUP_SIZE = N_EXPERTS // N_EXPERT_GROUPS
TOPK_GROUPS = 4
D_EXPERT = 256
ROUTED_SCALE = 2.5
EPS = 1e-6
PAST_LEN = 16384

HALO = 16
CONV_HALO = 8
TS = 256
TRUNK_SEQS = 1
BM = 512
RUN_ALIGN = 16
RUN_CHUNKS = (64, 32, 16)
SORT_CHUNK = 512
VMEM_LIMIT = 56 * 1024 * 1024


def _round_up(x, m):
    return (x + m - 1) // m * m


def _sorted_rows(tokens):
    return _round_up(tokens * TOP_K + N_EXPERTS * (RUN_ALIGN - 1), SORT_CHUNK)


def _group_lanes(tokens):
    return _round_up(_sorted_rows(tokens) // RUN_ALIGN, 128)


def _const_spec(shape):
    nd = len(shape)
    return pl.BlockSpec(shape, lambda *_: (0,) * nd, pipeline_mode=pl.Buffered(1))


def _rms(x, g):
    ms = jnp.mean(x * x, axis=-1, keepdims=True)
    return x * lax.rsqrt(ms + EPS) * g


def _dot(a, b):
    return jnp.dot(a, b, preferred_element_type=F32)


def _dot_nt(a, b, precision=None):
    return lax.dot_general(a, b, (((1,), (1,)), ((), ())), precision=precision,
                           preferred_element_type=F32)


def _softplus(x):
    return jnp.maximum(x, 0.0) + jnp.log1p(jnp.exp(-jnp.abs(x)))


def _gates_and_decay(c, pos_is_zero, wa_ref, ba_ref, wx_ref, bx_ref, lam_ref):
    cb = c.astype(BF16)
    half = D_RNN // 2
    ga = jnp.concatenate([_dot(cb[:, :half], wa_ref[0]), _dot(cb[:, half:], wa_ref[1])], axis=1) + ba_ref[...]
    gx = jnp.concatenate([_dot(cb[:, :half], wx_ref[0]), _dot(cb[:, half:], wx_ref[1])], axis=1) + bx_ref[...]
    r = jax.nn.sigmoid(ga)
    i = jax.nn.sigmoid(gx)
    log_a = (-LRU_C) * r * _softplus(-lam_ref[...])
    a = jnp.exp(log_a)
    m2 = 1.0 - a * a
    mult = jnp.where(m2 > 0.0, m2 * lax.rsqrt(m2), 0.0)
    if pos_is_zero is not None:
        mult = jnp.where(pos_is_zero, 1.0, mult)
    return a, mult * i * c


def _pool_project(mean, u_pool, pw_ref, pscale_ref):
    d = (mean - u_pool).astype(BF16)
    half = D_POOL // 2
    y = jnp.concatenate([_dot(d[:, :half], pw_ref[0]), _dot(d[:, half:], pw_ref[1])], axis=1)
    return y * pscale_ref[...]


def _merge_out(y_pool, hs, u_gate, gpool_ref, grnn_ref, wout_ref):
    y_rnn = hs * jax.nn.gelu(u_gate)
    merged = jnp.concatenate([_rms(y_pool, gpool_ref[...]), _rms(y_rnn, grnn_ref[...])], axis=1)
    return _dot(merged.astype(BF16), wout_ref[...])


def _route(h3, wr_ref, rbias_ref, before):
    r_tok = h3.shape[0]
    logits = _dot_nt(wr_ref[...], h3, precision=lax.Precision.HIGHEST)
    scores = jax.nn.sigmoid(logits)
    biased = scores + rbias_ref[...]
    neg = jnp.float32(-jnp.inf)
    gs = []
    for g in range(N_EXPERT_GROUPS):
        xg = biased[g * GROUP_SIZE:(g + 1) * GROUP_SIZE]
        m1 = jnp.max(xg, axis=0, keepdims=True)
        eq = xg == m1
        cnt = jnp.sum(eq.astype(F32), axis=0, keepdims=True)
        m2 = jnp.max(jnp.where(eq, neg, xg), axis=0, keepdims=True)
        gs.append(m1 + jnp.where(cnt >= 2.0, m1, m2))
    pieces = []
    for g in range(N_EXPERT_GROUPS):
        beaten = jnp.zeros_like(gs[g])
        for o in range(N_EXPERT_GROUPS):
            if o == g:
                continue
            wins = (gs[o] > gs[g]) | (gs[o] == gs[g]) if o < g else (gs[o] > gs[g])
            beaten = beaten + wins.astype(F32)
        keep = beaten < float(TOPK_GROUPS)
        xg = biased[g * GROUP_SIZE:(g + 1) * GROUP_SIZE]
        pieces.append(jnp.where(keep, xg, neg))
    cur = jnp.concatenate(pieces, axis=0)
    eid = lax.broadcasted_iota(I32, (N_EXPERTS, r_tok), 0).astype(F32)
    idx_rows, score_rows = [], []
    sel = jnp.zeros((N_EXPERTS, r_tok), F32)
    for _ in range(TOP_K):
        m = jnp.max(cur, axis=0, keepdims=True)
        idx = jnp.min(jnp.where(cur == m, eid, float(N_EXPERTS)), axis=0, keepdims=True)
        oh = eid == idx
        score_rows.append(jnp.sum(jnp.where(oh, scores, 0.0), axis=0, keepdims=True))
        idx_rows.append(idx)
        sel = sel + oh.astype(F32)
        cur = jnp.where(oh, neg, cur)
    tot = score_rows[0]
    for s in score_rows[1:]:
        tot = tot + s
    w_rows = [s / tot * ROUTED_SCALE for s in score_rows]
    rr = lax.broadcasted_iota(I32, (r_tok, r_tok), 0)
    cc = lax.broadcasted_iota(I32, (r_tok, r_tok), 1)
    earlier = _dot(sel.astype(BF16), (rr < cc).astype(BF16))
    counts = jnp.sum(sel, axis=1, keepdims=True)
    run_len = jnp.floor((counts + (RUN_ALIGN - 1.0)) * (1.0 / RUN_ALIGN)) * RUN_ALIGN
    er = lax.broadcasted_iota(I32, (N_EXPERTS, N_EXPERTS), 0)
    ec = lax.broadcasted_iota(I32, (N_EXPERTS, N_EXPERTS), 1)
    run_start = _dot((ec < er).astype(BF16), jnp.broadcast_to(run_len, (N_EXPERTS, 128)).astype(BF16))[:, :1]
    slot = earlier + run_start
    slot_rows = [jnp.sum(jnp.where(eid == idx, slot, 0.0), axis=0, keepdims=True) for idx in idx_rows]
    n_lanes = _group_lanes(r_tok)
    g_row = lax.broadcasted_iota(I32, (N_EXPERTS, n_lanes), 1).astype(F32) * float(RUN_ALIGN)
    owns = (run_start <= g_row) & (g_row < run_start + run_len)
    e_col = lax.broadcasted_iota(I32, (N_EXPERTS, n_lanes), 0).astype(F32)
    g_exp = jnp.sum(jnp.where(owns, e_col, 0.0), axis=0, keepdims=True)
    g_rel = jnp.sum(jnp.where(owns, before + g_row - run_start, 0.0), axis=0, keepdims=True)
    return slot_rows, w_rows, run_len, g_exp, g_rel


def _moe_prologue(x2, gffn_ref, sg_ref, su_ref, sd_ref):
    h3 = _rms(x2, gffn_ref[...])
    h3b = h3.astype(BF16)
    act = jax.nn.silu(_dot(h3b, sg_ref[...])) * _dot(h3b, su_ref[...])
    shared = _dot(act.astype(BF16), sd_ref[...])
    return h3, x2 + shared


def _store_rows(ref, rows, dtype):
    for k, row in enumerate(rows):
        ref[k:k + 1, :] = row.astype(dtype)


def _memkv_kernel(mem_ref, g_ref, wk_ref, wv_ref, k_ref, v_ref, kb_ref, vb_ref):
    m = _rms(mem_ref[0], g_ref[...]).astype(BF16)
    k = _dot(m, wk_ref[...])
    v = _dot(m, wv_ref[...])
    k_ref[0] = k
    v_ref[0] = v
    kb_ref[0] = k.astype(BF16)
    vb_ref[0] = v.astype(BF16)


def _memkv(mem, g, wk, wv):
    b = mem.shape[0]
    blk = pl.BlockSpec((1, N_MEM, D_MODEL), lambda i: (i, 0, 0))
    return pl.pallas_call(
        _memkv_kernel,
        grid=(b,),
        in_specs=[blk, _const_spec((1, D_MODEL)), _const_spec((D_MODEL, D_MODEL)), _const_spec((D_MODEL, D_MODEL))],
        out_specs=[blk, blk, blk, blk],
        out_shape=[jax.ShapeDtypeStruct((b, N_MEM, D_MODEL), F32)] * 2
        + [jax.ShapeDtypeStruct((b, N_MEM, D_MODEL), BF16)] * 2,
        compiler_params=pltpu.CompilerParams(dimension_semantics=("arbitrary",), vmem_limit_bytes=VMEM_LIMIT),
        name="memkv",
    )(mem, g, wk, wv)


def _trunk_p_kernel(x_ref, kb_ref, vb_ref,
                    gmix_ref, win_ref, pw_ref, pscale_ref, cw_ref, cb_ref, wa_ref, ba_ref, wx_ref, bx_ref,
                    lam_ref, gpool_ref, grnn_ref, wout_ref,
                    gxa_ref, wq_ref, wo_ref,
                    gffn_ref, sg_ref, su_ref, sd_ref, wr_ref, rbias_ref,
                    xres_ref, hb_ref, slot_ref, gw_ref, gexp_ref, grel_ref, cover_ref, cnt_ref,
                    pool_ref, conv_ref, hT_ref,
                    pool_prev, conv_prev, h_prev, carry):
    g = pl.program_id(0)
    j = pl.program_id(1)

    @pl.when(j == 0)
    def _():
        pool_prev[...] = jnp.zeros_like(pool_prev)
        conv_prev[...] = jnp.zeros_like(conv_prev)
        h_prev[...] = jnp.zeros_like(h_prev)

    @pl.when((g == 0) & (j == 0))
    def _():
        carry[...] = jnp.zeros_like(carry)

    row = lax.broadcasted_iota(I32, (TS, 1), 0)
    pos = j * TS + row

    for sq in range(TRUNK_SEQS):
        x = x_ref[sq]

        h = _rms(x, gmix_ref[...]).astype(BF16)
        z = _dot(h, win_ref[...])
        u_pool = z[:, :D_POOL]
        u_rnn = z[:, D_POOL:D_POOL + D_RNN]
        u_gate = z[:, D_POOL + D_RNN:]

        ext = jnp.concatenate([pool_prev[sq], u_pool], axis=0)
        means = []
        for grp, w in enumerate(POOL_WINDOWS):
            s = ext[:, grp * POOL_GROUP:(grp + 1) * POOL_GROUP]
            k = 1
            while k < w:
                s = s + pltpu.roll(s, k, 0)
                k *= 2
            inv = 1.0 / jnp.minimum(pos + 1, w).astype(F32)
            means.append(s[HALO:] * inv)
        mean = jnp.concatenate(means, axis=1)
        y_pool = _pool_project(mean, u_pool, pw_ref, pscale_ref)

        extc = jnp.concatenate([conv_prev[sq], u_rnn], axis=0)
        c = u_rnn * cw_ref[CONV_WIDTH - 1:CONV_WIDTH, :]
        for k in range(1, CONV_WIDTH):
            c = c + pltpu.roll(extc, k, 0)[CONV_HALO:] * cw_ref[CONV_WIDTH - 1 - k:CONV_WIDTH - k, :]
        c = c + cb_ref[...]

        a, bt = _gates_and_decay(c, pos == 0, wa_ref, ba_ref, wx_ref, bx_ref, lam_ref)
        k = 1
        while k < TS:
            valid = row >= k
            a_sh = jnp.where(valid, pltpu.roll(a, k, 0), 1.0)
            b_sh = jnp.where(valid, pltpu.roll(bt, k, 0), 0.0)
            bt = bt + a * b_sh
            a = a * a_sh
            k *= 2
        hs = bt + a * h_prev[sq]

        pool_prev[sq] = u_pool[TS - HALO:]
        conv_prev[sq] = u_rnn[TS - CONV_HALO:]
        h_prev[sq] = hs[TS - 1:]
        pool_ref[sq] = u_pool[TS - POOL_BUF:]
        conv_ref[sq] = u_rnn[TS - (CONV_WIDTH - 1):]
        hT_ref[sq] = hs[TS - 1:]

        x1 = x + _merge_out(y_pool, hs, u_gate, gpool_ref, grnn_ref, wout_ref)

        h2 = _rms(x1, gxa_ref[...]).astype(BF16)
        q = (_dot(h2, wq_ref[...]) * (XA_HEAD_DIM ** -0.5)).astype(BF16)
        outs = []
        for hd in range(XA_HEADS):
            sl = slice(hd * XA_HEAD_DIM, (hd + 1) * XA_HEAD_DIM)
            s = _dot_nt(q[:, sl], kb_ref[sq, :, sl])
            s = s - jnp.max(s, axis=-1, keepdims=True)
            p = jnp.exp(s)
            p = p / jnp.sum(p, axis=-1, keepdims=True)
            outs.append(_dot(p.astype(BF16), vb_ref[sq, :, sl]))
        o = jnp.concatenate(outs, axis=1).astype(BF16)
        x2 = x1 + _dot(o, wo_ref[...])

        h3, xres = _moe_prologue(x2, gffn_ref, sg_ref, su_ref, sd_ref)
        xres_ref[sq] = xres
        hb_ref[sq] = h3.astype(BF16)
        slot_rows, w_rows, run_len, g_exp, g_rel = _route(h3, wr_ref, rbias_ref, carry[...])
        _store_rows(slot_ref.at[sq], slot_rows, I32)
        _store_rows(gw_ref.at[sq], w_rows, F32)
        gexp_ref[sq, 0] = g_exp.astype(I32)
        grel_ref[sq, 0] = g_rel.astype(I32)
        cover_ref[sq, 0] = jnp.broadcast_to(jnp.sum(run_len, axis=0, keepdims=True), cover_ref.shape[2:])
        carry[...] = carry[...] + run_len
    cnt_ref[...] = jnp.broadcast_to(carry[...], cnt_ref.shape)


def _trunk_p(x, kb, vb, mixw, xaw, moew):
    bsz, seq, _ = x.shape
    n_j = seq // TS
    q = TRUNK_SEQS
    assert bsz % q == 0
    seq_tile = lambda g, j: (g, j, 0)
    lane_tile = lambda g, j: (g, 0, j)
    per_tile = lambda g, j: (g, j, 0, 0)
    per_seq = lambda g, j: (g, 0, 0)
    weights = list(mixw) + list(xaw) + list(moew)
    in_specs = [pl.BlockSpec((q, TS, D_MODEL), seq_tile),
                pl.BlockSpec((q, N_MEM, D_MODEL), per_seq),
                pl.BlockSpec((q, N_MEM, D_MODEL), per_seq)] + [_const_spec(w.shape) for w in weights]
    lanes = _group_lanes(TS)
    out_shape = [jax.ShapeDtypeStruct((bsz, seq, D_MODEL), F32),
                 jax.ShapeDtypeStruct((bsz, seq, D_MODEL), BF16),
                 jax.ShapeDtypeStruct((bsz, TOP_K, seq), I32),
                 jax.ShapeDtypeStruct((bsz, TOP_K, seq), F32),
                 jax.ShapeDtypeStruct((bsz, n_j, 1, lanes), I32),
                 jax.ShapeDtypeStruct((bsz, n_j, 1, lanes), I32),
                 jax.ShapeDtypeStruct((bsz, n_j, 1, 128), F32),
                 jax.ShapeDtypeStruct((N_EXPERTS, 128), F32),
                 jax.ShapeDtypeStruct((bsz, POOL_BUF, D_POOL), F32),
                 jax.ShapeDtypeStruct((bsz, CONV_WIDTH - 1, D_RNN), F32),
                 jax.ShapeDtypeStruct((bsz, 1, D_RNN), F32)]
    out_specs = [pl.BlockSpec((q, TS, D_MODEL), seq_tile),
                 pl.BlockSpec((q, TS, D_MODEL), seq_tile),
                 pl.BlockSpec((q, TOP_K, TS), lane_tile),
                 pl.BlockSpec((q, TOP_K, TS), lane_tile),
                 pl.BlockSpec((q, 1, 1, lanes), per_tile),
                 pl.BlockSpec((q, 1, 1, lanes), per_tile),
                 pl.BlockSpec((q, 1, 1, 128), per_tile),
                 pl.BlockSpec((N_EXPERTS, 128), lambda g, j: (0, 0)),
                 pl.BlockSpec((q, POOL_BUF, D_POOL), per_seq),
                 pl.BlockSpec((q, CONV_WIDTH - 1, D_RNN), per_seq),
                 pl.BlockSpec((q, 1, D_RNN), per_seq)]
    return pl.pallas_call(
        _trunk_p_kernel,
        grid=(bsz // q, n_j),
        in_specs=in_specs,
        out_specs=out_specs,
        out_shape=out_shape,
        scratch_shapes=[pltpu.VMEM((q, HALO, D_POOL), F32), pltpu.VMEM((q, CONV_HALO, D_RNN), F32),
                        pltpu.VMEM((q, 1, D_RNN), F32), pltpu.VMEM((N_EXPERTS, 1), F32)],
        compiler_params=pltpu.CompilerParams(dimension_semantics=("arbitrary", "arbitrary"),
                                             vmem_limit_bytes=VMEM_LIMIT),
        name="trunk_p",
    )(x, kb, vb, *weights)


def _mix_s_kernel(x_ref, pool_ref, conv_ref, h0_ref,
                  gmix_ref, win_ref, pw_ref, pscale_ref, cw_ref, cb_ref, wa_ref, ba_ref, wx_ref, bx_ref,
                  lam_ref, gpool_ref, grnn_ref, wout_ref, gxa_ref, wq_ref,
                  x1_ref, q_ref, npool_ref, nconv_ref, nh_ref):
    x = x_ref[...]
    h = _rms(x, gmix_ref[...]).astype(BF16)
    z = _dot(h, win_ref[...])
    u_pool = z[:, :D_POOL]
    u_rnn = z[:, D_POOL:D_POOL + D_RNN]
    u_gate = z[:, D_POOL + D_RNN:]

    means = []
    for g, w in enumerate(POOL_WINDOWS):
        sl = slice(g * POOL_GROUP, (g + 1) * POOL_GROUP)
        s = u_pool[:, sl]
        for k in range(1, w):
            s = s + pool_ref[POOL_BUF - k, :, sl]
        means.append(s * (1.0 / min(w, PAST_LEN + 1)))
    mean = jnp.concatenate(means, axis=1)
    y_pool = _pool_project(mean, u_pool, pw_ref, pscale_ref)

    c = u_rnn * cw_ref[CONV_WIDTH - 1:CONV_WIDTH, :]
    for k in range(1, CONV_WIDTH):
        c = c + conv_ref[CONV_WIDTH - 1 - k] * cw_ref[CONV_WIDTH - 1 - k:CONV_WIDTH - k, :]
    c = c + cb_ref[...]
    a, bt = _gates_and_decay(c, None, wa_ref, ba_ref, wx_ref, bx_ref, lam_ref)
    hs = a * h0_ref[...] + bt

    x1 = x + _merge_out(y_pool, hs, u_gate, gpool_ref, grnn_ref, wout_ref)
    x1_ref[...] = x1
    h2 = _rms(x1, gxa_ref[...]).astype(BF16)
    q_ref[...] = _dot(h2, wq_ref[...]) * (XA_HEAD_DIM ** -0.5)

    npool_ref[:POOL_BUF - 1] = pool_ref[1:]
    npool_ref[POOL_BUF - 1] = u_pool
    nconv_ref[:CONV_WIDTH - 2] = conv_ref[1:]
    nconv_ref[CONV_WIDTH - 2] = u_rnn
    nh_ref[...] = hs


def _mix_s(x, pool, conv, h0, mixw, gxa, wq):
    bsz = x.shape[0]
    args = [x, pool, conv, h0] + list(mixw) + [gxa, wq]
    return pl.pallas_call(
        _mix_s_kernel,
        grid=(1,),
        in_specs=[_const_spec(a.shape) for a in args],
        out_specs=[_const_spec((bsz, D_MODEL)), _const_spec((bsz, D_MODEL)), _const_spec(pool.shape),
                   _const_spec(conv.shape), _const_spec((bsz, D_RNN))],
        out_shape=[jax.ShapeDtypeStruct((bsz, D_MODEL), F32), jax.ShapeDtypeStruct((bsz, D_MODEL), F32),
                   jax.ShapeDtypeStruct(pool.shape, F32), jax.ShapeDtypeStruct(conv.shape, F32),
                   jax.ShapeDtypeStruct((bsz, D_RNN), F32)],
        compiler_params=pltpu.CompilerParams(dimension_semantics=("arbitrary",), vmem_limit_bytes=VMEM_LIMIT),
        name="mix_s",
    )(*args)


ATTN_S_BB = 4


def _attn_s_kernel(q_ref, k_ref, v_ref, o_ref):
    q = q_ref[...][:, None]
    s = jnp.sum(k_ref[...] * q, axis=-1, keepdims=True)
    s = s - jnp.max(s, axis=1, keepdims=True)
    p = jnp.exp(s)
    p = p / jnp.sum(p, axis=1, keepdims=True)
    o_ref[...] = jnp.sum(p * v_ref[...], axis=1)


def _attn_s(q, k, v):
    bsz = q.shape[0]
    kv_spec = pl.BlockSpec((ATTN_S_BB, N_MEM, XA_HEADS, XA_HEAD_DIM), lambda i: (i, 0, 0, 0))
    q_spec = pl.BlockSpec((ATTN_S_BB, XA_HEADS, XA_HEAD_DIM), lambda i: (i, 0, 0))
    o = pl.pallas_call(
        _attn_s_kernel,
        grid=(bsz // ATTN_S_BB,),
        in_specs=[q_spec, kv_spec, kv_spec],
        out_specs=q_spec,
        out_shape=jax.ShapeDtypeStruct((bsz, XA_HEADS, XA_HEAD_DIM), F32),
        compiler_params=pltpu.CompilerParams(dimension_semantics=("arbitrary",), vmem_limit_bytes=VMEM_LIMIT),
        name="attn_s",
    )(q.reshape(bsz, XA_HEADS, XA_HEAD_DIM), k, v)
    return o.reshape(bsz, D_MODEL)


def _post_s_kernel(x1_ref, o_ref, wo_ref, gffn_ref, sg_ref, su_ref, sd_ref, wr_ref, rbias_ref,
                   xres_ref, hb_ref, slot_ref, gw_ref, gexp_ref, grel_ref, cover_ref, cnt_ref):
    x2 = x1_ref[...] + _dot(o_ref[...].astype(BF16), wo_ref[...])
    h3, xres = _moe_prologue(x2, gffn_ref, sg_ref, su_ref, sd_ref)
    xres_ref[...] = xres
    hb_ref[...] = h3.astype(BF16)
    slot_rows, w_rows, run_len, g_exp, g_rel = _route(h3, wr_ref, rbias_ref, jnp.zeros((N_EXPERTS, 1), F32))
    _store_rows(slot_ref, slot_rows, I32)
    _store_rows(gw_ref, w_rows, F32)
    gexp_ref[0] = g_exp.astype(I32)
    grel_ref[0] = g_rel.astype(I32)
    cover_ref[0] = jnp.broadcast_to(jnp.sum(run_len, axis=0, keepdims=True), cover_ref.shape[1:])
    cnt_ref[...] = jnp.broadcast_to(run_len, cnt_ref.shape)


def _post_s(x1, o, wo, moew):
    bsz = x1.shape[0]
    args = [x1, o, wo] + list(moew)
    out_shape = [jax.ShapeDtypeStruct((bsz, D_MODEL), F32),
                 jax.ShapeDtypeStruct((bsz, D_MODEL), BF16),
                 jax.ShapeDtypeStruct((TOP_K, bsz), I32),
                 jax.ShapeDtypeStruct((TOP_K, bsz), F32),
                 jax.ShapeDtypeStruct((1, 1, _group_lanes(bsz)), I32),
                 jax.ShapeDtypeStruct((1, 1, _group_lanes(bsz)), I32),
                 jax.ShapeDtypeStruct((1, 1, 128), F32),
                 jax.ShapeDtypeStruct((N_EXPERTS, 128), F32)]
    return pl.pallas_call(
        _post_s_kernel,
        grid=(1,),
        in_specs=[_const_spec(a.shape) for a in args],
        out_specs=[_const_spec(s.shape) for s in out_shape],
        out_shape=out_shape,
        compiler_params=pltpu.CompilerParams(dimension_semantics=("arbitrary",), vmem_limit_bytes=VMEM_LIMIT),
        name="post_s",
    )(*args)


def _plan_kernel(rp_ref, rs_ref, base_p_ref, base_s_ref, fill_at_ref, fill_n_ref, exp_ref, nreal_ref):
    n_blocks = exp_ref.shape[0]
    shift = BM.bit_length() - 1

    def per_expert(e, carry):
        blk0, last_e = carry
        rows = rp_ref[e] + rs_ref[e]
        start = lax.shift_left(blk0, shift)
        n_blk = lax.shift_right_logical(rows + (BM - 1), shift)
        base_p_ref[e] = start
        base_s_ref[e] = start + rp_ref[e]
        fill_at_ref[e] = start + rows
        fill_n_ref[e] = _groups(lax.shift_left(n_blk, shift) - rows)

        def per_block(j, _):
            exp_ref[blk0 + j] = e
            return _

        lax.fori_loop(0, n_blk, per_block, 0)
        return blk0 + n_blk, jnp.where(rows > 0, e, last_e)

    n_real, last_e = lax.fori_loop(0, N_EXPERTS, per_expert, (jnp.int32(0), jnp.int32(0)))
    nreal_ref[0] = n_real

    def rest(b, _):
        exp_ref[b] = last_e
        return _

    lax.fori_loop(n_real, n_blocks, rest, 0)


def _plan(rows_p, rows_s, n_rows):
    assert BM & (BM - 1) == 0 and n_rows % BM == 0
    smem = pl.BlockSpec(memory_space=pltpu.SMEM)
    return pl.pallas_call(
        _plan_kernel,
        in_specs=[smem, smem],
        out_specs=[smem] * 6,
        out_shape=[jax.ShapeDtypeStruct((N_EXPERTS,), I32)] * 4
        + [jax.ShapeDtypeStruct((n_rows // BM,), I32), jax.ShapeDtypeStruct((1,), I32)],
        name="plan",
    )(rows_p, rows_s)


def _groups(n_rows):
    return lax.shift_right_logical(n_rows, RUN_ALIGN.bit_length() - 1)


GROUP_UNROLL = 8


def _for_each_group(n_groups, grow_ref, fn):
    def one(g, priority):
        fn(pl.multiple_of(g * RUN_ALIGN, RUN_ALIGN), pl.multiple_of(grow_ref[0, 0, g], RUN_ALIGN), priority)

    def several(j, _):
        for u in range(GROUP_UNROLL):
            one(j * GROUP_UNROLL + u, u % 2)
        return _

    def single(g, _):
        one(g, 0)
        return _

    n_full = lax.shift_right_logical(n_groups, GROUP_UNROLL.bit_length() - 1)
    lax.fori_loop(0, n_full, several, 0)
    lax.fori_loop(n_full * GROUP_UNROLL, n_groups, single, 0)


def _wait_groups(n_groups, group_wait, bulk_wait):
    def bulk(j, _):
        bulk_wait()
        return _

    def single(g, _):
        group_wait()
        return _

    n_full = lax.shift_right_logical(n_groups, GROUP_UNROLL.bit_length() - 1)
    lax.fori_loop(0, n_full, bulk, 0)
    lax.fori_loop(n_full * GROUP_UNROLL, n_groups, single, 0)


def _group_rows_kernel(gexp_ref, grel_ref, base_ref, out_ref):
    rows = grel_ref[...]
    e = gexp_ref[...]
    for ex in range(N_EXPERTS):
        rows = rows + jnp.where(e == ex, base_ref[ex], 0)
    out_ref[...] = rows


def _group_rows(gexp, grel, base):
    shape = gexp.shape
    flat = (shape[0], shape[-1])
    out = pl.pallas_call(
        _group_rows_kernel,
        in_specs=[pl.BlockSpec(memory_space=pltpu.VMEM), pl.BlockSpec(memory_space=pltpu.VMEM),
                  pl.BlockSpec(memory_space=pltpu.SMEM)],
        out_specs=pl.BlockSpec(memory_space=pltpu.VMEM),
        out_shape=jax.ShapeDtypeStruct(flat, I32),
        name="group_rows",
    )(gexp.reshape(flat), grel.reshape(flat), base)
    return out.reshape(shape)


def _dispatch_kernel(cover_p, cover_s, fill_at, fill_n, nreal_ref, grow_p, grow_s,
                     slot_p_ref, h_p_ref, slot_s_ref, h_s_ref, xs_ref, sbuf, zbuf, sem):
    i = pl.program_id(0)
    last = pl.num_programs(0) - 1
    cur = lax.rem(i, 2)

    def drain(s, n_rows):
        def wait_rows(n):
            pltpu.make_async_copy(sbuf.at[s, pl.ds(0, n)], xs_ref.at[pl.ds(0, n)], sem.at[s]).wait()
        _wait_groups(_groups(n_rows), lambda: wait_rows(RUN_ALIGN), lambda: wait_rows(RUN_ALIGN * GROUP_UNROLL))

    def tile(n_rows, grow_ref, slot_ref, h_ref):
        def sort_chunk(c):
            rid = c * SORT_CHUNK + lax.broadcasted_iota(I32, (SORT_CHUNK, 1), 0)
            hit = rid == slot_ref[0:1, :]
            for k in range(1, TOP_K):
                hit = hit | (rid == slot_ref[k:k + 1, :])
            sbuf[cur, c * SORT_CHUNK:(c + 1) * SORT_CHUNK] = _dot(hit.astype(BF16), h_ref[...]).astype(BF16)

        def send(local, glob, priority):
            pltpu.make_async_copy(sbuf.at[cur, pl.ds(local, RUN_ALIGN)], xs_ref.at[pl.ds(glob, RUN_ALIGN)],
                                  sem.at[cur]).start(priority=priority)

        always = h_ref.shape[0] * TOP_K // SORT_CHUNK
        for c in range(_sorted_rows(h_ref.shape[0]) // SORT_CHUNK):
            if c < always:
                sort_chunk(c)
            else:
                pl.when(c * SORT_CHUNK < n_rows)(functools.partial(sort_chunk, c))
        _for_each_group(_groups(n_rows), grow_ref, send)

    @pl.when(i >= 2)
    def _():
        drain(cur, cover_p[i - 2])

    @pl.when(i < last)
    def _():
        tile(cover_p[i], grow_p, slot_p_ref, h_p_ref)

    @pl.when(i == last)
    def _():
        tile(cover_s[0], grow_s, slot_s_ref, h_s_ref)
        drain(1 - cur, cover_p[last - 1])
        drain(cur, cover_s[0])
        zbuf[...] = jnp.zeros_like(zbuf)

        def group_fill(e, g):
            return pltpu.make_async_copy(
                zbuf.at[pl.ds(0, RUN_ALIGN)],
                xs_ref.at[pl.ds(pl.multiple_of(fill_at[e] + g * RUN_ALIGN, RUN_ALIGN), RUN_ALIGN)], sem.at[2])

        def per_expert(e, n):
            def start(g, _):
                group_fill(e, g).start()
                return _
            lax.fori_loop(0, fill_n[e], start, 0)
            return n + fill_n[e]

        n_fill = lax.fori_loop(0, N_EXPERTS, per_expert, jnp.int32(0))

        def wait_group(g, _):
            group_fill(0, 0).wait()
            return _

        lax.fori_loop(0, n_fill, wait_group, 0)

        n_real = nreal_ref[0]
        n_tail = xs_ref.shape[0] // BM - n_real

        def blk_fill(b):
            return pltpu.make_async_copy(zbuf, xs_ref.at[pl.ds(pl.multiple_of((n_real + b) * BM, BM), BM)], sem.at[2])

        def start_blk(b, _):
            blk_fill(b).start()
            return _

        def wait_blk(b, _):
            blk_fill(b).wait()
            return _

        lax.fori_loop(0, n_tail, start_blk, 0)
        lax.fori_loop(0, n_tail, wait_blk, 0)


def _dispatch(cover_p, cover_s, fills, grow_p, grow_s, slot_p, hb_p, slot_s, hb_s, n_rows):
    n_p = hb_p.shape[0] // TS
    smem = pl.BlockSpec(memory_space=pltpu.SMEM)
    clamp = lambda i: jnp.minimum(i, n_p - 1)
    per_tile = pl.BlockSpec((1, 1, grow_p.shape[-1]), lambda i: (clamp(i), 0, 0), memory_space=pltpu.SMEM)
    return pl.pallas_call(
        _dispatch_kernel,
        grid=(n_p + 1,),
        in_specs=[smem] * 5 + [per_tile, smem,
                               pl.BlockSpec((TOP_K, TS), lambda i: (0, clamp(i))),
                               pl.BlockSpec((TS, D_MODEL), lambda i: (clamp(i), 0)),
                               _const_spec(slot_s.shape), _const_spec(hb_s.shape)],
        out_specs=pl.BlockSpec(memory_space=pl.ANY),
        out_shape=jax.ShapeDtypeStruct((n_rows, D_MODEL), BF16),
        scratch_shapes=[pltpu.VMEM((2, _sorted_rows(TS), D_MODEL), BF16),
                        pltpu.VMEM((BM, D_MODEL), BF16), pltpu.SemaphoreType.DMA((3,))],
        compiler_params=pltpu.CompilerParams(dimension_semantics=("arbitrary",), has_side_effects=True,
                                             vmem_limit_bytes=VMEM_LIMIT),
        name="dispatch",
    )(cover_p, cover_s, *fills, grow_p, grow_s, slot_p, hb_p, slot_s, hb_s)


GMM_LANES = 2


def _gmm_kernel(exp_ref, nreal_ref, *refs):
    x_ref = refs[0]
    w_refs = refs[1:1 + 3 * GMM_LANES]
    y_ref = refs[1 + 3 * GMM_LANES]
    scratch = refs[2 + 3 * GMM_LANES:]
    s = pl.program_id(0)
    n_real = nreal_ref[0]

    for lane in range(GMM_LANES):
        wg_ref, wu_ref, wd_ref = w_refs[3 * lane:3 * lane + 3]
        wgu, wdn = scratch[2 * lane:2 * lane + 2]
        b = s * GMM_LANES + lane
        new_expert = (s == 0) | (exp_ref[b] != exp_ref[jnp.maximum(b - GMM_LANES, 0)])

        @pl.when((b < n_real) & new_expert)
        def _():
            wgu[:, :D_EXPERT] = wg_ref[0].astype(BF16)
            wgu[:, D_EXPERT:] = wu_ref[0].astype(BF16)
            wdn[...] = wd_ref[0].astype(BF16)

    @pl.when(s * GMM_LANES < n_real)
    def _():
        for lane in range(GMM_LANES):
            wgu, wdn = scratch[2 * lane:2 * lane + 2]
            rows = slice(lane * BM, (lane + 1) * BM)
            gu = _dot(x_ref[rows, :], wgu[...])
            act = jax.nn.silu(gu[:, :D_EXPERT]) * gu[:, D_EXPERT:]
            y = _dot(act.astype(BF16), wdn[...]).astype(BF16)
            y_ref[rows, :] = jnp.where(s * GMM_LANES + lane < n_real, y, jnp.zeros_like(y))

    @pl.when(s * GMM_LANES >= n_real)
    def _():
        y_ref[...] = jnp.zeros_like(y_ref)


def _gmm(blk_exp, n_real, xs, wg, wu, wd):
    n_rows = xs.shape[0]
    step_rows = BM * GMM_LANES
    assert n_rows % step_rows == 0
    last_step = lambda nreal: (nreal[0] - 1) // GMM_LANES
    w_specs, scratch = [], []
    for lane in range(GMM_LANES):
        weight = lambda s, exp, nreal, lane=lane: (exp[s * GMM_LANES + lane], 0, 0)
        w_specs += [pl.BlockSpec((1, D_MODEL, D_EXPERT), weight), pl.BlockSpec((1, D_MODEL, D_EXPERT), weight),
                    pl.BlockSpec((1, D_EXPERT, D_MODEL), weight)]
        scratch += [pltpu.VMEM((D_MODEL, 2 * D_EXPERT), BF16), pltpu.VMEM((D_EXPERT, D_MODEL), BF16)]
    grid_spec = pltpu.PrefetchScalarGridSpec(
        num_scalar_prefetch=2,
        grid=(n_rows // step_rows,),
        in_specs=[pl.BlockSpec((step_rows, D_MODEL), lambda s, exp, nreal: (jnp.minimum(s, last_step(nreal)), 0))]
        + w_specs,
        out_specs=pl.BlockSpec((step_rows, D_MODEL), lambda s, exp, nreal: (s, 0)),
        scratch_shapes=scratch,
    )
    return pl.pallas_call(
        _gmm_kernel,
        grid_spec=grid_spec,
        out_shape=jax.ShapeDtypeStruct((n_rows, D_MODEL), BF16),
        compiler_params=pltpu.CompilerParams(dimension_semantics=("arbitrary",), vmem_limit_bytes=VMEM_LIMIT),
        name="gmm",
    )(blk_exp, n_real, xs, *([wg, wu, wd] * GMM_LANES))


def _combine_kernel(cover_ref, grow_ref, grow_next_ref,
                    slot_ref, gw_ref, xres_ref, gfin_ref, ys_ref, out_ref, ybuf, sem):
    i = pl.program_id(0)
    n = pl.num_programs(0)
    cur = lax.rem(i, 2)
    rows = ybuf.shape[1]

    def rows_copy(s, local, glob, n_rows):
        return pltpu.make_async_copy(ys_ref.at[pl.ds(glob, n_rows)], ybuf.at[s, pl.ds(local, n_rows)], sem.at[s])

    def gather(n_rows, gr_ref, s):
        _for_each_group(_groups(n_rows), gr_ref,
                        lambda l, g, priority: rows_copy(s, l, g, RUN_ALIGN).start(priority=priority))

    @pl.when(i == 0)
    def _():
        ybuf[...] = jnp.zeros_like(ybuf)
        gather(cover_ref[0], grow_ref, 0)

    @pl.when(i + 1 < n)
    def _():
        gather(cover_ref[jnp.minimum(i + 1, n - 1)], grow_next_ref, 1 - cur)

    _wait_groups(_groups(cover_ref[i]), lambda: rows_copy(cur, 0, 0, RUN_ALIGN).wait(),
                 lambda: rows_copy(cur, 0, 0, RUN_ALIGN * GROUP_UNROLL).wait())

    tm = slot_ref.shape[0]
    slot_b = [jnp.broadcast_to(slot_ref[:, k:k + 1], (tm, 128)) for k in range(TOP_K)]
    gw_b = [jnp.broadcast_to(gw_ref[:, k:k + 1], (tm, 128)) for k in range(TOP_K)]
    lane = lax.broadcasted_iota(I32, (1, 128), 1)

    def chunk_sum(c):
        pieces = []
        for p in range(SORT_CHUNK // 128):
            col = lane + (c * SORT_CHUNK + p * 128)
            w = jnp.zeros((tm, 128), F32)
            for k in range(TOP_K):
                w = jnp.where(slot_b[k] == col, gw_b[k], w)
            pieces.append(w.astype(BF16))
        return _dot(jnp.concatenate(pieces, axis=1), ybuf[cur, c * SORT_CHUNK:(c + 1) * SORT_CHUNK])

    always = tm * TOP_K // SORT_CHUNK
    acc = xres_ref[...]
    for c in range(always):
        acc = acc + chunk_sum(c)
    out_ref[...] = acc
    for c in range(always, rows // SORT_CHUNK):
        @pl.when(c * SORT_CHUNK < cover_ref[i])
        def _():
            out_ref[...] += chunk_sum(c)
    out_ref[...] = _rms(out_ref[...], gfin_ref[...])


def _combine(cover, grow, slot_t, gw_t, xres, gfin, ys):
    t = xres.shape[0]
    tm = min(t, TS)
    n = t // tm
    smem = pl.BlockSpec(memory_space=pltpu.SMEM)
    lanes = grow.shape[-1]
    this_tile = pl.BlockSpec((1, 1, lanes), lambda i: (i, 0, 0), memory_space=pltpu.SMEM)
    next_tile = pl.BlockSpec((1, 1, lanes), lambda i: (jnp.minimum(i + 1, n - 1), 0, 0), memory_space=pltpu.SMEM)
    return pl.pallas_call(
        _combine_kernel,
        grid=(n,),
        in_specs=[smem, this_tile, next_tile,
                  pl.BlockSpec((tm, TOP_K), lambda i: (i, 0)),
                  pl.BlockSpec((tm, TOP_K), lambda i: (i, 0)),
                  pl.BlockSpec((tm, D_MODEL), lambda i: (i, 0)),
                  _const_spec((1, D_MODEL)),
                  pl.BlockSpec(memory_space=pl.ANY)],
        out_specs=pl.BlockSpec((tm, D_MODEL), lambda i: (i, 0)),
        out_shape=jax.ShapeDtypeStruct((t, D_MODEL), F32),
        scratch_shapes=[pltpu.VMEM((2, _sorted_rows(tm), D_MODEL), BF16), pltpu.SemaphoreType.DMA((2,))],
        compiler_params=pltpu.CompilerParams(dimension_semantics=("arbitrary",), vmem_limit_bytes=VMEM_LIMIT),
        name="combine",
    )(cover, grow, grow, slot_t, gw_t, xres, gfin, ys)


def _block_diag_pairs(w):
    n_h, d, _ = w.shape
    half = n_h // 2
    out = jnp.zeros((2, half * d, half * d), F32)
    for hh in range(n_h):
        p, q = divmod(hh, half)
        out = out.at[p, q * d:(q + 1) * d, q * d:(q + 1) * d].set(w[hh])
    return out.astype(BF16)


def kernel(x_prompt, x_sample, state_pool, state_conv, state_h, cache_mem_k, cache_mem_v, mem_prompt, norm_mix, w_in, pool_w, pool_scale, conv_w, conv_b, gate_a_w, gate_a_b, gate_x_w, gate_x_b, lru_lambda, norm_pool_out, norm_rnn_out, w_out, norm_xattn, norm_mem, xa_wq, xa_wk, xa_wv, xa_wo, norm_ffn, router_w, router_bias, exp_w_gate, exp_w_up, exp_w_down, sh_w_gate, sh_w_up, sh_w_down, norm_final):
    bp, seq, _ = x_prompt.shape
    bs = x_sample.shape[0]
    tp = bp * seq
    row = lambda v: v.reshape(1, -1)
    bf = lambda v: v.astype(BF16)

    mixw = (row(norm_mix[0]), bf(w_in[0]), _block_diag_pairs(pool_w[0]), row(pool_scale[0]), conv_w[0],
            row(conv_b[0]), _block_diag_pairs(gate_a_w[0]), row(gate_a_b[0]), _block_diag_pairs(gate_x_w[0]),
            row(gate_x_b[0]), row(lru_lambda[0]), row(norm_pool_out[0]), row(norm_rnn_out[0]), bf(w_out[0]))
    xaw = (row(norm_xattn[0]), bf(xa_wq[0]), bf(xa_wo[0]))
    moew = (row(norm_ffn[0]), bf(sh_w_gate[0]), bf(sh_w_up[0]), bf(sh_w_down[0]), router_w[0].T,
            router_bias[0].reshape(N_EXPERTS, 1))

    mk, mv, kb, vb = _memkv(mem_prompt, row(norm_mem[0]), bf(xa_wk[0]), bf(xa_wv[0]))
    (xres_p, hb_p, slot_p, gw_p, gexp_p, grel_p, cover_p, cnt_p, pool_p, conv_p, h_p) = _trunk_p(
        x_prompt, kb, vb, mixw, xaw, moew)

    x1_s, q_s, pool_s, conv_s, h_s = _mix_s(x_sample.reshape(bs, D_MODEL), state_pool[0].transpose(1, 0, 2),
                                            state_conv[0].transpose(1, 0, 2), state_h[0], mixw, xaw[0], xaw[1])
    pool_s = pool_s.transpose(1, 0, 2)
    conv_s = conv_s.transpose(1, 0, 2)
    o_s = _attn_s(q_s, cache_mem_k[0], cache_mem_v[0])
    xres_s, hb_s, slot_s, gw_s, gexp_s, grel_s, cover_s, cnt_s = _post_s(x1_s, o_s, xaw[2], moew)

    n_tiles = tp // TS + 1
    n_rows = _round_up((tp + bs) * TOP_K + n_tiles * N_EXPERTS * (RUN_ALIGN - 1) + N_EXPERTS * (BM - 1),
                       BM * GMM_LANES)
    ints = lambda v: v[..., 0].astype(I32)
    base_p, base_s, fill_at, fill_n, blk_exp, n_real = _plan(ints(cnt_p), ints(cnt_s), n_rows)
    cover_p, cover_s = ints(cover_p).reshape(-1), ints(cover_s).reshape(-1)
    xres_p = xres_p.reshape(tp, D_MODEL)
    hb_p = hb_p.reshape(tp, D_MODEL)
    slot_p = slot_p.transpose(1, 0, 2).reshape(TOP_K, tp)
    gw_p = gw_p.transpose(1, 0, 2).reshape(TOP_K, tp)
    gexp_p = gexp_p.reshape(-1, 1, gexp_p.shape[-1])
    grel_p = grel_p.reshape(-1, 1, grel_p.shape[-1])
    grow_p = _group_rows(gexp_p, grel_p, base_p)
    grow_s = _group_rows(gexp_s, grel_s, base_s)

    xs = _dispatch(cover_p, cover_s, (fill_at, fill_n, n_real), grow_p, grow_s, slot_p, hb_p, slot_s, hb_s, n_rows)
    ys = _gmm(blk_exp, n_real, xs, exp_w_gate[0], exp_w_up[0], exp_w_down[0])

    gfin = row(norm_final)
    y_p = _combine(cover_p, grow_p, slot_p.T, gw_p.T, xres_p, gfin, ys)
    y_s = _combine(cover_s, grow_s, slot_s.T, gw_s.T, xres_s, gfin, ys)

    return (y_p.reshape(bp, seq, D_MODEL), y_s.reshape(bs, 1, D_MODEL),
            pool_p[None], conv_p[None], h_p.reshape(1, bp, D_RNN),
            mk.reshape(1, bp, N_MEM, XA_HEADS, XA_HEAD_DIM), mv.reshape(1, bp, N_MEM, XA_HEADS, XA_HEAD_DIM),
            pool_s[None], conv_s[None], h_s[None])
```

```python
import functools

import jax
import jax.numpy as jnp
from jax import lax
from jax.experimental import pallas as pl
from jax.experimental.pallas import tpu as pltpu

F32 = jnp.float32
BF16 = jnp.bfloat16
I32 = jnp.int32
U32 = jnp.uint32

D_MODEL = 1024
D_POOL = 512
D_RNN = 512
D_IN = D_POOL + 2 * D_RNN
POOL_WINDOWS = (2, 4, 8, 16)
POOL_GROUP = 128
POOL_BUF = 15
CONV_WIDTH = 4
LRU_C = 8.0
N_MEM = 256
XA_HEADS = 4
XA_HEAD_DIM = 256
N_EXPERTS = 64
TOP_K = 8
N_EXPERT_GROUPS = 8
GROUP_SIZE = N_EXPERTS // N_EXPERT_GROUPS
TOPK_GROUPS = 4
D_EXPERT = 256
ROUTED_SCALE = 2.5
EPS = 1e-6
PAST_LEN = 16384

HALO = 16
CONV_HALO = 8
TS = 256
TRUNK_SEQS = 1
BM = 512
RUN_ALIGN = 16
RUN_CHUNKS = (64, 32, 16)
SORT_CHUNK = 512
VMEM_LIMIT = 56 * 1024 * 1024


def _round_up(x, m):
    return (x + m - 1) // m * m


def _sorted_rows(tokens):
    return _round_up(tokens * TOP_K + N_EXPERTS * (RUN_ALIGN - 1), SORT_CHUNK)


def _group_lanes(tokens):
    return _round_up(_sorted_rows(tokens) // RUN_ALIGN, 128)


def _const_spec(shape):
    nd = len(shape)
    return pl.BlockSpec(shape, lambda *_: (0,) * nd, pipeline_mode=pl.Buffered(1))


def _rms(x, g):
    ms = jnp.mean(x * x, axis=-1, keepdims=True)
    return x * lax.rsqrt(ms + EPS) * g


def _dot(a, b):
    return jnp.dot(a, b, preferred_element_type=F32)


def _dot_nt(a, b, precision=None):
    return lax.dot_general(a, b, (((1,), (1,)), ((), ())), precision=precision,
                           preferred_element_type=F32)


def _softplus(x):
    return jnp.maximum(x, 0.0) + jnp.log1p(jnp.exp(-jnp.abs(x)))


def _gates_and_decay(c, pos_is_zero, wa_ref, ba_ref, wx_ref, bx_ref, lam_ref):
    cb = c.astype(BF16)
    half = D_RNN // 2
    ga = jnp.concatenate([_dot(cb[:, :half], wa_ref[0]), _dot(cb[:, half:], wa_ref[1])], axis=1) + ba_ref[...]
    gx = jnp.concatenate([_dot(cb[:, :half], wx_ref[0]), _dot(cb[:, half:], wx_ref[1])], axis=1) + bx_ref[...]
    r = jax.nn.sigmoid(ga)
    i = jax.nn.sigmoid(gx)
    log_a = (-LRU_C) * r * _softplus(-lam_ref[...])
    a = jnp.exp(log_a)
    m2 = 1.0 - a * a
    mult = jnp.where(m2 > 0.0, m2 * lax.rsqrt(m2), 0.0)
    if pos_is_zero is not None:
        mult = jnp.where(pos_is_zero, 1.0, mult)
    return a, mult * i * c


def _pool_project(mean, u_pool, pw_ref, pscale_ref):
    d = (mean - u_pool).astype(BF16)
    half = D_POOL // 2
    y = jnp.concatenate([_dot(d[:, :half], pw_ref[0]), _dot(d[:, half:], pw_ref[1])], axis=1)
    return y * pscale_ref[...]


def _merge_out(y_pool, hs, u_gate, gpool_ref, grnn_ref, wout_ref):
    y_rnn = hs * jax.nn.gelu(u_gate)
    merged = jnp.concatenate([_rms(y_pool, gpool_ref[...]), _rms(y_rnn, grnn_ref[...])], axis=1)
    return _dot(merged.astype(BF16), wout_ref[...])


def _route(h3, wr_ref, rbias_ref, before):
    r_tok = h3.shape[0]
    logits = _dot_nt(wr_ref[...], h3, precision=lax.Precision.HIGHEST)
    scores = jax.nn.sigmoid(logits)
    biased = scores + rbias_ref[...]
    neg = jnp.float32(-jnp.inf)
    gs = []
    for g in range(N_EXPERT_GROUPS):
        xg = biased[g * GROUP_SIZE:(g + 1) * GROUP_SIZE]
        m1 = jnp.max(xg, axis=0, keepdims=True)
        eq = xg == m1
        cnt = jnp.sum(eq.astype(F32), axis=0, keepdims=True)
        m2 = jnp.max(jnp.where(eq, neg, xg), axis=0, keepdims=True)
        gs.append(m1 + jnp.where(cnt >= 2.0, m1, m2))
    pieces = []
    for g in range(N_EXPERT_GROUPS):
        beaten = jnp.zeros_like(gs[g])
        for o in range(N_EXPERT_GROUPS):
            if o == g:
                continue
            wins = (gs[o] > gs[g]) | (gs[o] == gs[g]) if o < g else (gs[o] > gs[g])
            beaten = beaten + wins.astype(F32)
        keep = beaten < float(TOPK_GROUPS)
        xg = biased[g * GROUP_SIZE:(g + 1) * GROUP_SIZE]
        pieces.append(jnp.where(keep, xg, neg))
    cur = jnp.concatenate(pieces, axis=0)
    eid = lax.broadcasted_iota(I32, (N_EXPERTS, r_tok), 0).astype(F32)
    idx_rows, score_rows = [], []
    sel = jnp.zeros((N_EXPERTS, r_tok), F32)
    for _ in range(TOP_K):
        m = jnp.max(cur, axis=0, keepdims=True)
        idx = jnp.min(jnp.where(cur == m, eid, float(N_EXPERTS)), axis=0, keepdims=True)
        oh = eid == idx
        score_rows.append(jnp.sum(jnp.where(oh, scores, 0.0), axis=0, keepdims=True))
        idx_rows.append(idx)
        sel = sel + oh.astype(F32)
        cur = jnp.where(oh, neg, cur)
    tot = score_rows[0]
    for s in score_rows[1:]:
        tot = tot + s
    w_rows = [s / tot * ROUTED_SCALE for s in score_rows]
    rr = lax.broadcasted_iota(I32, (r_tok, r_tok), 0)
    cc = lax.broadcasted_iota(I32, (r_tok, r_tok), 1)
    earlier = _dot(sel.astype(BF16), (rr < cc).astype(BF16))
    counts = jnp.sum(sel, axis=1, keepdims=True)
    run_len = jnp.floor((counts + (RUN_ALIGN - 1.0)) * (1.0 / RUN_ALIGN)) * RUN_ALIGN
    er = lax.broadcasted_iota(I32, (N_EXPERTS, N_EXPERTS), 0)
    ec = lax.broadcasted_iota(I32, (N_EXPERTS, N_EXPERTS), 1)
    run_start = _dot((ec < er).astype(BF16), jnp.broadcast_to(run_len, (N_EXPERTS, 128)).astype(BF16))[:, :1]
    slot = earlier + run_start
    slot_rows = [jnp.sum(jnp.where(eid == idx, slot, 0.0), axis=0, keepdims=True) for idx in idx_rows]
    n_lanes = _group_lanes(r_tok)
    g_row = lax.broadcasted_iota(I32, (N_EXPERTS, n_lanes), 1).astype(F32) * float(RUN_ALIGN)
    owns = (run_start <= g_row) & (g_row < run_start + run_len)
    e_col = lax.broadcasted_iota(I32, (N_EXPERTS, n_lanes), 0).astype(F32)
    g_exp = jnp.sum(jnp.where(owns, e_col, 0.0), axis=0, keepdims=True)
    g_rel = jnp.sum(jnp.where(owns, before + g_row - run_start, 0.0), axis=0, keepdims=True)
    return slot_rows, w_rows, run_len, g_exp, g_rel


def _moe_prologue(x2, gffn_ref, sg_ref, su_ref, sd_ref):
    h3 = _rms(x2, gffn_ref[...])
    h3b = h3.astype(BF16)
    act = jax.nn.silu(_dot(h3b, sg_ref[...])) * _dot(h3b, su_ref[...])
    shared = _dot(act.astype(BF16), sd_ref[...])
    return h3, x2 + shared


def _store_rows(ref, rows, dtype):
    for k, row in enumerate(rows):
        ref[k:k + 1, :] = row.astype(dtype)


def _memkv_kernel(mem_ref, g_ref, wk_ref, wv_ref, k_ref, v_ref, kb_ref, vb_ref):
    m = _rms(mem_ref[0], g_ref[...]).astype(BF16)
    k = _dot(m, wk_ref[...])
    v = _dot(m, wv_ref[...])
    k_ref[0] = k
    v_ref[0] = v
    kb_ref[0] = k.astype(BF16)
    vb_ref[0] = v.astype(BF16)


def _memkv(mem, g, wk, wv):
    b = mem.shape[0]
    blk = pl.BlockSpec((1, N_MEM, D_MODEL), lambda i: (i, 0, 0))
    return pl.pallas_call(
        _memkv_kernel,
        grid=(b,),
        in_specs=[blk, _const_spec((1, D_MODEL)), _const_spec((D_MODEL, D_MODEL)), _const_spec((D_MODEL, D_MODEL))],
        out_specs=[blk, blk, blk, blk],
        out_shape=[jax.ShapeDtypeStruct((b, N_MEM, D_MODEL), F32)] * 2
        + [jax.ShapeDtypeStruct((b, N_MEM, D_MODEL), BF16)] * 2,
        compiler_params=pltpu.CompilerParams(dimension_semantics=("arbitrary",), vmem_limit_bytes=VMEM_LIMIT),
        name="memkv",
    )(mem, g, wk, wv)


def _trunk_p_kernel(x_ref, kb_ref, vb_ref,
                    gmix_ref, win_ref, pw_ref, pscale_ref, cw_ref, cb_ref, wa_ref, ba_ref, wx_ref, bx_ref,
                    lam_ref, gpool_ref, grnn_ref, wout_ref,
                    gxa_ref, wq_ref, wo_ref,
                    gffn_ref, sg_ref, su_ref, sd_ref, wr_ref, rbias_ref,
                    xres_ref, hb_ref, slot_ref, gw_ref, gexp_ref, grel_ref, cover_ref, cnt_ref,
                    pool_ref, conv_ref, hT_ref,
                    pool_prev, conv_prev, h_prev, carry):
    g = pl.program_id(0)
    j = pl.program_id(1)

    @pl.when(j == 0)
    def _():
        pool_prev[...] = jnp.zeros_like(pool_prev)
        conv_prev[...] = jnp.zeros_like(conv_prev)
        h_prev[...] = jnp.zeros_like(h_prev)

    @pl.when((g == 0) & (j == 0))
    def _():
        carry[...] = jnp.zeros_like(carry)

    row = lax.broadcasted_iota(I32, (TS, 1), 0)
    pos = j * TS + row

    for sq in range(TRUNK_SEQS):
        x = x_ref[sq]

        h = _rms(x, gmix_ref[...]).astype(BF16)
        z = _dot(h, win_ref[...])
        u_pool = z[:, :D_POOL]
        u_rnn = z[:, D_POOL:D_POOL + D_RNN]
        u_gate = z[:, D_POOL + D_RNN:]

        ext = jnp.concatenate([pool_prev[sq], u_pool], axis=0)
        means = []
        for grp, w in enumerate(POOL_WINDOWS):
            s = ext[:, grp * POOL_GROUP:(grp + 1) * POOL_GROUP]
            k = 1
            while k < w:
                s = s + pltpu.roll(s, k, 0)
                k *= 2
            inv = 1.0 / jnp.minimum(pos + 1, w).astype(F32)
            means.append(s[HALO:] * inv)
        mean = jnp.concatenate(means, axis=1)
        y_pool = _pool_project(mean, u_pool, pw_ref, pscale_ref)

        extc = jnp.concatenate([conv_prev[sq], u_rnn], axis=0)
        c = u_rnn * cw_ref[CONV_WIDTH - 1:CONV_WIDTH, :]
        for k in range(1, CONV_WIDTH):
            c = c + pltpu.roll(extc, k, 0)[CONV_HALO:] * cw_ref[CONV_WIDTH - 1 - k:CONV_WIDTH - k, :]
        c = c + cb_ref[...]

        a, bt = _gates_and_decay(c, pos == 0, wa_ref, ba_ref, wx_ref, bx_ref, lam_ref)
        k = 1
        while k < TS:
            valid = row >= k
            a_sh = jnp.where(valid, pltpu.roll(a, k, 0), 1.0)
            b_sh = jnp.where(valid, pltpu.roll(bt, k, 0), 0.0)
            bt = bt + a * b_sh
            a = a * a_sh
            k *= 2
        hs = bt + a * h_prev[sq]

        pool_prev[sq] = u_pool[TS - HALO:]
        conv_prev[sq] = u_rnn[TS - CONV_HALO:]
        h_prev[sq] = hs[TS - 1:]
        pool_ref[sq] = u_pool[TS - POOL_BUF:]
        conv_ref[sq] = u_rnn[TS - (CONV_WIDTH - 1):]
        hT_ref[sq] = hs[TS - 1:]

        x1 = x + _merge_out(y_pool, hs, u_gate, gpool_ref, grnn_ref, wout_ref)

        h2 = _rms(x1, gxa_ref[...]).astype(BF16)
        q = (_dot(h2, wq_ref[...]) * (XA_HEAD_DIM ** -0.5)).astype(BF16)
        outs = []
        for hd in range(XA_HEADS):
            sl = slice(hd * XA_HEAD_DIM, (hd + 1) * XA_HEAD_DIM)
            s = _dot_nt(q[:, sl], kb_ref[sq, :, sl])
            s = s - jnp.max(s, axis=-1, keepdims=True)
            p = jnp.exp(s)
            p = p / jnp.sum(p, axis=-1, keepdims=True)
            outs.append(_dot(p.astype(BF16), vb_ref[sq, :, sl]))
        o = jnp.concatenate(outs, axis=1).astype(BF16)
        x2 = x1 + _dot(o, wo_ref[...])

        h3, xres = _moe_prologue(x2, gffn_ref, sg_ref, su_ref, sd_ref)
        xres_ref[sq] = xres
        hb_ref[sq] = h3.astype(BF16)
        slot_rows, w_rows, run_len, g_exp, g_rel = _route(h3, wr_ref, rbias_ref, carry[...])
        _store_rows(slot_ref.at[sq], slot_rows, I32)
        _store_rows(gw_ref.at[sq], w_rows, F32)
        gexp_ref[sq, 0] = g_exp.astype(I32)
        grel_ref[sq, 0] = g_rel.astype(I32)
        cover_ref[sq, 0] = jnp.broadcast_to(jnp.sum(run_len, axis=0, keepdims=True), cover_ref.shape[2:])
        carry[...] = carry[...] + run_len
    cnt_ref[...] = jnp.broadcast_to(carry[...], cnt_ref.shape)


def _trunk_p(x, kb, vb, mixw, xaw, moew):
    bsz, seq, _ = x.shape
    n_j = seq // TS
    q = TRUNK_SEQS
    assert bsz % q == 0
    seq_tile = lambda g, j: (g, j, 0)
    lane_tile = lambda g, j: (g, 0, j)
    per_tile = lambda g, j: (g, j, 0, 0)
    per_seq = lambda g, j: (g, 0, 0)
    weights = list(mixw) + list(xaw) + list(moew)
    in_specs = [pl.BlockSpec((q, TS, D_MODEL), seq_tile),
                pl.BlockSpec((q, N_MEM, D_MODEL), per_seq),
                pl.BlockSpec((q, N_MEM, D_MODEL), per_seq)] + [_const_spec(w.shape) for w in weights]
    lanes = _group_lanes(TS)
    out_shape = [jax.ShapeDtypeStruct((bsz, seq, D_MODEL), F32),
                 jax.ShapeDtypeStruct((bsz, seq, D_MODEL), BF16),
                 jax.ShapeDtypeStruct((bsz, TOP_K, seq), I32),
                 jax.ShapeDtypeStruct((bsz, TOP_K, seq), F32),
                 jax.ShapeDtypeStruct((bsz, n_j, 1, lanes), I32),
                 jax.ShapeDtypeStruct((bsz, n_j, 1, lanes), I32),
                 jax.ShapeDtypeStruct((bsz, n_j, 1, 128), F32),
                 jax.ShapeDtypeStruct((N_EXPERTS, 128), F32),
                 jax.ShapeDtypeStruct((bsz, POOL_BUF, D_POOL), F32),
                 jax.ShapeDtypeStruct((bsz, CONV_WIDTH - 1, D_RNN), F32),
                 jax.ShapeDtypeStruct((bsz, 1, D_RNN), F32)]
    out_specs = [pl.BlockSpec((q, TS, D_MODEL), seq_tile),
                 pl.BlockSpec((q, TS, D_MODEL), seq_tile),
                 pl.BlockSpec((q, TOP_K, TS), lane_tile),
                 pl.BlockSpec((q, TOP_K, TS), lane_tile),
                 pl.BlockSpec((q, 1, 1, lanes), per_tile),
                 pl.BlockSpec((q, 1, 1, lanes), per_tile),
                 pl.BlockSpec((q, 1, 1, 128), per_tile),
                 pl.BlockSpec((N_EXPERTS, 128), lambda g, j: (0, 0)),
                 pl.BlockSpec((q, POOL_BUF, D_POOL), per_seq),
                 pl.BlockSpec((q, CONV_WIDTH - 1, D_RNN), per_seq),
                 pl.BlockSpec((q, 1, D_RNN), per_seq)]
    return pl.pallas_call(
        _trunk_p_kernel,
        grid=(bsz // q, n_j),
        in_specs=in_specs,
        out_specs=out_specs,
        out_shape=out_shape,
        scratch_shapes=[pltpu.VMEM((q, HALO, D_POOL), F32), pltpu.VMEM((q, CONV_HALO, D_RNN), F32),
                        pltpu.VMEM((q, 1, D_RNN), F32), pltpu.VMEM((N_EXPERTS, 1), F32)],
        compiler_params=pltpu.CompilerParams(dimension_semantics=("arbitrary", "arbitrary"),
                                             vmem_limit_bytes=VMEM_LIMIT),
        name="trunk_p",
    )(x, kb, vb, *weights)


def _mix_s_kernel(x_ref, pool_ref, conv_ref, h0_ref,
                  gmix_ref, win_ref, pw_ref, pscale_ref, cw_ref, cb_ref, wa_ref, ba_ref, wx_ref, bx_ref,
                  lam_ref, gpool_ref, grnn_ref, wout_ref, gxa_ref, wq_ref,
                  x1_ref, q_ref, npool_ref, nconv_ref, nh_ref):
    x = x_ref[...]
    h = _rms(x, gmix_ref[...]).astype(BF16)
    z = _dot(h, win_ref[...])
    u_pool = z[:, :D_POOL]
    u_rnn = z[:, D_POOL:D_POOL + D_RNN]
    u_gate = z[:, D_POOL + D_RNN:]

    means = []
    for g, w in enumerate(POOL_WINDOWS):
        sl = slice(g * POOL_GROUP, (g + 1) * POOL_GROUP)
        s = u_pool[:, sl]
        for k in range(1, w):
            s = s + pool_ref[POOL_BUF - k, :, sl]
        means.append(s * (1.0 / min(w, PAST_LEN + 1)))
    mean = jnp.concatenate(means, axis=1)
    y_pool = _pool_project(mean, u_pool, pw_ref, pscale_ref)

    c = u_rnn * cw_ref[CONV_WIDTH - 1:CONV_WIDTH, :]
    for k in range(1, CONV_WIDTH):
        c = c + conv_ref[CONV_WIDTH - 1 - k] * cw_ref[CONV_WIDTH - 1 - k:CONV_WIDTH - k, :]
    c = c + cb_ref[...]
    a, bt = _gates_and_decay(c, None, wa_ref, ba_ref, wx_ref, bx_ref, lam_ref)
    hs = a * h0_ref[...] + bt

    x1 = x + _merge_out(y_pool, hs, u_gate, gpool_ref, grnn_ref, wout_ref)
    x1_ref[...] = x1
    h2 = _rms(x1, gxa_ref[...]).astype(BF16)
    q_ref[...] = _dot(h2, wq_ref[...]) * (XA_HEAD_DIM ** -0.5)

    npool_ref[:POOL_BUF - 1] = pool_ref[1:]
    npool_ref[POOL_BUF - 1] = u_pool
    nconv_ref[:CONV_WIDTH - 2] = conv_ref[1:]
    nconv_ref[CONV_WIDTH - 2] = u_rnn
    nh_ref[...] = hs


def _mix_s(x, pool, conv, h0, mixw, gxa, wq):
    bsz = x.shape[0]
    args = [x, pool, conv, h0] + list(mixw) + [gxa, wq]
    return pl.pallas_call(
        _mix_s_kernel,
        grid=(1,),
        in_specs=[_const_spec(a.shape) for a in args],
        out_specs=[_const_spec((bsz, D_MODEL)), _const_spec((bsz, D_MODEL)), _const_spec(pool.shape),
                   _const_spec(conv.shape), _const_spec((bsz, D_RNN))],
        out_shape=[jax.ShapeDtypeStruct((bsz, D_MODEL), F32), jax.ShapeDtypeStruct((bsz, D_MODEL), F32),
                   jax.ShapeDtypeStruct(pool.shape, F32), jax.ShapeDtypeStruct(conv.shape, F32),
                   jax.ShapeDtypeStruct((bsz, D_RNN), F32)],
        compiler_params=pltpu.CompilerParams(dimension_semantics=("arbitrary",), vmem_limit_bytes=VMEM_LIMIT),
        name="mix_s",
    )(*args)


ATTN_S_BB = 4


def _attn_s_kernel(q_ref, k_ref, v_ref, o_ref):
    q = q_ref[...][:, None]
    s = jnp.sum(k_ref[...] * q, axis=-1, keepdims=True)
    s = s - jnp.max(s, axis=1, keepdims=True)
    p = jnp.exp(s)
    p = p / jnp.sum(p, axis=1, keepdims=True)
    o_ref[...] = jnp.sum(p * v_ref[...], axis=1)


def _attn_s(q, k, v):
    bsz = q.shape[0]
    kv_spec = pl.BlockSpec((ATTN_S_BB, N_MEM, XA_HEADS, XA_HEAD_DIM), lambda i: (i, 0, 0, 0))
    q_spec = pl.BlockSpec((ATTN_S_BB, XA_HEADS, XA_HEAD_DIM), lambda i: (i, 0, 0))
    o = pl.pallas_call(
        _attn_s_kernel,
        grid=(bsz // ATTN_S_BB,),
        in_specs=[q_spec, kv_spec, kv_spec],
        out_specs=q_spec,
        out_shape=jax.ShapeDtypeStruct((bsz, XA_HEADS, XA_HEAD_DIM), F32),
        compiler_params=pltpu.CompilerParams(dimension_semantics=("arbitrary",), vmem_limit_bytes=VMEM_LIMIT),
        name="attn_s",
    )(q.reshape(bsz, XA_HEADS, XA_HEAD_DIM), k, v)
    return o.reshape(bsz, D_MODEL)


def _post_s_kernel(x1_ref, o_ref, wo_ref, gffn_ref, sg_ref, su_ref, sd_ref, wr_ref, rbias_ref,
                   xres_ref, hb_ref, slot_ref, gw_ref, gexp_ref, grel_ref, cover_ref, cnt_ref):
    x2 = x1_ref[...] + _dot(o_ref[...].astype(BF16), wo_ref[...])
    h3, xres = _moe_prologue(x2, gffn_ref, sg_ref, su_ref, sd_ref)
    xres_ref[...] = xres
    hb_ref[...] = h3.astype(BF16)
    slot_rows, w_rows, run_len, g_exp, g_rel = _route(h3, wr_ref, rbias_ref, jnp.zeros((N_EXPERTS, 1), F32))
    _store_rows(slot_ref, slot_rows, I32)
    _store_rows(gw_ref, w_rows, F32)
    gexp_ref[0] = g_exp.astype(I32)
    grel_ref[0] = g_rel.astype(I32)
    cover_ref[0] = jnp.broadcast_to(jnp.sum(run_len, axis=0, keepdims=True), cover_ref.shape[1:])
    cnt_ref[...] = jnp.broadcast_to(run_len, cnt_ref.shape)


def _post_s(x1, o, wo, moew):
    bsz = x1.shape[0]
    args = [x1, o, wo] + list(moew)
    out_shape = [jax.ShapeDtypeStruct((bsz, D_MODEL), F32),
                 jax.ShapeDtypeStruct((bsz, D_MODEL), BF16),
                 jax.ShapeDtypeStruct((TOP_K, bsz), I32),
                 jax.ShapeDtypeStruct((TOP_K, bsz), F32),
                 jax.ShapeDtypeStruct((1, 1, _group_lanes(bsz)), I32),
                 jax.ShapeDtypeStruct((1, 1, _group_lanes(bsz)), I32),
                 jax.ShapeDtypeStruct((1, 1, 128), F32),
                 jax.ShapeDtypeStruct((N_EXPERTS, 128), F32)]
    return pl.pallas_call(
        _post_s_kernel,
        grid=(1,),
        in_specs=[_const_spec(a.shape) for a in args],
        out_specs=[_const_spec(s.shape) for s in out_shape],
        out_shape=out_shape,
        compiler_params=pltpu.CompilerParams(dimension_semantics=("arbitrary",), vmem_limit_bytes=VMEM_LIMIT),
        name="post_s",
    )(*args)


def _plan_kernel(rp_ref, rs_ref, base_p_ref, base_s_ref, fill_at_ref, fill_n_ref, exp_ref, nreal_ref):
    n_blocks = exp_ref.shape[0]
    shift = BM.bit_length() - 1

    def per_expert(e, carry):
        blk0, last_e = carry
        rows = rp_ref[e] + rs_ref[e]
        start = lax.shift_left(blk0, shift)
        n_blk = lax.shift_right_logical(rows + (BM - 1), shift)
        base_p_ref[e] = start
        base_s_ref[e] = start + rp_ref[e]
        fill_at_ref[e] = start + rows
        fill_n_ref[e] = _groups(lax.shift_left(n_blk, shift) - rows)

        def per_block(j, _):
            exp_ref[blk0 + j] = e
            return _

        lax.fori_loop(0, n_blk, per_block, 0)
        return blk0 + n_blk, jnp.where(rows > 0, e, last_e)

    n_real, last_e = lax.fori_loop(0, N_EXPERTS, per_expert, (jnp.int32(0), jnp.int32(0)))
    nreal_ref[0] = n_real

    def rest(b, _):
        exp_ref[b] = last_e
        return _

    lax.fori_loop(n_real, n_blocks, rest, 0)


def _plan(rows_p, rows_s, n_rows):
    assert BM & (BM - 1) == 0 and n_rows % BM == 0
    smem = pl.BlockSpec(memory_space=pltpu.SMEM)
    return pl.pallas_call(
        _plan_kernel,
        in_specs=[smem, smem],
        out_specs=[smem] * 6,
        out_shape=[jax.ShapeDtypeStruct((N_EXPERTS,), I32)] * 4
        + [jax.ShapeDtypeStruct((n_rows // BM,), I32), jax.ShapeDtypeStruct((1,), I32)],
        name="plan",
    )(rows_p, rows_s)


def _groups(n_rows):
    return lax.shift_right_logical(n_rows, RUN_ALIGN.bit_length() - 1)


GROUP_UNROLL = 8


def _for_each_group(n_groups, grow_ref, fn):
    def one(g, priority):
        fn(pl.multiple_of(g * RUN_ALIGN, RUN_ALIGN), pl.multiple_of(grow_ref[0, 0, g], RUN_ALIGN), priority)

    def several(j, _):
        for u in range(GROUP_UNROLL):
            one(j * GROUP_UNROLL + u, u % 2)
        return _

    def single(g, _):
        one(g, 0)
        return _

    n_full = lax.shift_right_logical(n_groups, GROUP_UNROLL.bit_length() - 1)
    lax.fori_loop(0, n_full, several, 0)
    lax.fori_loop(n_full * GROUP_UNROLL, n_groups, single, 0)


def _wait_groups(n_groups, group_wait, bulk_wait):
    def bulk(j, _):
        bulk_wait()
        return _

    def single(g, _):
        group_wait()
        return _

    n_full = lax.shift_right_logical(n_groups, GROUP_UNROLL.bit_length() - 1)
    lax.fori_loop(0, n_full, bulk, 0)
    lax.fori_loop(n_full * GROUP_UNROLL, n_groups, single, 0)


def _group_rows_kernel(gexp_ref, grel_ref, base_ref, out_ref):
    rows = grel_ref[...]
    e = gexp_ref[...]
    for ex in range(N_EXPERTS):
        rows = rows + jnp.where(e == ex, base_ref[ex], 0)
    out_ref[...] = rows


def _group_rows(gexp, grel, base):
    shape = gexp.shape
    flat = (shape[0], shape[-1])
    out = pl.pallas_call(
        _group_rows_kernel,
        in_specs=[pl.BlockSpec(memory_space=pltpu.VMEM), pl.BlockSpec(memory_space=pltpu.VMEM),
                  pl.BlockSpec(memory_space=pltpu.SMEM)],
        out_specs=pl.BlockSpec(memory_space=pltpu.VMEM),
        out_shape=jax.ShapeDtypeStruct(flat, I32),
        name="group_rows",
    )(gexp.reshape(flat), grel.reshape(flat), base)
    return out.reshape(shape)


def _dispatch_kernel(cover_p, cover_s, fill_at, fill_n, nreal_ref, grow_p, grow_s,
                     slot_p_ref, h_p_ref, slot_s_ref, h_s_ref, xs_ref, sbuf, zbuf, sem):
    i = pl.program_id(0)
    last = pl.num_programs(0) - 1
    cur = lax.rem(i, 2)

    def drain(s, n_rows):
        def wait_rows(n):
            pltpu.make_async_copy(sbuf.at[s, pl.ds(0, n)], xs_ref.at[pl.ds(0, n)], sem.at[s]).wait()
        _wait_groups(_groups(n_rows), lambda: wait_rows(RUN_ALIGN), lambda: wait_rows(RUN_ALIGN * GROUP_UNROLL))

    def tile(n_rows, grow_ref, slot_ref, h_ref):
        def sort_chunk(c):
            rid = c * SORT_CHUNK + lax.broadcasted_iota(I32, (SORT_CHUNK, 1), 0)
            hit = rid == slot_ref[0:1, :]
            for k in range(1, TOP_K):
                hit = hit | (rid == slot_ref[k:k + 1, :])
            sbuf[cur, c * SORT_CHUNK:(c + 1) * SORT_CHUNK] = _dot(hit.astype(BF16), h_ref[...]).astype(BF16)

        def send(local, glob, priority):
            pltpu.make_async_copy(sbuf.at[cur, pl.ds(local, RUN_ALIGN)], xs_ref.at[pl.ds(glob, RUN_ALIGN)],
                                  sem.at[cur]).start(priority=priority)

        always = h_ref.shape[0] * TOP_K // SORT_CHUNK
        for c in range(_sorted_rows(h_ref.shape[0]) // SORT_CHUNK):
            if c < always:
                sort_chunk(c)
            else:
                pl.when(c * SORT_CHUNK < n_rows)(functools.partial(sort_chunk, c))
        _for_each_group(_groups(n_rows), grow_ref, send)

    @pl.when(i >= 2)
    def _():
        drain(cur, cover_p[i - 2])

    @pl.when(i < last)
    def _():
        tile(cover_p[i], grow_p, slot_p_ref, h_p_ref)

    @pl.when(i == last)
    def _():
        tile(cover_s[0], grow_s, slot_s_ref, h_s_ref)
        drain(1 - cur, cover_p[last - 1])
        drain(cur, cover_s[0])
        zbuf[...] = jnp.zeros_like(zbuf)

        def group_fill(e, g):
            return pltpu.make_async_copy(
                zbuf.at[pl.ds(0, RUN_ALIGN)],
                xs_ref.at[pl.ds(pl.multiple_of(fill_at[e] + g * RUN_ALIGN, RUN_ALIGN), RUN_ALIGN)], sem.at[2])

        def per_expert(e, n):
            def start(g, _):
                group_fill(e, g).start()
                return _
            lax.fori_loop(0, fill_n[e], start, 0)
            return n + fill_n[e]

        n_fill = lax.fori_loop(0, N_EXPERTS, per_expert, jnp.int32(0))

        def wait_group(g, _):
            group_fill(0, 0).wait()
            return _

        lax.fori_loop(0, n_fill, wait_group, 0)

        n_real = nreal_ref[0]
        n_tail = xs_ref.shape[0] // BM - n_real

        def blk_fill(b):
            return pltpu.make_async_copy(zbuf, xs_ref.at[pl.ds(pl.multiple_of((n_real + b) * BM, BM), BM)], sem.at[2])

        def start_blk(b, _):
            blk_fill(b).start()
            return _

        def wait_blk(b, _):
            blk_fill(b).wait()
            return _

        lax.fori_loop(0, n_tail, start_blk, 0)
        lax.fori_loop(0, n_tail, wait_blk, 0)


def _dispatch(cover_p, cover_s, fills, grow_p, grow_s, slot_p, hb_p, slot_s, hb_s, n_rows):
    n_p = hb_p.shape[0] // TS
    smem = pl.BlockSpec(memory_space=pltpu.SMEM)
    clamp = lambda i: jnp.minimum(i, n_p - 1)
    per_tile = pl.BlockSpec((1, 1, grow_p.shape[-1]), lambda i: (clamp(i), 0, 0), memory_space=pltpu.SMEM)
    return pl.pallas_call(
        _dispatch_kernel,
        grid=(n_p + 1,),
        in_specs=[smem] * 5 + [per_tile, smem,
                               pl.BlockSpec((TOP_K, TS), lambda i: (0, clamp(i))),
                               pl.BlockSpec((TS, D_MODEL), lambda i: (clamp(i), 0)),
                               _const_spec(slot_s.shape), _const_spec(hb_s.shape)],
        out_specs=pl.BlockSpec(memory_space=pl.ANY),
        out_shape=jax.ShapeDtypeStruct((n_rows, D_MODEL), BF16),
        scratch_shapes=[pltpu.VMEM((2, _sorted_rows(TS), D_MODEL), BF16),
                        pltpu.VMEM((BM, D_MODEL), BF16), pltpu.SemaphoreType.DMA((3,))],
        compiler_params=pltpu.CompilerParams(dimension_semantics=("arbitrary",), has_side_effects=True,
                                             vmem_limit_bytes=VMEM_LIMIT),
        name="dispatch",
    )(cover_p, cover_s, *fills, grow_p, grow_s, slot_p, hb_p, slot_s, hb_s)


GMM_LANES = 1


def _gmm_kernel(exp_ref, nreal_ref, *refs):
    x_ref = refs[0]
    w_refs = refs[1:1 + 3 * GMM_LANES]
    y_ref = refs[1 + 3 * GMM_LANES]
    scratch = refs[2 + 3 * GMM_LANES:]
    s = pl.program_id(0)
    n_real = nreal_ref[0]

    for lane in range(GMM_LANES):
        wg_ref, wu_ref, wd_ref = w_refs[3 * lane:3 * lane + 3]
        wgu, wdn = scratch[2 * lane:2 * lane + 2]
        b = s * GMM_LANES + lane
        new_expert = (s == 0) | (exp_ref[b] != exp_ref[jnp.maximum(b - GMM_LANES, 0)])

        @pl.when((b < n_real) & new_expert)
        def _():
            wgu[:, :D_EXPERT] = wg_ref[0].astype(BF16)
            wgu[:, D_EXPERT:] = wu_ref[0].astype(BF16)
            wdn[...] = wd_ref[0].astype(BF16)

    @pl.when(s * GMM_LANES < n_real)
    def _():
        for lane in range(GMM_LANES):
            wgu, wdn = scratch[2 * lane:2 * lane + 2]
            rows = slice(lane * BM, (lane + 1) * BM)
            gu = _dot(x_ref[rows, :], wgu[...])
            act = jax.nn.silu(gu[:, :D_EXPERT]) * gu[:, D_EXPERT:]
            y = _dot(act.astype(BF16), wdn[...]).astype(BF16)
            y_ref[rows, :] = jnp.where(s * GMM_LANES + lane < n_real, y, jnp.zeros_like(y))

    @pl.when(s * GMM_LANES >= n_real)
    def _():
        y_ref[...] = jnp.zeros_like(y_ref)


def _gmm(blk_exp, n_real, xs, wg, wu, wd):
    n_rows = xs.shape[0]
    step_rows = BM * GMM_LANES
    assert n_rows % step_rows == 0
    last_step = lambda nreal: (nreal[0] - 1) // GMM_LANES
    w_specs, scratch = [], []
    for lane in range(GMM_LANES):
        weight = lambda s, exp, nreal, lane=lane: (exp[s * GMM_LANES + lane], 0, 0)
        w_specs += [pl.BlockSpec((1, D_MODEL, D_EXPERT), weight), pl.BlockSpec((1, D_MODEL, D_EXPERT), weight),
                    pl.BlockSpec((1, D_EXPERT, D_MODEL), weight)]
        scratch += [pltpu.VMEM((D_MODEL, 2 * D_EXPERT), BF16), pltpu.VMEM((D_EXPERT, D_MODEL), BF16)]
    grid_spec = pltpu.PrefetchScalarGridSpec(
        num_scalar_prefetch=2,
        grid=(n_rows // step_rows,),
        in_specs=[pl.BlockSpec((step_rows, D_MODEL), lambda s, exp, nreal: (jnp.minimum(s, last_step(nreal)), 0))]
        + w_specs,
        out_specs=pl.BlockSpec((step_rows, D_MODEL), lambda s, exp, nreal: (s, 0)),
        scratch_shapes=scratch,
    )
    return pl.pallas_call(
        _gmm_kernel,
        grid_spec=grid_spec,
        out_shape=jax.ShapeDtypeStruct((n_rows, D_MODEL), BF16),
        compiler_params=pltpu.CompilerParams(dimension_semantics=("arbitrary",), vmem_limit_bytes=VMEM_LIMIT),
        name="gmm",
    )(blk_exp, n_real, xs, *([wg, wu, wd] * GMM_LANES))


def _combine_kernel(cover_ref, grow_ref, grow_next_ref,
                    slot_ref, gw_ref, xres_ref, gfin_ref, ys_ref, out_ref, ybuf, sem):
    i = pl.program_id(0)
    n = pl.num_programs(0)
    cur = lax.rem(i, 2)
    rows = ybuf.shape[1]

    def rows_copy(s, local, glob, n_rows):
        return pltpu.make_async_copy(ys_ref.at[pl.ds(glob, n_rows)], ybuf.at[s, pl.ds(local, n_rows)], sem.at[s])

    def gather(n_rows, gr_ref, s):
        _for_each_group(_groups(n_rows), gr_ref,
                        lambda l, g, priority: rows_copy(s, l, g, RUN_ALIGN).start(priority=priority))

    @pl.when(i == 0)
    def _():
        ybuf[...] = jnp.zeros_like(ybuf)
        gather(cover_ref[0], grow_ref, 0)

    @pl.when(i + 1 < n)
    def _():
        gather(cover_ref[jnp.minimum(i + 1, n - 1)], grow_next_ref, 1 - cur)

    _wait_groups(_groups(cover_ref[i]), lambda: rows_copy(cur, 0, 0, RUN_ALIGN).wait(),
                 lambda: rows_copy(cur, 0, 0, RUN_ALIGN * GROUP_UNROLL).wait())

    tm = slot_ref.shape[0]
    slot_b = [jnp.broadcast_to(slot_ref[:, k:k + 1], (tm, 128)) for k in range(TOP_K)]
    gw_b = [jnp.broadcast_to(gw_ref[:, k:k + 1], (tm, 128)) for k in range(TOP_K)]
    lane = lax.broadcasted_iota(I32, (1, 128), 1)

    def chunk_sum(c):
        pieces = []
        for p in range(SORT_CHUNK // 128):
            col = lane + (c * SORT_CHUNK + p * 128)
            w = jnp.zeros((tm, 128), F32)
            for k in range(TOP_K):
                w = jnp.where(slot_b[k] == col, gw_b[k], w)
            pieces.append(w.astype(BF16))
        return _dot(jnp.concatenate(pieces, axis=1), ybuf[cur, c * SORT_CHUNK:(c + 1) * SORT_CHUNK])

    always = tm * TOP_K // SORT_CHUNK
    acc = xres_ref[...]
    for c in range(always):
        acc = acc + chunk_sum(c)
    out_ref[...] = acc
    for c in range(always, rows // SORT_CHUNK):
        @pl.when(c * SORT_CHUNK < cover_ref[i])
        def _():
            out_ref[...] += chunk_sum(c)
    out_ref[...] = _rms(out_ref[...], gfin_ref[...])


def _combine(cover, grow, slot_t, gw_t, xres, gfin, ys):
    t = xres.shape[0]
    tm = min(t, TS)
    n = t // tm
    smem = pl.BlockSpec(memory_space=pltpu.SMEM)
    lanes = grow.shape[-1]
    this_tile = pl.BlockSpec((1, 1, lanes), lambda i: (i, 0, 0), memory_space=pltpu.SMEM)
    next_tile = pl.BlockSpec((1, 1, lanes), lambda i: (jnp.minimum(i + 1, n - 1), 0, 0), memory_space=pltpu.SMEM)
    return pl.pallas_call(
        _combine_kernel,
        grid=(n,),
        in_specs=[smem, this_tile, next_tile,
                  pl.BlockSpec((tm, TOP_K), lambda i: (i, 0)),
                  pl.BlockSpec((tm, TOP_K), lambda i: (i, 0)),
                  pl.BlockSpec((tm, D_MODEL), lambda i: (i, 0)),
                  _const_spec((1, D_MODEL)),
                  pl.BlockSpec(memory_space=pl.ANY)],
        out_specs=pl.BlockSpec((tm, D_MODEL), lambda i: (i, 0)),
        out_shape=jax.ShapeDtypeStruct((t, D_MODEL), F32),
        scratch_shapes=[pltpu.VMEM((2, _sorted_rows(tm), D_MODEL), BF16), pltpu.SemaphoreType.DMA((2,))],
        compiler_params=pltpu.CompilerParams(dimension_semantics=("arbitrary",), vmem_limit_bytes=VMEM_LIMIT),
        name="combine",
    )(cover, grow, grow, slot_t, gw_t, xres, gfin, ys)


def _block_diag_pairs(w):
    n_h, d, _ = w.shape
    half = n_h // 2
    out = jnp.zeros((2, half * d, half * d), F32)
    for hh in range(n_h):
        p, q = divmod(hh, half)
        out = out.at[p, q * d:(q + 1) * d, q * d:(q + 1) * d].set(w[hh])
    return out.astype(BF16)


def kernel(x_prompt, x_sample, state_pool, state_conv, state_h, cache_mem_k, cache_mem_v, mem_prompt, norm_mix, w_in, pool_w, pool_scale, conv_w, conv_b, gate_a_w, gate_a_b, gate_x_w, gate_x_b, lru_lambda, norm_pool_out, norm_rnn_out, w_out, norm_xattn, norm_mem, xa_wq, xa_wk, xa_wv, xa_wo, norm_ffn, router_w, router_bias, exp_w_gate, exp_w_up, exp_w_down, sh_w_gate, sh_w_up, sh_w_down, norm_final):
    bp, seq, _ = x_prompt.shape
    bs = x_sample.shape[0]
    tp = bp * seq
    row = lambda v: v.reshape(1, -1)
    bf = lambda v: v.astype(BF16)

    mixw = (row(norm_mix[0]), bf(w_in[0]), _block_diag_pairs(pool_w[0]), row(pool_scale[0]), conv_w[0],
            row(conv_b[0]), _block_diag_pairs(gate_a_w[0]), row(gate_a_b[0]), _block_diag_pairs(gate_x_w[0]),
            row(gate_x_b[0]), row(lru_lambda[0]), row(norm_pool_out[0]), row(norm_rnn_out[0]), bf(w_out[0]))
    xaw = (row(norm_xattn[0]), bf(xa_wq[0]), bf(xa_wo[0]))
    moew = (row(norm_ffn[0]), bf(sh_w_gate[0]), bf(sh_w_up[0]), bf(sh_w_down[0]), router_w[0].T,
            router_bias[0].reshape(N_EXPERTS, 1))

    mk, mv, kb, vb = _memkv(mem_prompt, row(norm_mem[0]), bf(xa_wk[0]), bf(xa_wv[0]))
    (xres_p, hb_p, slot_p, gw_p, gexp_p, grel_p, cover_p, cnt_p, pool_p, conv_p, h_p) = _trunk_p(
        x_prompt, kb, vb, mixw, xaw, moew)

    x1_s, q_s, pool_s, conv_s, h_s = _mix_s(x_sample.reshape(bs, D_MODEL), state_pool[0].transpose(1, 0, 2),
                                            state_conv[0].transpose(1, 0, 2), state_h[0], mixw, xaw[0], xaw[1])
    pool_s = pool_s.transpose(1, 0, 2)
    conv_s = conv_s.transpose(1, 0, 2)
    o_s = _attn_s(q_s, cache_mem_k[0], cache_mem_v[0])
    xres_s, hb_s, slot_s, gw_s, gexp_s, grel_s, cover_s, cnt_s = _post_s(x1_s, o_s, xaw[2], moew)

    n_tiles = tp // TS + 1
    n_rows = _round_up((tp + bs) * TOP_K + n_tiles * N_EXPERTS * (RUN_ALIGN - 1) + N_EXPERTS * (BM - 1),
                       BM * GMM_LANES)
    ints = lambda v: v[..., 0].astype(I32)
    base_p, base_s, fill_at, fill_n, blk_exp, n_real = _plan(ints(cnt_p), ints(cnt_s), n_rows)
    cover_p, cover_s = ints(cover_p).reshape(-1), ints(cover_s).reshape(-1)
    xres_p = xres_p.reshape(tp, D_MODEL)
    hb_p = hb_p.reshape(tp, D_MODEL)
    slot_p = slot_p.transpose(1, 0, 2).reshape(TOP_K, tp)
    gw_p = gw_p.transpose(1, 0, 2).reshape(TOP_K, tp)
    gexp_p = gexp_p.reshape(-1, 1, gexp_p.shape[-1])
    grel_p = grel_p.reshape(-1, 1, grel_p.shape[-1])
    grow_p = _group_rows(gexp_p, grel_p, base_p)
    grow_s = _group_rows(gexp_s, grel_s, base_s)

    xs = _dispatch(cover_p, cover_s, (fill_at, fill_n, n_real), grow_p, grow_s, slot_p, hb_p, slot_s, hb_s, n_rows)
    ys = _gmm(blk_exp, n_real, xs, exp_w_gate[0], exp_w_up[0], exp_w_down[0])

    gfin = row(norm_final)
    y_p = _combine(cover_p, grow_p, slot_p.T, gw_p.T, xres_p, gfin, ys)
    y_s = _combine(cover_s, grow_s, slot_s.T, gw_s.T, xres_s, gfin, ys)

    return (y_p.reshape(bp, seq, D_MODEL), y_s.reshape(bs, 1, D_MODEL),
            pool_p[None], conv_p[None], h_p.reshape(1, bp, D_RNN),
            mk.reshape(1, bp, N_MEM, XA_HEADS, XA_HEAD_DIM), mv.reshape(1, bp, N_MEM, XA_HEADS, XA_HEAD_DIM),
            pool_s[None], conv_s[None], h_s[None])
```

```python
import functools

import jax
import jax.numpy as jnp
from jax import lax
from jax.experimental import pallas as pl
from jax.experimental.pallas import tpu as pltpu

F32 = jnp.float32
BF16 = jnp.bfloat16
I32 = jnp.int32
U32 = jnp.uint32

D_MODEL = 1024
D_POOL = 512
D_RNN = 512
D_IN = D_POOL + 2 * D_RNN
POOL_WINDOWS = (2, 4, 8, 16)
POOL_GROUP = 128
POOL_BUF = 15
CONV_WIDTH = 4
LRU_C = 8.0
N_MEM = 256
XA_HEADS = 4
XA_HEAD_DIM = 256
N_EXPERTS = 64
TOP_K = 8
N_EXPERT_GROUPS = 8
GROUP_SIZE = N_EXPERTS // N_EXPERT_GROUPS
TOPK_GROUPS = 4
D_EXPERT = 256
ROUTED_SCALE = 2.5
EPS = 1e-6
PAST_LEN = 16384

HALO = 16
CONV_HALO = 8
TS = 256
TRUNK_SEQS = 1
BM = 512
RUN_ALIGN = 16
RUN_CHUNKS = (64, 32, 16)
SORT_CHUNK = 512
VMEM_LIMIT = 56 * 1024 * 1024


def _round_up(x, m):
    return (x + m - 1) // m * m


def _sorted_rows(tokens):
    return _round_up(tokens * TOP_K + N_EXPERTS * (RUN_ALIGN - 1), SORT_CHUNK)


def _group_lanes(tokens):
    return _round_up(_sorted_rows(tokens) // RUN_ALIGN, 128)


def _const_spec(shape):
    nd = len(shape)
    return pl.BlockSpec(shape, lambda *_: (0,) * nd, pipeline_mode=pl.Buffered(1))


def _rms(x, g):
    ms = jnp.mean(x * x, axis=-1, keepdims=True)
    return x * lax.rsqrt(ms + EPS) * g


def _dot(a, b):
    return jnp.dot(a, b, preferred_element_type=F32)


def _dot_nt(a, b, precision=None):
    return lax.dot_general(a, b, (((1,), (1,)), ((), ())), precision=precision,
                           preferred_element_type=F32)


def _softplus(x):
    return jnp.maximum(x, 0.0) + jnp.log1p(jnp.exp(-jnp.abs(x)))


def _gates_and_decay(c, pos_is_zero, wa_ref, ba_ref, wx_ref, bx_ref, lam_ref):
    cb = c.astype(BF16)
    half = D_RNN // 2
    ga = jnp.concatenate([_dot(cb[:, :half], wa_ref[0]), _dot(cb[:, half:], wa_ref[1])], axis=1) + ba_ref[...]
    gx = jnp.concatenate([_dot(cb[:, :half], wx_ref[0]), _dot(cb[:, half:], wx_ref[1])], axis=1) + bx_ref[...]
    r = jax.nn.sigmoid(ga)
    i = jax.nn.sigmoid(gx)
    log_a = (-LRU_C) * r * _softplus(-lam_ref[...])
    a = jnp.exp(log_a)
    m2 = 1.0 - a * a
    mult = jnp.where(m2 > 0.0, m2 * lax.rsqrt(m2), 0.0)
    if pos_is_zero is not None:
        mult = jnp.where(pos_is_zero, 1.0, mult)
    return a, mult * i * c


def _pool_project(mean, u_pool, pw_ref, pscale_ref):
    d = (mean - u_pool).astype(BF16)
    half = D_POOL // 2
    y = jnp.concatenate([_dot(d[:, :half], pw_ref[0]), _dot(d[:, half:], pw_ref[1])], axis=1)
    return y * pscale_ref[...]


def _merge_out(y_pool, hs, u_gate, gpool_ref, grnn_ref, wout_ref):
    y_rnn = hs * jax.nn.gelu(u_gate)
    merged = jnp.concatenate([_rms(y_pool, gpool_ref[...]), _rms(y_rnn, grnn_ref[...])], axis=1)
    return _dot(merged.astype(BF16), wout_ref[...])


def _route(h3, wr_ref, rbias_ref, before):
    r_tok = h3.shape[0]
    logits = _dot_nt(wr_ref[...], h3, precision=lax.Precision.HIGHEST)
    scores = jax.nn.sigmoid(logits)
    biased = scores + rbias_ref[...]
    neg = jnp.float32(-jnp.inf)
    gs = []
    for g in range(N_EXPERT_GROUPS):
        xg = biased[g * GROUP_SIZE:(g + 1) * GROUP_SIZE]
        m1 = jnp.max(xg, axis=0, keepdims=True)
        eq = xg == m1
        cnt = jnp.sum(eq.astype(F32), axis=0, keepdims=True)
        m2 = jnp.max(jnp.where(eq, neg, xg), axis=0, keepdims=True)
        gs.append(m1 + jnp.where(cnt >= 2.0, m1, m2))
    pieces = []
    for g in range(N_EXPERT_GROUPS):
        beaten = jnp.zeros_like(gs[g])
        for o in range(N_EXPERT_GROUPS):
            if o == g:
                continue
            wins = (gs[o] > gs[g]) | (gs[o] == gs[g]) if o < g else (gs[o] > gs[g])
            beaten = beaten + wins.astype(F32)
        keep = beaten < float(TOPK_GROUPS)
        xg = biased[g * GROUP_SIZE:(g + 1) * GROUP_SIZE]
        pieces.append(jnp.where(keep, xg, neg))
    cur = jnp.concatenate(pieces, axis=0)
    eid = lax.broadcasted_iota(I32, (N_EXPERTS, r_tok), 0).astype(F32)
    idx_rows, score_rows = [], []
    sel = jnp.zeros((N_EXPERTS, r_tok), F32)
    for _ in range(TOP_K):
        m = jnp.max(cur, axis=0, keepdims=True)
        idx = jnp.min(jnp.where(cur == m, eid, float(N_EXPERTS)), axis=0, keepdims=True)
        oh = eid == idx
        score_rows.append(jnp.sum(jnp.where(oh, scores, 0.0), axis=0, keepdims=True))
        idx_rows.append(idx)
        sel = sel + oh.astype(F32)
        cur = jnp.where(oh, neg, cur)
    tot = score_rows[0]
    for s in score_rows[1:]:
        tot = tot + s
    w_rows = [s / tot * ROUTED_SCALE for s in score_rows]
    rr = lax.broadcasted_iota(I32, (r_tok, r_tok), 0)
    cc = lax.broadcasted_iota(I32, (r_tok, r_tok), 1)
    earlier = _dot(sel.astype(BF16), (rr < cc).astype(BF16))
    counts = jnp.sum(sel, axis=1, keepdims=True)
    run_len = jnp.floor((counts + (RUN_ALIGN - 1.0)) * (1.0 / RUN_ALIGN)) * RUN_ALIGN
    er = lax.broadcasted_iota(I32, (N_EXPERTS, N_EXPERTS), 0)
    ec = lax.broadcasted_iota(I32, (N_EXPERTS, N_EXPERTS), 1)
    run_start = _dot((ec < er).astype(BF16), jnp.broadcast_to(run_len, (N_EXPERTS, 128)).astype(BF16))[:, :1]
    slot = earlier + run_start
    slot_rows = [jnp.sum(jnp.where(eid == idx, slot, 0.0), axis=0, keepdims=True) for idx in idx_rows]
    n_lanes = _group_lanes(r_tok)
    g_row = lax.broadcasted_iota(I32, (N_EXPERTS, n_lanes), 1).astype(F32) * float(RUN_ALIGN)
    owns = (run_start <= g_row) & (g_row < run_start + run_len)
    e_col = lax.broadcasted_iota(I32, (N_EXPERTS, n_lanes), 0).astype(F32)
    g_exp = jnp.sum(jnp.where(owns, e_col, 0.0), axis=0, keepdims=True)
    g_rel = jnp.sum(jnp.where(owns, before + g_row - run_start, 0.0), axis=0, keepdims=True)
    return slot_rows, w_rows, run_len, g_exp, g_rel


def _moe_prologue(x2, gffn_ref, sg_ref, su_ref, sd_ref):
    h3 = _rms(x2, gffn_ref[...])
    h3b = h3.astype(BF16)
    act = jax.nn.silu(_dot(h3b, sg_ref[...])) * _dot(h3b, su_ref[...])
    shared = _dot(act.astype(BF16), sd_ref[...])
    return h3, x2 + shared


def _store_rows(ref, rows, dtype):
    for k, row in enumerate(rows):
        ref[k:k + 1, :] = row.astype(dtype)


def _memkv_kernel(mem_ref, g_ref, wk_ref, wv_ref, k_ref, v_ref, kb_ref, vb_ref):
    m = _rms(mem_ref[0], g_ref[...]).astype(BF16)
    k = _dot(m, wk_ref[...])
    v = _dot(m, wv_ref[...])
    k_ref[0] = k.reshape(N_MEM, XA_HEADS, XA_HEAD_DIM)
    v_ref[0] = v.reshape(N_MEM, XA_HEADS, XA_HEAD_DIM)
    kb_ref[0] = k.astype(BF16)
    vb_ref[0] = v.astype(BF16)


def _memkv(mem, g, wk, wv):
    b = mem.shape[0]
    blk = pl.BlockSpec((1, N_MEM, D_MODEL), lambda i: (i, 0, 0))
    return pl.pallas_call(
        _memkv_kernel,
        grid=(b,),
        in_specs=[blk, _const_spec((1, D_MODEL)), _const_spec((D_MODEL, D_MODEL)), _const_spec((D_MODEL, D_MODEL))],
        out_specs=[pl.BlockSpec((1, N_MEM, XA_HEADS, XA_HEAD_DIM), lambda i: (i, 0, 0, 0))] * 2 + [blk, blk],
        out_shape=[jax.ShapeDtypeStruct((b, N_MEM, XA_HEADS, XA_HEAD_DIM), F32)] * 2
        + [jax.ShapeDtypeStruct((b, N_MEM, D_MODEL), BF16)] * 2,
        compiler_params=pltpu.CompilerParams(dimension_semantics=("arbitrary",), vmem_limit_bytes=VMEM_LIMIT),
        name="memkv",
    )(mem, g, wk, wv)


def _trunk_p_kernel(x_ref, kb_ref, vb_ref,
                    gmix_ref, win_ref, pw_ref, pscale_ref, cw_ref, cb_ref, wa_ref, ba_ref, wx_ref, bx_ref,
                    lam_ref, gpool_ref, grnn_ref, wout_ref,
                    gxa_ref, wq_ref, wo_ref,
                    gffn_ref, sg_ref, su_ref, sd_ref, wr_ref, rbias_ref,
                    xres_ref, hb_ref, slot_ref, gw_ref, gexp_ref, grel_ref, cover_ref, cnt_ref,
                    pool_ref, conv_ref, hT_ref,
                    pool_prev, conv_prev, h_prev, carry):
    g = pl.program_id(0)
    j = pl.program_id(1)

    @pl.when(j == 0)
    def _():
        pool_prev[...] = jnp.zeros_like(pool_prev)
        conv_prev[...] = jnp.zeros_like(conv_prev)
        h_prev[...] = jnp.zeros_like(h_prev)

    @pl.when((g == 0) & (j == 0))
    def _():
        carry[...] = jnp.zeros_like(carry)

    row = lax.broadcasted_iota(I32, (TS, 1), 0)
    pos = j * TS + row

    for sq in range(TRUNK_SEQS):
        x = x_ref[sq]

        h = _rms(x, gmix_ref[...]).astype(BF16)
        z = _dot(h, win_ref[...])
        u_pool = z[:, :D_POOL]
        u_rnn = z[:, D_POOL:D_POOL + D_RNN]
        u_gate = z[:, D_POOL + D_RNN:]

        ext = jnp.concatenate([pool_prev[sq], u_pool], axis=0)
        means = []
        for grp, w in enumerate(POOL_WINDOWS):
            s = ext[:, grp * POOL_GROUP:(grp + 1) * POOL_GROUP]
            k = 1
            while k < w:
                s = s + pltpu.roll(s, k, 0)
                k *= 2
            inv = 1.0 / jnp.minimum(pos + 1, w).astype(F32)
            means.append(s[HALO:] * inv)
        mean = jnp.concatenate(means, axis=1)
        y_pool = _pool_project(mean, u_pool, pw_ref, pscale_ref)

        extc = jnp.concatenate([conv_prev[sq], u_rnn], axis=0)
        c = u_rnn * cw_ref[CONV_WIDTH - 1:CONV_WIDTH, :]
        for k in range(1, CONV_WIDTH):
            c = c + pltpu.roll(extc, k, 0)[CONV_HALO:] * cw_ref[CONV_WIDTH - 1 - k:CONV_WIDTH - k, :]
        c = c + cb_ref[...]

        a, bt = _gates_and_decay(c, pos == 0, wa_ref, ba_ref, wx_ref, bx_ref, lam_ref)
        k = 1
        while k < TS:
            valid = row >= k
            a_sh = jnp.where(valid, pltpu.roll(a, k, 0), 1.0)
            b_sh = jnp.where(valid, pltpu.roll(bt, k, 0), 0.0)
            bt = bt + a * b_sh
            a = a * a_sh
            k *= 2
        hs = bt + a * h_prev[sq]

        pool_prev[sq] = u_pool[TS - HALO:]
        conv_prev[sq] = u_rnn[TS - CONV_HALO:]
        h_prev[sq] = hs[TS - 1:]
        pool_ref[sq] = u_pool[TS - POOL_BUF:]
        conv_ref[sq] = u_rnn[TS - (CONV_WIDTH - 1):]
        hT_ref[sq] = hs[TS - 1:]

        x1 = x + _merge_out(y_pool, hs, u_gate, gpool_ref, grnn_ref, wout_ref)

        h2 = _rms(x1, gxa_ref[...]).astype(BF16)
        q = (_dot(h2, wq_ref[...]) * (XA_HEAD_DIM ** -0.5)).astype(BF16)
        outs = []
        for hd in range(XA_HEADS):
            sl = slice(hd * XA_HEAD_DIM, (hd + 1) * XA_HEAD_DIM)
            s = _dot_nt(q[:, sl], kb_ref[sq, :, sl])
            s = s - jnp.max(s, axis=-1, keepdims=True)
            p = jnp.exp(s)
            p = p / jnp.sum(p, axis=-1, keepdims=True)
            outs.append(_dot(p.astype(BF16), vb_ref[sq, :, sl]))
        o = jnp.concatenate(outs, axis=1).astype(BF16)
        x2 = x1 + _dot(o, wo_ref[...])

        h3, xres = _moe_prologue(x2, gffn_ref, sg_ref, su_ref, sd_ref)
        xres_ref[sq] = xres
        hb_ref[sq] = h3.astype(BF16)
        slot_rows, w_rows, run_len, g_exp, g_rel = _route(h3, wr_ref, rbias_ref, carry[...])
        _store_rows(slot_ref.at[sq], slot_rows, I32)
        _store_rows(gw_ref.at[sq], w_rows, F32)
        gexp_ref[sq, 0] = g_exp.astype(I32)
        grel_ref[sq, 0] = g_rel.astype(I32)
        cover_ref[sq, 0] = jnp.broadcast_to(jnp.sum(run_len, axis=0, keepdims=True), cover_ref.shape[2:])
        carry[...] = carry[...] + run_len
    cnt_ref[...] = jnp.broadcast_to(carry[...], cnt_ref.shape)


def _trunk_p(x, kb, vb, mixw, xaw, moew):
    bsz, seq, _ = x.shape
    n_j = seq // TS
    q = TRUNK_SEQS
    assert bsz % q == 0
    seq_tile = lambda g, j: (g, j, 0)
    lane_tile = lambda g, j: (g, 0, j)
    per_tile = lambda g, j: (g, j, 0, 0)
    per_seq = lambda g, j: (g, 0, 0)
    weights = list(mixw) + list(xaw) + list(moew)
    in_specs = [pl.BlockSpec((q, TS, D_MODEL), seq_tile),
                pl.BlockSpec((q, N_MEM, D_MODEL), per_seq),
                pl.BlockSpec((q, N_MEM, D_MODEL), per_seq)] + [_const_spec(w.shape) for w in weights]
    lanes = _group_lanes(TS)
    out_shape = [jax.ShapeDtypeStruct((bsz, seq, D_MODEL), F32),
                 jax.ShapeDtypeStruct((bsz, seq, D_MODEL), BF16),
                 jax.ShapeDtypeStruct((bsz, TOP_K, seq), I32),
                 jax.ShapeDtypeStruct((bsz, TOP_K, seq), F32),
                 jax.ShapeDtypeStruct((bsz, n_j, 1, lanes), I32),
                 jax.ShapeDtypeStruct((bsz, n_j, 1, lanes), I32),
                 jax.ShapeDtypeStruct((bsz, n_j, 1, 128), F32),
                 jax.ShapeDtypeStruct((N_EXPERTS, 128), F32),
                 jax.ShapeDtypeStruct((bsz, POOL_BUF, D_POOL), F32),
                 jax.ShapeDtypeStruct((bsz, CONV_WIDTH - 1, D_RNN), F32),
                 jax.ShapeDtypeStruct((bsz, 1, D_RNN), F32)]
    out_specs = [pl.BlockSpec((q, TS, D_MODEL), seq_tile),
                 pl.BlockSpec((q, TS, D_MODEL), seq_tile),
                 pl.BlockSpec((q, TOP_K, TS), lane_tile),
                 pl.BlockSpec((q, TOP_K, TS), lane_tile),
                 pl.BlockSpec((q, 1, 1, lanes), per_tile),
                 pl.BlockSpec((q, 1, 1, lanes), per_tile),
                 pl.BlockSpec((q, 1, 1, 128), per_tile),
                 pl.BlockSpec((N_EXPERTS, 128), lambda g, j: (0, 0)),
                 pl.BlockSpec((q, POOL_BUF, D_POOL), per_seq),
                 pl.BlockSpec((q, CONV_WIDTH - 1, D_RNN), per_seq),
                 pl.BlockSpec((q, 1, D_RNN), per_seq)]
    return pl.pallas_call(
        _trunk_p_kernel,
        grid=(bsz // q, n_j),
        in_specs=in_specs,
        out_specs=out_specs,
        out_shape=out_shape,
        scratch_shapes=[pltpu.VMEM((q, HALO, D_POOL), F32), pltpu.VMEM((q, CONV_HALO, D_RNN), F32),
                        pltpu.VMEM((q, 1, D_RNN), F32), pltpu.VMEM((N_EXPERTS, 1), F32)],
        compiler_params=pltpu.CompilerParams(dimension_semantics=("arbitrary", "arbitrary"),
                                             vmem_limit_bytes=VMEM_LIMIT),
        name="trunk_p",
    )(x, kb, vb, *weights)


def _mix_s_kernel(x_ref, pool_ref, conv_ref, h0_ref,
                  gmix_ref, win_ref, pw_ref, pscale_ref, cw_ref, cb_ref, wa_ref, ba_ref, wx_ref, bx_ref,
                  lam_ref, gpool_ref, grnn_ref, wout_ref, gxa_ref, wq_ref,
                  x1_ref, q_ref, npool_ref, nconv_ref, nh_ref):
    x = x_ref[...]
    h = _rms(x, gmix_ref[...]).astype(BF16)
    z = _dot(h, win_ref[...])
    u_pool = z[:, :D_POOL]
    u_rnn = z[:, D_POOL:D_POOL + D_RNN]
    u_gate = z[:, D_POOL + D_RNN:]

    means = []
    for g, w in enumerate(POOL_WINDOWS):
        sl = slice(g * POOL_GROUP, (g + 1) * POOL_GROUP)
        s = u_pool[:, sl]
        for k in range(1, w):
            s = s + pool_ref[POOL_BUF - k, :, sl]
        means.append(s * (1.0 / min(w, PAST_LEN + 1)))
    mean = jnp.concatenate(means, axis=1)
    y_pool = _pool_project(mean, u_pool, pw_ref, pscale_ref)

    c = u_rnn * cw_ref[CONV_WIDTH - 1:CONV_WIDTH, :]
    for k in range(1, CONV_WIDTH):
        c = c + conv_ref[CONV_WIDTH - 1 - k] * cw_ref[CONV_WIDTH - 1 - k:CONV_WIDTH - k, :]
    c = c + cb_ref[...]
    a, bt = _gates_and_decay(c, None, wa_ref, ba_ref, wx_ref, bx_ref, lam_ref)
    hs = a * h0_ref[...] + bt

    x1 = x + _merge_out(y_pool, hs, u_gate, gpool_ref, grnn_ref, wout_ref)
    x1_ref[...] = x1
    h2 = _rms(x1, gxa_ref[...]).astype(BF16)
    q_ref[...] = _dot(h2, wq_ref[...]) * (XA_HEAD_DIM ** -0.5)

    npool_ref[:POOL_BUF - 1] = pool_ref[1:]
    npool_ref[POOL_BUF - 1] = u_pool
    nconv_ref[:CONV_WIDTH - 2] = conv_ref[1:]
    nconv_ref[CONV_WIDTH - 2] = u_rnn
    nh_ref[...] = hs


def _mix_s(x, pool, conv, h0, mixw, gxa, wq):
    bsz = x.shape[0]
    args = [x, pool, conv, h0] + list(mixw) + [gxa, wq]
    return pl.pallas_call(
        _mix_s_kernel,
        grid=(1,),
        in_specs=[_const_spec(a.shape) for a in args],
        out_specs=[_const_spec((bsz, D_MODEL)), _const_spec((bsz, D_MODEL)), _const_spec(pool.shape),
                   _const_spec(conv.shape), _const_spec((bsz, D_RNN))],
        out_shape=[jax.ShapeDtypeStruct((bsz, D_MODEL), F32), jax.ShapeDtypeStruct((bsz, D_MODEL), F32),
                   jax.ShapeDtypeStruct(pool.shape, F32), jax.ShapeDtypeStruct(conv.shape, F32),
                   jax.ShapeDtypeStruct((bsz, D_RNN), F32)],
        compiler_params=pltpu.CompilerParams(dimension_semantics=("arbitrary",), vmem_limit_bytes=VMEM_LIMIT),
        name="mix_s",
    )(*args)


ATTN_S_BB = 4


def _attn_s_kernel(q_ref, k_ref, v_ref, o_ref):
    q = q_ref[...][:, None]
    s = jnp.sum(k_ref[...] * q, axis=-1, keepdims=True)
    s = s - jnp.max(s, axis=1, keepdims=True)
    p = jnp.exp(s)
    p = p / jnp.sum(p, axis=1, keepdims=True)
    o_ref[...] = jnp.sum(p * v_ref[...], axis=1)


def _attn_s(q, k, v):
    bsz = q.shape[0]
    kv_spec = pl.BlockSpec((ATTN_S_BB, N_MEM, XA_HEADS, XA_HEAD_DIM), lambda i: (i, 0, 0, 0))
    q_spec = pl.BlockSpec((ATTN_S_BB, XA_HEADS, XA_HEAD_DIM), lambda i: (i, 0, 0))
    o = pl.pallas_call(
        _attn_s_kernel,
        grid=(bsz // ATTN_S_BB,),
        in_specs=[q_spec, kv_spec, kv_spec],
        out_specs=q_spec,
        out_shape=jax.ShapeDtypeStruct((bsz, XA_HEADS, XA_HEAD_DIM), F32),
        compiler_params=pltpu.CompilerParams(dimension_semantics=("arbitrary",), vmem_limit_bytes=VMEM_LIMIT),
        name="attn_s",
    )(q.reshape(bsz, XA_HEADS, XA_HEAD_DIM), k, v)
    return o.reshape(bsz, D_MODEL)


def _post_s_kernel(x1_ref, o_ref, wo_ref, gffn_ref, sg_ref, su_ref, sd_ref, wr_ref, rbias_ref,
                   xres_ref, hb_ref, slot_ref, gw_ref, gexp_ref, grel_ref, cover_ref, cnt_ref):
    x2 = x1_ref[...] + _dot(o_ref[...].astype(BF16), wo_ref[...])
    h3, xres = _moe_prologue(x2, gffn_ref, sg_ref, su_ref, sd_ref)
    xres_ref[...] = xres
    hb_ref[...] = h3.astype(BF16)
    slot_rows, w_rows, run_len, g_exp, g_rel = _route(h3, wr_ref, rbias_ref, jnp.zeros((N_EXPERTS, 1), F32))
    _store_rows(slot_ref, slot_rows, I32)
    _store_rows(gw_ref, w_rows, F32)
    gexp_ref[0] = g_exp.astype(I32)
    grel_ref[0] = g_rel.astype(I32)
    cover_ref[0] = jnp.broadcast_to(jnp.sum(run_len, axis=0, keepdims=True), cover_ref.shape[1:])
    cnt_ref[...] = jnp.broadcast_to(run_len, cnt_ref.shape)


def _post_s(x1, o, wo, moew):
    bsz = x1.shape[0]
    args = [x1, o, wo] + list(moew)
    out_shape = [jax.ShapeDtypeStruct((bsz, D_MODEL), F32),
                 jax.ShapeDtypeStruct((bsz, D_MODEL), BF16),
                 jax.ShapeDtypeStruct((TOP_K, bsz), I32),
                 jax.ShapeDtypeStruct((TOP_K, bsz), F32),
                 jax.ShapeDtypeStruct((1, 1, _group_lanes(bsz)), I32),
                 jax.ShapeDtypeStruct((1, 1, _group_lanes(bsz)), I32),
                 jax.ShapeDtypeStruct((1, 1, 128), F32),
                 jax.ShapeDtypeStruct((N_EXPERTS, 128), F32)]
    return pl.pallas_call(
        _post_s_kernel,
        grid=(1,),
        in_specs=[_const_spec(a.shape) for a in args],
        out_specs=[_const_spec(s.shape) for s in out_shape],
        out_shape=out_shape,
        compiler_params=pltpu.CompilerParams(dimension_semantics=("arbitrary",), vmem_limit_bytes=VMEM_LIMIT),
        name="post_s",
    )(*args)


def _plan_kernel(rp_ref, rs_ref, base_p_ref, base_s_ref, fill_at_ref, fill_n_ref, exp_ref, nreal_ref):
    n_blocks = exp_ref.shape[0]
    shift = BM.bit_length() - 1

    def per_expert(e, carry):
        blk0, last_e = carry
        rows = rp_ref[e] + rs_ref[e]
        start = lax.shift_left(blk0, shift)
        n_blk = lax.shift_right_logical(rows + (BM - 1), shift)
        base_p_ref[e] = start
        base_s_ref[e] = start + rp_ref[e]
        fill_at_ref[e] = start + rows
        fill_n_ref[e] = _groups(lax.shift_left(n_blk, shift) - rows)

        def per_block(j, _):
            exp_ref[blk0 + j] = e
            return _

        lax.fori_loop(0, n_blk, per_block, 0)
        return blk0 + n_blk, jnp.where(rows > 0, e, last_e)

    n_real, last_e = lax.fori_loop(0, N_EXPERTS, per_expert, (jnp.int32(0), jnp.int32(0)))
    nreal_ref[0] = n_real

    def rest(b, _):
        exp_ref[b] = last_e
        return _

    lax.fori_loop(n_real, n_blocks, rest, 0)


def _plan(rows_p, rows_s, n_rows):
    assert BM & (BM - 1) == 0 and n_rows % BM == 0
    smem = pl.BlockSpec(memory_space=pltpu.SMEM)
    return pl.pallas_call(
        _plan_kernel,
        in_specs=[smem, smem],
        out_specs=[smem] * 6,
        out_shape=[jax.ShapeDtypeStruct((N_EXPERTS,), I32)] * 4
        + [jax.ShapeDtypeStruct((n_rows // BM,), I32), jax.ShapeDtypeStruct((1,), I32)],
        name="plan",
    )(rows_p, rows_s)


def _groups(n_rows):
    return lax.shift_right_logical(n_rows, RUN_ALIGN.bit_length() - 1)


GROUP_UNROLL = 8


def _for_each_group(n_groups, grow_ref, fn):
    def one(g, priority):
        fn(pl.multiple_of(g * RUN_ALIGN, RUN_ALIGN), pl.multiple_of(grow_ref[0, 0, g], RUN_ALIGN), priority)

    def several(j, _):
        for u in range(GROUP_UNROLL):
            one(j * GROUP_UNROLL + u, u % 2)
        return _

    def single(g, _):
        one(g, 0)
        return _

    n_full = lax.shift_right_logical(n_groups, GROUP_UNROLL.bit_length() - 1)
    lax.fori_loop(0, n_full, several, 0)
    lax.fori_loop(n_full * GROUP_UNROLL, n_groups, single, 0)


def _wait_groups(n_groups, group_wait, bulk_wait):
    def bulk(j, _):
        bulk_wait()
        return _

    def single(g, _):
        group_wait()
        return _

    n_full = lax.shift_right_logical(n_groups, GROUP_UNROLL.bit_length() - 1)
    lax.fori_loop(0, n_full, bulk, 0)
    lax.fori_loop(n_full * GROUP_UNROLL, n_groups, single, 0)


def _group_rows_kernel(gexp_ref, grel_ref, base_ref, out_ref):
    rows = grel_ref[...]
    e = gexp_ref[...]
    for ex in range(N_EXPERTS):
        rows = rows + jnp.where(e == ex, base_ref[ex], 0)
    out_ref[...] = rows


def _group_rows(gexp, grel, base):
    shape = gexp.shape
    flat = (shape[0], shape[-1])
    out = pl.pallas_call(
        _group_rows_kernel,
        in_specs=[pl.BlockSpec(memory_space=pltpu.VMEM), pl.BlockSpec(memory_space=pltpu.VMEM),
                  pl.BlockSpec(memory_space=pltpu.SMEM)],
        out_specs=pl.BlockSpec(memory_space=pltpu.VMEM),
        out_shape=jax.ShapeDtypeStruct(flat, I32),
        name="group_rows",
    )(gexp.reshape(flat), grel.reshape(flat), base)
    return out.reshape(shape)


def _dispatch_kernel(cover_p, cover_s, fill_at, fill_n, nreal_ref, grow_p, grow_s,
                     slot_p_ref, h_p_ref, slot_s_ref, h_s_ref, xs_ref, sbuf, zbuf, sem):
    i = pl.program_id(0)
    last = pl.num_programs(0) - 1
    cur = lax.rem(i, 2)

    def drain(s, n_rows):
        def wait_rows(n):
            pltpu.make_async_copy(sbuf.at[s, pl.ds(0, n)], xs_ref.at[pl.ds(0, n)], sem.at[s]).wait()
        _wait_groups(_groups(n_rows), lambda: wait_rows(RUN_ALIGN), lambda: wait_rows(RUN_ALIGN * GROUP_UNROLL))

    def tile(n_rows, grow_ref, slot_ref, h_ref):
        def sort_chunk(c):
            rid = c * SORT_CHUNK + lax.broadcasted_iota(I32, (SORT_CHUNK, 1), 0)
            hit = rid == slot_ref[0:1, :]
            for k in range(1, TOP_K):
                hit = hit | (rid == slot_ref[k:k + 1, :])
            sbuf[cur, c * SORT_CHUNK:(c + 1) * SORT_CHUNK] = _dot(hit.astype(BF16), h_ref[...]).astype(BF16)

        def send(local, glob, priority):
            pltpu.make_async_copy(sbuf.at[cur, pl.ds(local, RUN_ALIGN)], xs_ref.at[pl.ds(glob, RUN_ALIGN)],
                                  sem.at[cur]).start(priority=priority)

        always = h_ref.shape[0] * TOP_K // SORT_CHUNK
        for c in range(_sorted_rows(h_ref.shape[0]) // SORT_CHUNK):
            if c < always:
                sort_chunk(c)
            else:
                pl.when(c * SORT_CHUNK < n_rows)(functools.partial(sort_chunk, c))
        _for_each_group(_groups(n_rows), grow_ref, send)

    @pl.when(i >= 2)
    def _():
        drain(cur, cover_p[i - 2])

    @pl.when(i < last)
    def _():
        tile(cover_p[i], grow_p, slot_p_ref, h_p_ref)

    @pl.when(i == last)
    def _():
        tile(cover_s[0], grow_s, slot_s_ref, h_s_ref)
        drain(1 - cur, cover_p[last - 1])
        drain(cur, cover_s[0])
        zbuf[...] = jnp.zeros_like(zbuf)

        def group_fill(e, g):
            return pltpu.make_async_copy(
                zbuf.at[pl.ds(0, RUN_ALIGN)],
                xs_ref.at[pl.ds(pl.multiple_of(fill_at[e] + g * RUN_ALIGN, RUN_ALIGN), RUN_ALIGN)], sem.at[2])

        def per_expert(e, n):
            def start(g, _):
                group_fill(e, g).start()
                return _
            lax.fori_loop(0, fill_n[e], start, 0)
            return n + fill_n[e]

        n_fill = lax.fori_loop(0, N_EXPERTS, per_expert, jnp.int32(0))

        def wait_group(g, _):
            group_fill(0, 0).wait()
            return _

        lax.fori_loop(0, n_fill, wait_group, 0)

        n_real = nreal_ref[0]
        n_tail = xs_ref.shape[0] // BM - n_real

        def blk_fill(b):
            return pltpu.make_async_copy(zbuf, xs_ref.at[pl.ds(pl.multiple_of((n_real + b) * BM, BM), BM)], sem.at[2])

        def start_blk(b, _):
            blk_fill(b).start()
            return _

        def wait_blk(b, _):
            blk_fill(b).wait()
            return _

        lax.fori_loop(0, n_tail, start_blk, 0)
        lax.fori_loop(0, n_tail, wait_blk, 0)


def _dispatch(cover_p, cover_s, fills, grow_p, grow_s, slot_p, hb_p, slot_s, hb_s, n_rows):
    n_p = hb_p.shape[0] // TS
    smem = pl.BlockSpec(memory_space=pltpu.SMEM)
    clamp = lambda i: jnp.minimum(i, n_p - 1)
    per_tile = pl.BlockSpec((1, 1, grow_p.shape[-1]), lambda i: (clamp(i), 0, 0), memory_space=pltpu.SMEM)
    return pl.pallas_call(
        _dispatch_kernel,
        grid=(n_p + 1,),
        in_specs=[smem] * 5 + [per_tile, smem,
                               pl.BlockSpec((TOP_K, TS), lambda i: (0, clamp(i))),
                               pl.BlockSpec((TS, D_MODEL), lambda i: (clamp(i), 0)),
                               _const_spec(slot_s.shape), _const_spec(hb_s.shape)],
        out_specs=pl.BlockSpec(memory_space=pl.ANY),
        out_shape=jax.ShapeDtypeStruct((n_rows, D_MODEL), BF16),
        scratch_shapes=[pltpu.VMEM((2, _sorted_rows(TS), D_MODEL), BF16),
                        pltpu.VMEM((BM, D_MODEL), BF16), pltpu.SemaphoreType.DMA((3,))],
        compiler_params=pltpu.CompilerParams(dimension_semantics=("arbitrary",), has_side_effects=True,
                                             vmem_limit_bytes=VMEM_LIMIT),
        name="dispatch",
    )(cover_p, cover_s, *fills, grow_p, grow_s, slot_p, hb_p, slot_s, hb_s)


GMM_LANES = 2


def _gmm_kernel(exp_ref, nreal_ref, *refs):
    x_ref = refs[0]
    w_refs = refs[1:1 + 3 * GMM_LANES]
    y_ref = refs[1 + 3 * GMM_LANES]
    scratch = refs[2 + 3 * GMM_LANES:]
    s = pl.program_id(0)
    n_real = nreal_ref[0]

    for lane in range(GMM_LANES):
        wg_ref, wu_ref, wd_ref = w_refs[3 * lane:3 * lane + 3]
        wgu, wdn = scratch[2 * lane:2 * lane + 2]
        b = s * GMM_LANES + lane
        new_expert = (s == 0) | (exp_ref[b] != exp_ref[jnp.maximum(b - GMM_LANES, 0)])

        @pl.when((b < n_real) & new_expert)
        def _():
            wgu[:, :D_EXPERT] = wg_ref[0].astype(BF16)
            wgu[:, D_EXPERT:] = wu_ref[0].astype(BF16)
            wdn[...] = wd_ref[0].astype(BF16)

    @pl.when(s * GMM_LANES < n_real)
    def _():
        for lane in range(GMM_LANES):
            wgu, wdn = scratch[2 * lane:2 * lane + 2]
            rows = slice(lane * BM, (lane + 1) * BM)
            gu = _dot(x_ref[rows, :], wgu[...])
            act = jax.nn.silu(gu[:, :D_EXPERT]) * gu[:, D_EXPERT:]
            y = _dot(act.astype(BF16), wdn[...]).astype(BF16)
            y_ref[rows, :] = jnp.where(s * GMM_LANES + lane < n_real, y, jnp.zeros_like(y))

    @pl.when(s * GMM_LANES >= n_real)
    def _():
        y_ref[...] = jnp.zeros_like(y_ref)


def _gmm(blk_exp, n_real, xs, wg, wu, wd):
    n_rows = xs.shape[0]
    step_rows = BM * GMM_LANES
    assert n_rows % step_rows == 0
    last_step = lambda nreal: (nreal[0] - 1) // GMM_LANES
    w_specs, scratch = [], []
    for lane in range(GMM_LANES):
        weight = lambda s, exp, nreal, lane=lane: (exp[s * GMM_LANES + lane], 0, 0)
        w_specs += [pl.BlockSpec((1, D_MODEL, D_EXPERT), weight), pl.BlockSpec((1, D_MODEL, D_EXPERT), weight),
                    pl.BlockSpec((1, D_EXPERT, D_MODEL), weight)]
        scratch += [pltpu.VMEM((D_MODEL, 2 * D_EXPERT), BF16), pltpu.VMEM((D_EXPERT, D_MODEL), BF16)]
    grid_spec = pltpu.PrefetchScalarGridSpec(
        num_scalar_prefetch=2,
        grid=(n_rows // step_rows,),
        in_specs=[pl.BlockSpec((step_rows, D_MODEL), lambda s, exp, nreal: (jnp.minimum(s, last_step(nreal)), 0))]
        + w_specs,
        out_specs=pl.BlockSpec((step_rows, D_MODEL), lambda s, exp, nreal: (s, 0)),
        scratch_shapes=scratch,
    )
    return pl.pallas_call(
        _gmm_kernel,
        grid_spec=grid_spec,
        out_shape=jax.ShapeDtypeStruct((n_rows, D_MODEL), BF16),
        compiler_params=pltpu.CompilerParams(dimension_semantics=("arbitrary",), vmem_limit_bytes=VMEM_LIMIT),
        name="gmm",
    )(blk_exp, n_real, xs, *([wg, wu, wd] * GMM_LANES))


def _combine_kernel(cover_ref, grow_ref, grow_next_ref,
                    slot_ref, gw_ref, xres_ref, gfin_ref, ys_ref, out_ref, ybuf, sem):
    i = pl.program_id(0)
    n = pl.num_programs(0)
    cur = lax.rem(i, 2)
    rows = ybuf.shape[1]

    def rows_copy(s, local, glob, n_rows):
        return pltpu.make_async_copy(ys_ref.at[pl.ds(glob, n_rows)], ybuf.at[s, pl.ds(local, n_rows)], sem.at[s])

    def gather(n_rows, gr_ref, s):
        _for_each_group(_groups(n_rows), gr_ref,
                        lambda l, g, priority: rows_copy(s, l, g, RUN_ALIGN).start(priority=priority))

    @pl.when(i == 0)
    def _():
        ybuf[...] = jnp.zeros_like(ybuf)
        gather(cover_ref[0], grow_ref, 0)

    @pl.when(i + 1 < n)
    def _():
        gather(cover_ref[jnp.minimum(i + 1, n - 1)], grow_next_ref, 1 - cur)

    _wait_groups(_groups(cover_ref[i]), lambda: rows_copy(cur, 0, 0, RUN_ALIGN).wait(),
                 lambda: rows_copy(cur, 0, 0, RUN_ALIGN * GROUP_UNROLL).wait())

    tm = slot_ref.shape[0]
    slot_b = [jnp.broadcast_to(slot_ref[:, k:k + 1], (tm, 128)) for k in range(TOP_K)]
    gw_b = [jnp.broadcast_to(gw_ref[:, k:k + 1], (tm, 128)) for k in range(TOP_K)]
    lane = lax.broadcasted_iota(I32, (1, 128), 1)

    def chunk_sum(c):
        pieces = []
        for p in range(SORT_CHUNK // 128):
            col = lane + (c * SORT_CHUNK + p * 128)
            w = jnp.zeros((tm, 128), F32)
            for k in range(TOP_K):
                w = jnp.where(slot_b[k] == col, gw_b[k], w)
            pieces.append(w.astype(BF16))
        return _dot(jnp.concatenate(pieces, axis=1), ybuf[cur, c * SORT_CHUNK:(c + 1) * SORT_CHUNK])

    always = tm * TOP_K // SORT_CHUNK
    acc = xres_ref[...]
    for c in range(always):
        acc = acc + chunk_sum(c)
    out_ref[...] = acc
    for c in range(always, rows // SORT_CHUNK):
        @pl.when(c * SORT_CHUNK < cover_ref[i])
        def _():
            out_ref[...] += chunk_sum(c)
    out_ref[...] = _rms(out_ref[...], gfin_ref[...])


def _combine(cover, grow, slot_t, gw_t, xres, gfin, ys):
    t = xres.shape[0]
    tm = min(t, TS)
    n = t // tm
    smem = pl.BlockSpec(memory_space=pltpu.SMEM)
    lanes = grow.shape[-1]
    this_tile = pl.BlockSpec((1, 1, lanes), lambda i: (i, 0, 0), memory_space=pltpu.SMEM)
    next_tile = pl.BlockSpec((1, 1, lanes), lambda i: (jnp.minimum(i + 1, n - 1), 0, 0), memory_space=pltpu.SMEM)
    return pl.pallas_call(
        _combine_kernel,
        grid=(n,),
        in_specs=[smem, this_tile, next_tile,
                  pl.BlockSpec((tm, TOP_K), lambda i: (i, 0)),
                  pl.BlockSpec((tm, TOP_K), lambda i: (i, 0)),
                  pl.BlockSpec((tm, D_MODEL), lambda i: (i, 0)),
                  _const_spec((1, D_MODEL)),
                  pl.BlockSpec(memory_space=pl.ANY)],
        out_specs=pl.BlockSpec((tm, D_MODEL), lambda i: (i, 0)),
        out_shape=jax.ShapeDtypeStruct((t, D_MODEL), F32),
        scratch_shapes=[pltpu.VMEM((2, _sorted_rows(tm), D_MODEL), BF16), pltpu.SemaphoreType.DMA((2,))],
        compiler_params=pltpu.CompilerParams(dimension_semantics=("arbitrary",), vmem_limit_bytes=VMEM_LIMIT),
        name="combine",
    )(cover, grow, grow, slot_t, gw_t, xres, gfin, ys)


def _block_diag_pairs(w):
    n_h, d, _ = w.shape
    half = n_h // 2
    on_diag = jnp.eye(half, dtype=bool)[None, :, None, :, None]
    blocks = w.reshape(2, half, d, 1, d)
    out = jnp.where(on_diag, blocks, 0.0)
    return out.reshape(2, half * d, half * d).astype(BF16)


def kernel(x_prompt, x_sample, state_pool, state_conv, state_h, cache_mem_k, cache_mem_v, mem_prompt, norm_mix, w_in, pool_w, pool_scale, conv_w, conv_b, gate_a_w, gate_a_b, gate_x_w, gate_x_b, lru_lambda, norm_pool_out, norm_rnn_out, w_out, norm_xattn, norm_mem, xa_wq, xa_wk, xa_wv, xa_wo, norm_ffn, router_w, router_bias, exp_w_gate, exp_w_up, exp_w_down, sh_w_gate, sh_w_up, sh_w_down, norm_final):
    bp, seq, _ = x_prompt.shape
    bs = x_sample.shape[0]
    tp = bp * seq
    row = lambda v: v.reshape(1, -1)
    bf = lambda v: v.astype(BF16)

    mixw = (row(norm_mix[0]), bf(w_in[0]), _block_diag_pairs(pool_w[0]), row(pool_scale[0]), conv_w[0],
            row(conv_b[0]), _block_diag_pairs(gate_a_w[0]), row(gate_a_b[0]), _block_diag_pairs(gate_x_w[0]),
            row(gate_x_b[0]), row(lru_lambda[0]), row(norm_pool_out[0]), row(norm_rnn_out[0]), bf(w_out[0]))
    xaw = (row(norm_xattn[0]), bf(xa_wq[0]), bf(xa_wo[0]))
    moew = (row(norm_ffn[0]), bf(sh_w_gate[0]), bf(sh_w_up[0]), bf(sh_w_down[0]), router_w[0].T,
            router_bias[0].reshape(N_EXPERTS, 1))

    mk, mv, kb, vb = _memkv(mem_prompt, row(norm_mem[0]), bf(xa_wk[0]), bf(xa_wv[0]))
    (xres_p, hb_p, slot_p, gw_p, gexp_p, grel_p, cover_p, cnt_p, pool_p, conv_p, h_p) = _trunk_p(
        x_prompt, kb, vb, mixw, xaw, moew)

    x1_s, q_s, pool_s, conv_s, h_s = _mix_s(x_sample.reshape(bs, D_MODEL), state_pool[0].transpose(1, 0, 2),
                                            state_conv[0].transpose(1, 0, 2), state_h[0], mixw, xaw[0], xaw[1])
    pool_s = pool_s.transpose(1, 0, 2)
    conv_s = conv_s.transpose(1, 0, 2)
    o_s = _attn_s(q_s, cache_mem_k[0], cache_mem_v[0])
    xres_s, hb_s, slot_s, gw_s, gexp_s, grel_s, cover_s, cnt_s = _post_s(x1_s, o_s, xaw[2], moew)

    n_tiles = tp // TS + 1
    n_rows = _round_up((tp + bs) * TOP_K + n_tiles * N_EXPERTS * (RUN_ALIGN - 1) + N_EXPERTS * (BM - 1),
                       BM * GMM_LANES)
    ints = lambda v: v[..., 0].astype(I32)
    base_p, base_s, fill_at, fill_n, blk_exp, n_real = _plan(ints(cnt_p), ints(cnt_s), n_rows)
    cover_p, cover_s = ints(cover_p).reshape(-1), ints(cover_s).reshape(-1)
    xres_p = xres_p.reshape(tp, D_MODEL)
    hb_p = hb_p.reshape(tp, D_MODEL)
    slot_p = slot_p.transpose(1, 0, 2).reshape(TOP_K, tp)
    gw_p = gw_p.transpose(1, 0, 2).reshape(TOP_K, tp)
    gexp_p = gexp_p.reshape(-1, 1, gexp_p.shape[-1])
    grel_p = grel_p.reshape(-1, 1, grel_p.shape[-1])
    grow_p = _group_rows(gexp_p, grel_p, base_p)
    grow_s = _group_rows(gexp_s, grel_s, base_s)

    xs = _dispatch(cover_p, cover_s, (fill_at, fill_n, n_real), grow_p, grow_s, slot_p, hb_p, slot_s, hb_s, n_rows)
    ys = _gmm(blk_exp, n_real, xs, exp_w_gate[0], exp_w_up[0], exp_w_down[0])

    gfin = row(norm_final)
    y_p = _combine(cover_p, grow_p, slot_p.T, gw_p.T, xres_p, gfin, ys)
    y_s = _combine(cover_s, grow_s, slot_s.T, gw_s.T, xres_s, gfin, ys)

    return (y_p.reshape(bp, seq, D_MODEL), y_s.reshape(bs, 1, D_MODEL),
            pool_p[None], conv_p[None], h_p.reshape(1, bp, D_RNN),
            mk[None], mv[None],
            pool_s[None], conv_s[None], h_s[None])
```

```python
import functools

import jax
import jax.numpy as jnp
from jax import lax
from jax.experimental import pallas as pl
from jax.experimental.pallas import tpu as pltpu

F32 = jnp.float32
BF16 = jnp.bfloat16
I32 = jnp.int32
U32 = jnp.uint32

D_MODEL = 1024
D_POOL = 512
D_RNN = 512
D_IN = D_POOL + 2 * D_RNN
POOL_WINDOWS = (2, 4, 8, 16)
POOL_GROUP = 128
POOL_BUF = 15
CONV_WIDTH = 4
LRU_C = 8.0
N_MEM = 256
XA_HEADS = 4
XA_HEAD_DIM = 256
N_EXPERTS = 64
TOP_K = 8
N_EXPERT_GROUPS = 8
GROUP_SIZE = N_EXPERTS // N_EXPERT_GROUPS
TOPK_GROUPS = 4
D_EXPERT = 256
ROUTED_SCALE = 2.5
EPS = 1e-6
PAST_LEN = 16384

HALO = 16
CONV_HALO = 8
TS = 256
TRUNK_SEQS = 1
BM = 512
RUN_ALIGN = 16
RUN_CHUNKS = (64, 32, 16)
SORT_CHUNK = 512
VMEM_LIMIT = 56 * 1024 * 1024


def _round_up(x, m):
    return (x + m - 1) // m * m


def _sorted_rows(tokens):
    return _round_up(tokens * TOP_K + N_EXPERTS * (RUN_ALIGN - 1), SORT_CHUNK)


def _group_lanes(tokens):
    return _round_up(_sorted_rows(tokens) // RUN_ALIGN, 128)


def _const_spec(shape):
    nd = len(shape)
    return pl.BlockSpec(shape, lambda *_: (0,) * nd, pipeline_mode=pl.Buffered(1))


def _rms(x, g):
    ms = jnp.mean(x * x, axis=-1, keepdims=True)
    return x * lax.rsqrt(ms + EPS) * g


def _dot(a, b):
    return jnp.dot(a, b, preferred_element_type=F32)


def _dot_nt(a, b, precision=None):
    return lax.dot_general(a, b, (((1,), (1,)), ((), ())), precision=precision,
                           preferred_element_type=F32)


def _split_bf16(x):
    hi = x.astype(BF16)
    return hi, (x - hi.astype(F32)).astype(BF16)


def _softplus(x):
    return jnp.maximum(x, 0.0) + jnp.log1p(jnp.exp(-jnp.abs(x)))


def _gates_and_decay(c, pos_is_zero, wa_ref, ba_ref, wx_ref, bx_ref, lam_ref):
    cb = c.astype(BF16)
    half = D_RNN // 2
    ga = jnp.concatenate([_dot(cb[:, :half], wa_ref[0]), _dot(cb[:, half:], wa_ref[1])], axis=1) + ba_ref[...]
    gx = jnp.concatenate([_dot(cb[:, :half], wx_ref[0]), _dot(cb[:, half:], wx_ref[1])], axis=1) + bx_ref[...]
    r = jax.nn.sigmoid(ga)
    i = jax.nn.sigmoid(gx)
    log_a = (-LRU_C) * r * _softplus(-lam_ref[...])
    a = jnp.exp(log_a)
    m2 = 1.0 - a * a
    mult = jnp.where(m2 > 0.0, m2 * lax.rsqrt(m2), 0.0)
    if pos_is_zero is not None:
        mult = jnp.where(pos_is_zero, 1.0, mult)
    return a, mult * i * c


def _pool_project(mean, u_pool, pw_ref, pscale_ref):
    d = (mean - u_pool).astype(BF16)
    half = D_POOL // 2
    y = jnp.concatenate([_dot(d[:, :half], pw_ref[0]), _dot(d[:, half:], pw_ref[1])], axis=1)
    return y * pscale_ref[...]


def _merge_out(y_pool, hs, u_gate, gpool_ref, grnn_ref, wout_ref):
    y_rnn = hs * jax.nn.gelu(u_gate)
    merged = jnp.concatenate([_rms(y_pool, gpool_ref[...]), _rms(y_rnn, grnn_ref[...])], axis=1)
    return _dot(merged.astype(BF16), wout_ref[...])


def _route(h3, wr_ref, rbias_ref, before):
    r_tok = h3.shape[0]
    w_hi, w_lo = _split_bf16(wr_ref[...])
    h_hi, h_lo = _split_bf16(h3)
    logits = _dot_nt(w_hi, h_hi) + (_dot_nt(w_hi, h_lo) + _dot_nt(w_lo, h_hi))
    scores = jax.nn.sigmoid(logits)
    biased = scores + rbias_ref[...]
    neg = jnp.float32(-jnp.inf)
    gs = []
    for g in range(N_EXPERT_GROUPS):
        xg = biased[g * GROUP_SIZE:(g + 1) * GROUP_SIZE]
        m1 = jnp.max(xg, axis=0, keepdims=True)
        eq = xg == m1
        cnt = jnp.sum(eq.astype(F32), axis=0, keepdims=True)
        m2 = jnp.max(jnp.where(eq, neg, xg), axis=0, keepdims=True)
        gs.append(m1 + jnp.where(cnt >= 2.0, m1, m2))
    pieces = []
    for g in range(N_EXPERT_GROUPS):
        beaten = jnp.zeros_like(gs[g])
        for o in range(N_EXPERT_GROUPS):
            if o == g:
                continue
            wins = (gs[o] > gs[g]) | (gs[o] == gs[g]) if o < g else (gs[o] > gs[g])
            beaten = beaten + wins.astype(F32)
        keep = beaten < float(TOPK_GROUPS)
        xg = biased[g * GROUP_SIZE:(g + 1) * GROUP_SIZE]
        pieces.append(jnp.where(keep, xg, neg))
    cur = jnp.concatenate(pieces, axis=0)
    eid = lax.broadcasted_iota(I32, (N_EXPERTS, r_tok), 0).astype(F32)
    idx_rows, score_rows = [], []
    sel = jnp.zeros((N_EXPERTS, r_tok), F32)
    for _ in range(TOP_K):
        m = jnp.max(cur, axis=0, keepdims=True)
        idx = jnp.min(jnp.where(cur == m, eid, float(N_EXPERTS)), axis=0, keepdims=True)
        oh = eid == idx
        score_rows.append(jnp.sum(jnp.where(oh, scores, 0.0), axis=0, keepdims=True))
        idx_rows.append(idx)
        sel = sel + oh.astype(F32)
        cur = jnp.where(oh, neg, cur)
    tot = score_rows[0]
    for s in score_rows[1:]:
        tot = tot + s
    w_rows = [s / tot * ROUTED_SCALE for s in score_rows]
    rr = lax.broadcasted_iota(I32, (r_tok, r_tok), 0)
    cc = lax.broadcasted_iota(I32, (r_tok, r_tok), 1)
    earlier = _dot(sel.astype(BF16), (rr < cc).astype(BF16))
    counts = jnp.sum(sel, axis=1, keepdims=True)
    run_len = jnp.floor((counts + (RUN_ALIGN - 1.0)) * (1.0 / RUN_ALIGN)) * RUN_ALIGN
    er = lax.broadcasted_iota(I32, (N_EXPERTS, N_EXPERTS), 0)
    ec = lax.broadcasted_iota(I32, (N_EXPERTS, N_EXPERTS), 1)
    run_start = _dot((ec < er).astype(BF16), jnp.broadcast_to(run_len, (N_EXPERTS, 128)).astype(BF16))[:, :1]
    slot = earlier + run_start
    slot_rows = [jnp.sum(jnp.where(eid == idx, slot, 0.0), axis=0, keepdims=True) for idx in idx_rows]
    n_lanes = _group_lanes(r_tok)
    g_row = lax.broadcasted_iota(I32, (N_EXPERTS, n_lanes), 1).astype(F32) * float(RUN_ALIGN)
    owns = (run_start <= g_row) & (g_row < run_start + run_len)
    e_col = lax.broadcasted_iota(I32, (N_EXPERTS, n_lanes), 0).astype(F32)
    g_exp = jnp.sum(jnp.where(owns, e_col, 0.0), axis=0, keepdims=True)
    g_rel = jnp.sum(jnp.where(owns, before + g_row - run_start, 0.0), axis=0, keepdims=True)
    return slot_rows, w_rows, run_len, g_exp, g_rel


def _moe_prologue(x2, gffn_ref, sg_ref, su_ref, sd_ref):
    h3 = _rms(x2, gffn_ref[...])
    h3b = h3.astype(BF16)
    act = jax.nn.silu(_dot(h3b, sg_ref[...])) * _dot(h3b, su_ref[...])
    shared = _dot(act.astype(BF16), sd_ref[...])
    return h3, x2 + shared


def _store_rows(ref, rows, dtype):
    for k, row in enumerate(rows):
        ref[k:k + 1, :] = row.astype(dtype)


def _memkv_kernel(mem_ref, g_ref, wk_ref, wv_ref, k_ref, v_ref, kb_ref, vb_ref):
    m = _rms(mem_ref[0], g_ref[...]).astype(BF16)
    k = _dot(m, wk_ref[...])
    v = _dot(m, wv_ref[...])
    k_ref[0] = k.reshape(N_MEM, XA_HEADS, XA_HEAD_DIM)
    v_ref[0] = v.reshape(N_MEM, XA_HEADS, XA_HEAD_DIM)
    kb_ref[0] = k.astype(BF16)
    vb_ref[0] = v.astype(BF16)


def _memkv(mem, g, wk, wv):
    b = mem.shape[0]
    blk = pl.BlockSpec((1, N_MEM, D_MODEL), lambda i: (i, 0, 0))
    return pl.pallas_call(
        _memkv_kernel,
        grid=(b,),
        in_specs=[blk, _const_spec((1, D_MODEL)), _const_spec((D_MODEL, D_MODEL)), _const_spec((D_MODEL, D_MODEL))],
        out_specs=[pl.BlockSpec((1, N_MEM, XA_HEADS, XA_HEAD_DIM), lambda i: (i, 0, 0, 0))] * 2 + [blk, blk],
        out_shape=[jax.ShapeDtypeStruct((b, N_MEM, XA_HEADS, XA_HEAD_DIM), F32)] * 2
        + [jax.ShapeDtypeStruct((b, N_MEM, D_MODEL), BF16)] * 2,
        compiler_params=pltpu.CompilerParams(dimension_semantics=("arbitrary",), vmem_limit_bytes=VMEM_LIMIT),
        name="memkv",
    )(mem, g, wk, wv)


def _trunk_p_kernel(x_ref, kb_ref, vb_ref,
                    gmix_ref, win_ref, pw_ref, pscale_ref, cw_ref, cb_ref, wa_ref, ba_ref, wx_ref, bx_ref,
                    lam_ref, gpool_ref, grnn_ref, wout_ref,
                    gxa_ref, wq_ref, wo_ref,
                    gffn_ref, sg_ref, su_ref, sd_ref, wr_ref, rbias_ref,
                    xres_ref, hb_ref, slot_ref, gw_ref, gexp_ref, grel_ref, cover_ref, cnt_ref,
                    pool_ref, conv_ref, hT_ref,
                    pool_prev, conv_prev, h_prev, carry):
    g = pl.program_id(0)
    j = pl.program_id(1)

    @pl.when(j == 0)
    def _():
        pool_prev[...] = jnp.zeros_like(pool_prev)
        conv_prev[...] = jnp.zeros_like(conv_prev)
        h_prev[...] = jnp.zeros_like(h_prev)

    @pl.when((g == 0) & (j == 0))
    def _():
        carry[...] = jnp.zeros_like(carry)

    row = lax.broadcasted_iota(I32, (TS, 1), 0)
    pos = j * TS + row

    for sq in range(TRUNK_SEQS):
        x = x_ref[sq]

        h = _rms(x, gmix_ref[...]).astype(BF16)
        z = _dot(h, win_ref[...])
        u_pool = z[:, :D_POOL]
        u_rnn = z[:, D_POOL:D_POOL + D_RNN]
        u_gate = z[:, D_POOL + D_RNN:]

        ext = jnp.concatenate([pool_prev[sq], u_pool], axis=0)
        means = []
        for grp, w in enumerate(POOL_WINDOWS):
            s = ext[:, grp * POOL_GROUP:(grp + 1) * POOL_GROUP]
            k = 1
            while k < w:
                s = s + pltpu.roll(s, k, 0)
                k *= 2
            inv = 1.0 / jnp.minimum(pos + 1, w).astype(F32)
            means.append(s[HALO:] * inv)
        mean = jnp.concatenate(means, axis=1)
        y_pool = _pool_project(mean, u_pool, pw_ref, pscale_ref)

        extc = jnp.concatenate([conv_prev[sq], u_rnn], axis=0)
        c = u_rnn * cw_ref[CONV_WIDTH - 1:CONV_WIDTH, :]
        for k in range(1, CONV_WIDTH):
            c = c + pltpu.roll(extc, k, 0)[CONV_HALO:] * cw_ref[CONV_WIDTH - 1 - k:CONV_WIDTH - k, :]
        c = c + cb_ref[...]

        a, bt = _gates_and_decay(c, pos == 0, wa_ref, ba_ref, wx_ref, bx_ref, lam_ref)
        k = 1
        while k < TS:
            valid = row >= k
            a_sh = jnp.where(valid, pltpu.roll(a, k, 0), 1.0)
            b_sh = jnp.where(valid, pltpu.roll(bt, k, 0), 0.0)
            bt = bt + a * b_sh
            a = a * a_sh
            k *= 2
        hs = bt + a * h_prev[sq]

        pool_prev[sq] = u_pool[TS - HALO:]
        conv_prev[sq] = u_rnn[TS - CONV_HALO:]
        h_prev[sq] = hs[TS - 1:]
        pool_ref[sq] = u_pool[TS - POOL_BUF:]
        conv_ref[sq] = u_rnn[TS - (CONV_WIDTH - 1):]
        hT_ref[sq] = hs[TS - 1:]

        x1 = x + _merge_out(y_pool, hs, u_gate, gpool_ref, grnn_ref, wout_ref)

        h2 = _rms(x1, gxa_ref[...]).astype(BF16)
        q = (_dot(h2, wq_ref[...]) * (XA_HEAD_DIM ** -0.5)).astype(BF16)
        outs = []
        for hd in range(XA_HEADS):
            sl = slice(hd * XA_HEAD_DIM, (hd + 1) * XA_HEAD_DIM)
            s = _dot_nt(q[:, sl], kb_ref[sq, :, sl])
            s = s - jnp.max(s, axis=-1, keepdims=True)
            p = jnp.exp(s)
            p = p * (1.0 / jnp.sum(p, axis=-1, keepdims=True))
            outs.append(_dot(p.astype(BF16), vb_ref[sq, :, sl]))
        o = jnp.concatenate(outs, axis=1).astype(BF16)
        x2 = x1 + _dot(o, wo_ref[...])

        h3, xres = _moe_prologue(x2, gffn_ref, sg_ref, su_ref, sd_ref)
        xres_ref[sq] = xres
        hb_ref[sq] = h3.astype(BF16)
        slot_rows, w_rows, run_len, g_exp, g_rel = _route(h3, wr_ref, rbias_ref, carry[...])
        _store_rows(slot_ref.at[sq], slot_rows, I32)
        _store_rows(gw_ref.at[sq], w_rows, F32)
        gexp_ref[sq, 0] = g_exp.astype(I32)
        grel_ref[sq, 0] = g_rel.astype(I32)
        cover_ref[sq, 0] = jnp.broadcast_to(jnp.sum(run_len, axis=0, keepdims=True), cover_ref.shape[2:])
        carry[...] = carry[...] + run_len
    cnt_ref[...] = jnp.broadcast_to(carry[...], cnt_ref.shape)


def _trunk_p(x, kb, vb, mixw, xaw, moew):
    bsz, seq, _ = x.shape
    n_j = seq // TS
    q = TRUNK_SEQS
    assert bsz % q == 0
    seq_tile = lambda g, j: (g, j, 0)
    lane_tile = lambda g, j: (g, 0, j)
    per_tile = lambda g, j: (g, j, 0, 0)
    per_seq = lambda g, j: (g, 0, 0)
    weights = list(mixw) + list(xaw) + list(moew)
    in_specs = [pl.BlockSpec((q, TS, D_MODEL), seq_tile),
                pl.BlockSpec((q, N_MEM, D_MODEL), per_seq),
                pl.BlockSpec((q, N_MEM, D_MODEL), per_seq)] + [_const_spec(w.shape) for w in weights]
    lanes = _group_lanes(TS)
    out_shape = [jax.ShapeDtypeStruct((bsz, seq, D_MODEL), F32),
                 jax.ShapeDtypeStruct((bsz, seq, D_MODEL), BF16),
                 jax.ShapeDtypeStruct((bsz, TOP_K, seq), I32),
                 jax.ShapeDtypeStruct((bsz, TOP_K, seq), F32),
                 jax.ShapeDtypeStruct((bsz, n_j, 1, lanes), I32),
                 jax.ShapeDtypeStruct((bsz, n_j, 1, lanes), I32),
                 jax.ShapeDtypeStruct((bsz, n_j, 1, 128), F32),
                 jax.ShapeDtypeStruct((N_EXPERTS, 128), F32),
                 jax.ShapeDtypeStruct((bsz, POOL_BUF, D_POOL), F32),
                 jax.ShapeDtypeStruct((bsz, CONV_WIDTH - 1, D_RNN), F32),
                 jax.ShapeDtypeStruct((bsz, 1, D_RNN), F32)]
    out_specs = [pl.BlockSpec((q, TS, D_MODEL), seq_tile),
                 pl.BlockSpec((q, TS, D_MODEL), seq_tile),
                 pl.BlockSpec((q, TOP_K, TS), lane_tile),
                 pl.BlockSpec((q, TOP_K, TS), lane_tile),
                 pl.BlockSpec((q, 1, 1, lanes), per_tile),
                 pl.BlockSpec((q, 1, 1, lanes), per_tile),
                 pl.BlockSpec((q, 1, 1, 128), per_tile),
                 pl.BlockSpec((N_EXPERTS, 128), lambda g, j: (0, 0)),
                 pl.BlockSpec((q, POOL_BUF, D_POOL), per_seq),
                 pl.BlockSpec((q, CONV_WIDTH - 1, D_RNN), per_seq),
                 pl.BlockSpec((q, 1, D_RNN), per_seq)]
    return pl.pallas_call(
        _trunk_p_kernel,
        grid=(bsz // q, n_j),
        in_specs=in_specs,
        out_specs=out_specs,
        out_shape=out_shape,
        scratch_shapes=[pltpu.VMEM((q, HALO, D_POOL), F32), pltpu.VMEM((q, CONV_HALO, D_RNN), F32),
                        pltpu.VMEM((q, 1, D_RNN), F32), pltpu.VMEM((N_EXPERTS, 1), F32)],
        compiler_params=pltpu.CompilerParams(dimension_semantics=("arbitrary", "arbitrary"),
                                             vmem_limit_bytes=VMEM_LIMIT),
        name="trunk_p",
    )(x, kb, vb, *weights)


def _mix_s_kernel(x_ref, pool_ref, conv_ref, h0_ref,
                  gmix_ref, win_ref, pw_ref, pscale_ref, cw_ref, cb_ref, wa_ref, ba_ref, wx_ref, bx_ref,
                  lam_ref, gpool_ref, grnn_ref, wout_ref, gxa_ref, wq_ref,
                  x1_ref, q_ref, npool_ref, nconv_ref, nh_ref):
    x = x_ref[...]
    h = _rms(x, gmix_ref[...]).astype(BF16)
    z = _dot(h, win_ref[...])
    u_pool = z[:, :D_POOL]
    u_rnn = z[:, D_POOL:D_POOL + D_RNN]
    u_gate = z[:, D_POOL + D_RNN:]

    means = []
    for g, w in enumerate(POOL_WINDOWS):
        sl = slice(g * POOL_GROUP, (g + 1) * POOL_GROUP)
        s = u_pool[:, sl]
        for k in range(1, w):
            s = s + pool_ref[POOL_BUF - k, :, sl]
        means.append(s * (1.0 / min(w, PAST_LEN + 1)))
    mean = jnp.concatenate(means, axis=1)
    y_pool = _pool_project(mean, u_pool, pw_ref, pscale_ref)

    c = u_rnn * cw_ref[CONV_WIDTH - 1:CONV_WIDTH, :]
    for k in range(1, CONV_WIDTH):
        c = c + conv_ref[CONV_WIDTH - 1 - k] * cw_ref[CONV_WIDTH - 1 - k:CONV_WIDTH - k, :]
    c = c + cb_ref[...]
    a, bt = _gates_and_decay(c, None, wa_ref, ba_ref, wx_ref, bx_ref, lam_ref)
    hs = a * h0_ref[...] + bt

    x1 = x + _merge_out(y_pool, hs, u_gate, gpool_ref, grnn_ref, wout_ref)
    x1_ref[...] = x1
    h2 = _rms(x1, gxa_ref[...]).astype(BF16)
    q_ref[...] = _dot(h2, wq_ref[...]) * (XA_HEAD_DIM ** -0.5)

    npool_ref[:POOL_BUF - 1] = pool_ref[1:]
    npool_ref[POOL_BUF - 1] = u_pool
    nconv_ref[:CONV_WIDTH - 2] = conv_ref[1:]
    nconv_ref[CONV_WIDTH - 2] = u_rnn
    nh_ref[...] = hs


def _mix_s(x, pool, conv, h0, mixw, gxa, wq):
    bsz = x.shape[0]
    args = [x, pool, conv, h0] + list(mixw) + [gxa, wq]
    return pl.pallas_call(
        _mix_s_kernel,
        grid=(1,),
        in_specs=[_const_spec(a.shape) for a in args],
        out_specs=[_const_spec((bsz, D_MODEL)), _const_spec((bsz, D_MODEL)), _const_spec(pool.shape),
                   _const_spec(conv.shape), _const_spec((bsz, D_RNN))],
        out_shape=[jax.ShapeDtypeStruct((bsz, D_MODEL), F32), jax.ShapeDtypeStruct((bsz, D_MODEL), F32),
                   jax.ShapeDtypeStruct(pool.shape, F32), jax.ShapeDtypeStruct(conv.shape, F32),
                   jax.ShapeDtypeStruct((bsz, D_RNN), F32)],
        compiler_params=pltpu.CompilerParams(dimension_semantics=("arbitrary",), vmem_limit_bytes=VMEM_LIMIT),
        name="mix_s",
    )(*args)


ATTN_S_BB = 4


def _attn_s_kernel(q_ref, k_ref, v_ref, o_ref):
    q = q_ref[...][:, None]
    s = jnp.sum(k_ref[...] * q, axis=-1, keepdims=True)
    s = s - jnp.max(s, axis=1, keepdims=True)
    p = jnp.exp(s)
    p = p / jnp.sum(p, axis=1, keepdims=True)
    o_ref[...] = jnp.sum(p * v_ref[...], axis=1)


def _attn_s(q, k, v):
    bsz = q.shape[0]
    kv_spec = pl.BlockSpec((ATTN_S_BB, N_MEM, XA_HEADS, XA_HEAD_DIM), lambda i: (i, 0, 0, 0))
    q_spec = pl.BlockSpec((ATTN_S_BB, XA_HEADS, XA_HEAD_DIM), lambda i: (i, 0, 0))
    o = pl.pallas_call(
        _attn_s_kernel,
        grid=(bsz // ATTN_S_BB,),
        in_specs=[q_spec, kv_spec, kv_spec],
        out_specs=q_spec,
        out_shape=jax.ShapeDtypeStruct((bsz, XA_HEADS, XA_HEAD_DIM), F32),
        compiler_params=pltpu.CompilerParams(dimension_semantics=("arbitrary",), vmem_limit_bytes=VMEM_LIMIT),
        name="attn_s",
    )(q.reshape(bsz, XA_HEADS, XA_HEAD_DIM), k, v)
    return o.reshape(bsz, D_MODEL)


def _post_s_kernel(x1_ref, o_ref, wo_ref, gffn_ref, sg_ref, su_ref, sd_ref, wr_ref, rbias_ref,
                   xres_ref, hb_ref, slot_ref, gw_ref, gexp_ref, grel_ref, cover_ref, cnt_ref):
    x2 = x1_ref[...] + _dot(o_ref[...].astype(BF16), wo_ref[...])
    h3, xres = _moe_prologue(x2, gffn_ref, sg_ref, su_ref, sd_ref)
    xres_ref[...] = xres
    hb_ref[...] = h3.astype(BF16)
    slot_rows, w_rows, run_len, g_exp, g_rel = _route(h3, wr_ref, rbias_ref, jnp.zeros((N_EXPERTS, 1), F32))
    _store_rows(slot_ref, slot_rows, I32)
    _store_rows(gw_ref, w_rows, F32)
    gexp_ref[0] = g_exp.astype(I32)
    grel_ref[0] = g_rel.astype(I32)
    cover_ref[0] = jnp.broadcast_to(jnp.sum(run_len, axis=0, keepdims=True), cover_ref.shape[1:])
    cnt_ref[...] = jnp.broadcast_to(run_len, cnt_ref.shape)


def _post_s(x1, o, wo, moew):
    bsz = x1.shape[0]
    args = [x1, o, wo] + list(moew)
    out_shape = [jax.ShapeDtypeStruct((bsz, D_MODEL), F32),
                 jax.ShapeDtypeStruct((bsz, D_MODEL), BF16),
                 jax.ShapeDtypeStruct((TOP_K, bsz), I32),
                 jax.ShapeDtypeStruct((TOP_K, bsz), F32),
                 jax.ShapeDtypeStruct((1, 1, _group_lanes(bsz)), I32),
                 jax.ShapeDtypeStruct((1, 1, _group_lanes(bsz)), I32),
                 jax.ShapeDtypeStruct((1, 1, 128), F32),
                 jax.ShapeDtypeStruct((N_EXPERTS, 128), F32)]
    return pl.pallas_call(
        _post_s_kernel,
        grid=(1,),
        in_specs=[_const_spec(a.shape) for a in args],
        out_specs=[_const_spec(s.shape) for s in out_shape],
        out_shape=out_shape,
        compiler_params=pltpu.CompilerParams(dimension_semantics=("arbitrary",), vmem_limit_bytes=VMEM_LIMIT),
        name="post_s",
    )(*args)


def _plan_kernel(rp_ref, rs_ref, base_p_ref, base_s_ref, fill_at_ref, fill_n_ref, exp_ref, nreal_ref):
    n_blocks = exp_ref.shape[0]
    shift = BM.bit_length() - 1

    def per_expert(e, carry):
        blk0, last_e = carry
        rows = rp_ref[e] + rs_ref[e]
        start = lax.shift_left(blk0, shift)
        n_blk = lax.shift_right_logical(rows + (BM - 1), shift)
        base_p_ref[e] = start
        base_s_ref[e] = start + rp_ref[e]
        fill_at_ref[e] = start + rows
        fill_n_ref[e] = _groups(lax.shift_left(n_blk, shift) - rows)

        def per_block(j, _):
            exp_ref[blk0 + j] = e
            return _

        lax.fori_loop(0, n_blk, per_block, 0)
        return blk0 + n_blk, jnp.where(rows > 0, e, last_e)

    n_real, last_e = lax.fori_loop(0, N_EXPERTS, per_expert, (jnp.int32(0), jnp.int32(0)))
    nreal_ref[0] = n_real

    def rest(b, _):
        exp_ref[b] = last_e
        return _

    lax.fori_loop(n_real, n_blocks, rest, 0)


def _plan(rows_p, rows_s, n_rows):
    assert BM & (BM - 1) == 0 and n_rows % BM == 0
    smem = pl.BlockSpec(memory_space=pltpu.SMEM)
    return pl.pallas_call(
        _plan_kernel,
        in_specs=[smem, smem],
        out_specs=[smem] * 6,
        out_shape=[jax.ShapeDtypeStruct((N_EXPERTS,), I32)] * 4
        + [jax.ShapeDtypeStruct((n_rows // BM,), I32), jax.ShapeDtypeStruct((1,), I32)],
        name="plan",
    )(rows_p, rows_s)


def _groups(n_rows):
    return lax.shift_right_logical(n_rows, RUN_ALIGN.bit_length() - 1)


GROUP_UNROLL = 8


def _for_each_group(n_groups, grow_ref, fn):
    def one(g, priority):
        fn(pl.multiple_of(g * RUN_ALIGN, RUN_ALIGN), pl.multiple_of(grow_ref[0, 0, g], RUN_ALIGN), priority)

    def several(j, _):
        for u in range(GROUP_UNROLL):
            one(j * GROUP_UNROLL + u, u % 2)
        return _

    def single(g, _):
        one(g, 0)
        return _

    n_full = lax.shift_right_logical(n_groups, GROUP_UNROLL.bit_length() - 1)
    lax.fori_loop(0, n_full, several, 0)
    lax.fori_loop(n_full * GROUP_UNROLL, n_groups, single, 0)


def _wait_groups(n_groups, group_wait, bulk_wait):
    def bulk(j, _):
        bulk_wait()
        return _

    def single(g, _):
        group_wait()
        return _

    n_full = lax.shift_right_logical(n_groups, GROUP_UNROLL.bit_length() - 1)
    lax.fori_loop(0, n_full, bulk, 0)
    lax.fori_loop(n_full * GROUP_UNROLL, n_groups, single, 0)


def _group_rows_kernel(gexp_ref, grel_ref, base_ref, out_ref):
    rows = grel_ref[...]
    e = gexp_ref[...]
    for ex in range(N_EXPERTS):
        rows = rows + jnp.where(e == ex, base_ref[ex], 0)
    out_ref[...] = rows


def _group_rows(gexp, grel, base):
    shape = gexp.shape
    flat = (shape[0], shape[-1])
    out = pl.pallas_call(
        _group_rows_kernel,
        in_specs=[pl.BlockSpec(memory_space=pltpu.VMEM), pl.BlockSpec(memory_space=pltpu.VMEM),
                  pl.BlockSpec(memory_space=pltpu.SMEM)],
        out_specs=pl.BlockSpec(memory_space=pltpu.VMEM),
        out_shape=jax.ShapeDtypeStruct(flat, I32),
        name="group_rows",
    )(gexp.reshape(flat), grel.reshape(flat), base)
    return out.reshape(shape)


def _dispatch_kernel(cover_p, cover_s, fill_at, fill_n, nreal_ref, grow_p, grow_s,
                     slot_p_ref, h_p_ref, slot_s_ref, h_s_ref, xs_ref, sbuf, zbuf, sem):
    i = pl.program_id(0)
    last = pl.num_programs(0) - 1
    cur = lax.rem(i, 2)

    def drain(s, n_rows):
        def wait_rows(n):
            pltpu.make_async_copy(sbuf.at[s, pl.ds(0, n)], xs_ref.at[pl.ds(0, n)], sem.at[s]).wait()
        _wait_groups(_groups(n_rows), lambda: wait_rows(RUN_ALIGN), lambda: wait_rows(RUN_ALIGN * GROUP_UNROLL))

    def tile(n_rows, grow_ref, slot_ref, h_ref):
        def sort_chunk(c):
            rid = c * SORT_CHUNK + lax.broadcasted_iota(I32, (SORT_CHUNK, 1), 0)
            hit = rid == slot_ref[0:1, :]
            for k in range(1, TOP_K):
                hit = hit | (rid == slot_ref[k:k + 1, :])
            sbuf[cur, c * SORT_CHUNK:(c + 1) * SORT_CHUNK] = _dot(hit.astype(BF16), h_ref[...]).astype(BF16)

        def send(local, glob, priority):
            pltpu.make_async_copy(sbuf.at[cur, pl.ds(local, RUN_ALIGN)], xs_ref.at[pl.ds(glob, RUN_ALIGN)],
                                  sem.at[cur]).start(priority=priority)

        always = h_ref.shape[0] * TOP_K // SORT_CHUNK
        for c in range(_sorted_rows(h_ref.shape[0]) // SORT_CHUNK):
            if c < always:
                sort_chunk(c)
            else:
                pl.when(c * SORT_CHUNK < n_rows)(functools.partial(sort_chunk, c))
        _for_each_group(_groups(n_rows), grow_ref, send)

    @pl.when(i >= 2)
    def _():
        drain(cur, cover_p[i - 2])

    @pl.when(i < last)
    def _():
        tile(cover_p[i], grow_p, slot_p_ref, h_p_ref)

    @pl.when(i == last)
    def _():
        tile(cover_s[0], grow_s, slot_s_ref, h_s_ref)
        drain(1 - cur, cover_p[last - 1])
        drain(cur, cover_s[0])
        zbuf[...] = jnp.zeros_like(zbuf)

        def group_fill(e, g):
            return pltpu.make_async_copy(
                zbuf.at[pl.ds(0, RUN_ALIGN)],
                xs_ref.at[pl.ds(pl.multiple_of(fill_at[e] + g * RUN_ALIGN, RUN_ALIGN), RUN_ALIGN)], sem.at[2])

        def per_expert(e, n):
            def start(g, _):
                group_fill(e, g).start()
                return _
            lax.fori_loop(0, fill_n[e], start, 0)
            return n + fill_n[e]

        n_fill = lax.fori_loop(0, N_EXPERTS, per_expert, jnp.int32(0))

        def wait_group(g, _):
            group_fill(0, 0).wait()
            return _

        lax.fori_loop(0, n_fill, wait_group, 0)

        n_real = nreal_ref[0]
        n_tail = xs_ref.shape[0] // BM - n_real

        def blk_fill(b):
            return pltpu.make_async_copy(zbuf, xs_ref.at[pl.ds(pl.multiple_of((n_real + b) * BM, BM), BM)], sem.at[2])

        def start_blk(b, _):
            blk_fill(b).start()
            return _

        def wait_blk(b, _):
            blk_fill(b).wait()
            return _

        lax.fori_loop(0, n_tail, start_blk, 0)
        lax.fori_loop(0, n_tail, wait_blk, 0)


def _dispatch(cover_p, cover_s, fills, grow_p, grow_s, slot_p, hb_p, slot_s, hb_s, n_rows):
    n_p = hb_p.shape[0] // TS
    smem = pl.BlockSpec(memory_space=pltpu.SMEM)
    clamp = lambda i: jnp.minimum(i, n_p - 1)
    per_tile = pl.BlockSpec((1, 1, grow_p.shape[-1]), lambda i: (clamp(i), 0, 0), memory_space=pltpu.SMEM)
    return pl.pallas_call(
        _dispatch_kernel,
        grid=(n_p + 1,),
        in_specs=[smem] * 5 + [per_tile, smem,
                               pl.BlockSpec((TOP_K, TS), lambda i: (0, clamp(i))),
                               pl.BlockSpec((TS, D_MODEL), lambda i: (clamp(i), 0)),
                               _const_spec(slot_s.shape), _const_spec(hb_s.shape)],
        out_specs=pl.BlockSpec(memory_space=pl.ANY),
        out_shape=jax.ShapeDtypeStruct((n_rows, D_MODEL), BF16),
        scratch_shapes=[pltpu.VMEM((2, _sorted_rows(TS), D_MODEL), BF16),
                        pltpu.VMEM((BM, D_MODEL), BF16), pltpu.SemaphoreType.DMA((3,))],
        compiler_params=pltpu.CompilerParams(dimension_semantics=("arbitrary",), has_side_effects=True,
                                             vmem_limit_bytes=VMEM_LIMIT),
        name="dispatch",
    )(cover_p, cover_s, *fills, grow_p, grow_s, slot_p, hb_p, slot_s, hb_s)


GMM_LANES = 2


def _gmm_kernel(exp_ref, nreal_ref, *refs):
    x_ref = refs[0]
    w_refs = refs[1:1 + 3 * GMM_LANES]
    y_ref = refs[1 + 3 * GMM_LANES]
    scratch = refs[2 + 3 * GMM_LANES:]
    s = pl.program_id(0)
    n_real = nreal_ref[0]

    for lane in range(GMM_LANES):
        wg_ref, wu_ref, wd_ref = w_refs[3 * lane:3 * lane + 3]
        wgu, wdn = scratch[2 * lane:2 * lane + 2]
        b = s * GMM_LANES + lane
        new_expert = (s == 0) | (exp_ref[b] != exp_ref[jnp.maximum(b - GMM_LANES, 0)])

        @pl.when((b < n_real) & new_expert)
        def _():
            wgu[:, :D_EXPERT] = wg_ref[0].astype(BF16)
            wgu[:, D_EXPERT:] = wu_ref[0].astype(BF16)
            wdn[...] = wd_ref[0].astype(BF16)

    @pl.when(s * GMM_LANES < n_real)
    def _():
        for lane in range(GMM_LANES):
            wgu, wdn = scratch[2 * lane:2 * lane + 2]
            rows = slice(lane * BM, (lane + 1) * BM)
            gu = _dot(x_ref[rows, :], wgu[...])
            act = jax.nn.silu(gu[:, :D_EXPERT]) * gu[:, D_EXPERT:]
            y = _dot(act.astype(BF16), wdn[...]).astype(BF16)
            y_ref[rows, :] = jnp.where(s * GMM_LANES + lane < n_real, y, jnp.zeros_like(y))

    @pl.when(s * GMM_LANES >= n_real)
    def _():
        y_ref[...] = jnp.zeros_like(y_ref)


def _gmm(blk_exp, n_real, xs, wg, wu, wd):
    n_rows = xs.shape[0]
    step_rows = BM * GMM_LANES
    assert n_rows % step_rows == 0
    last_step = lambda nreal: (nreal[0] - 1) // GMM_LANES
    w_specs, scratch = [], []
    for lane in range(GMM_LANES):
        weight = lambda s, exp, nreal, lane=lane: (exp[s * GMM_LANES + lane], 0, 0)
        w_specs += [pl.BlockSpec((1, D_MODEL, D_EXPERT), weight), pl.BlockSpec((1, D_MODEL, D_EXPERT), weight),
                    pl.BlockSpec((1, D_EXPERT, D_MODEL), weight)]
        scratch += [pltpu.VMEM((D_MODEL, 2 * D_EXPERT), BF16), pltpu.VMEM((D_EXPERT, D_MODEL), BF16)]
    grid_spec = pltpu.PrefetchScalarGridSpec(
        num_scalar_prefetch=2,
        grid=(n_rows // step_rows,),
        in_specs=[pl.BlockSpec((step_rows, D_MODEL), lambda s, exp, nreal: (jnp.minimum(s, last_step(nreal)), 0))]
        + w_specs,
        out_specs=pl.BlockSpec((step_rows, D_MODEL), lambda s, exp, nreal: (s, 0)),
        scratch_shapes=scratch,
    )
    return pl.pallas_call(
        _gmm_kernel,
        grid_spec=grid_spec,
        out_shape=jax.ShapeDtypeStruct((n_rows, D_MODEL), BF16),
        compiler_params=pltpu.CompilerParams(dimension_semantics=("arbitrary",), vmem_limit_bytes=VMEM_LIMIT),
        name="gmm",
    )(blk_exp, n_real, xs, *([wg, wu, wd] * GMM_LANES))


def _combine_kernel(cover_ref, grow_ref, grow_next_ref,
                    slot_ref, gw_ref, xres_ref, gfin_ref, ys_ref, out_ref, ybuf, sem):
    i = pl.program_id(0)
    n = pl.num_programs(0)
    cur = lax.rem(i, 2)
    rows = ybuf.shape[1]

    def rows_copy(s, local, glob, n_rows):
        return pltpu.make_async_copy(ys_ref.at[pl.ds(glob, n_rows)], ybuf.at[s, pl.ds(local, n_rows)], sem.at[s])

    def gather(n_rows, gr_ref, s):
        _for_each_group(_groups(n_rows), gr_ref,
                        lambda l, g, priority: rows_copy(s, l, g, RUN_ALIGN).start(priority=priority))

    @pl.when(i == 0)
    def _():
        ybuf[...] = jnp.zeros_like(ybuf)
        gather(cover_ref[0], grow_ref, 0)

    @pl.when(i + 1 < n)
    def _():
        gather(cover_ref[jnp.minimum(i + 1, n - 1)], grow_next_ref, 1 - cur)

    _wait_groups(_groups(cover_ref[i]), lambda: rows_copy(cur, 0, 0, RUN_ALIGN).wait(),
                 lambda: rows_copy(cur, 0, 0, RUN_ALIGN * GROUP_UNROLL).wait())

    tm = slot_ref.shape[0]
    slot_b = [jnp.broadcast_to(slot_ref[:, k:k + 1], (tm, 128)) for k in range(TOP_K)]
    gw_b = [jnp.broadcast_to(gw_ref[:, k:k + 1], (tm, 128)) for k in range(TOP_K)]
    lane = lax.broadcasted_iota(I32, (1, 128), 1)

    def chunk_sum(c):
        pieces = []
        for p in range(SORT_CHUNK // 128):
            col = lane + (c * SORT_CHUNK + p * 128)
            w = jnp.zeros((tm, 128), F32)
            for k in range(TOP_K):
                w = jnp.where(slot_b[k] == col, gw_b[k], w)
            pieces.append(w.astype(BF16))
        return _dot(jnp.concatenate(pieces, axis=1), ybuf[cur, c * SORT_CHUNK:(c + 1) * SORT_CHUNK])

    always = tm * TOP_K // SORT_CHUNK
    acc = xres_ref[...]
    for c in range(always):
        acc = acc + chunk_sum(c)
    out_ref[...] = acc
    for c in range(always, rows // SORT_CHUNK):
        @pl.when(c * SORT_CHUNK < cover_ref[i])
        def _():
            out_ref[...] += chunk_sum(c)
    out_ref[...] = _rms(out_ref[...], gfin_ref[...])


def _combine(cover, grow, slot_t, gw_t, xres, gfin, ys):
    t = xres.shape[0]
    tm = min(t, TS)
    n = t // tm
    smem = pl.BlockSpec(memory_space=pltpu.SMEM)
    lanes = grow.shape[-1]
    this_tile = pl.BlockSpec((1, 1, lanes), lambda i: (i, 0, 0), memory_space=pltpu.SMEM)
    next_tile = pl.BlockSpec((1, 1, lanes), lambda i: (jnp.minimum(i + 1, n - 1), 0, 0), memory_space=pltpu.SMEM)
    return pl.pallas_call(
        _combine_kernel,
        grid=(n,),
        in_specs=[smem, this_tile, next_tile,
                  pl.BlockSpec((tm, TOP_K), lambda i: (i, 0)),
                  pl.BlockSpec((tm, TOP_K), lambda i: (i, 0)),
                  pl.BlockSpec((tm, D_MODEL), lambda i: (i, 0)),
                  _const_spec((1, D_MODEL)),
                  pl.BlockSpec(memory_space=pl.ANY)],
        out_specs=pl.BlockSpec((tm, D_MODEL), lambda i: (i, 0)),
        out_shape=jax.ShapeDtypeStruct((t, D_MODEL), F32),
        scratch_shapes=[pltpu.VMEM((2, _sorted_rows(tm), D_MODEL), BF16), pltpu.SemaphoreType.DMA((2,))],
        compiler_params=pltpu.CompilerParams(dimension_semantics=("arbitrary",), vmem_limit_bytes=VMEM_LIMIT),
        name="combine",
    )(cover, grow, grow, slot_t, gw_t, xres, gfin, ys)


def _block_diag_pairs(w):
    n_h, d, _ = w.shape
    half = n_h // 2
    on_diag = jnp.eye(half, dtype=bool)[None, :, None, :, None]
    blocks = w.reshape(2, half, d, 1, d)
    out = jnp.where(on_diag, blocks, 0.0)
    return out.reshape(2, half * d, half * d).astype(BF16)


def kernel(x_prompt, x_sample, state_pool, state_conv, state_h, cache_mem_k, cache_mem_v, mem_prompt, norm_mix, w_in, pool_w, pool_scale, conv_w, conv_b, gate_a_w, gate_a_b, gate_x_w, gate_x_b, lru_lambda, norm_pool_out, norm_rnn_out, w_out, norm_xattn, norm_mem, xa_wq, xa_wk, xa_wv, xa_wo, norm_ffn, router_w, router_bias, exp_w_gate, exp_w_up, exp_w_down, sh_w_gate, sh_w_up, sh_w_down, norm_final):
    bp, seq, _ = x_prompt.shape
    bs = x_sample.shape[0]
    tp = bp * seq
    row = lambda v: v.reshape(1, -1)
    bf = lambda v: v.astype(BF16)

    mixw = (row(norm_mix[0]), bf(w_in[0]), _block_diag_pairs(pool_w[0]), row(pool_scale[0]), conv_w[0],
            row(conv_b[0]), _block_diag_pairs(gate_a_w[0]), row(gate_a_b[0]), _block_diag_pairs(gate_x_w[0]),
            row(gate_x_b[0]), row(lru_lambda[0]), row(norm_pool_out[0]), row(norm_rnn_out[0]), bf(w_out[0]))
    xaw = (row(norm_xattn[0]), bf(xa_wq[0]), bf(xa_wo[0]))
    moew = (row(norm_ffn[0]), bf(sh_w_gate[0]), bf(sh_w_up[0]), bf(sh_w_down[0]), router_w[0].T,
            router_bias[0].reshape(N_EXPERTS, 1))

    mk, mv, kb, vb = _memkv(mem_prompt, row(norm_mem[0]), bf(xa_wk[0]), bf(xa_wv[0]))
    (xres_p, hb_p, slot_p, gw_p, gexp_p, grel_p, cover_p, cnt_p, pool_p, conv_p, h_p) = _trunk_p(
        x_prompt, kb, vb, mixw, xaw, moew)

    x1_s, q_s, pool_s, conv_s, h_s = _mix_s(x_sample.reshape(bs, D_MODEL), state_pool[0].transpose(1, 0, 2),
                                            state_conv[0].transpose(1, 0, 2), state_h[0], mixw, xaw[0], xaw[1])
    pool_s = pool_s.transpose(1, 0, 2)
    conv_s = conv_s.transpose(1, 0, 2)
    o_s = _attn_s(q_s, cache_mem_k[0], cache_mem_v[0])
    xres_s, hb_s, slot_s, gw_s, gexp_s, grel_s, cover_s, cnt_s = _post_s(x1_s, o_s, xaw[2], moew)

    n_tiles = tp // TS + 1
    n_rows = _round_up((tp + bs) * TOP_K + n_tiles * N_EXPERTS * (RUN_ALIGN - 1) + N_EXPERTS * (BM - 1),
                       BM * GMM_LANES)
    ints = lambda v: v[..., 0].astype(I32)
    base_p, base_s, fill_at, fill_n, blk_exp, n_real = _plan(ints(cnt_p), ints(cnt_s), n_rows)
    cover_p, cover_s = ints(cover_p).reshape(-1), ints(cover_s).reshape(-1)
    xres_p = xres_p.reshape(tp, D_MODEL)
    hb_p = hb_p.reshape(tp, D_MODEL)
    slot_p = slot_p.transpose(1, 0, 2).reshape(TOP_K, tp)
    gw_p = gw_p.transpose(1, 0, 2).reshape(TOP_K, tp)
    gexp_p = gexp_p.reshape(-1, 1, gexp_p.shape[-1])
    grel_p = grel_p.reshape(-1, 1, grel_p.shape[-1])
    grow_p = _group_rows(gexp_p, grel_p, base_p)
    grow_s = _group_rows(gexp_s, grel_s, base_s)

    xs = _dispatch(cover_p, cover_s, (fill_at, fill_n, n_real), grow_p, grow_s, slot_p, hb_p, slot_s, hb_s, n_rows)
    ys = _gmm(blk_exp, n_real, xs, exp_w_gate[0], exp_w_up[0], exp_w_down[0])

    gfin = row(norm_final)
    y_p = _combine(cover_p, grow_p, slot_p.T, gw_p.T, xres_p, gfin, ys)
    y_s = _combine(cover_s, grow_s, slot_s.T, gw_s.T, xres_s, gfin, ys)

    return (y_p.reshape(bp, seq, D_MODEL), y_s.reshape(bs, 1, D_MODEL),
            pool_p[None], conv_p[None], h_p.reshape(1, bp, D_RNN),
            mk[None], mv[None],
            pool_s[None], conv_s[None], h_s[None])
```

```python
import functools

import jax
import jax.numpy as jnp
from jax import lax
from jax.experimental import pallas as pl
from jax.experimental.pallas import tpu as pltpu

F32 = jnp.float32
BF16 = jnp.bfloat16
I32 = jnp.int32
U32 = jnp.uint32

D_MODEL = 1024
D_POOL = 512
D_RNN = 512
D_IN = D_POOL + 2 * D_RNN
POOL_WINDOWS = (2, 4, 8, 16)
POOL_GROUP = 128
POOL_BUF = 15
CONV_WIDTH = 4
LRU_C = 8.0
N_MEM = 256
XA_HEADS = 4
XA_HEAD_DIM = 256
N_EXPERTS = 64
TOP_K = 8
N_EXPERT_GROUPS = 8
GROUP_SIZE = N_EXPERTS // N_EXPERT_GROUPS
TOPK_GROUPS = 4
D_EXPERT = 256
ROUTED_SCALE = 2.5
EPS = 1e-6
PAST_LEN = 16384

HALO = 16
CONV_HALO = 8
TS = 256
TRUNK_SEQS = 1
BM = 512
RUN_ALIGN = 16
RUN_CHUNKS = (64, 32, 16)
SORT_CHUNK = 512
VMEM_LIMIT = 56 * 1024 * 1024


def _round_up(x, m):
    return (x + m - 1) // m * m


def _sorted_rows(tokens):
    return _round_up(tokens * TOP_K + N_EXPERTS * (RUN_ALIGN - 1), SORT_CHUNK)


assert _sorted_rows(TS) < 2 ** 15


def _group_lanes(tokens):
    return _round_up(_sorted_rows(tokens) // RUN_ALIGN, 128)


def _const_spec(shape):
    nd = len(shape)
    return pl.BlockSpec(shape, lambda *_: (0,) * nd, pipeline_mode=pl.Buffered(1))


def _rms(x, g):
    ms = jnp.mean(x * x, axis=-1, keepdims=True)
    return x * lax.rsqrt(ms + EPS) * g


def _dot(a, b):
    return jnp.dot(a, b, preferred_element_type=F32)


def _dot_nt(a, b, precision=None):
    return lax.dot_general(a, b, (((1,), (1,)), ((), ())), precision=precision,
                           preferred_element_type=F32)


def _split_bf16(x):
    hi = x.astype(BF16)
    return hi, (x - hi.astype(F32)).astype(BF16)


def _softplus(x):
    return jnp.maximum(x, 0.0) + jnp.log1p(jnp.exp(-jnp.abs(x)))


def _gates_and_decay(c, pos_is_zero, wa_ref, ba_ref, wx_ref, bx_ref, lam_ref):
    cb = c.astype(BF16)
    half = D_RNN // 2
    ga = jnp.concatenate([_dot(cb[:, :half], wa_ref[0]), _dot(cb[:, half:], wa_ref[1])], axis=1) + ba_ref[...]
    gx = jnp.concatenate([_dot(cb[:, :half], wx_ref[0]), _dot(cb[:, half:], wx_ref[1])], axis=1) + bx_ref[...]
    r = jax.nn.sigmoid(ga)
    i = jax.nn.sigmoid(gx)
    log_a = (-LRU_C) * r * _softplus(-lam_ref[...])
    a = jnp.exp(log_a)
    m2 = 1.0 - a * a
    mult = jnp.where(m2 > 0.0, m2 * lax.rsqrt(m2), 0.0)
    if pos_is_zero is not None:
        mult = jnp.where(pos_is_zero, 1.0, mult)
    return a, mult * i * c


def _pool_project(mean, u_pool, pw_ref, pscale_ref):
    d = (mean - u_pool).astype(BF16)
    half = D_POOL // 2
    y = jnp.concatenate([_dot(d[:, :half], pw_ref[0]), _dot(d[:, half:], pw_ref[1])], axis=1)
    return y * pscale_ref[...]


def _merge_out(y_pool, hs, u_gate, gpool_ref, grnn_ref, wout_ref):
    y_rnn = hs * jax.nn.gelu(u_gate)
    merged = jnp.concatenate([_rms(y_pool, gpool_ref[...]), _rms(y_rnn, grnn_ref[...])], axis=1)
    return _dot(merged.astype(BF16), wout_ref[...])


def _route(h3, wr_ref, rbias_ref, before):
    r_tok = h3.shape[0]
    w_hi, w_lo = _split_bf16(wr_ref[...])
    h_hi, h_lo = _split_bf16(h3)
    logits = _dot_nt(w_hi, h_hi) + (_dot_nt(w_hi, h_lo) + _dot_nt(w_lo, h_hi))
    scores = jax.nn.sigmoid(logits)
    biased = scores + rbias_ref[...]
    neg = jnp.float32(-jnp.inf)
    gs = []
    for g in range(N_EXPERT_GROUPS):
        xg = biased[g * GROUP_SIZE:(g + 1) * GROUP_SIZE]
        m1 = jnp.max(xg, axis=0, keepdims=True)
        eq = xg == m1
        cnt = jnp.sum(eq.astype(F32), axis=0, keepdims=True)
        m2 = jnp.max(jnp.where(eq, neg, xg), axis=0, keepdims=True)
        gs.append(m1 + jnp.where(cnt >= 2.0, m1, m2))
    pieces = []
    for g in range(N_EXPERT_GROUPS):
        beaten = jnp.zeros_like(gs[g])
        for o in range(N_EXPERT_GROUPS):
            if o == g:
                continue
            wins = (gs[o] > gs[g]) | (gs[o] == gs[g]) if o < g else (gs[o] > gs[g])
            beaten = beaten + wins.astype(F32)
        keep = beaten < float(TOPK_GROUPS)
        xg = biased[g * GROUP_SIZE:(g + 1) * GROUP_SIZE]
        pieces.append(jnp.where(keep, xg, neg))
    cur = jnp.concatenate(pieces, axis=0)
    eid = lax.broadcasted_iota(I32, (N_EXPERTS, r_tok), 0).astype(F32)
    idx_rows, score_rows = [], []
    sel = jnp.zeros((N_EXPERTS, r_tok), F32)
    for _ in range(TOP_K):
        m = jnp.max(cur, axis=0, keepdims=True)
        idx = jnp.min(jnp.where(cur == m, eid, float(N_EXPERTS)), axis=0, keepdims=True)
        oh = eid == idx
        score_rows.append(jnp.sum(jnp.where(oh, scores, 0.0), axis=0, keepdims=True))
        idx_rows.append(idx)
        sel = sel + oh.astype(F32)
        cur = jnp.where(oh, neg, cur)
    tot = score_rows[0]
    for s in score_rows[1:]:
        tot = tot + s
    w_rows = [s / tot * ROUTED_SCALE for s in score_rows]
    rr = lax.broadcasted_iota(I32, (r_tok, r_tok), 0)
    cc = lax.broadcasted_iota(I32, (r_tok, r_tok), 1)
    earlier = _dot(sel.astype(BF16), (rr < cc).astype(BF16))
    counts = jnp.sum(sel, axis=1, keepdims=True)
    run_len = jnp.floor((counts + (RUN_ALIGN - 1.0)) * (1.0 / RUN_ALIGN)) * RUN_ALIGN
    er = lax.broadcasted_iota(I32, (N_EXPERTS, N_EXPERTS), 0)
    ec = lax.broadcasted_iota(I32, (N_EXPERTS, N_EXPERTS), 1)
    run_start = _dot((ec < er).astype(BF16), jnp.broadcast_to(run_len, (N_EXPERTS, 128)).astype(BF16))[:, :1]
    slot = earlier + run_start
    slot_rows = [jnp.sum(jnp.where(eid == idx, slot, 0.0), axis=0, keepdims=True) for idx in idx_rows]
    n_lanes = _group_lanes(r_tok)
    g_row = lax.broadcasted_iota(I32, (N_EXPERTS, n_lanes), 1).astype(F32) * float(RUN_ALIGN)
    owns = (run_start <= g_row) & (g_row < run_start + run_len)
    e_col = lax.broadcasted_iota(I32, (N_EXPERTS, n_lanes), 0).astype(F32)
    g_exp = jnp.sum(jnp.where(owns, e_col, 0.0), axis=0, keepdims=True)
    g_rel = jnp.sum(jnp.where(owns, before + g_row - run_start, 0.0), axis=0, keepdims=True)
    return slot_rows, w_rows, run_len, g_exp, g_rel


def _moe_prologue(x2, gffn_ref, sg_ref, su_ref, sd_ref):
    h3 = _rms(x2, gffn_ref[...])
    h3b = h3.astype(BF16)
    act = jax.nn.silu(_dot(h3b, sg_ref[...])) * _dot(h3b, su_ref[...])
    shared = _dot(act.astype(BF16), sd_ref[...])
    return h3, x2 + shared


def _store_rows(ref, rows, dtype):
    for k, row in enumerate(rows):
        ref[k:k + 1, :] = row.astype(dtype)


def _memkv_kernel(mem_ref, g_ref, wk_ref, wv_ref, k_ref, v_ref, kb_ref, vb_ref):
    m = _rms(mem_ref[0], g_ref[...]).astype(BF16)
    k = _dot(m, wk_ref[...])
    v = _dot(m, wv_ref[...])
    k_ref[0] = k.reshape(N_MEM, XA_HEADS, XA_HEAD_DIM)
    v_ref[0] = v.reshape(N_MEM, XA_HEADS, XA_HEAD_DIM)
    kb_ref[0] = k.astype(BF16)
    vb_ref[0] = v.astype(BF16)


def _memkv(mem, g, wk, wv):
    b = mem.shape[0]
    blk = pl.BlockSpec((1, N_MEM, D_MODEL), lambda i: (i, 0, 0))
    return pl.pallas_call(
        _memkv_kernel,
        grid=(b,),
        in_specs=[blk, _const_spec((1, D_MODEL)), _const_spec((D_MODEL, D_MODEL)), _const_spec((D_MODEL, D_MODEL))],
        out_specs=[pl.BlockSpec((1, N_MEM, XA_HEADS, XA_HEAD_DIM), lambda i: (i, 0, 0, 0))] * 2 + [blk, blk],
        out_shape=[jax.ShapeDtypeStruct((b, N_MEM, XA_HEADS, XA_HEAD_DIM), F32)] * 2
        + [jax.ShapeDtypeStruct((b, N_MEM, D_MODEL), BF16)] * 2,
        compiler_params=pltpu.CompilerParams(dimension_semantics=("arbitrary",), vmem_limit_bytes=VMEM_LIMIT),
        name="memkv",
    )(mem, g, wk, wv)


def _trunk_p_kernel(x_ref, kb_ref, vb_ref,
                    gmix_ref, win_ref, pw_ref, pscale_ref, cw_ref, cb_ref, wa_ref, ba_ref, wx_ref, bx_ref,
                    lam_ref, gpool_ref, grnn_ref, wout_ref,
                    gxa_ref, wq_ref, wo_ref,
                    gffn_ref, sg_ref, su_ref, sd_ref, wr_ref, rbias_ref,
                    xres_ref, hb_ref, slot_ref, gw_ref, gexp_ref, grel_ref, cover_ref, cnt_ref,
                    pool_ref, conv_ref, hT_ref,
                    pool_prev, conv_prev, h_prev, carry):
    g = pl.program_id(0)
    j = pl.program_id(1)

    @pl.when(j == 0)
    def _():
        pool_prev[...] = jnp.zeros_like(pool_prev)
        conv_prev[...] = jnp.zeros_like(conv_prev)
        h_prev[...] = jnp.zeros_like(h_prev)

    @pl.when((g == 0) & (j == 0))
    def _():
        carry[...] = jnp.zeros_like(carry)

    row = lax.broadcasted_iota(I32, (TS, 1), 0)
    pos = j * TS + row

    for sq in range(TRUNK_SEQS):
        x = x_ref[sq]

        h = _rms(x, gmix_ref[...]).astype(BF16)
        z = _dot(h, win_ref[...])
        u_pool = z[:, :D_POOL]
        u_rnn = z[:, D_POOL:D_POOL + D_RNN]
        u_gate = z[:, D_POOL + D_RNN:]

        ext = jnp.concatenate([pool_prev[sq], u_pool], axis=0)
        means = []
        for grp, w in enumerate(POOL_WINDOWS):
            s = ext[:, grp * POOL_GROUP:(grp + 1) * POOL_GROUP]
            k = 1
            while k < w:
                s = s + pltpu.roll(s, k, 0)
                k *= 2
            inv = 1.0 / jnp.minimum(pos + 1, w).astype(F32)
            means.append(s[HALO:] * inv)
        mean = jnp.concatenate(means, axis=1)
        y_pool = _pool_project(mean, u_pool, pw_ref, pscale_ref)

        extc = jnp.concatenate([conv_prev[sq], u_rnn], axis=0)
        c = u_rnn * cw_ref[CONV_WIDTH - 1:CONV_WIDTH, :]
        for k in range(1, CONV_WIDTH):
            c = c + pltpu.roll(extc, k, 0)[CONV_HALO:] * cw_ref[CONV_WIDTH - 1 - k:CONV_WIDTH - k, :]
        c = c + cb_ref[...]

        a, bt = _gates_and_decay(c, pos == 0, wa_ref, ba_ref, wx_ref, bx_ref, lam_ref)
        k = 1
        while k < TS:
            valid = row >= k
            a_sh = jnp.where(valid, pltpu.roll(a, k, 0), 1.0)
            b_sh = jnp.where(valid, pltpu.roll(bt, k, 0), 0.0)
            bt = bt + a * b_sh
            a = a * a_sh
            k *= 2
        hs = bt + a * h_prev[sq]

        pool_prev[sq] = u_pool[TS - HALO:]
        conv_prev[sq] = u_rnn[TS - CONV_HALO:]
        h_prev[sq] = hs[TS - 1:]
        pool_ref[sq] = u_pool[TS - POOL_BUF:]
        conv_ref[sq] = u_rnn[TS - (CONV_WIDTH - 1):]
        hT_ref[sq] = hs[TS - 1:]

        x1 = x + _merge_out(y_pool, hs, u_gate, gpool_ref, grnn_ref, wout_ref)

        h2 = _rms(x1, gxa_ref[...]).astype(BF16)
        q = (_dot(h2, wq_ref[...]) * (XA_HEAD_DIM ** -0.5)).astype(BF16)
        outs = []
        for hd in range(XA_HEADS):
            sl = slice(hd * XA_HEAD_DIM, (hd + 1) * XA_HEAD_DIM)
            s = _dot_nt(q[:, sl], kb_ref[sq, :, sl])
            s = s - jnp.max(s, axis=-1, keepdims=True)
            p = jnp.exp(s)
            p = p * (1.0 / jnp.sum(p, axis=-1, keepdims=True))
            outs.append(_dot(p.astype(BF16), vb_ref[sq, :, sl]))
        o = jnp.concatenate(outs, axis=1).astype(BF16)
        x2 = x1 + _dot(o, wo_ref[...])

        h3, xres = _moe_prologue(x2, gffn_ref, sg_ref, su_ref, sd_ref)
        xres_ref[sq] = xres
        hb_ref[sq] = h3.astype(BF16)
        slot_rows, w_rows, run_len, g_exp, g_rel = _route(h3, wr_ref, rbias_ref, carry[...])
        _store_rows(slot_ref.at[sq], slot_rows, I32)
        _store_rows(gw_ref.at[sq], w_rows, F32)
        gexp_ref[sq, 0] = g_exp.astype(I32)
        grel_ref[sq, 0] = g_rel.astype(I32)
        cover_ref[sq, 0] = jnp.broadcast_to(jnp.sum(run_len, axis=0, keepdims=True), cover_ref.shape[2:])
        carry[...] = carry[...] + run_len
    cnt_ref[...] = jnp.broadcast_to(carry[...], cnt_ref.shape)


def _trunk_p(x, kb, vb, mixw, xaw, moew):
    bsz, seq, _ = x.shape
    n_j = seq // TS
    q = TRUNK_SEQS
    assert bsz % q == 0
    seq_tile = lambda g, j: (g, j, 0)
    lane_tile = lambda g, j: (g, 0, j)
    per_tile = lambda g, j: (g, j, 0, 0)
    per_seq = lambda g, j: (g, 0, 0)
    weights = list(mixw) + list(xaw) + list(moew)
    in_specs = [pl.BlockSpec((q, TS, D_MODEL), seq_tile),
                pl.BlockSpec((q, N_MEM, D_MODEL), per_seq),
                pl.BlockSpec((q, N_MEM, D_MODEL), per_seq)] + [_const_spec(w.shape) for w in weights]
    lanes = _group_lanes(TS)
    out_shape = [jax.ShapeDtypeStruct((bsz, seq, D_MODEL), F32),
                 jax.ShapeDtypeStruct((bsz, seq, D_MODEL), BF16),
                 jax.ShapeDtypeStruct((bsz, TOP_K, seq), I32),
                 jax.ShapeDtypeStruct((bsz, TOP_K, seq), F32),
                 jax.ShapeDtypeStruct((bsz, n_j, 1, lanes), I32),
                 jax.ShapeDtypeStruct((bsz, n_j, 1, lanes), I32),
                 jax.ShapeDtypeStruct((bsz, n_j, 1, 128), F32),
                 jax.ShapeDtypeStruct((N_EXPERTS, 128), F32),
                 jax.ShapeDtypeStruct((bsz, POOL_BUF, D_POOL), F32),
                 jax.ShapeDtypeStruct((bsz, CONV_WIDTH - 1, D_RNN), F32),
                 jax.ShapeDtypeStruct((bsz, 1, D_RNN), F32)]
    out_specs = [pl.BlockSpec((q, TS, D_MODEL), seq_tile),
                 pl.BlockSpec((q, TS, D_MODEL), seq_tile),
                 pl.BlockSpec((q, TOP_K, TS), lane_tile),
                 pl.BlockSpec((q, TOP_K, TS), lane_tile),
                 pl.BlockSpec((q, 1, 1, lanes), per_tile),
                 pl.BlockSpec((q, 1, 1, lanes), per_tile),
                 pl.BlockSpec((q, 1, 1, 128), per_tile),
                 pl.BlockSpec((N_EXPERTS, 128), lambda g, j: (0, 0)),
                 pl.BlockSpec((q, POOL_BUF, D_POOL), per_seq),
                 pl.BlockSpec((q, CONV_WIDTH - 1, D_RNN), per_seq),
                 pl.BlockSpec((q, 1, D_RNN), per_seq)]
    return pl.pallas_call(
        _trunk_p_kernel,
        grid=(bsz // q, n_j),
        in_specs=in_specs,
        out_specs=out_specs,
        out_shape=out_shape,
        scratch_shapes=[pltpu.VMEM((q, HALO, D_POOL), F32), pltpu.VMEM((q, CONV_HALO, D_RNN), F32),
                        pltpu.VMEM((q, 1, D_RNN), F32), pltpu.VMEM((N_EXPERTS, 1), F32)],
        compiler_params=pltpu.CompilerParams(dimension_semantics=("arbitrary", "arbitrary"),
                                             vmem_limit_bytes=VMEM_LIMIT),
        name="trunk_p",
    )(x, kb, vb, *weights)


def _mix_s_kernel(x_ref, pool_ref, conv_ref, h0_ref,
                  gmix_ref, win_ref, pw_ref, pscale_ref, cw_ref, cb_ref, wa_ref, ba_ref, wx_ref, bx_ref,
                  lam_ref, gpool_ref, grnn_ref, wout_ref, gxa_ref, wq_ref,
                  x1_ref, q_ref, npool_ref, nconv_ref, nh_ref):
    x = x_ref[...]
    h = _rms(x, gmix_ref[...]).astype(BF16)
    z = _dot(h, win_ref[...])
    u_pool = z[:, :D_POOL]
    u_rnn = z[:, D_POOL:D_POOL + D_RNN]
    u_gate = z[:, D_POOL + D_RNN:]

    means = []
    for g, w in enumerate(POOL_WINDOWS):
        sl = slice(g * POOL_GROUP, (g + 1) * POOL_GROUP)
        s = u_pool[:, sl]
        for k in range(1, w):
            s = s + pool_ref[POOL_BUF - k, :, sl]
        means.append(s * (1.0 / min(w, PAST_LEN + 1)))
    mean = jnp.concatenate(means, axis=1)
    y_pool = _pool_project(mean, u_pool, pw_ref, pscale_ref)

    c = u_rnn * cw_ref[CONV_WIDTH - 1:CONV_WIDTH, :]
    for k in range(1, CONV_WIDTH):
        c = c + conv_ref[CONV_WIDTH - 1 - k] * cw_ref[CONV_WIDTH - 1 - k:CONV_WIDTH - k, :]
    c = c + cb_ref[...]
    a, bt = _gates_and_decay(c, None, wa_ref, ba_ref, wx_ref, bx_ref, lam_ref)
    hs = a * h0_ref[...] + bt

    x1 = x + _merge_out(y_pool, hs, u_gate, gpool_ref, grnn_ref, wout_ref)
    x1_ref[...] = x1
    h2 = _rms(x1, gxa_ref[...]).astype(BF16)
    q_ref[...] = _dot(h2, wq_ref[...]) * (XA_HEAD_DIM ** -0.5)

    npool_ref[:POOL_BUF - 1] = pool_ref[1:]
    npool_ref[POOL_BUF - 1] = u_pool
    nconv_ref[:CONV_WIDTH - 2] = conv_ref[1:]
    nconv_ref[CONV_WIDTH - 2] = u_rnn
    nh_ref[...] = hs


def _mix_s(x, pool, conv, h0, mixw, gxa, wq):
    bsz = x.shape[0]
    args = [x, pool, conv, h0] + list(mixw) + [gxa, wq]
    return pl.pallas_call(
        _mix_s_kernel,
        grid=(1,),
        in_specs=[_const_spec(a.shape) for a in args],
        out_specs=[_const_spec((bsz, D_MODEL)), _const_spec((bsz, D_MODEL)), _const_spec(pool.shape),
                   _const_spec(conv.shape), _const_spec((bsz, D_RNN))],
        out_shape=[jax.ShapeDtypeStruct((bsz, D_MODEL), F32), jax.ShapeDtypeStruct((bsz, D_MODEL), F32),
                   jax.ShapeDtypeStruct(pool.shape, F32), jax.ShapeDtypeStruct(conv.shape, F32),
                   jax.ShapeDtypeStruct((bsz, D_RNN), F32)],
        compiler_params=pltpu.CompilerParams(dimension_semantics=("arbitrary",), vmem_limit_bytes=VMEM_LIMIT),
        name="mix_s",
    )(*args)


ATTN_S_BB = 4


def _attn_s_kernel(q_ref, k_ref, v_ref, o_ref):
    q = q_ref[...][:, None]
    s = jnp.sum(k_ref[...] * q, axis=-1, keepdims=True)
    s = s - jnp.max(s, axis=1, keepdims=True)
    p = jnp.exp(s)
    p = p / jnp.sum(p, axis=1, keepdims=True)
    o_ref[...] = jnp.sum(p * v_ref[...], axis=1)


def _attn_s(q, k, v):
    bsz = q.shape[0]
    kv_spec = pl.BlockSpec((ATTN_S_BB, N_MEM, XA_HEADS, XA_HEAD_DIM), lambda i: (i, 0, 0, 0))
    q_spec = pl.BlockSpec((ATTN_S_BB, XA_HEADS, XA_HEAD_DIM), lambda i: (i, 0, 0))
    o = pl.pallas_call(
        _attn_s_kernel,
        grid=(bsz // ATTN_S_BB,),
        in_specs=[q_spec, kv_spec, kv_spec],
        out_specs=q_spec,
        out_shape=jax.ShapeDtypeStruct((bsz, XA_HEADS, XA_HEAD_DIM), F32),
        compiler_params=pltpu.CompilerParams(dimension_semantics=("arbitrary",), vmem_limit_bytes=VMEM_LIMIT),
        name="attn_s",
    )(q.reshape(bsz, XA_HEADS, XA_HEAD_DIM), k, v)
    return o.reshape(bsz, D_MODEL)


def _post_s_kernel(x1_ref, o_ref, wo_ref, gffn_ref, sg_ref, su_ref, sd_ref, wr_ref, rbias_ref,
                   xres_ref, hb_ref, slot_ref, gw_ref, gexp_ref, grel_ref, cover_ref, cnt_ref):
    x2 = x1_ref[...] + _dot(o_ref[...].astype(BF16), wo_ref[...])
    h3, xres = _moe_prologue(x2, gffn_ref, sg_ref, su_ref, sd_ref)
    xres_ref[...] = xres
    hb_ref[...] = h3.astype(BF16)
    slot_rows, w_rows, run_len, g_exp, g_rel = _route(h3, wr_ref, rbias_ref, jnp.zeros((N_EXPERTS, 1), F32))
    _store_rows(slot_ref, slot_rows, I32)
    _store_rows(gw_ref, w_rows, F32)
    gexp_ref[0] = g_exp.astype(I32)
    grel_ref[0] = g_rel.astype(I32)
    cover_ref[0] = jnp.broadcast_to(jnp.sum(run_len, axis=0, keepdims=True), cover_ref.shape[1:])
    cnt_ref[...] = jnp.broadcast_to(run_len, cnt_ref.shape)


def _post_s(x1, o, wo, moew):
    bsz = x1.shape[0]
    args = [x1, o, wo] + list(moew)
    out_shape = [jax.ShapeDtypeStruct((bsz, D_MODEL), F32),
                 jax.ShapeDtypeStruct((bsz, D_MODEL), BF16),
                 jax.ShapeDtypeStruct((TOP_K, bsz), I32),
                 jax.ShapeDtypeStruct((TOP_K, bsz), F32),
                 jax.ShapeDtypeStruct((1, 1, _group_lanes(bsz)), I32),
                 jax.ShapeDtypeStruct((1, 1, _group_lanes(bsz)), I32),
                 jax.ShapeDtypeStruct((1, 1, 128), F32),
                 jax.ShapeDtypeStruct((N_EXPERTS, 128), F32)]
    return pl.pallas_call(
        _post_s_kernel,
        grid=(1,),
        in_specs=[_const_spec(a.shape) for a in args],
        out_specs=[_const_spec(s.shape) for s in out_shape],
        out_shape=out_shape,
        compiler_params=pltpu.CompilerParams(dimension_semantics=("arbitrary",), vmem_limit_bytes=VMEM_LIMIT),
        name="post_s",
    )(*args)


def _plan_kernel(rp_ref, rs_ref, base_p_ref, base_s_ref, fill_at_ref, fill_n_ref, exp_ref, nreal_ref):
    n_blocks = exp_ref.shape[0]
    shift = BM.bit_length() - 1

    def per_expert(e, carry):
        blk0, last_e = carry
        rows = rp_ref[e] + rs_ref[e]
        start = lax.shift_left(blk0, shift)
        n_blk = lax.shift_right_logical(rows + (BM - 1), shift)
        base_p_ref[e] = start
        base_s_ref[e] = start + rp_ref[e]
        fill_at_ref[e] = start + rows
        fill_n_ref[e] = _groups(lax.shift_left(n_blk, shift) - rows)

        def per_block(j, _):
            exp_ref[blk0 + j] = e
            return _

        lax.fori_loop(0, n_blk, per_block, 0)
        return blk0 + n_blk, jnp.where(rows > 0, e, last_e)

    n_real, last_e = lax.fori_loop(0, N_EXPERTS, per_expert, (jnp.int32(0), jnp.int32(0)))
    nreal_ref[0] = n_real

    def rest(b, _):
        exp_ref[b] = last_e
        return _

    lax.fori_loop(n_real, n_blocks, rest, 0)


def _plan(rows_p, rows_s, n_rows):
    assert BM & (BM - 1) == 0 and n_rows % BM == 0
    smem = pl.BlockSpec(memory_space=pltpu.SMEM)
    return pl.pallas_call(
        _plan_kernel,
        in_specs=[smem, smem],
        out_specs=[smem] * 6,
        out_shape=[jax.ShapeDtypeStruct((N_EXPERTS,), I32)] * 4
        + [jax.ShapeDtypeStruct((n_rows // BM,), I32), jax.ShapeDtypeStruct((1,), I32)],
        name="plan",
    )(rows_p, rows_s)


def _groups(n_rows):
    return lax.shift_right_logical(n_rows, RUN_ALIGN.bit_length() - 1)


GROUP_UNROLL = 8


def _for_each_group(n_groups, grow_ref, fn):
    def one(g, priority):
        fn(pl.multiple_of(g * RUN_ALIGN, RUN_ALIGN), pl.multiple_of(grow_ref[0, 0, g], RUN_ALIGN), priority)

    def several(j, _):
        for u in range(GROUP_UNROLL):
            one(j * GROUP_UNROLL + u, u % 2)
        return _

    def single(g, _):
        one(g, 0)
        return _

    n_full = lax.shift_right_logical(n_groups, GROUP_UNROLL.bit_length() - 1)
    lax.fori_loop(0, n_full, several, 0)
    lax.fori_loop(n_full * GROUP_UNROLL, n_groups, single, 0)


def _wait_groups(n_groups, group_wait, bulk_wait):
    def bulk(j, _):
        bulk_wait()
        return _

    def single(g, _):
        group_wait()
        return _

    n_full = lax.shift_right_logical(n_groups, GROUP_UNROLL.bit_length() - 1)
    lax.fori_loop(0, n_full, bulk, 0)
    lax.fori_loop(n_full * GROUP_UNROLL, n_groups, single, 0)


def _group_rows_kernel(gexp_ref, grel_ref, base_ref, out_ref):
    rows = grel_ref[...]
    e = gexp_ref[...]
    for ex in range(N_EXPERTS):
        rows = rows + jnp.where(e == ex, base_ref[ex], 0)
    out_ref[...] = rows


def _group_rows(gexp, grel, base):
    shape = gexp.shape
    flat = (shape[0], shape[-1])
    out = pl.pallas_call(
        _group_rows_kernel,
        in_specs=[pl.BlockSpec(memory_space=pltpu.VMEM), pl.BlockSpec(memory_space=pltpu.VMEM),
                  pl.BlockSpec(memory_space=pltpu.SMEM)],
        out_specs=pl.BlockSpec(memory_space=pltpu.VMEM),
        out_shape=jax.ShapeDtypeStruct(flat, I32),
        name="group_rows",
    )(gexp.reshape(flat), grel.reshape(flat), base)
    return out.reshape(shape)


def _dispatch_kernel(cover_p, cover_s, fill_at, fill_n, nreal_ref, grow_p, grow_s,
                     slot_p_ref, h_p_ref, slot_s_ref, h_s_ref, xs_ref, sbuf, zbuf, sem):
    i = pl.program_id(0)
    last = pl.num_programs(0) - 1
    cur = lax.rem(i, 2)

    def drain(s, n_rows):
        def wait_rows(n):
            pltpu.make_async_copy(sbuf.at[s, pl.ds(0, n)], xs_ref.at[pl.ds(0, n)], sem.at[s]).wait()
        _wait_groups(_groups(n_rows), lambda: wait_rows(RUN_ALIGN), lambda: wait_rows(RUN_ALIGN * GROUP_UNROLL))

    def tile(n_rows, grow_ref, slot_ref, h_ref):
        def sort_chunk(c):
            rid = (c * SORT_CHUNK + lax.broadcasted_iota(I32, (SORT_CHUNK, 1), 0)).astype(jnp.int16)
            slots = slot_ref[...].astype(jnp.int16)
            hit = rid == slots[0:1, :]
            for k in range(1, TOP_K):
                hit = hit | (rid == slots[k:k + 1, :])
            sbuf[cur, c * SORT_CHUNK:(c + 1) * SORT_CHUNK] = _dot(hit.astype(BF16), h_ref[...]).astype(BF16)

        def send(local, glob, priority):
            pltpu.make_async_copy(sbuf.at[cur, pl.ds(local, RUN_ALIGN)], xs_ref.at[pl.ds(glob, RUN_ALIGN)],
                                  sem.at[cur]).start(priority=priority)

        always = h_ref.shape[0] * TOP_K // SORT_CHUNK
        for c in range(_sorted_rows(h_ref.shape[0]) // SORT_CHUNK):
            if c < always:
                sort_chunk(c)
            else:
                pl.when(c * SORT_CHUNK < n_rows)(functools.partial(sort_chunk, c))
        _for_each_group(_groups(n_rows), grow_ref, send)

    @pl.when(i >= 2)
    def _():
        drain(cur, cover_p[i - 2])

    @pl.when(i < last)
    def _():
        tile(cover_p[i], grow_p, slot_p_ref, h_p_ref)

    @pl.when(i == last)
    def _():
        tile(cover_s[0], grow_s, slot_s_ref, h_s_ref)
        drain(1 - cur, cover_p[last - 1])
        drain(cur, cover_s[0])
        zbuf[...] = jnp.zeros_like(zbuf)

        def group_fill(e, g):
            return pltpu.make_async_copy(
                zbuf.at[pl.ds(0, RUN_ALIGN)],
                xs_ref.at[pl.ds(pl.multiple_of(fill_at[e] + g * RUN_ALIGN, RUN_ALIGN), RUN_ALIGN)], sem.at[2])

        def per_expert(e, n):
            def start(g, _):
                group_fill(e, g).start()
                return _
            lax.fori_loop(0, fill_n[e], start, 0)
            return n + fill_n[e]

        n_fill = lax.fori_loop(0, N_EXPERTS, per_expert, jnp.int32(0))

        def wait_group(g, _):
            group_fill(0, 0).wait()
            return _

        lax.fori_loop(0, n_fill, wait_group, 0)

        n_real = nreal_ref[0]
        n_tail = xs_ref.shape[0] // BM - n_real

        def blk_fill(b):
            return pltpu.make_async_copy(zbuf, xs_ref.at[pl.ds(pl.multiple_of((n_real + b) * BM, BM), BM)], sem.at[2])

        def start_blk(b, _):
            blk_fill(b).start()
            return _

        def wait_blk(b, _):
            blk_fill(b).wait()
            return _

        lax.fori_loop(0, n_tail, start_blk, 0)
        lax.fori_loop(0, n_tail, wait_blk, 0)


def _dispatch(cover_p, cover_s, fills, grow_p, grow_s, slot_p, hb_p, slot_s, hb_s, n_rows):
    n_p = hb_p.shape[0] // TS
    smem = pl.BlockSpec(memory_space=pltpu.SMEM)
    clamp = lambda i: jnp.minimum(i, n_p - 1)
    per_tile = pl.BlockSpec((1, 1, grow_p.shape[-1]), lambda i: (clamp(i), 0, 0), memory_space=pltpu.SMEM)
    return pl.pallas_call(
        _dispatch_kernel,
        grid=(n_p + 1,),
        in_specs=[smem] * 5 + [per_tile, smem,
                               pl.BlockSpec((TOP_K, TS), lambda i: (0, clamp(i))),
                               pl.BlockSpec((TS, D_MODEL), lambda i: (clamp(i), 0)),
                               _const_spec(slot_s.shape), _const_spec(hb_s.shape)],
        out_specs=pl.BlockSpec(memory_space=pl.ANY),
        out_shape=jax.ShapeDtypeStruct((n_rows, D_MODEL), BF16),
        scratch_shapes=[pltpu.VMEM((2, _sorted_rows(TS), D_MODEL), BF16),
                        pltpu.VMEM((BM, D_MODEL), BF16), pltpu.SemaphoreType.DMA((3,))],
        compiler_params=pltpu.CompilerParams(dimension_semantics=("arbitrary",), has_side_effects=True,
                                             vmem_limit_bytes=VMEM_LIMIT),
        name="dispatch",
    )(cover_p, cover_s, *fills, grow_p, grow_s, slot_p, hb_p, slot_s, hb_s)


GMM_LANES = 2


def _gmm_kernel(exp_ref, nreal_ref, *refs):
    x_ref = refs[0]
    w_refs = refs[1:1 + 3 * GMM_LANES]
    y_ref = refs[1 + 3 * GMM_LANES]
    scratch = refs[2 + 3 * GMM_LANES:]
    s = pl.program_id(0)
    n_real = nreal_ref[0]

    for lane in range(GMM_LANES):
        wg_ref, wu_ref, wd_ref = w_refs[3 * lane:3 * lane + 3]
        wgu, wdn = scratch[2 * lane:2 * lane + 2]
        b = s * GMM_LANES + lane
        new_expert = (s == 0) | (exp_ref[b] != exp_ref[jnp.maximum(b - GMM_LANES, 0)])

        @pl.when((b < n_real) & new_expert)
        def _():
            wgu[:, :D_EXPERT] = wg_ref[0].astype(BF16)
            wgu[:, D_EXPERT:] = wu_ref[0].astype(BF16)
            wdn[...] = wd_ref[0].astype(BF16)

    @pl.when(s * GMM_LANES < n_real)
    def _():
        for lane in range(GMM_LANES):
            wgu, wdn = scratch[2 * lane:2 * lane + 2]
            rows = slice(lane * BM, (lane + 1) * BM)
            gu = _dot(x_ref[rows, :], wgu[...])
            act = jax.nn.silu(gu[:, :D_EXPERT]) * gu[:, D_EXPERT:]
            y = _dot(act.astype(BF16), wdn[...]).astype(BF16)
            y_ref[rows, :] = jnp.where(s * GMM_LANES + lane < n_real, y, jnp.zeros_like(y))

    @pl.when(s * GMM_LANES >= n_real)
    def _():
        y_ref[...] = jnp.zeros_like(y_ref)


def _gmm(blk_exp, n_real, xs, wg, wu, wd):
    n_rows = xs.shape[0]
    step_rows = BM * GMM_LANES
    assert n_rows % step_rows == 0
    last_step = lambda nreal: (nreal[0] - 1) // GMM_LANES
    w_specs, scratch = [], []
    for lane in range(GMM_LANES):
        weight = lambda s, exp, nreal, lane=lane: (exp[s * GMM_LANES + lane], 0, 0)
        w_specs += [pl.BlockSpec((1, D_MODEL, D_EXPERT), weight), pl.BlockSpec((1, D_MODEL, D_EXPERT), weight),
                    pl.BlockSpec((1, D_EXPERT, D_MODEL), weight)]
        scratch += [pltpu.VMEM((D_MODEL, 2 * D_EXPERT), BF16), pltpu.VMEM((D_EXPERT, D_MODEL), BF16)]
    grid_spec = pltpu.PrefetchScalarGridSpec(
        num_scalar_prefetch=2,
        grid=(n_rows // step_rows,),
        in_specs=[pl.BlockSpec((step_rows, D_MODEL), lambda s, exp, nreal: (jnp.minimum(s, last_step(nreal)), 0))]
        + w_specs,
        out_specs=pl.BlockSpec((step_rows, D_MODEL), lambda s, exp, nreal: (s, 0)),
        scratch_shapes=scratch,
    )
    return pl.pallas_call(
        _gmm_kernel,
        grid_spec=grid_spec,
        out_shape=jax.ShapeDtypeStruct((n_rows, D_MODEL), BF16),
        compiler_params=pltpu.CompilerParams(dimension_semantics=("arbitrary",), vmem_limit_bytes=VMEM_LIMIT),
        name="gmm",
    )(blk_exp, n_real, xs, *([wg, wu, wd] * GMM_LANES))


def _combine_kernel(cover_ref, grow_ref, grow_next_ref,
                    slot_ref, gw_ref, xres_ref, gfin_ref, ys_ref, out_ref, ybuf, sem):
    i = pl.program_id(0)
    n = pl.num_programs(0)
    cur = lax.rem(i, 2)
    rows = ybuf.shape[1]

    def rows_copy(s, local, glob, n_rows):
        return pltpu.make_async_copy(ys_ref.at[pl.ds(glob, n_rows)], ybuf.at[s, pl.ds(local, n_rows)], sem.at[s])

    def gather(n_rows, gr_ref, s):
        _for_each_group(_groups(n_rows), gr_ref,
                        lambda l, g, priority: rows_copy(s, l, g, RUN_ALIGN).start(priority=priority))

    @pl.when(i == 0)
    def _():
        ybuf[...] = jnp.zeros_like(ybuf)
        gather(cover_ref[0], grow_ref, 0)

    @pl.when(i + 1 < n)
    def _():
        gather(cover_ref[jnp.minimum(i + 1, n - 1)], grow_next_ref, 1 - cur)

    _wait_groups(_groups(cover_ref[i]), lambda: rows_copy(cur, 0, 0, RUN_ALIGN).wait(),
                 lambda: rows_copy(cur, 0, 0, RUN_ALIGN * GROUP_UNROLL).wait())

    tm = slot_ref.shape[0]
    slot_b = [jnp.broadcast_to(slot_ref[:, k:k + 1], (tm, 128)).astype(jnp.int16) for k in range(TOP_K)]
    gw_b = [jnp.broadcast_to(gw_ref[:, k:k + 1], (tm, 128)).astype(BF16) for k in range(TOP_K)]
    lane = lax.broadcasted_iota(I32, (1, 128), 1)

    def chunk_sum(c):
        pieces = []
        for p in range(SORT_CHUNK // 128):
            col = (lane + (c * SORT_CHUNK + p * 128)).astype(jnp.int16)
            w = jnp.zeros((tm, 128), BF16)
            for k in range(TOP_K):
                w = jnp.where(slot_b[k] == col, gw_b[k], w)
            pieces.append(w)
        return _dot(jnp.concatenate(pieces, axis=1), ybuf[cur, c * SORT_CHUNK:(c + 1) * SORT_CHUNK])

    always = tm * TOP_K // SORT_CHUNK
    acc = xres_ref[...]
    for c in range(always):
        acc = acc + chunk_sum(c)
    out_ref[...] = acc
    for c in range(always, rows // SORT_CHUNK):
        @pl.when(c * SORT_CHUNK < cover_ref[i])
        def _():
            out_ref[...] += chunk_sum(c)
    out_ref[...] = _rms(out_ref[...], gfin_ref[...])


def _combine(cover, grow, slot_t, gw_t, xres, gfin, ys):
    t = xres.shape[0]
    tm = min(t, TS)
    n = t // tm
    smem = pl.BlockSpec(memory_space=pltpu.SMEM)
    lanes = grow.shape[-1]
    this_tile = pl.BlockSpec((1, 1, lanes), lambda i: (i, 0, 0), memory_space=pltpu.SMEM)
    next_tile = pl.BlockSpec((1, 1, lanes), lambda i: (jnp.minimum(i + 1, n - 1), 0, 0), memory_space=pltpu.SMEM)
    return pl.pallas_call(
        _combine_kernel,
        grid=(n,),
        in_specs=[smem, this_tile, next_tile,
                  pl.BlockSpec((tm, TOP_K), lambda i: (i, 0)),
                  pl.BlockSpec((tm, TOP_K), lambda i: (i, 0)),
                  pl.BlockSpec((tm, D_MODEL), lambda i: (i, 0)),
                  _const_spec((1, D_MODEL)),
                  pl.BlockSpec(memory_space=pl.ANY)],
        out_specs=pl.BlockSpec((tm, D_MODEL), lambda i: (i, 0)),
        out_shape=jax.ShapeDtypeStruct((t, D_MODEL), F32),
        scratch_shapes=[pltpu.VMEM((2, _sorted_rows(tm), D_MODEL), BF16), pltpu.SemaphoreType.DMA((2,))],
        compiler_params=pltpu.CompilerParams(dimension_semantics=("arbitrary",), vmem_limit_bytes=VMEM_LIMIT),
        name="combine",
    )(cover, grow, grow, slot_t, gw_t, xres, gfin, ys)


def _block_diag_pairs(w):
    n_h, d, _ = w.shape
    half = n_h // 2
    on_diag = jnp.eye(half, dtype=bool)[None, :, None, :, None]
    blocks = w.reshape(2, half, d, 1, d)
    out = jnp.where(on_diag, blocks, 0.0)
    return out.reshape(2, half * d, half * d).astype(BF16)


def kernel(x_prompt, x_sample, state_pool, state_conv, state_h, cache_mem_k, cache_mem_v, mem_prompt, norm_mix, w_in, pool_w, pool_scale, conv_w, conv_b, gate_a_w, gate_a_b, gate_x_w, gate_x_b, lru_lambda, norm_pool_out, norm_rnn_out, w_out, norm_xattn, norm_mem, xa_wq, xa_wk, xa_wv, xa_wo, norm_ffn, router_w, router_bias, exp_w_gate, exp_w_up, exp_w_down, sh_w_gate, sh_w_up, sh_w_down, norm_final):
    bp, seq, _ = x_prompt.shape
    bs = x_sample.shape[0]
    tp = bp * seq
    row = lambda v: v.reshape(1, -1)
    bf = lambda v: v.astype(BF16)

    mixw = (row(norm_mix[0]), bf(w_in[0]), _block_diag_pairs(pool_w[0]), row(pool_scale[0]), conv_w[0],
            row(conv_b[0]), _block_diag_pairs(gate_a_w[0]), row(gate_a_b[0]), _block_diag_pairs(gate_x_w[0]),
            row(gate_x_b[0]), row(lru_lambda[0]), row(norm_pool_out[0]), row(norm_rnn_out[0]), bf(w_out[0]))
    xaw = (row(norm_xattn[0]), bf(xa_wq[0]), bf(xa_wo[0]))
    moew = (row(norm_ffn[0]), bf(sh_w_gate[0]), bf(sh_w_up[0]), bf(sh_w_down[0]), router_w[0].T,
            router_bias[0].reshape(N_EXPERTS, 1))

    mk, mv, kb, vb = _memkv(mem_prompt, row(norm_mem[0]), bf(xa_wk[0]), bf(xa_wv[0]))
    (xres_p, hb_p, slot_p, gw_p, gexp_p, grel_p, cover_p, cnt_p, pool_p, conv_p, h_p) = _trunk_p(
        x_prompt, kb, vb, mixw, xaw, moew)

    x1_s, q_s, pool_s, conv_s, h_s = _mix_s(x_sample.reshape(bs, D_MODEL), state_pool[0].transpose(1, 0, 2),
                                            state_conv[0].transpose(1, 0, 2), state_h[0], mixw, xaw[0], xaw[1])
    pool_s = pool_s.transpose(1, 0, 2)
    conv_s = conv_s.transpose(1, 0, 2)
    o_s = _attn_s(q_s, cache_mem_k[0], cache_mem_v[0])
    xres_s, hb_s, slot_s, gw_s, gexp_s, grel_s, cover_s, cnt_s = _post_s(x1_s, o_s, xaw[2], moew)

    n_tiles = tp // TS + 1
    n_rows = _round_up((tp + bs) * TOP_K + n_tiles * N_EXPERTS * (RUN_ALIGN - 1) + N_EXPERTS * (BM - 1),
                       BM * GMM_LANES)
    ints = lambda v: v[..., 0].astype(I32)
    base_p, base_s, fill_at, fill_n, blk_exp, n_real = _plan(ints(cnt_p), ints(cnt_s), n_rows)
    cover_p, cover_s = ints(cover_p).reshape(-1), ints(cover_s).reshape(-1)
    xres_p = xres_p.reshape(tp, D_MODEL)
    hb_p = hb_p.reshape(tp, D_MODEL)
    slot_p = slot_p.transpose(1, 0, 2).reshape(TOP_K, tp)
    gw_p = gw_p.transpose(1, 0, 2).reshape(TOP_K, tp)
    gexp_p = gexp_p.reshape(-1, 1, gexp_p.shape[-1])
    grel_p = grel_p.reshape(-1, 1, grel_p.shape[-1])
    grow_p = _group_rows(gexp_p, grel_p, base_p)
    grow_s = _group_rows(gexp_s, grel_s, base_s)

    xs = _dispatch(cover_p, cover_s, (fill_at, fill_n, n_real), grow_p, grow_s, slot_p, hb_p, slot_s, hb_s, n_rows)
    ys = _gmm(blk_exp, n_real, xs, exp_w_gate[0], exp_w_up[0], exp_w_down[0])

    gfin = row(norm_final)
    y_p = _combine(cover_p, grow_p, slot_p.T, gw_p.T, xres_p, gfin, ys)
    y_s = _combine(cover_s, grow_s, slot_s.T, gw_s.T, xres_s, gfin, ys)

    return (y_p.reshape(bp, seq, D_MODEL), y_s.reshape(bs, 1, D_MODEL),
            pool_p[None], conv_p[None], h_p.reshape(1, bp, D_RNN),
            mk[None], mv[None],
            pool_s[None], conv_s[None], h_s[None])
```

```python
import functools

import jax
import jax.numpy as jnp
from jax import lax
from jax.experimental import pallas as pl
from jax.experimental.pallas import tpu as pltpu

F32 = jnp.float32
BF16 = jnp.bfloat16
I32 = jnp.int32

D_MODEL = 1024
D_POOL = 512
D_RNN = 512
D_IN = D_POOL + 2 * D_RNN
POOL_WINDOWS = (2, 4, 8, 16)
POOL_GROUP = 128
POOL_BUF = 15
CONV_WIDTH = 4
LRU_C = 8.0
N_MEM = 256
XA_HEADS = 4
XA_HEAD_DIM = 256
N_EXPERTS = 64
TOP_K = 8
N_EXPERT_GROUPS = 8
GROUP_SIZE = N_EXPERTS // N_EXPERT_GROUPS
TOPK_GROUPS = 4
D_EXPERT = 256
ROUTED_SCALE = 2.5
EPS = 1e-6
PAST_LEN = 16384

HALO = 16
CONV_HALO = 8
TS = 256
TRUNK_SEQS = 1
BM = 512
RUN_ALIGN = 16
SORT_CHUNK = 512
VMEM_LIMIT = 56 * 1024 * 1024


def _round_up(x, m):
    return (x + m - 1) // m * m


def _sorted_rows(tokens):
    return _round_up(tokens * TOP_K + N_EXPERTS * (RUN_ALIGN - 1), SORT_CHUNK)


assert _sorted_rows(TS) < 2 ** 15


def _group_lanes(tokens):
    return _round_up(_sorted_rows(tokens) // RUN_ALIGN, 128)


def _const_spec(shape):
    nd = len(shape)
    return pl.BlockSpec(shape, lambda *_: (0,) * nd, pipeline_mode=pl.Buffered(1))


def _rms(x, g):
    ms = jnp.mean(x * x, axis=-1, keepdims=True)
    return x * lax.rsqrt(ms + EPS) * g


def _dot(a, b):
    return jnp.dot(a, b, preferred_element_type=F32)


def _dot_nt(a, b, precision=None):
    return lax.dot_general(a, b, (((1,), (1,)), ((), ())), precision=precision,
                           preferred_element_type=F32)


def _split_bf16(x):
    hi = x.astype(BF16)
    return hi, (x - hi.astype(F32)).astype(BF16)


def _softplus(x):
    return jnp.maximum(x, 0.0) + jnp.log1p(jnp.exp(-jnp.abs(x)))


def _gates_and_decay(c, pos_is_zero, wa_ref, ba_ref, wx_ref, bx_ref, lam_ref):
    cb = c.astype(BF16)
    half = D_RNN // 2
    ga = jnp.concatenate([_dot(cb[:, :half], wa_ref[0]), _dot(cb[:, half:], wa_ref[1])], axis=1) + ba_ref[...]
    gx = jnp.concatenate([_dot(cb[:, :half], wx_ref[0]), _dot(cb[:, half:], wx_ref[1])], axis=1) + bx_ref[...]
    r = jax.nn.sigmoid(ga)
    i = jax.nn.sigmoid(gx)
    log_a = (-LRU_C) * r * _softplus(-lam_ref[...])
    a = jnp.exp(log_a)
    m2 = 1.0 - a * a
    mult = jnp.where(m2 > 0.0, m2 * lax.rsqrt(m2), 0.0)
    if pos_is_zero is not None:
        mult = jnp.where(pos_is_zero, 1.0, mult)
    return a, mult * i * c


def _pool_project(mean, u_pool, pw_ref, pscale_ref):
    d = (mean - u_pool).astype(BF16)
    half = D_POOL // 2
    y = jnp.concatenate([_dot(d[:, :half], pw_ref[0]), _dot(d[:, half:], pw_ref[1])], axis=1)
    return y * pscale_ref[...]


def _merge_out(y_pool, hs, u_gate, gpool_ref, grnn_ref, wout_ref):
    y_rnn = hs * jax.nn.gelu(u_gate)
    merged = jnp.concatenate([_rms(y_pool, gpool_ref[...]), _rms(y_rnn, grnn_ref[...])], axis=1)
    return _dot(merged.astype(BF16), wout_ref[...])


def _route(h3, wr_ref, rbias_ref, before):
    r_tok = h3.shape[0]
    w_hi, w_lo = _split_bf16(wr_ref[...])
    h_hi, h_lo = _split_bf16(h3)
    logits = _dot_nt(w_hi, h_hi) + (_dot_nt(w_hi, h_lo) + _dot_nt(w_lo, h_hi))
    scores = jax.nn.sigmoid(logits)
    biased = scores + rbias_ref[...]
    neg = jnp.float32(-jnp.inf)
    gs = []
    for g in range(N_EXPERT_GROUPS):
        xg = biased[g * GROUP_SIZE:(g + 1) * GROUP_SIZE]
        m1 = jnp.max(xg, axis=0, keepdims=True)
        eq = xg == m1
        cnt = jnp.sum(eq.astype(F32), axis=0, keepdims=True)
        m2 = jnp.max(jnp.where(eq, neg, xg), axis=0, keepdims=True)
        gs.append(m1 + jnp.where(cnt >= 2.0, m1, m2))
    pieces = []
    for g in range(N_EXPERT_GROUPS):
        beaten = jnp.zeros_like(gs[g])
        for o in range(N_EXPERT_GROUPS):
            if o == g:
                continue
            wins = (gs[o] > gs[g]) | (gs[o] == gs[g]) if o < g else (gs[o] > gs[g])
            beaten = beaten + wins.astype(F32)
        keep = beaten < float(TOPK_GROUPS)
        xg = biased[g * GROUP_SIZE:(g + 1) * GROUP_SIZE]
        pieces.append(jnp.where(keep, xg, neg))
    cur = jnp.concatenate(pieces, axis=0)
    eid = lax.broadcasted_iota(I32, (N_EXPERTS, r_tok), 0).astype(F32)
    idx_rows, score_rows = [], []
    sel = jnp.zeros((N_EXPERTS, r_tok), F32)
    for _ in range(TOP_K):
        m = jnp.max(cur, axis=0, keepdims=True)
        idx = jnp.min(jnp.where(cur == m, eid, float(N_EXPERTS)), axis=0, keepdims=True)
        oh = eid == idx
        score_rows.append(jnp.sum(jnp.where(oh, scores, 0.0), axis=0, keepdims=True))
        idx_rows.append(idx)
        sel = sel + oh.astype(F32)
        cur = jnp.where(oh, neg, cur)
    tot = score_rows[0]
    for s in score_rows[1:]:
        tot = tot + s
    w_rows = [s / tot * ROUTED_SCALE for s in score_rows]
    rr = lax.broadcasted_iota(I32, (r_tok, r_tok), 0)
    cc = lax.broadcasted_iota(I32, (r_tok, r_tok), 1)
    earlier = _dot(sel.astype(BF16), (rr < cc).astype(BF16))
    counts = jnp.sum(sel, axis=1, keepdims=True)
    run_len = jnp.floor((counts + (RUN_ALIGN - 1.0)) * (1.0 / RUN_ALIGN)) * RUN_ALIGN
    er = lax.broadcasted_iota(I32, (N_EXPERTS, N_EXPERTS), 0)
    ec = lax.broadcasted_iota(I32, (N_EXPERTS, N_EXPERTS), 1)
    run_start = _dot((ec < er).astype(BF16), jnp.broadcast_to(run_len, (N_EXPERTS, 128)).astype(BF16))[:, :1]
    slot = earlier + run_start
    slot_rows = [jnp.sum(jnp.where(eid == idx, slot, 0.0), axis=0, keepdims=True) for idx in idx_rows]
    n_lanes = _group_lanes(r_tok)
    g_row = lax.broadcasted_iota(I32, (N_EXPERTS, n_lanes), 1).astype(F32) * float(RUN_ALIGN)
    owns = (run_start <= g_row) & (g_row < run_start + run_len)
    e_col = lax.broadcasted_iota(I32, (N_EXPERTS, n_lanes), 0).astype(F32)
    g_exp = jnp.sum(jnp.where(owns, e_col, 0.0), axis=0, keepdims=True)
    g_rel = jnp.sum(jnp.where(owns, before + g_row - run_start, 0.0), axis=0, keepdims=True)
    return slot_rows, w_rows, run_len, g_exp, g_rel


def _moe_prologue(x2, gffn_ref, sg_ref, su_ref, sd_ref):
    h3 = _rms(x2, gffn_ref[...])
    h3b = h3.astype(BF16)
    act = jax.nn.silu(_dot(h3b, sg_ref[...])) * _dot(h3b, su_ref[...])
    shared = _dot(act.astype(BF16), sd_ref[...])
    return h3, x2 + shared


def _store_rows(ref, rows, dtype):
    for k, row in enumerate(rows):
        ref[k:k + 1, :] = row.astype(dtype)


def _memkv_kernel(mem_ref, g_ref, wk_ref, wv_ref, k_ref, v_ref, kb_ref, vb_ref):
    m = _rms(mem_ref[0], g_ref[...]).astype(BF16)
    k = _dot(m, wk_ref[...])
    v = _dot(m, wv_ref[...])
    k_ref[0] = k.reshape(N_MEM, XA_HEADS, XA_HEAD_DIM)
    v_ref[0] = v.reshape(N_MEM, XA_HEADS, XA_HEAD_DIM)
    kb_ref[0] = k.astype(BF16)
    vb_ref[0] = v.astype(BF16)


def _memkv(mem, g, wk, wv):
    b = mem.shape[0]
    blk = pl.BlockSpec((1, N_MEM, D_MODEL), lambda i: (i, 0, 0))
    return pl.pallas_call(
        _memkv_kernel,
        grid=(b,),
        in_specs=[blk, _const_spec((1, D_MODEL)), _const_spec((D_MODEL, D_MODEL)), _const_spec((D_MODEL, D_MODEL))],
        out_specs=[pl.BlockSpec((1, N_MEM, XA_HEADS, XA_HEAD_DIM), lambda i: (i, 0, 0, 0))] * 2 + [blk, blk],
        out_shape=[jax.ShapeDtypeStruct((b, N_MEM, XA_HEADS, XA_HEAD_DIM), F32)] * 2
        + [jax.ShapeDtypeStruct((b, N_MEM, D_MODEL), BF16)] * 2,
        compiler_params=pltpu.CompilerParams(dimension_semantics=("arbitrary",), vmem_limit_bytes=VMEM_LIMIT),
        name="memkv",
    )(mem, g, wk, wv)


def _trunk_p_kernel(x_ref, kb_ref, vb_ref,
                    gmix_ref, win_ref, pw_ref, pscale_ref, cw_ref, cb_ref, wa_ref, ba_ref, wx_ref, bx_ref,
                    lam_ref, gpool_ref, grnn_ref, wout_ref,
                    gxa_ref, wq_ref, wo_ref,
                    gffn_ref, sg_ref, su_ref, sd_ref, wr_ref, rbias_ref,
                    xres_ref, hb_ref, slot_ref, gw_ref, gexp_ref, grel_ref, cover_ref, cnt_ref,
                    pool_ref, conv_ref, hT_ref,
                    pool_prev, conv_prev, h_prev, carry):
    g = pl.program_id(0)
    j = pl.program_id(1)

    @pl.when(j == 0)
    def _():
        pool_prev[...] = jnp.zeros_like(pool_prev)
        conv_prev[...] = jnp.zeros_like(conv_prev)
        h_prev[...] = jnp.zeros_like(h_prev)

    @pl.when((g == 0) & (j == 0))
    def _():
        carry[...] = jnp.zeros_like(carry)

    row = lax.broadcasted_iota(I32, (TS, 1), 0)
    pos = j * TS + row

    for sq in range(TRUNK_SEQS):
        x = x_ref[sq]

        h = _rms(x, gmix_ref[...]).astype(BF16)
        z = _dot(h, win_ref[...])
        u_pool = z[:, :D_POOL]
        u_rnn = z[:, D_POOL:D_POOL + D_RNN]
        u_gate = z[:, D_POOL + D_RNN:]

        ext = jnp.concatenate([pool_prev[sq], u_pool], axis=0)
        means = []
        for grp, w in enumerate(POOL_WINDOWS):
            s = ext[:, grp * POOL_GROUP:(grp + 1) * POOL_GROUP]
            k = 1
            while k < w:
                s = s + pltpu.roll(s, k, 0)
                k *= 2
            inv = 1.0 / jnp.minimum(pos + 1, w).astype(F32)
            means.append(s[HALO:] * inv)
        mean = jnp.concatenate(means, axis=1)
        y_pool = _pool_project(mean, u_pool, pw_ref, pscale_ref)

        extc = jnp.concatenate([conv_prev[sq], u_rnn], axis=0)
        c = u_rnn * cw_ref[CONV_WIDTH - 1:CONV_WIDTH, :]
        for k in range(1, CONV_WIDTH):
            c = c + pltpu.roll(extc, k, 0)[CONV_HALO:] * cw_ref[CONV_WIDTH - 1 - k:CONV_WIDTH - k, :]
        c = c + cb_ref[...]

        a, bt = _gates_and_decay(c, pos == 0, wa_ref, ba_ref, wx_ref, bx_ref, lam_ref)
        k = 1
        while k < TS:
            valid = row >= k
            a_sh = jnp.where(valid, pltpu.roll(a, k, 0), 1.0)
            b_sh = jnp.where(valid, pltpu.roll(bt, k, 0), 0.0)
            bt = bt + a * b_sh
            a = a * a_sh
            k *= 2
        hs = bt + a * h_prev[sq]

        pool_prev[sq] = u_pool[TS - HALO:]
        conv_prev[sq] = u_rnn[TS - CONV_HALO:]
        h_prev[sq] = hs[TS - 1:]
        pool_ref[sq] = u_pool[TS - POOL_BUF:]
        conv_ref[sq] = u_rnn[TS - (CONV_WIDTH - 1):]
        hT_ref[sq] = hs[TS - 1:]

        x1 = x + _merge_out(y_pool, hs, u_gate, gpool_ref, grnn_ref, wout_ref)

        h2 = _rms(x1, gxa_ref[...]).astype(BF16)
        q = (_dot(h2, wq_ref[...]) * (XA_HEAD_DIM ** -0.5)).astype(BF16)
        outs = []
        for hd in range(XA_HEADS):
            sl = slice(hd * XA_HEAD_DIM, (hd + 1) * XA_HEAD_DIM)
            s = _dot_nt(q[:, sl], kb_ref[sq, :, sl])
            s = s - jnp.max(s, axis=-1, keepdims=True)
            p = jnp.exp(s)
            p = p * (1.0 / jnp.sum(p, axis=-1, keepdims=True))
            outs.append(_dot(p.astype(BF16), vb_ref[sq, :, sl]))
        o = jnp.concatenate(outs, axis=1).astype(BF16)
        x2 = x1 + _dot(o, wo_ref[...])

        h3, xres = _moe_prologue(x2, gffn_ref, sg_ref, su_ref, sd_ref)
        xres_ref[sq] = xres
        hb_ref[sq] = h3.astype(BF16)
        slot_rows, w_rows, run_len, g_exp, g_rel = _route(h3, wr_ref, rbias_ref, carry[...])
        _store_rows(slot_ref.at[sq], slot_rows, I32)
        _store_rows(gw_ref.at[sq], w_rows, F32)
        gexp_ref[sq, 0] = g_exp.astype(I32)
        grel_ref[sq, 0] = g_rel.astype(I32)
        cover_ref[sq, 0] = jnp.broadcast_to(jnp.sum(run_len, axis=0, keepdims=True), cover_ref.shape[2:])
        carry[...] = carry[...] + run_len
    cnt_ref[...] = jnp.broadcast_to(carry[...], cnt_ref.shape)


def _trunk_p(x, kb, vb, mixw, xaw, moew):
    bsz, seq, _ = x.shape
    n_j = seq // TS
    q = TRUNK_SEQS
    assert bsz % q == 0
    seq_tile = lambda g, j: (g, j, 0)
    lane_tile = lambda g, j: (g, 0, j)
    per_tile = lambda g, j: (g, j, 0, 0)
    per_seq = lambda g, j: (g, 0, 0)
    weights = list(mixw) + list(xaw) + list(moew)
    in_specs = [pl.BlockSpec((q, TS, D_MODEL), seq_tile),
                pl.BlockSpec((q, N_MEM, D_MODEL), per_seq),
                pl.BlockSpec((q, N_MEM, D_MODEL), per_seq)] + [_const_spec(w.shape) for w in weights]
    lanes = _group_lanes(TS)
    out_shape = [jax.ShapeDtypeStruct((bsz, seq, D_MODEL), F32),
                 jax.ShapeDtypeStruct((bsz, seq, D_MODEL), BF16),
                 jax.ShapeDtypeStruct((bsz, TOP_K, seq), I32),
                 jax.ShapeDtypeStruct((bsz, TOP_K, seq), F32),
                 jax.ShapeDtypeStruct((bsz, n_j, 1, lanes), I32),
                 jax.ShapeDtypeStruct((bsz, n_j, 1, lanes), I32),
                 jax.ShapeDtypeStruct((bsz, n_j, 1, 128), F32),
                 jax.ShapeDtypeStruct((N_EXPERTS, 128), F32),
                 jax.ShapeDtypeStruct((bsz, POOL_BUF, D_POOL), F32),
                 jax.ShapeDtypeStruct((bsz, CONV_WIDTH - 1, D_RNN), F32),
                 jax.ShapeDtypeStruct((bsz, 1, D_RNN), F32)]
    out_specs = [pl.BlockSpec((q, TS, D_MODEL), seq_tile),
                 pl.BlockSpec((q, TS, D_MODEL), seq_tile),
                 pl.BlockSpec((q, TOP_K, TS), lane_tile),
                 pl.BlockSpec((q, TOP_K, TS), lane_tile),
                 pl.BlockSpec((q, 1, 1, lanes), per_tile),
                 pl.BlockSpec((q, 1, 1, lanes), per_tile),
                 pl.BlockSpec((q, 1, 1, 128), per_tile),
                 pl.BlockSpec((N_EXPERTS, 128), lambda g, j: (0, 0)),
                 pl.BlockSpec((q, POOL_BUF, D_POOL), per_seq),
                 pl.BlockSpec((q, CONV_WIDTH - 1, D_RNN), per_seq),
                 pl.BlockSpec((q, 1, D_RNN), per_seq)]
    return pl.pallas_call(
        _trunk_p_kernel,
        grid=(bsz // q, n_j),
        in_specs=in_specs,
        out_specs=out_specs,
        out_shape=out_shape,
        scratch_shapes=[pltpu.VMEM((q, HALO, D_POOL), F32), pltpu.VMEM((q, CONV_HALO, D_RNN), F32),
                        pltpu.VMEM((q, 1, D_RNN), F32), pltpu.VMEM((N_EXPERTS, 1), F32)],
        compiler_params=pltpu.CompilerParams(dimension_semantics=("arbitrary", "arbitrary"),
                                             vmem_limit_bytes=VMEM_LIMIT),
        name="trunk_p",
    )(x, kb, vb, *weights)


def _mix_s_kernel(x_ref, pool_ref, conv_ref, h0_ref,
                  gmix_ref, win_ref, pw_ref, pscale_ref, cw_ref, cb_ref, wa_ref, ba_ref, wx_ref, bx_ref,
                  lam_ref, gpool_ref, grnn_ref, wout_ref, gxa_ref, wq_ref,
                  x1_ref, q_ref, npool_ref, nconv_ref, nh_ref):
    x = x_ref[...]
    h = _rms(x, gmix_ref[...]).astype(BF16)
    z = _dot(h, win_ref[...])
    u_pool = z[:, :D_POOL]
    u_rnn = z[:, D_POOL:D_POOL + D_RNN]
    u_gate = z[:, D_POOL + D_RNN:]

    means = []
    for g, w in enumerate(POOL_WINDOWS):
        sl = slice(g * POOL_GROUP, (g + 1) * POOL_GROUP)
        s = u_pool[:, sl]
        for k in range(1, w):
            s = s + pool_ref[POOL_BUF - k, :, sl]
        means.append(s * (1.0 / min(w, PAST_LEN + 1)))
    mean = jnp.concatenate(means, axis=1)
    y_pool = _pool_project(mean, u_pool, pw_ref, pscale_ref)

    c = u_rnn * cw_ref[CONV_WIDTH - 1:CONV_WIDTH, :]
    for k in range(1, CONV_WIDTH):
        c = c + conv_ref[CONV_WIDTH - 1 - k] * cw_ref[CONV_WIDTH - 1 - k:CONV_WIDTH - k, :]
    c = c + cb_ref[...]
    a, bt = _gates_and_decay(c, None, wa_ref, ba_ref, wx_ref, bx_ref, lam_ref)
    hs = a * h0_ref[...] + bt

    x1 = x + _merge_out(y_pool, hs, u_gate, gpool_ref, grnn_ref, wout_ref)
    x1_ref[...] = x1
    h2 = _rms(x1, gxa_ref[...]).astype(BF16)
    q_ref[...] = _dot(h2, wq_ref[...]) * (XA_HEAD_DIM ** -0.5)

    npool_ref[:POOL_BUF - 1] = pool_ref[1:]
    npool_ref[POOL_BUF - 1] = u_pool
    nconv_ref[:CONV_WIDTH - 2] = conv_ref[1:]
    nconv_ref[CONV_WIDTH - 2] = u_rnn
    nh_ref[...] = hs


def _mix_s(x, pool, conv, h0, mixw, gxa, wq):
    bsz = x.shape[0]
    args = [x, pool, conv, h0] + list(mixw) + [gxa, wq]
    return pl.pallas_call(
        _mix_s_kernel,
        grid=(1,),
        in_specs=[_const_spec(a.shape) for a in args],
        out_specs=[_const_spec((bsz, D_MODEL)), _const_spec((bsz, D_MODEL)), _const_spec(pool.shape),
                   _const_spec(conv.shape), _const_spec((bsz, D_RNN))],
        out_shape=[jax.ShapeDtypeStruct((bsz, D_MODEL), F32), jax.ShapeDtypeStruct((bsz, D_MODEL), F32),
                   jax.ShapeDtypeStruct(pool.shape, F32), jax.ShapeDtypeStruct(conv.shape, F32),
                   jax.ShapeDtypeStruct((bsz, D_RNN), F32)],
        compiler_params=pltpu.CompilerParams(dimension_semantics=("arbitrary",), vmem_limit_bytes=VMEM_LIMIT),
        name="mix_s",
    )(*args)


ATTN_S_BB = 4


def _attn_s_kernel(q_ref, k_ref, v_ref, o_ref):
    q = q_ref[...][:, None]
    s = jnp.sum(k_ref[...] * q, axis=-1, keepdims=True)
    s = s - jnp.max(s, axis=1, keepdims=True)
    p = jnp.exp(s)
    p = p / jnp.sum(p, axis=1, keepdims=True)
    o_ref[...] = jnp.sum(p * v_ref[...], axis=1)


def _attn_s(q, k, v):
    bsz = q.shape[0]
    kv_spec = pl.BlockSpec((ATTN_S_BB, N_MEM, XA_HEADS, XA_HEAD_DIM), lambda i: (i, 0, 0, 0))
    q_spec = pl.BlockSpec((ATTN_S_BB, XA_HEADS, XA_HEAD_DIM), lambda i: (i, 0, 0))
    o = pl.pallas_call(
        _attn_s_kernel,
        grid=(bsz // ATTN_S_BB,),
        in_specs=[q_spec, kv_spec, kv_spec],
        out_specs=q_spec,
        out_shape=jax.ShapeDtypeStruct((bsz, XA_HEADS, XA_HEAD_DIM), F32),
        compiler_params=pltpu.CompilerParams(dimension_semantics=("arbitrary",), vmem_limit_bytes=VMEM_LIMIT),
        name="attn_s",
    )(q.reshape(bsz, XA_HEADS, XA_HEAD_DIM), k, v)
    return o.reshape(bsz, D_MODEL)


def _post_s_kernel(x1_ref, o_ref, wo_ref, gffn_ref, sg_ref, su_ref, sd_ref, wr_ref, rbias_ref,
                   xres_ref, hb_ref, slot_ref, gw_ref, gexp_ref, grel_ref, cover_ref, cnt_ref):
    x2 = x1_ref[...] + _dot(o_ref[...].astype(BF16), wo_ref[...])
    h3, xres = _moe_prologue(x2, gffn_ref, sg_ref, su_ref, sd_ref)
    xres_ref[...] = xres
    hb_ref[...] = h3.astype(BF16)
    slot_rows, w_rows, run_len, g_exp, g_rel = _route(h3, wr_ref, rbias_ref, jnp.zeros((N_EXPERTS, 1), F32))
    _store_rows(slot_ref, slot_rows, I32)
    _store_rows(gw_ref, w_rows, F32)
    gexp_ref[0] = g_exp.astype(I32)
    grel_ref[0] = g_rel.astype(I32)
    cover_ref[0] = jnp.broadcast_to(jnp.sum(run_len, axis=0, keepdims=True), cover_ref.shape[1:])
    cnt_ref[...] = jnp.broadcast_to(run_len, cnt_ref.shape)


def _post_s(x1, o, wo, moew):
    bsz = x1.shape[0]
    args = [x1, o, wo] + list(moew)
    out_shape = [jax.ShapeDtypeStruct((bsz, D_MODEL), F32),
                 jax.ShapeDtypeStruct((bsz, D_MODEL), BF16),
                 jax.ShapeDtypeStruct((TOP_K, bsz), I32),
                 jax.ShapeDtypeStruct((TOP_K, bsz), F32),
                 jax.ShapeDtypeStruct((1, 1, _group_lanes(bsz)), I32),
                 jax.ShapeDtypeStruct((1, 1, _group_lanes(bsz)), I32),
                 jax.ShapeDtypeStruct((1, 1, 128), F32),
                 jax.ShapeDtypeStruct((N_EXPERTS, 128), F32)]
    return pl.pallas_call(
        _post_s_kernel,
        grid=(1,),
        in_specs=[_const_spec(a.shape) for a in args],
        out_specs=[_const_spec(s.shape) for s in out_shape],
        out_shape=out_shape,
        compiler_params=pltpu.CompilerParams(dimension_semantics=("arbitrary",), vmem_limit_bytes=VMEM_LIMIT),
        name="post_s",
    )(*args)


def _plan_kernel(rp_ref, rs_ref, base_p_ref, base_s_ref, fill_at_ref, fill_n_ref, exp_ref, nreal_ref):
    n_blocks = exp_ref.shape[0]
    shift = BM.bit_length() - 1

    def per_expert(e, carry):
        blk0, last_e = carry
        rows = rp_ref[e] + rs_ref[e]
        start = lax.shift_left(blk0, shift)
        n_blk = lax.shift_right_logical(rows + (BM - 1), shift)
        base_p_ref[e] = start
        base_s_ref[e] = start + rp_ref[e]
        fill_at_ref[e] = start + rows
        fill_n_ref[e] = _groups(lax.shift_left(n_blk, shift) - rows)

        def per_block(j, _):
            exp_ref[blk0 + j] = e
            return _

        lax.fori_loop(0, n_blk, per_block, 0)
        return blk0 + n_blk, jnp.where(rows > 0, e, last_e)

    n_real, last_e = lax.fori_loop(0, N_EXPERTS, per_expert, (jnp.int32(0), jnp.int32(0)))
    nreal_ref[0] = n_real

    def rest(b, _):
        exp_ref[b] = last_e
        return _

    lax.fori_loop(n_real, n_blocks, rest, 0)


def _plan(rows_p, rows_s, n_rows):
    assert BM & (BM - 1) == 0 and n_rows % BM == 0
    smem = pl.BlockSpec(memory_space=pltpu.SMEM)
    return pl.pallas_call(
        _plan_kernel,
        in_specs=[smem, smem],
        out_specs=[smem] * 6,
        out_shape=[jax.ShapeDtypeStruct((N_EXPERTS,), I32)] * 4
        + [jax.ShapeDtypeStruct((n_rows // BM,), I32), jax.ShapeDtypeStruct((1,), I32)],
        name="plan",
    )(rows_p, rows_s)


def _groups(n_rows):
    return lax.shift_right_logical(n_rows, RUN_ALIGN.bit_length() - 1)


GROUP_UNROLL = 8


def _for_each_group(n_groups, grow_ref, fn):
    def one(g, priority):
        fn(pl.multiple_of(g * RUN_ALIGN, RUN_ALIGN), pl.multiple_of(grow_ref[0, 0, g], RUN_ALIGN), priority)

    def several(j, _):
        for u in range(GROUP_UNROLL):
            one(j * GROUP_UNROLL + u, u % 2)
        return _

    def single(g, _):
        one(g, 0)
        return _

    n_full = lax.shift_right_logical(n_groups, GROUP_UNROLL.bit_length() - 1)
    lax.fori_loop(0, n_full, several, 0)
    lax.fori_loop(n_full * GROUP_UNROLL, n_groups, single, 0)


def _wait_groups(n_groups, group_wait, bulk_wait):
    def bulk(j, _):
        bulk_wait()
        return _

    def single(g, _):
        group_wait()
        return _

    n_full = lax.shift_right_logical(n_groups, GROUP_UNROLL.bit_length() - 1)
    lax.fori_loop(0, n_full, bulk, 0)
    lax.fori_loop(n_full * GROUP_UNROLL, n_groups, single, 0)


def _group_rows_kernel(gexp_ref, grel_ref, base_ref, out_ref):
    rows = grel_ref[...]
    e = gexp_ref[...]
    for ex in range(N_EXPERTS):
        rows = rows + jnp.where(e == ex, base_ref[ex], 0)
    out_ref[...] = rows


def _group_rows(gexp, grel, base):
    shape = gexp.shape
    flat = (shape[0], shape[-1])
    out = pl.pallas_call(
        _group_rows_kernel,
        in_specs=[pl.BlockSpec(memory_space=pltpu.VMEM), pl.BlockSpec(memory_space=pltpu.VMEM),
                  pl.BlockSpec(memory_space=pltpu.SMEM)],
        out_specs=pl.BlockSpec(memory_space=pltpu.VMEM),
        out_shape=jax.ShapeDtypeStruct(flat, I32),
        name="group_rows",
    )(gexp.reshape(flat), grel.reshape(flat), base)
    return out.reshape(shape)


def _dispatch_kernel(cover_p, cover_s, fill_at, fill_n, nreal_ref, grow_p, grow_s,
                     slot_p_ref, h_p_ref, slot_s_ref, h_s_ref, xs_ref, sbuf, zbuf, sem):
    i = pl.program_id(0)
    last = pl.num_programs(0) - 1
    cur = lax.rem(i, 2)

    def drain(s, n_rows):
        def wait_rows(n):
            pltpu.make_async_copy(sbuf.at[s, pl.ds(0, n)], xs_ref.at[pl.ds(0, n)], sem.at[s]).wait()
        _wait_groups(_groups(n_rows), lambda: wait_rows(RUN_ALIGN), lambda: wait_rows(RUN_ALIGN * GROUP_UNROLL))

    def tile(n_rows, grow_ref, slot_ref, h_ref):
        def sort_chunk(c):
            rid = (c * SORT_CHUNK + lax.broadcasted_iota(I32, (SORT_CHUNK, 1), 0)).astype(jnp.int16)
            slots = slot_ref[...].astype(jnp.int16)
            hit = rid == slots[0:1, :]
            for k in range(1, TOP_K):
                hit = hit | (rid == slots[k:k + 1, :])
            sbuf[cur, c * SORT_CHUNK:(c + 1) * SORT_CHUNK] = _dot(hit.astype(BF16), h_ref[...]).astype(BF16)

        def send(local, glob, priority):
            pltpu.make_async_copy(sbuf.at[cur, pl.ds(local, RUN_ALIGN)], xs_ref.at[pl.ds(glob, RUN_ALIGN)],
                                  sem.at[cur]).start(priority=priority)

        always = h_ref.shape[0] * TOP_K // SORT_CHUNK
        for c in range(_sorted_rows(h_ref.shape[0]) // SORT_CHUNK):
            if c < always:
                sort_chunk(c)
            else:
                pl.when(c * SORT_CHUNK < n_rows)(functools.partial(sort_chunk, c))
        _for_each_group(_groups(n_rows), grow_ref, send)

    @pl.when(i >= 2)
    def _():
        drain(cur, cover_p[i - 2])

    @pl.when(i < last)
    def _():
        tile(cover_p[i], grow_p, slot_p_ref, h_p_ref)

    @pl.when(i == last)
    def _():
        tile(cover_s[0], grow_s, slot_s_ref, h_s_ref)
        drain(1 - cur, cover_p[last - 1])
        drain(cur, cover_s[0])
        zbuf[...] = jnp.zeros_like(zbuf)

        def group_fill(e, g):
            return pltpu.make_async_copy(
                zbuf.at[pl.ds(0, RUN_ALIGN)],
                xs_ref.at[pl.ds(pl.multiple_of(fill_at[e] + g * RUN_ALIGN, RUN_ALIGN), RUN_ALIGN)], sem.at[2])

        def per_expert(e, n):
            def start(g, _):
                group_fill(e, g).start()
                return _
            lax.fori_loop(0, fill_n[e], start, 0)
            return n + fill_n[e]

        n_fill = lax.fori_loop(0, N_EXPERTS, per_expert, jnp.int32(0))

        def wait_group(g, _):
            group_fill(0, 0).wait()
            return _

        lax.fori_loop(0, n_fill, wait_group, 0)

        n_real = nreal_ref[0]
        n_tail = xs_ref.shape[0] // BM - n_real

        def blk_fill(b):
            return pltpu.make_async_copy(zbuf, xs_ref.at[pl.ds(pl.multiple_of((n_real + b) * BM, BM), BM)], sem.at[2])

        def start_blk(b, _):
            blk_fill(b).start()
            return _

        def wait_blk(b, _):
            blk_fill(b).wait()
            return _

        lax.fori_loop(0, n_tail, start_blk, 0)
        lax.fori_loop(0, n_tail, wait_blk, 0)


def _dispatch(cover_p, cover_s, fills, grow_p, grow_s, slot_p, hb_p, slot_s, hb_s, n_rows):
    n_p = hb_p.shape[0] // TS
    smem = pl.BlockSpec(memory_space=pltpu.SMEM)
    clamp = lambda i: jnp.minimum(i, n_p - 1)
    per_tile = pl.BlockSpec((1, 1, grow_p.shape[-1]), lambda i: (clamp(i), 0, 0), memory_space=pltpu.SMEM)
    return pl.pallas_call(
        _dispatch_kernel,
        grid=(n_p + 1,),
        in_specs=[smem] * 5 + [per_tile, smem,
                               pl.BlockSpec((TOP_K, TS), lambda i: (0, clamp(i))),
                               pl.BlockSpec((TS, D_MODEL), lambda i: (clamp(i), 0)),
                               _const_spec(slot_s.shape), _const_spec(hb_s.shape)],
        out_specs=pl.BlockSpec(memory_space=pl.ANY),
        out_shape=jax.ShapeDtypeStruct((n_rows, D_MODEL), BF16),
        scratch_shapes=[pltpu.VMEM((2, _sorted_rows(TS), D_MODEL), BF16),
                        pltpu.VMEM((BM, D_MODEL), BF16), pltpu.SemaphoreType.DMA((3,))],
        compiler_params=pltpu.CompilerParams(dimension_semantics=("arbitrary",), has_side_effects=True,
                                             vmem_limit_bytes=VMEM_LIMIT),
        name="dispatch",
    )(cover_p, cover_s, *fills, grow_p, grow_s, slot_p, hb_p, slot_s, hb_s)


GMM_LANES = 2


def _gmm_kernel(exp_ref, nreal_ref, *refs):
    x_ref = refs[0]
    w_refs = refs[1:1 + 3 * GMM_LANES]
    y_ref = refs[1 + 3 * GMM_LANES]
    scratch = refs[2 + 3 * GMM_LANES:]
    s = pl.program_id(0)
    n_real = nreal_ref[0]

    for lane in range(GMM_LANES):
        wg_ref, wu_ref, wd_ref = w_refs[3 * lane:3 * lane + 3]
        wgu, wdn = scratch[2 * lane:2 * lane + 2]
        b = s * GMM_LANES + lane
        new_expert = (s == 0) | (exp_ref[b] != exp_ref[jnp.maximum(b - GMM_LANES, 0)])

        @pl.when((b < n_real) & new_expert)
        def _():
            wgu[:, :D_EXPERT] = wg_ref[0].astype(BF16)
            wgu[:, D_EXPERT:] = wu_ref[0].astype(BF16)
            wdn[...] = wd_ref[0].astype(BF16)

    @pl.when(s * GMM_LANES < n_real)
    def _():
        for lane in range(GMM_LANES):
            wgu, wdn = scratch[2 * lane:2 * lane + 2]
            rows = slice(lane * BM, (lane + 1) * BM)
            gu = _dot(x_ref[rows, :], wgu[...])
            act = jax.nn.silu(gu[:, :D_EXPERT]) * gu[:, D_EXPERT:]
            y = _dot(act.astype(BF16), wdn[...]).astype(BF16)
            y_ref[rows, :] = jnp.where(s * GMM_LANES + lane < n_real, y, jnp.zeros_like(y))

    @pl.when(s * GMM_LANES >= n_real)
    def _():
        y_ref[...] = jnp.zeros_like(y_ref)


def _gmm(blk_exp, n_real, xs, wg, wu, wd):
    n_rows = xs.shape[0]
    step_rows = BM * GMM_LANES
    assert n_rows % step_rows == 0
    last_step = lambda nreal: (nreal[0] - 1) // GMM_LANES
    w_specs, scratch = [], []
    for lane in range(GMM_LANES):
        weight = lambda s, exp, nreal, lane=lane: (exp[s * GMM_LANES + lane], 0, 0)
        w_specs += [pl.BlockSpec((1, D_MODEL, D_EXPERT), weight), pl.BlockSpec((1, D_MODEL, D_EXPERT), weight),
                    pl.BlockSpec((1, D_EXPERT, D_MODEL), weight)]
        scratch += [pltpu.VMEM((D_MODEL, 2 * D_EXPERT), BF16), pltpu.VMEM((D_EXPERT, D_MODEL), BF16)]
    grid_spec = pltpu.PrefetchScalarGridSpec(
        num_scalar_prefetch=2,
        grid=(n_rows // step_rows,),
        in_specs=[pl.BlockSpec((step_rows, D_MODEL), lambda s, exp, nreal: (jnp.minimum(s, last_step(nreal)), 0))]
        + w_specs,
        out_specs=pl.BlockSpec((step_rows, D_MODEL), lambda s, exp, nreal: (s, 0)),
        scratch_shapes=scratch,
    )
    return pl.pallas_call(
        _gmm_kernel,
        grid_spec=grid_spec,
        out_shape=jax.ShapeDtypeStruct((n_rows, D_MODEL), BF16),
        compiler_params=pltpu.CompilerParams(dimension_semantics=("arbitrary",), vmem_limit_bytes=VMEM_LIMIT),
        name="gmm",
    )(blk_exp, n_real, xs, *([wg, wu, wd] * GMM_LANES))


def _combine_kernel(cover_ref, grow_ref, grow_next_ref,
                    slot_ref, gw_ref, xres_ref, gfin_ref, ys_ref, out_ref, ybuf, sem):
    i = pl.program_id(0)
    n = pl.num_programs(0)
    cur = lax.rem(i, 2)
    rows = ybuf.shape[1]

    def rows_copy(s, local, glob, n_rows):
        return pltpu.make_async_copy(ys_ref.at[pl.ds(glob, n_rows)], ybuf.at[s, pl.ds(local, n_rows)], sem.at[s])

    def gather(n_rows, gr_ref, s):
        _for_each_group(_groups(n_rows), gr_ref,
                        lambda l, g, priority: rows_copy(s, l, g, RUN_ALIGN).start(priority=priority))

    @pl.when(i == 0)
    def _():
        ybuf[...] = jnp.zeros_like(ybuf)
        gather(cover_ref[0], grow_ref, 0)

    @pl.when(i + 1 < n)
    def _():
        gather(cover_ref[jnp.minimum(i + 1, n - 1)], grow_next_ref, 1 - cur)

    _wait_groups(_groups(cover_ref[i]), lambda: rows_copy(cur, 0, 0, RUN_ALIGN).wait(),
                 lambda: rows_copy(cur, 0, 0, RUN_ALIGN * GROUP_UNROLL).wait())

    tm = slot_ref.shape[0]
    slot_b = [jnp.broadcast_to(slot_ref[:, k:k + 1], (tm, 128)).astype(jnp.int16) for k in range(TOP_K)]
    gw_b = [jnp.broadcast_to(gw_ref[:, k:k + 1], (tm, 128)).astype(BF16) for k in range(TOP_K)]
    lane = lax.broadcasted_iota(I32, (1, 128), 1)

    def chunk_sum(c):
        pieces = []
        for p in range(SORT_CHUNK // 128):
            col = (lane + (c * SORT_CHUNK + p * 128)).astype(jnp.int16)
            w = jnp.zeros((tm, 128), BF16)
            for k in range(TOP_K):
                w = jnp.where(slot_b[k] == col, gw_b[k], w)
            pieces.append(w)
        return _dot(jnp.concatenate(pieces, axis=1), ybuf[cur, c * SORT_CHUNK:(c + 1) * SORT_CHUNK])

    always = tm * TOP_K // SORT_CHUNK
    acc = xres_ref[...]
    for c in range(always):
        acc = acc + chunk_sum(c)
    out_ref[...] = acc
    for c in range(always, rows // SORT_CHUNK):
        @pl.when(c * SORT_CHUNK < cover_ref[i])
        def _():
            out_ref[...] += chunk_sum(c)
    out_ref[...] = _rms(out_ref[...], gfin_ref[...])


def _combine(cover, grow, slot_t, gw_t, xres, gfin, ys):
    t = xres.shape[0]
    tm = min(t, TS)
    n = t // tm
    smem = pl.BlockSpec(memory_space=pltpu.SMEM)
    lanes = grow.shape[-1]
    this_tile = pl.BlockSpec((1, 1, lanes), lambda i: (i, 0, 0), memory_space=pltpu.SMEM)
    next_tile = pl.BlockSpec((1, 1, lanes), lambda i: (jnp.minimum(i + 1, n - 1), 0, 0), memory_space=pltpu.SMEM)
    return pl.pallas_call(
        _combine_kernel,
        grid=(n,),
        in_specs=[smem, this_tile, next_tile,
                  pl.BlockSpec((tm, TOP_K), lambda i: (i, 0)),
                  pl.BlockSpec((tm, TOP_K), lambda i: (i, 0)),
                  pl.BlockSpec((tm, D_MODEL), lambda i: (i, 0)),
                  _const_spec((1, D_MODEL)),
                  pl.BlockSpec(memory_space=pl.ANY)],
        out_specs=pl.BlockSpec((tm, D_MODEL), lambda i: (i, 0)),
        out_shape=jax.ShapeDtypeStruct((t, D_MODEL), F32),
        scratch_shapes=[pltpu.VMEM((2, _sorted_rows(tm), D_MODEL), BF16), pltpu.SemaphoreType.DMA((2,))],
        compiler_params=pltpu.CompilerParams(dimension_semantics=("arbitrary",), vmem_limit_bytes=VMEM_LIMIT),
        name="combine",
    )(cover, grow, grow, slot_t, gw_t, xres, gfin, ys)


def _block_diag_pairs(w):
    n_h, d, _ = w.shape
    half = n_h // 2
    on_diag = jnp.eye(half, dtype=bool)[None, :, None, :, None]
    blocks = w.reshape(2, half, d, 1, d)
    out = jnp.where(on_diag, blocks, 0.0)
    return out.reshape(2, half * d, half * d).astype(BF16)


def kernel(x_prompt, x_sample, state_pool, state_conv, state_h, cache_mem_k, cache_mem_v, mem_prompt, norm_mix, w_in, pool_w, pool_scale, conv_w, conv_b, gate_a_w, gate_a_b, gate_x_w, gate_x_b, lru_lambda, norm_pool_out, norm_rnn_out, w_out, norm_xattn, norm_mem, xa_wq, xa_wk, xa_wv, xa_wo, norm_ffn, router_w, router_bias, exp_w_gate, exp_w_up, exp_w_down, sh_w_gate, sh_w_up, sh_w_down, norm_final):
    bp, seq, _ = x_prompt.shape
    bs = x_sample.shape[0]
    tp = bp * seq
    row = lambda v: v.reshape(1, -1)
    bf = lambda v: v.astype(BF16)

    mixw = (row(norm_mix[0]), bf(w_in[0]), _block_diag_pairs(pool_w[0]), row(pool_scale[0]), conv_w[0],
            row(conv_b[0]), _block_diag_pairs(gate_a_w[0]), row(gate_a_b[0]), _block_diag_pairs(gate_x_w[0]),
            row(gate_x_b[0]), row(lru_lambda[0]), row(norm_pool_out[0]), row(norm_rnn_out[0]), bf(w_out[0]))
    xaw = (row(norm_xattn[0]), bf(xa_wq[0]), bf(xa_wo[0]))
    moew = (row(norm_ffn[0]), bf(sh_w_gate[0]), bf(sh_w_up[0]), bf(sh_w_down[0]), router_w[0].T,
            router_bias[0].reshape(N_EXPERTS, 1))

    mk, mv, kb, vb = _memkv(mem_prompt, row(norm_mem[0]), bf(xa_wk[0]), bf(xa_wv[0]))
    (xres_p, hb_p, slot_p, gw_p, gexp_p, grel_p, cover_p, cnt_p, pool_p, conv_p, h_p) = _trunk_p(
        x_prompt, kb, vb, mixw, xaw, moew)

    x1_s, q_s, pool_s, conv_s, h_s = _mix_s(x_sample.reshape(bs, D_MODEL), state_pool[0].transpose(1, 0, 2),
                                            state_conv[0].transpose(1, 0, 2), state_h[0], mixw, xaw[0], xaw[1])
    pool_s = pool_s.transpose(1, 0, 2)
    conv_s = conv_s.transpose(1, 0, 2)
    o_s = _attn_s(q_s, cache_mem_k[0], cache_mem_v[0])
    xres_s, hb_s, slot_s, gw_s, gexp_s, grel_s, cover_s, cnt_s = _post_s(x1_s, o_s, xaw[2], moew)

    n_tiles = tp // TS + 1
    n_rows = _round_up((tp + bs) * TOP_K + n_tiles * N_EXPERTS * (RUN_ALIGN - 1) + N_EXPERTS * (BM - 1),
                       BM * GMM_LANES)
    ints = lambda v: v[..., 0].astype(I32)
    base_p, base_s, fill_at, fill_n, blk_exp, n_real = _plan(ints(cnt_p), ints(cnt_s), n_rows)
    cover_p, cover_s = ints(cover_p).reshape(-1), ints(cover_s).reshape(-1)
    xres_p = xres_p.reshape(tp, D_MODEL)
    hb_p = hb_p.reshape(tp, D_MODEL)
    slot_p = slot_p.transpose(1, 0, 2).reshape(TOP_K, tp)
    gw_p = gw_p.transpose(1, 0, 2).reshape(TOP_K, tp)
    gexp_p = gexp_p.reshape(-1, 1, gexp_p.shape[-1])
    grel_p = grel_p.reshape(-1, 1, grel_p.shape[-1])
    grow_p = _group_rows(gexp_p, grel_p, base_p)
    grow_s = _group_rows(gexp_s, grel_s, base_s)

    xs = _dispatch(cover_p, cover_s, (fill_at, fill_n, n_real), grow_p, grow_s, slot_p, hb_p, slot_s, hb_s, n_rows)
    ys = _gmm(blk_exp, n_real, xs, exp_w_gate[0], exp_w_up[0], exp_w_down[0])

    gfin = row(norm_final)
    y_p = _combine(cover_p, grow_p, slot_p.T, gw_p.T, xres_p, gfin, ys)
    y_s = _combine(cover_s, grow_s, slot_s.T, gw_s.T, xres_s, gfin, ys)

    return (y_p.reshape(bp, seq, D_MODEL), y_s.reshape(bs, 1, D_MODEL),
            pool_p[None], conv_p[None], h_p.reshape(1, bp, D_RNN),
            mk[None], mv[None],
            pool_s[None], conv_s[None], h_s[None])
```

```python
import functools

import jax
import jax.numpy as jnp
from jax import lax
from jax.experimental import pallas as pl
from jax.experimental.pallas import tpu as pltpu

F32 = jnp.float32
BF16 = jnp.bfloat16
I32 = jnp.int32

D_MODEL = 1024
D_POOL = 512
D_RNN = 512
D_IN = D_POOL + 2 * D_RNN
POOL_WINDOWS = (2, 4, 8, 16)
POOL_GROUP = 128
POOL_BUF = 15
CONV_WIDTH = 4
LRU_C = 8.0
N_MEM = 256
XA_HEADS = 4
XA_HEAD_DIM = 256
N_EXPERTS = 64
TOP_K = 8
N_EXPERT_GROUPS = 8
GROUP_SIZE = N_EXPERTS // N_EXPERT_GROUPS
TOPK_GROUPS = 4
D_EXPERT = 256
ROUTED_SCALE = 2.5
EPS = 1e-6
PAST_LEN = 16384

HALO = 16
CONV_HALO = 8
TS = 256
TRUNK_SEQS = 1
BM = 512
RUN_ALIGN = 16
SORT_CHUNK = 512
VMEM_LIMIT = 56 * 1024 * 1024


def _round_up(x, m):
    return (x + m - 1) // m * m


def _sorted_rows(tokens):
    return _round_up(tokens * TOP_K + N_EXPERTS * (RUN_ALIGN - 1), SORT_CHUNK)


assert _sorted_rows(TS) < 2 ** 15


def _group_lanes(tokens):
    return _round_up(_sorted_rows(tokens) // RUN_ALIGN, 128)


def _const_spec(shape):
    nd = len(shape)
    return pl.BlockSpec(shape, lambda *_: (0,) * nd, pipeline_mode=pl.Buffered(1))


def _rms(x, g):
    ms = jnp.mean(x * x, axis=-1, keepdims=True)
    return x * lax.rsqrt(ms + EPS) * g


def _dot(a, b):
    return jnp.dot(a, b, preferred_element_type=F32)


def _dot_nt(a, b, precision=None):
    return lax.dot_general(a, b, (((1,), (1,)), ((), ())), precision=precision,
                           preferred_element_type=F32)


def _split_bf16(x):
    hi = x.astype(BF16)
    return hi, (x - hi.astype(F32)).astype(BF16)


def _softplus(x):
    return jnp.maximum(x, 0.0) + jnp.log1p(jnp.exp(-jnp.abs(x)))


def _gates_and_decay(c, pos_is_zero, wa_ref, ba_ref, wx_ref, bx_ref, lam_ref):
    cb = c.astype(BF16)
    half = D_RNN // 2
    ga = jnp.concatenate([_dot(cb[:, :half], wa_ref[0]), _dot(cb[:, half:], wa_ref[1])], axis=1) + ba_ref[...]
    gx = jnp.concatenate([_dot(cb[:, :half], wx_ref[0]), _dot(cb[:, half:], wx_ref[1])], axis=1) + bx_ref[...]
    r = jax.nn.sigmoid(ga)
    i = jax.nn.sigmoid(gx)
    log_a = (-LRU_C) * r * _softplus(-lam_ref[...])
    a = jnp.exp(log_a)
    m2 = 1.0 - a * a
    mult = jnp.where(m2 > 0.0, m2 * lax.rsqrt(m2), 0.0)
    if pos_is_zero is not None:
        mult = jnp.where(pos_is_zero, 1.0, mult)
    return a, mult * i * c


def _pool_project(mean, u_pool, pw_ref, pscale_ref):
    d = (mean - u_pool).astype(BF16)
    half = D_POOL // 2
    y = jnp.concatenate([_dot(d[:, :half], pw_ref[0]), _dot(d[:, half:], pw_ref[1])], axis=1)
    return y * pscale_ref[...]


def _merge_out(y_pool, hs, u_gate, gpool_ref, grnn_ref, wout_ref):
    y_rnn = hs * jax.nn.gelu(u_gate)
    merged = jnp.concatenate([_rms(y_pool, gpool_ref[...]), _rms(y_rnn, grnn_ref[...])], axis=1)
    return _dot(merged.astype(BF16), wout_ref[...])


def _route(h3, wr_ref, rbias_ref, before):
    r_tok = h3.shape[0]
    w_hi, w_lo = _split_bf16(wr_ref[...])
    h_hi, h_lo = _split_bf16(h3)
    logits = _dot_nt(w_hi, h_hi) + (_dot_nt(w_hi, h_lo) + _dot_nt(w_lo, h_hi))
    scores = jax.nn.sigmoid(logits)
    biased = scores + rbias_ref[...]
    neg = jnp.float32(-jnp.inf)
    gs = []
    for g in range(N_EXPERT_GROUPS):
        xg = biased[g * GROUP_SIZE:(g + 1) * GROUP_SIZE]
        m1 = jnp.max(xg, axis=0, keepdims=True)
        eq = xg == m1
        cnt = jnp.sum(eq.astype(F32), axis=0, keepdims=True)
        m2 = jnp.max(jnp.where(eq, neg, xg), axis=0, keepdims=True)
        gs.append(m1 + jnp.where(cnt >= 2.0, m1, m2))
    pieces = []
    for g in range(N_EXPERT_GROUPS):
        beaten = jnp.zeros_like(gs[g])
        for o in range(N_EXPERT_GROUPS):
            if o == g:
                continue
            wins = (gs[o] > gs[g]) | (gs[o] == gs[g]) if o < g else (gs[o] > gs[g])
            beaten = beaten + wins.astype(F32)
        keep = beaten < float(TOPK_GROUPS)
        xg = biased[g * GROUP_SIZE:(g + 1) * GROUP_SIZE]
        pieces.append(jnp.where(keep, xg, neg))
    cur = jnp.concatenate(pieces, axis=0)
    eid = lax.broadcasted_iota(I32, (N_EXPERTS, r_tok), 0).astype(F32)
    idx_rows, score_rows = [], []
    sel = jnp.zeros((N_EXPERTS, r_tok), F32)
    for _ in range(TOP_K):
        m = jnp.max(cur, axis=0, keepdims=True)
        idx = jnp.min(jnp.where(cur == m, eid, float(N_EXPERTS)), axis=0, keepdims=True)
        oh = eid == idx
        score_rows.append(jnp.sum(jnp.where(oh, scores, 0.0), axis=0, keepdims=True))
        idx_rows.append(idx)
        sel = sel + oh.astype(F32)
        cur = jnp.where(oh, neg, cur)
    tot = score_rows[0]
    for s in score_rows[1:]:
        tot = tot + s
    w_rows = [s / tot * ROUTED_SCALE for s in score_rows]
    rr = lax.broadcasted_iota(I32, (r_tok, r_tok), 0)
    cc = lax.broadcasted_iota(I32, (r_tok, r_tok), 1)
    earlier = _dot(sel.astype(BF16), (rr < cc).astype(BF16))
    counts = jnp.sum(sel, axis=1, keepdims=True)
    run_len = jnp.floor((counts + (RUN_ALIGN - 1.0)) * (1.0 / RUN_ALIGN)) * RUN_ALIGN
    er = lax.broadcasted_iota(I32, (N_EXPERTS, N_EXPERTS), 0)
    ec = lax.broadcasted_iota(I32, (N_EXPERTS, N_EXPERTS), 1)
    run_start = _dot((ec < er).astype(BF16), jnp.broadcast_to(run_len, (N_EXPERTS, 128)).astype(BF16))[:, :1]
    slot = earlier + run_start
    slot_rows = [jnp.sum(jnp.where(eid == idx, slot, 0.0), axis=0, keepdims=True) for idx in idx_rows]
    n_lanes = _group_lanes(r_tok)
    g_row = lax.broadcasted_iota(I32, (N_EXPERTS, n_lanes), 1).astype(F32) * float(RUN_ALIGN)
    owns = (run_start <= g_row) & (g_row < run_start + run_len)
    e_col = lax.broadcasted_iota(I32, (N_EXPERTS, n_lanes), 0).astype(F32)
    g_exp = jnp.sum(jnp.where(owns, e_col, 0.0), axis=0, keepdims=True)
    g_rel = jnp.sum(jnp.where(owns, before + g_row - run_start, 0.0), axis=0, keepdims=True)
    return slot_rows, w_rows, run_len, g_exp, g_rel


def _moe_prologue(x2, gffn_ref, sg_ref, su_ref, sd_ref):
    h3 = _rms(x2, gffn_ref[...])
    h3b = h3.astype(BF16)
    act = jax.nn.silu(_dot(h3b, sg_ref[...])) * _dot(h3b, su_ref[...])
    shared = _dot(act.astype(BF16), sd_ref[...])
    return h3, x2 + shared


def _store_rows(ref, rows, dtype):
    for k, row in enumerate(rows):
        ref[k:k + 1, :] = row.astype(dtype)


def _memkv_kernel(mem_ref, g_ref, wk_ref, wv_ref, k_ref, v_ref, kb_ref, vb_ref):
    m = _rms(mem_ref[0], g_ref[...]).astype(BF16)
    k = _dot(m, wk_ref[...])
    v = _dot(m, wv_ref[...])
    k_ref[0] = k.reshape(N_MEM, XA_HEADS, XA_HEAD_DIM)
    v_ref[0] = v.reshape(N_MEM, XA_HEADS, XA_HEAD_DIM)
    kb_ref[0] = k.T.astype(BF16)
    vb_ref[0] = v.astype(BF16)


def _memkv(mem, g, wk, wv):
    b = mem.shape[0]
    blk = pl.BlockSpec((1, N_MEM, D_MODEL), lambda i: (i, 0, 0))
    return pl.pallas_call(
        _memkv_kernel,
        grid=(b,),
        in_specs=[blk, _const_spec((1, D_MODEL)), _const_spec((D_MODEL, D_MODEL)), _const_spec((D_MODEL, D_MODEL))],
        out_specs=[pl.BlockSpec((1, N_MEM, XA_HEADS, XA_HEAD_DIM), lambda i: (i, 0, 0, 0))] * 2
        + [pl.BlockSpec((1, D_MODEL, N_MEM), lambda i: (i, 0, 0)), blk],
        out_shape=[jax.ShapeDtypeStruct((b, N_MEM, XA_HEADS, XA_HEAD_DIM), F32)] * 2
        + [jax.ShapeDtypeStruct((b, D_MODEL, N_MEM), BF16), jax.ShapeDtypeStruct((b, N_MEM, D_MODEL), BF16)],
        compiler_params=pltpu.CompilerParams(dimension_semantics=("arbitrary",), vmem_limit_bytes=VMEM_LIMIT),
        name="memkv",
    )(mem, g, wk, wv)


def _trunk_p_kernel(x_ref, kb_ref, vb_ref,
                    gmix_ref, win_ref, pw_ref, pscale_ref, cw_ref, cb_ref, wa_ref, ba_ref, wx_ref, bx_ref,
                    lam_ref, gpool_ref, grnn_ref, wout_ref,
                    gxa_ref, wq_ref, wo_ref,
                    gffn_ref, sg_ref, su_ref, sd_ref, wr_ref, rbias_ref,
                    xres_ref, hb_ref, slot_ref, gw_ref, gexp_ref, grel_ref, cover_ref, cnt_ref,
                    pool_ref, conv_ref, hT_ref,
                    pool_prev, conv_prev, h_prev, carry):
    g = pl.program_id(0)
    j = pl.program_id(1)

    @pl.when(j == 0)
    def _():
        pool_prev[...] = jnp.zeros_like(pool_prev)
        conv_prev[...] = jnp.zeros_like(conv_prev)
        h_prev[...] = jnp.zeros_like(h_prev)

    @pl.when((g == 0) & (j == 0))
    def _():
        carry[...] = jnp.zeros_like(carry)

    row = lax.broadcasted_iota(I32, (TS, 1), 0)
    pos = j * TS + row

    for sq in range(TRUNK_SEQS):
        x = x_ref[sq]

        h = _rms(x, gmix_ref[...]).astype(BF16)
        z = _dot(h, win_ref[...])
        u_pool = z[:, :D_POOL]
        u_rnn = z[:, D_POOL:D_POOL + D_RNN]
        u_gate = z[:, D_POOL + D_RNN:]

        ext = jnp.concatenate([pool_prev[sq], u_pool], axis=0)
        means = []
        for grp, w in enumerate(POOL_WINDOWS):
            s = ext[:, grp * POOL_GROUP:(grp + 1) * POOL_GROUP]
            k = 1
            while k < w:
                s = s + pltpu.roll(s, k, 0)
                k *= 2
            inv = 1.0 / jnp.minimum(pos + 1, w).astype(F32)
            means.append(s[HALO:] * inv)
        mean = jnp.concatenate(means, axis=1)
        y_pool = _pool_project(mean, u_pool, pw_ref, pscale_ref)

        extc = jnp.concatenate([conv_prev[sq], u_rnn], axis=0)
        c = u_rnn * cw_ref[CONV_WIDTH - 1:CONV_WIDTH, :]
        for k in range(1, CONV_WIDTH):
            c = c + pltpu.roll(extc, k, 0)[CONV_HALO:] * cw_ref[CONV_WIDTH - 1 - k:CONV_WIDTH - k, :]
        c = c + cb_ref[...]

        a, bt = _gates_and_decay(c, pos == 0, wa_ref, ba_ref, wx_ref, bx_ref, lam_ref)
        k = 1
        while k < TS:
            valid = row >= k
            a_sh = jnp.where(valid, pltpu.roll(a, k, 0), 1.0)
            b_sh = jnp.where(valid, pltpu.roll(bt, k, 0), 0.0)
            bt = bt + a * b_sh
            a = a * a_sh
            k *= 2
        hs = bt + a * h_prev[sq]

        pool_prev[sq] = u_pool[TS - HALO:]
        conv_prev[sq] = u_rnn[TS - CONV_HALO:]
        h_prev[sq] = hs[TS - 1:]
        pool_ref[sq] = u_pool[TS - POOL_BUF:]
        conv_ref[sq] = u_rnn[TS - (CONV_WIDTH - 1):]
        hT_ref[sq] = hs[TS - 1:]

        x1 = x + _merge_out(y_pool, hs, u_gate, gpool_ref, grnn_ref, wout_ref)

        h2 = _rms(x1, gxa_ref[...]).astype(BF16)
        q = (_dot(h2, wq_ref[...]) * (XA_HEAD_DIM ** -0.5)).astype(BF16)
        outs = []
        for hd in range(XA_HEADS):
            sl = slice(hd * XA_HEAD_DIM, (hd + 1) * XA_HEAD_DIM)
            s = _dot(q[:, sl], kb_ref[sq, sl, :])
            s = s - jnp.max(s, axis=-1, keepdims=True)
            p = jnp.exp(s)
            p = p * (1.0 / jnp.sum(p, axis=-1, keepdims=True))
            outs.append(_dot(p.astype(BF16), vb_ref[sq, :, sl]))
        o = jnp.concatenate(outs, axis=1).astype(BF16)
        x2 = x1 + _dot(o, wo_ref[...])

        h3, xres = _moe_prologue(x2, gffn_ref, sg_ref, su_ref, sd_ref)
        xres_ref[sq] = xres
        hb_ref[sq] = h3.astype(BF16)
        slot_rows, w_rows, run_len, g_exp, g_rel = _route(h3, wr_ref, rbias_ref, carry[...])
        _store_rows(slot_ref.at[sq], slot_rows, I32)
        _store_rows(gw_ref.at[sq], w_rows, F32)
        gexp_ref[sq, 0] = g_exp.astype(I32)
        grel_ref[sq, 0] = g_rel.astype(I32)
        cover_ref[sq, 0] = jnp.broadcast_to(jnp.sum(run_len, axis=0, keepdims=True), cover_ref.shape[2:])
        carry[...] = carry[...] + run_len
    cnt_ref[...] = jnp.broadcast_to(carry[...], cnt_ref.shape)


def _trunk_p(x, kb, vb, mixw, xaw, moew):
    bsz, seq, _ = x.shape
    n_j = seq // TS
    q = TRUNK_SEQS
    assert bsz % q == 0
    seq_tile = lambda g, j: (g, j, 0)
    lane_tile = lambda g, j: (g, 0, j)
    per_tile = lambda g, j: (g, j, 0, 0)
    per_seq = lambda g, j: (g, 0, 0)
    weights = list(mixw) + list(xaw) + list(moew)
    in_specs = [pl.BlockSpec((q, TS, D_MODEL), seq_tile),
                pl.BlockSpec((q, D_MODEL, N_MEM), per_seq),
                pl.BlockSpec((q, N_MEM, D_MODEL), per_seq)] + [_const_spec(w.shape) for w in weights]
    lanes = _group_lanes(TS)
    out_shape = [jax.ShapeDtypeStruct((bsz, seq, D_MODEL), F32),
                 jax.ShapeDtypeStruct((bsz, seq, D_MODEL), BF16),
                 jax.ShapeDtypeStruct((bsz, TOP_K, seq), I32),
                 jax.ShapeDtypeStruct((bsz, TOP_K, seq), F32),
                 jax.ShapeDtypeStruct((bsz, n_j, 1, lanes), I32),
                 jax.ShapeDtypeStruct((bsz, n_j, 1, lanes), I32),
                 jax.ShapeDtypeStruct((bsz, n_j, 1, 128), F32),
                 jax.ShapeDtypeStruct((N_EXPERTS, 128), F32),
                 jax.ShapeDtypeStruct((bsz, POOL_BUF, D_POOL), F32),
                 jax.ShapeDtypeStruct((bsz, CONV_WIDTH - 1, D_RNN), F32),
                 jax.ShapeDtypeStruct((bsz, 1, D_RNN), F32)]
    out_specs = [pl.BlockSpec((q, TS, D_MODEL), seq_tile),
                 pl.BlockSpec((q, TS, D_MODEL), seq_tile),
                 pl.BlockSpec((q, TOP_K, TS), lane_tile),
                 pl.BlockSpec((q, TOP_K, TS), lane_tile),
                 pl.BlockSpec((q, 1, 1, lanes), per_tile),
                 pl.BlockSpec((q, 1, 1, lanes), per_tile),
                 pl.BlockSpec((q, 1, 1, 128), per_tile),
                 pl.BlockSpec((N_EXPERTS, 128), lambda g, j: (0, 0)),
                 pl.BlockSpec((q, POOL_BUF, D_POOL), per_seq),
                 pl.BlockSpec((q, CONV_WIDTH - 1, D_RNN), per_seq),
                 pl.BlockSpec((q, 1, D_RNN), per_seq)]
    return pl.pallas_call(
        _trunk_p_kernel,
        grid=(bsz // q, n_j),
        in_specs=in_specs,
        out_specs=out_specs,
        out_shape=out_shape,
        scratch_shapes=[pltpu.VMEM((q, HALO, D_POOL), F32), pltpu.VMEM((q, CONV_HALO, D_RNN), F32),
                        pltpu.VMEM((q, 1, D_RNN), F32), pltpu.VMEM((N_EXPERTS, 1), F32)],
        compiler_params=pltpu.CompilerParams(dimension_semantics=("arbitrary", "arbitrary"),
                                             vmem_limit_bytes=VMEM_LIMIT),
        name="trunk_p",
    )(x, kb, vb, *weights)


def _mix_s_kernel(x_ref, pool_ref, conv_ref, h0_ref,
                  gmix_ref, win_ref, pw_ref, pscale_ref, cw_ref, cb_ref, wa_ref, ba_ref, wx_ref, bx_ref,
                  lam_ref, gpool_ref, grnn_ref, wout_ref, gxa_ref, wq_ref,
                  x1_ref, q_ref, npool_ref, nconv_ref, nh_ref):
    x = x_ref[...]
    h = _rms(x, gmix_ref[...]).astype(BF16)
    z = _dot(h, win_ref[...])
    u_pool = z[:, :D_POOL]
    u_rnn = z[:, D_POOL:D_POOL + D_RNN]
    u_gate = z[:, D_POOL + D_RNN:]

    means = []
    for g, w in enumerate(POOL_WINDOWS):
        sl = slice(g * POOL_GROUP, (g + 1) * POOL_GROUP)
        s = u_pool[:, sl]
        for k in range(1, w):
            s = s + pool_ref[POOL_BUF - k, :, sl]
        means.append(s * (1.0 / min(w, PAST_LEN + 1)))
    mean = jnp.concatenate(means, axis=1)
    y_pool = _pool_project(mean, u_pool, pw_ref, pscale_ref)

    c = u_rnn * cw_ref[CONV_WIDTH - 1:CONV_WIDTH, :]
    for k in range(1, CONV_WIDTH):
        c = c + conv_ref[CONV_WIDTH - 1 - k] * cw_ref[CONV_WIDTH - 1 - k:CONV_WIDTH - k, :]
    c = c + cb_ref[...]
    a, bt = _gates_and_decay(c, None, wa_ref, ba_ref, wx_ref, bx_ref, lam_ref)
    hs = a * h0_ref[...] + bt

    x1 = x + _merge_out(y_pool, hs, u_gate, gpool_ref, grnn_ref, wout_ref)
    x1_ref[...] = x1
    h2 = _rms(x1, gxa_ref[...]).astype(BF16)
    q_ref[...] = _dot(h2, wq_ref[...]) * (XA_HEAD_DIM ** -0.5)

    npool_ref[:POOL_BUF - 1] = pool_ref[1:]
    npool_ref[POOL_BUF - 1] = u_pool
    nconv_ref[:CONV_WIDTH - 2] = conv_ref[1:]
    nconv_ref[CONV_WIDTH - 2] = u_rnn
    nh_ref[...] = hs


def _mix_s(x, pool, conv, h0, mixw, gxa, wq):
    bsz = x.shape[0]
    args = [x, pool, conv, h0] + list(mixw) + [gxa, wq]
    return pl.pallas_call(
        _mix_s_kernel,
        grid=(1,),
        in_specs=[_const_spec(a.shape) for a in args],
        out_specs=[_const_spec((bsz, D_MODEL)), _const_spec((bsz, D_MODEL)), _const_spec(pool.shape),
                   _const_spec(conv.shape), _const_spec((bsz, D_RNN))],
        out_shape=[jax.ShapeDtypeStruct((bsz, D_MODEL), F32), jax.ShapeDtypeStruct((bsz, D_MODEL), F32),
                   jax.ShapeDtypeStruct(pool.shape, F32), jax.ShapeDtypeStruct(conv.shape, F32),
                   jax.ShapeDtypeStruct((bsz, D_RNN), F32)],
        compiler_params=pltpu.CompilerParams(dimension_semantics=("arbitrary",), vmem_limit_bytes=VMEM_LIMIT),
        name="mix_s",
    )(*args)


ATTN_S_BB = 4


def _attn_s_kernel(q_ref, k_ref, v_ref, o_ref):
    q = q_ref[...][:, None]
    s = jnp.sum(k_ref[...] * q, axis=-1, keepdims=True)
    s = s - jnp.max(s, axis=1, keepdims=True)
    p = jnp.exp(s)
    p = p / jnp.sum(p, axis=1, keepdims=True)
    o_ref[...] = jnp.sum(p * v_ref[...], axis=1)


def _attn_s(q, k, v):
    bsz = q.shape[0]
    kv_spec = pl.BlockSpec((ATTN_S_BB, N_MEM, XA_HEADS, XA_HEAD_DIM), lambda i: (i, 0, 0, 0))
    q_spec = pl.BlockSpec((ATTN_S_BB, XA_HEADS, XA_HEAD_DIM), lambda i: (i, 0, 0))
    o = pl.pallas_call(
        _attn_s_kernel,
        grid=(bsz // ATTN_S_BB,),
        in_specs=[q_spec, kv_spec, kv_spec],
        out_specs=q_spec,
        out_shape=jax.ShapeDtypeStruct((bsz, XA_HEADS, XA_HEAD_DIM), F32),
        compiler_params=pltpu.CompilerParams(dimension_semantics=("arbitrary",), vmem_limit_bytes=VMEM_LIMIT),
        name="attn_s",
    )(q.reshape(bsz, XA_HEADS, XA_HEAD_DIM), k, v)
    return o.reshape(bsz, D_MODEL)


def _post_s_kernel(x1_ref, o_ref, wo_ref, gffn_ref, sg_ref, su_ref, sd_ref, wr_ref, rbias_ref,
                   xres_ref, hb_ref, slot_ref, gw_ref, gexp_ref, grel_ref, cover_ref, cnt_ref):
    x2 = x1_ref[...] + _dot(o_ref[...].astype(BF16), wo_ref[...])
    h3, xres = _moe_prologue(x2, gffn_ref, sg_ref, su_ref, sd_ref)
    xres_ref[...] = xres
    hb_ref[...] = h3.astype(BF16)
    slot_rows, w_rows, run_len, g_exp, g_rel = _route(h3, wr_ref, rbias_ref, jnp.zeros((N_EXPERTS, 1), F32))
    _store_rows(slot_ref, slot_rows, I32)
    _store_rows(gw_ref, w_rows, F32)
    gexp_ref[0] = g_exp.astype(I32)
    grel_ref[0] = g_rel.astype(I32)
    cover_ref[0] = jnp.broadcast_to(jnp.sum(run_len, axis=0, keepdims=True), cover_ref.shape[1:])
    cnt_ref[...] = jnp.broadcast_to(run_len, cnt_ref.shape)


def _post_s(x1, o, wo, moew):
    bsz = x1.shape[0]
    args = [x1, o, wo] + list(moew)
    out_shape = [jax.ShapeDtypeStruct((bsz, D_MODEL), F32),
                 jax.ShapeDtypeStruct((bsz, D_MODEL), BF16),
                 jax.ShapeDtypeStruct((TOP_K, bsz), I32),
                 jax.ShapeDtypeStruct((TOP_K, bsz), F32),
                 jax.ShapeDtypeStruct((1, 1, _group_lanes(bsz)), I32),
                 jax.ShapeDtypeStruct((1, 1, _group_lanes(bsz)), I32),
                 jax.ShapeDtypeStruct((1, 1, 128), F32),
                 jax.ShapeDtypeStruct((N_EXPERTS, 128), F32)]
    return pl.pallas_call(
        _post_s_kernel,
        grid=(1,),
        in_specs=[_const_spec(a.shape) for a in args],
        out_specs=[_const_spec(s.shape) for s in out_shape],
        out_shape=out_shape,
        compiler_params=pltpu.CompilerParams(dimension_semantics=("arbitrary",), vmem_limit_bytes=VMEM_LIMIT),
        name="post_s",
    )(*args)


def _plan_kernel(rp_ref, rs_ref, base_p_ref, base_s_ref, fill_at_ref, fill_n_ref, exp_ref, nreal_ref):
    n_blocks = exp_ref.shape[0]
    shift = BM.bit_length() - 1

    def per_expert(e, carry):
        blk0, last_e = carry
        rows = rp_ref[e] + rs_ref[e]
        start = lax.shift_left(blk0, shift)
        n_blk = lax.shift_right_logical(rows + (BM - 1), shift)
        base_p_ref[e] = start
        base_s_ref[e] = start + rp_ref[e]
        fill_at_ref[e] = start + rows
        fill_n_ref[e] = _groups(lax.shift_left(n_blk, shift) - rows)

        def per_block(j, _):
            exp_ref[blk0 + j] = e
            return _

        lax.fori_loop(0, n_blk, per_block, 0)
        return blk0 + n_blk, jnp.where(rows > 0, e, last_e)

    n_real, last_e = lax.fori_loop(0, N_EXPERTS, per_expert, (jnp.int32(0), jnp.int32(0)))
    nreal_ref[0] = n_real

    def rest(b, _):
        exp_ref[b] = last_e
        return _

    lax.fori_loop(n_real, n_blocks, rest, 0)


def _plan(rows_p, rows_s, n_rows):
    assert BM & (BM - 1) == 0 and n_rows % BM == 0
    smem = pl.BlockSpec(memory_space=pltpu.SMEM)
    return pl.pallas_call(
        _plan_kernel,
        in_specs=[smem, smem],
        out_specs=[smem] * 6,
        out_shape=[jax.ShapeDtypeStruct((N_EXPERTS,), I32)] * 4
        + [jax.ShapeDtypeStruct((n_rows // BM,), I32), jax.ShapeDtypeStruct((1,), I32)],
        name="plan",
    )(rows_p, rows_s)


def _groups(n_rows):
    return lax.shift_right_logical(n_rows, RUN_ALIGN.bit_length() - 1)


GROUP_UNROLL = 8


def _for_each_group(n_groups, grow_ref, fn):
    def one(g, priority):
        fn(pl.multiple_of(g * RUN_ALIGN, RUN_ALIGN), pl.multiple_of(grow_ref[0, 0, g], RUN_ALIGN), priority)

    def several(j, _):
        for u in range(GROUP_UNROLL):
            one(j * GROUP_UNROLL + u, u % 2)
        return _

    def single(g, _):
        one(g, 0)
        return _

    n_full = lax.shift_right_logical(n_groups, GROUP_UNROLL.bit_length() - 1)
    lax.fori_loop(0, n_full, several, 0)
    lax.fori_loop(n_full * GROUP_UNROLL, n_groups, single, 0)


def _wait_groups(n_groups, group_wait, bulk_wait):
    def bulk(j, _):
        bulk_wait()
        return _

    def single(g, _):
        group_wait()
        return _

    n_full = lax.shift_right_logical(n_groups, GROUP_UNROLL.bit_length() - 1)
    lax.fori_loop(0, n_full, bulk, 0)
    lax.fori_loop(n_full * GROUP_UNROLL, n_groups, single, 0)


def _group_rows_kernel(gexp_ref, grel_ref, base_ref, out_ref):
    rows = grel_ref[...]
    e = gexp_ref[...]
    for ex in range(N_EXPERTS):
        rows = rows + jnp.where(e == ex, base_ref[ex], 0)
    out_ref[...] = rows


def _group_rows(gexp, grel, base):
    shape = gexp.shape
    flat = (shape[0], shape[-1])
    out = pl.pallas_call(
        _group_rows_kernel,
        in_specs=[pl.BlockSpec(memory_space=pltpu.VMEM), pl.BlockSpec(memory_space=pltpu.VMEM),
                  pl.BlockSpec(memory_space=pltpu.SMEM)],
        out_specs=pl.BlockSpec(memory_space=pltpu.VMEM),
        out_shape=jax.ShapeDtypeStruct(flat, I32),
        name="group_rows",
    )(gexp.reshape(flat), grel.reshape(flat), base)
    return out.reshape(shape)


def _dispatch_kernel(cover_p, cover_s, fill_at, fill_n, nreal_ref, grow_p, grow_s,
                     slot_p_ref, h_p_ref, slot_s_ref, h_s_ref, xs_ref, sbuf, zbuf, sem):
    i = pl.program_id(0)
    last = pl.num_programs(0) - 1
    cur = lax.rem(i, 2)

    def drain(s, n_rows):
        def wait_rows(n):
            pltpu.make_async_copy(sbuf.at[s, pl.ds(0, n)], xs_ref.at[pl.ds(0, n)], sem.at[s]).wait()
        _wait_groups(_groups(n_rows), lambda: wait_rows(RUN_ALIGN), lambda: wait_rows(RUN_ALIGN * GROUP_UNROLL))

    def tile(n_rows, grow_ref, slot_ref, h_ref):
        def sort_chunk(c):
            rid = (c * SORT_CHUNK + lax.broadcasted_iota(I32, (SORT_CHUNK, 1), 0)).astype(jnp.int16)
            slots = slot_ref[...].astype(jnp.int16)
            hit = rid == slots[0:1, :]
            for k in range(1, TOP_K):
                hit = hit | (rid == slots[k:k + 1, :])
            sbuf[cur, c * SORT_CHUNK:(c + 1) * SORT_CHUNK] = _dot(hit.astype(BF16), h_ref[...]).astype(BF16)

        def send(local, glob, priority):
            pltpu.make_async_copy(sbuf.at[cur, pl.ds(local, RUN_ALIGN)], xs_ref.at[pl.ds(glob, RUN_ALIGN)],
                                  sem.at[cur]).start(priority=priority)

        always = h_ref.shape[0] * TOP_K // SORT_CHUNK
        for c in range(_sorted_rows(h_ref.shape[0]) // SORT_CHUNK):
            if c < always:
                sort_chunk(c)
            else:
                pl.when(c * SORT_CHUNK < n_rows)(functools.partial(sort_chunk, c))
        _for_each_group(_groups(n_rows), grow_ref, send)

    @pl.when(i >= 2)
    def _():
        drain(cur, cover_p[i - 2])

    @pl.when(i < last)
    def _():
        tile(cover_p[i], grow_p, slot_p_ref, h_p_ref)

    @pl.when(i == last)
    def _():
        tile(cover_s[0], grow_s, slot_s_ref, h_s_ref)
        drain(1 - cur, cover_p[last - 1])
        drain(cur, cover_s[0])
        zbuf[...] = jnp.zeros_like(zbuf)

        def group_fill(e, g):
            return pltpu.make_async_copy(
                zbuf.at[pl.ds(0, RUN_ALIGN)],
                xs_ref.at[pl.ds(pl.multiple_of(fill_at[e] + g * RUN_ALIGN, RUN_ALIGN), RUN_ALIGN)], sem.at[2])

        def per_expert(e, n):
            def start(g, _):
                group_fill(e, g).start()
                return _
            lax.fori_loop(0, fill_n[e], start, 0)
            return n + fill_n[e]

        n_fill = lax.fori_loop(0, N_EXPERTS, per_expert, jnp.int32(0))

        def wait_group(g, _):
            group_fill(0, 0).wait()
            return _

        lax.fori_loop(0, n_fill, wait_group, 0)

        n_real = nreal_ref[0]
        n_tail = xs_ref.shape[0] // BM - n_real

        def blk_fill(b):
            return pltpu.make_async_copy(zbuf, xs_ref.at[pl.ds(pl.multiple_of((n_real + b) * BM, BM), BM)], sem.at[2])

        def start_blk(b, _):
            blk_fill(b).start()
            return _

        def wait_blk(b, _):
            blk_fill(b).wait()
            return _

        lax.fori_loop(0, n_tail, start_blk, 0)
        lax.fori_loop(0, n_tail, wait_blk, 0)


def _dispatch(cover_p, cover_s, fills, grow_p, grow_s, slot_p, hb_p, slot_s, hb_s, n_rows):
    n_p = hb_p.shape[0] // TS
    smem = pl.BlockSpec(memory_space=pltpu.SMEM)
    clamp = lambda i: jnp.minimum(i, n_p - 1)
    per_tile = pl.BlockSpec((1, 1, grow_p.shape[-1]), lambda i: (clamp(i), 0, 0), memory_space=pltpu.SMEM)
    return pl.pallas_call(
        _dispatch_kernel,
        grid=(n_p + 1,),
        in_specs=[smem] * 5 + [per_tile, smem,
                               pl.BlockSpec((TOP_K, TS), lambda i: (0, clamp(i))),
                               pl.BlockSpec((TS, D_MODEL), lambda i: (clamp(i), 0)),
                               _const_spec(slot_s.shape), _const_spec(hb_s.shape)],
        out_specs=pl.BlockSpec(memory_space=pl.ANY),
        out_shape=jax.ShapeDtypeStruct((n_rows, D_MODEL), BF16),
        scratch_shapes=[pltpu.VMEM((2, _sorted_rows(TS), D_MODEL), BF16),
                        pltpu.VMEM((BM, D_MODEL), BF16), pltpu.SemaphoreType.DMA((3,))],
        compiler_params=pltpu.CompilerParams(dimension_semantics=("arbitrary",), has_side_effects=True,
                                             vmem_limit_bytes=VMEM_LIMIT),
        name="dispatch",
    )(cover_p, cover_s, *fills, grow_p, grow_s, slot_p, hb_p, slot_s, hb_s)


GMM_LANES = 2


def _gmm_kernel(exp_ref, nreal_ref, *refs):
    x_ref = refs[0]
    w_refs = refs[1:1 + 3 * GMM_LANES]
    y_ref = refs[1 + 3 * GMM_LANES]
    scratch = refs[2 + 3 * GMM_LANES:]
    s = pl.program_id(0)
    n_real = nreal_ref[0]

    for lane in range(GMM_LANES):
        wg_ref, wu_ref, wd_ref = w_refs[3 * lane:3 * lane + 3]
        wgu, wdn = scratch[2 * lane:2 * lane + 2]
        b = s * GMM_LANES + lane
        new_expert = (s == 0) | (exp_ref[b] != exp_ref[jnp.maximum(b - GMM_LANES, 0)])

        @pl.when((b < n_real) & new_expert)
        def _():
            wgu[:, :D_EXPERT] = wg_ref[0].astype(BF16)
            wgu[:, D_EXPERT:] = wu_ref[0].astype(BF16)
            wdn[...] = wd_ref[0].astype(BF16)

    @pl.when(s * GMM_LANES < n_real)
    def _():
        for lane in range(GMM_LANES):
            wgu, wdn = scratch[2 * lane:2 * lane + 2]
            rows = slice(lane * BM, (lane + 1) * BM)
            gu = _dot(x_ref[rows, :], wgu[...])
            act = jax.nn.silu(gu[:, :D_EXPERT]) * gu[:, D_EXPERT:]
            y = _dot(act.astype(BF16), wdn[...]).astype(BF16)
            y_ref[rows, :] = jnp.where(s * GMM_LANES + lane < n_real, y, jnp.zeros_like(y))

    @pl.when(s * GMM_LANES >= n_real)
    def _():
        y_ref[...] = jnp.zeros_like(y_ref)


def _gmm(blk_exp, n_real, xs, wg, wu, wd):
    n_rows = xs.shape[0]
    step_rows = BM * GMM_LANES
    assert n_rows % step_rows == 0
    last_step = lambda nreal: (nreal[0] - 1) // GMM_LANES
    w_specs, scratch = [], []
    for lane in range(GMM_LANES):
        weight = lambda s, exp, nreal, lane=lane: (exp[s * GMM_LANES + lane], 0, 0)
        w_specs += [pl.BlockSpec((1, D_MODEL, D_EXPERT), weight), pl.BlockSpec((1, D_MODEL, D_EXPERT), weight),
                    pl.BlockSpec((1, D_EXPERT, D_MODEL), weight)]
        scratch += [pltpu.VMEM((D_MODEL, 2 * D_EXPERT), BF16), pltpu.VMEM((D_EXPERT, D_MODEL), BF16)]
    grid_spec = pltpu.PrefetchScalarGridSpec(
        num_scalar_prefetch=2,
        grid=(n_rows // step_rows,),
        in_specs=[pl.BlockSpec((step_rows, D_MODEL), lambda s, exp, nreal: (jnp.minimum(s, last_step(nreal)), 0))]
        + w_specs,
        out_specs=pl.BlockSpec((step_rows, D_MODEL), lambda s, exp, nreal: (s, 0)),
        scratch_shapes=scratch,
    )
    return pl.pallas_call(
        _gmm_kernel,
        grid_spec=grid_spec,
        out_shape=jax.ShapeDtypeStruct((n_rows, D_MODEL), BF16),
        compiler_params=pltpu.CompilerParams(dimension_semantics=("arbitrary",), vmem_limit_bytes=VMEM_LIMIT),
        name="gmm",
    )(blk_exp, n_real, xs, *([wg, wu, wd] * GMM_LANES))


def _combine_kernel(cover_ref, grow_ref, grow_next_ref,
                    slot_ref, gw_ref, xres_ref, gfin_ref, ys_ref, out_ref, ybuf, sem):
    i = pl.program_id(0)
    n = pl.num_programs(0)
    cur = lax.rem(i, 2)
    rows = ybuf.shape[1]

    def rows_copy(s, local, glob, n_rows):
        return pltpu.make_async_copy(ys_ref.at[pl.ds(glob, n_rows)], ybuf.at[s, pl.ds(local, n_rows)], sem.at[s])

    def gather(n_rows, gr_ref, s):
        _for_each_group(_groups(n_rows), gr_ref,
                        lambda l, g, priority: rows_copy(s, l, g, RUN_ALIGN).start(priority=priority))

    @pl.when(i == 0)
    def _():
        ybuf[...] = jnp.zeros_like(ybuf)
        gather(cover_ref[0], grow_ref, 0)

    @pl.when(i + 1 < n)
    def _():
        gather(cover_ref[jnp.minimum(i + 1, n - 1)], grow_next_ref, 1 - cur)

    _wait_groups(_groups(cover_ref[i]), lambda: rows_copy(cur, 0, 0, RUN_ALIGN).wait(),
                 lambda: rows_copy(cur, 0, 0, RUN_ALIGN * GROUP_UNROLL).wait())

    tm = slot_ref.shape[0]
    slot_b = [jnp.broadcast_to(slot_ref[:, k:k + 1], (tm, 128)).astype(jnp.int16) for k in range(TOP_K)]
    gw_b = [jnp.broadcast_to(gw_ref[:, k:k + 1], (tm, 128)).astype(BF16) for k in range(TOP_K)]
    lane = lax.broadcasted_iota(I32, (1, 128), 1)

    def chunk_sum(c):
        pieces = []
        for p in range(SORT_CHUNK // 128):
            col = (lane + (c * SORT_CHUNK + p * 128)).astype(jnp.int16)
            w = jnp.zeros((tm, 128), BF16)
            for k in range(TOP_K):
                w = jnp.where(slot_b[k] == col, gw_b[k], w)
            pieces.append(w)
        return _dot(jnp.concatenate(pieces, axis=1), ybuf[cur, c * SORT_CHUNK:(c + 1) * SORT_CHUNK])

    always = tm * TOP_K // SORT_CHUNK
    acc = xres_ref[...]
    for c in range(always):
        acc = acc + chunk_sum(c)
    out_ref[...] = acc
    for c in range(always, rows // SORT_CHUNK):
        @pl.when(c * SORT_CHUNK < cover_ref[i])
        def _():
            out_ref[...] += chunk_sum(c)
    out_ref[...] = _rms(out_ref[...], gfin_ref[...])


def _combine(cover, grow, slot_t, gw_t, xres, gfin, ys):
    t = xres.shape[0]
    tm = min(t, TS)
    n = t // tm
    smem = pl.BlockSpec(memory_space=pltpu.SMEM)
    lanes = grow.shape[-1]
    this_tile = pl.BlockSpec((1, 1, lanes), lambda i: (i, 0, 0), memory_space=pltpu.SMEM)
    next_tile = pl.BlockSpec((1, 1, lanes), lambda i: (jnp.minimum(i + 1, n - 1), 0, 0), memory_space=pltpu.SMEM)
    return pl.pallas_call(
        _combine_kernel,
        grid=(n,),
        in_specs=[smem, this_tile, next_tile,
                  pl.BlockSpec((tm, TOP_K), lambda i: (i, 0)),
                  pl.BlockSpec((tm, TOP_K), lambda i: (i, 0)),
                  pl.BlockSpec((tm, D_MODEL), lambda i: (i, 0)),
                  _const_spec((1, D_MODEL)),
                  pl.BlockSpec(memory_space=pl.ANY)],
        out_specs=pl.BlockSpec((tm, D_MODEL), lambda i: (i, 0)),
        out_shape=jax.ShapeDtypeStruct((t, D_MODEL), F32),
        scratch_shapes=[pltpu.VMEM((2, _sorted_rows(tm), D_MODEL), BF16), pltpu.SemaphoreType.DMA((2,))],
        compiler_params=pltpu.CompilerParams(dimension_semantics=("arbitrary",), vmem_limit_bytes=VMEM_LIMIT),
        name="combine",
    )(cover, grow, grow, slot_t, gw_t, xres, gfin, ys)


def _block_diag_pairs(w):
    n_h, d, _ = w.shape
    half = n_h // 2
    on_diag = jnp.eye(half, dtype=bool)[None, :, None, :, None]
    blocks = w.reshape(2, half, d, 1, d)
    out = jnp.where(on_diag, blocks, 0.0)
    return out.reshape(2, half * d, half * d).astype(BF16)


def kernel(x_prompt, x_sample, state_pool, state_conv, state_h, cache_mem_k, cache_mem_v, mem_prompt, norm_mix, w_in, pool_w, pool_scale, conv_w, conv_b, gate_a_w, gate_a_b, gate_x_w, gate_x_b, lru_lambda, norm_pool_out, norm_rnn_out, w_out, norm_xattn, norm_mem, xa_wq, xa_wk, xa_wv, xa_wo, norm_ffn, router_w, router_bias, exp_w_gate, exp_w_up, exp_w_down, sh_w_gate, sh_w_up, sh_w_down, norm_final):
    bp, seq, _ = x_prompt.shape
    bs = x_sample.shape[0]
    tp = bp * seq
    row = lambda v: v.reshape(1, -1)
    bf = lambda v: v.astype(BF16)

    mixw = (row(norm_mix[0]), bf(w_in[0]), _block_diag_pairs(pool_w[0]), row(pool_scale[0]), conv_w[0],
            row(conv_b[0]), _block_diag_pairs(gate_a_w[0]), row(gate_a_b[0]), _block_diag_pairs(gate_x_w[0]),
            row(gate_x_b[0]), row(lru_lambda[0]), row(norm_pool_out[0]), row(norm_rnn_out[0]), bf(w_out[0]))
    xaw = (row(norm_xattn[0]), bf(xa_wq[0]), bf(xa_wo[0]))
    moew = (row(norm_ffn[0]), bf(sh_w_gate[0]), bf(sh_w_up[0]), bf(sh_w_down[0]), router_w[0].T,
            router_bias[0].reshape(N_EXPERTS, 1))

    mk, mv, kb, vb = _memkv(mem_prompt, row(norm_mem[0]), bf(xa_wk[0]), bf(xa_wv[0]))
    (xres_p, hb_p, slot_p, gw_p, gexp_p, grel_p, cover_p, cnt_p, pool_p, conv_p, h_p) = _trunk_p(
        x_prompt, kb, vb, mixw, xaw, moew)

    x1_s, q_s, pool_s, conv_s, h_s = _mix_s(x_sample.reshape(bs, D_MODEL), state_pool[0].transpose(1, 0, 2),
                                            state_conv[0].transpose(1, 0, 2), state_h[0], mixw, xaw[0], xaw[1])
    pool_s = pool_s.transpose(1, 0, 2)
    conv_s = conv_s.transpose(1, 0, 2)
    o_s = _attn_s(q_s, cache_mem_k[0], cache_mem_v[0])
    xres_s, hb_s, slot_s, gw_s, gexp_s, grel_s, cover_s, cnt_s = _post_s(x1_s, o_s, xaw[2], moew)

    n_tiles = tp // TS + 1
    n_rows = _round_up((tp + bs) * TOP_K + n_tiles * N_EXPERTS * (RUN_ALIGN - 1) + N_EXPERTS * (BM - 1),
                       BM * GMM_LANES)
    ints = lambda v: v[..., 0].astype(I32)
    base_p, base_s, fill_at, fill_n, blk_exp, n_real = _plan(ints(cnt_p), ints(cnt_s), n_rows)
    cover_p, cover_s = ints(cover_p).reshape(-1), ints(cover_s).reshape(-1)
    xres_p = xres_p.reshape(tp, D_MODEL)
    hb_p = hb_p.reshape(tp, D_MODEL)
    slot_p = slot_p.transpose(1, 0, 2).reshape(TOP_K, tp)
    gw_p = gw_p.transpose(1, 0, 2).reshape(TOP_K, tp)
    gexp_p = gexp_p.reshape(-1, 1, gexp_p.shape[-1])
    grel_p = grel_p.reshape(-1, 1, grel_p.shape[-1])
    grow_p = _group_rows(gexp_p, grel_p, base_p)
    grow_s = _group_rows(gexp_s, grel_s, base_s)

    xs = _dispatch(cover_p, cover_s, (fill_at, fill_n, n_real), grow_p, grow_s, slot_p, hb_p, slot_s, hb_s, n_rows)
    ys = _gmm(blk_exp, n_real, xs, exp_w_gate[0], exp_w_up[0], exp_w_down[0])

    gfin = row(norm_final)
    y_p = _combine(cover_p, grow_p, slot_p.T, gw_p.T, xres_p, gfin, ys)
    y_s = _combine(cover_s, grow_s, slot_s.T, gw_s.T, xres_s, gfin, ys)

    return (y_p.reshape(bp, seq, D_MODEL), y_s.reshape(bs, 1, D_MODEL),
            pool_p[None], conv_p[None], h_p.reshape(1, bp, D_RNN),
            mk[None], mv[None],
            pool_s[None], conv_s[None], h_s[None])
```
